```python
import jax, jax.numpy as jnp
from jax import lax
import numpy as np

D_MODEL = 1024
BATCH = 8
SEQ = 8192
DEPTH = 2

N_MIXERS = 2
POOL_WINDOWS = (2, 4, 8, 16)
N_POOL_GROUPS = len(POOL_WINDOWS)
POOL_GROUP_DIM = D_MODEL // N_POOL_GROUPS
CONV_WIDTH = 31
FFN_DIM = ((8 * D_MODEL // 3 + 255) // 256) * 256
FFN_CONV_WIDTH = 3
N_MOD = 6
EPS = 1e-6
N_POOL_LAYERS = (DEPTH + 1) // 2
N_CONV_LAYERS = DEPTH // 2

kernel_name = "hybrid_pool_conformer_convffn_trunk"


def rms_norm(x, g):
    xf = x.astype(jnp.float32)
    y = xf * lax.rsqrt(jnp.mean(xf * xf, axis=-1, keepdims=True) + EPS)
    return (y * g.astype(jnp.float32)).astype(x.dtype)


def layer_norm(x, g, b):
    xf = x.astype(jnp.float32)
    mu = jnp.mean(xf, axis=-1, keepdims=True)
    var = jnp.mean(jnp.square(xf - mu), axis=-1, keepdims=True)
    y = (xf - mu) * lax.rsqrt(var + EPS)
    return (y * g.astype(jnp.float32) + b.astype(jnp.float32)).astype(x.dtype)


def causal_depthwise_conv(x, w):
    k = w.shape[0]
    return lax.conv_general_dilated(
        x, w[:, None, :].astype(x.dtype), window_strides=(1,),
        padding=((k - 1, 0),), dimension_numbers=("NWC", "WIO", "NWC"),
        feature_group_count=x.shape[-1])


def causal_mean_pool(u, window):
    s = u.shape[1]
    cs = jnp.cumsum(u.astype(jnp.float32), axis=1)
    lag = jnp.pad(cs, ((0, 0), (window, 0), (0, 0)))[:, :s]
    cnt = jnp.minimum(jnp.arange(1, s + 1), window).astype(jnp.float32)
    return ((cs - lag) / cnt[None, :, None]).astype(u.dtype)


def pool_mixer(h, w_groups, scale):
    b, s, d = h.shape
    hg = h.reshape(b, s, N_POOL_GROUPS, POOL_GROUP_DIM)
    pooled = jnp.stack(
        [causal_mean_pool(hg[:, :, g], POOL_WINDOWS[g]) - hg[:, :, g]
         for g in range(N_POOL_GROUPS)], axis=2)
    y = jnp.einsum("bsgc,gcd->bsgd", pooled, w_groups).reshape(b, s, d)
    return y * scale


def conformer_conv_module(h, w_pw1, b_pw1, w_dw, b_dw, ln_g, ln_b, w_pw2, b_pw2):
    a = h @ w_pw1 + b_pw1
    val, gt = jnp.split(a, 2, axis=-1)
    u = val * jax.nn.sigmoid(gt)
    u = causal_depthwise_conv(u, w_dw) + b_dw
    u = jax.nn.silu(layer_norm(u, ln_g, ln_b))
    return u @ w_pw2 + b_pw2


def conv_ffn(h, w_up, w_dw, w_down):
    a = causal_depthwise_conv(h @ w_up, w_dw)
    g, v = jnp.split(a, 2, axis=-1)
    return (jax.nn.silu(g) * v) @ w_down


def _fwd_setup_inputs(seed: int = 0) -> dict:
    key = jax.random.key(seed)
    ks = jax.random.split(key, 24)
    d, f = D_MODEL, FFN_DIM
    nrm = lambda k, shape, s: jax.random.normal(k, shape, jnp.float32) * s
    return {
        "x": nrm(ks[0], (BATCH, SEQ, d), 1.0),
        "c": nrm(ks[1], (BATCH, d), 1.0),
        "ada_w": nrm(ks[2], (DEPTH, d, N_MOD * d), 0.5 * d ** -0.5),
        "ada_b": nrm(ks[3], (DEPTH, N_MOD * d), 0.02),
        "pre_g": 1.0 + nrm(ks[4], (DEPTH, 2, d), 0.05),
        "post_g": 1.0 + nrm(ks[5], (DEPTH, 2, d), 0.05),
        "pool_w": nrm(ks[6], (N_POOL_LAYERS, N_POOL_GROUPS, POOL_GROUP_DIM, POOL_GROUP_DIM), POOL_GROUP_DIM ** -0.5),
        "pool_scale": 1.0 + nrm(ks[7], (N_POOL_LAYERS, d), 0.1),
        "cv_w_pw1": nrm(ks[8], (N_CONV_LAYERS, d, 2 * d), d ** -0.5),
        "cv_b_pw1": nrm(ks[9], (N_CONV_LAYERS, 2 * d), 0.02),
        "cv_w_dw": nrm(ks[10], (N_CONV_LAYERS, CONV_WIDTH, d), CONV_WIDTH ** -0.5),
        "cv_b_dw": nrm(ks[11], (N_CONV_LAYERS, d), 0.02),
        "cv_ln_g": 1.0 + nrm(ks[12], (N_CONV_LAYERS, d), 0.05),
        "cv_ln_b": nrm(ks[13], (N_CONV_LAYERS, d), 0.02),
        "cv_w_pw2": nrm(ks[14], (N_CONV_LAYERS, d, d), d ** -0.5),
        "cv_b_pw2": nrm(ks[15], (N_CONV_LAYERS, d), 0.02),
        "ffn_w_up": nrm(ks[16], (DEPTH, d, 2 * f), d ** -0.5),
        "ffn_w_dw": nrm(ks[17], (DEPTH, FFN_CONV_WIDTH, 2 * f), FFN_CONV_WIDTH ** -0.5),
        "ffn_w_down": nrm(ks[18], (DEPTH, f, d), f ** -0.5),
    }


def _fwd_reference(x, c, ada_w, ada_b, pre_g, post_g, pool_w, pool_scale,
              cv_w_pw1, cv_b_pw1, cv_w_dw, cv_b_dw, cv_ln_g, cv_ln_b, cv_w_pw2, cv_b_pw2,
              ffn_w_up, ffn_w_dw, ffn_w_down):
    c_act = jax.nn.silu(c)
    for i in range(DEPTH):
        mod = c_act @ ada_w[i] + ada_b[i]
        sh_m, sc_m, gt_m, sh_f, sc_f, gt_f = [m[:, None, :] for m in jnp.split(mod, N_MOD, axis=-1)]

        h = rms_norm(x, pre_g[i, 0]) * (1.0 + sc_m) + sh_m
        j = i // N_MIXERS
        if i % N_MIXERS == 0:
            y = pool_mixer(h, pool_w[j], pool_scale[j])
        else:
            y = conformer_conv_module(h, cv_w_pw1[j], cv_b_pw1[j], cv_w_dw[j], cv_b_dw[j],
                                      cv_ln_g[j], cv_ln_b[j], cv_w_pw2[j], cv_b_pw2[j])
        x = x + gt_m * rms_norm(y, post_g[i, 0])

        h = rms_norm(x, pre_g[i, 1]) * (1.0 + sc_f) + sh_f
        y = conv_ffn(h, ffn_w_up[i], ffn_w_dw[i], ffn_w_down[i])
        x = x + gt_f * rms_norm(y, post_g[i, 1])
    return x


import jax as _jax
import jax.numpy as _jnp

TWIN_FORMAT = 'train_step'
FWD_PARAMS = ['x', 'c', 'ada_w', 'ada_b', 'pre_g', 'post_g', 'pool_w', 'pool_scale', 'cv_w_pw1', 'cv_b_pw1', 'cv_w_dw', 'cv_b_dw', 'cv_ln_g', 'cv_ln_b', 'cv_w_pw2', 'cv_b_pw2', 'ffn_w_up', 'ffn_w_dw', 'ffn_w_down']
TWIN_WEIGHTS = ['ada_w', 'ada_b', 'pre_g', 'post_g', 'pool_w', 'pool_scale', 'cv_w_pw1', 'cv_b_pw1', 'cv_w_dw', 'cv_b_dw', 'cv_ln_g', 'cv_ln_b', 'cv_w_pw2', 'cv_b_pw2', 'ffn_w_up', 'ffn_w_dw', 'ffn_w_down']
TWIN_DIFF_INPUT = 'x'
TWIN_INPUTS = ['x', 'c', 'ada_w', 'ada_b', 'pre_g', 'post_g', 'pool_w', 'pool_scale', 'cv_w_pw1', 'cv_b_pw1', 'cv_w_dw', 'cv_b_dw', 'cv_ln_g', 'cv_ln_b', 'cv_w_pw2', 'cv_b_pw2', 'ffn_w_up', 'ffn_w_dw', 'ffn_w_down', 'loss_target', 'm_ada_w', 'm_ada_b', 'm_pre_g', 'm_post_g', 'm_pool_w', 'm_pool_scale', 'm_cv_w_pw1', 'm_cv_b_pw1', 'm_cv_w_dw', 'm_cv_b_dw', 'm_cv_ln_g', 'm_cv_ln_b', 'm_cv_w_pw2', 'm_cv_b_pw2', 'm_ffn_w_up', 'm_ffn_w_dw', 'm_ffn_w_down', 'v_ada_w', 'v_ada_b', 'v_pre_g', 'v_post_g', 'v_pool_w', 'v_pool_scale', 'v_cv_w_pw1', 'v_cv_b_pw1', 'v_cv_w_dw', 'v_cv_b_dw', 'v_cv_ln_g', 'v_cv_ln_b', 'v_cv_w_pw2', 'v_cv_b_pw2', 'v_ffn_w_up', 'v_ffn_w_dw', 'v_ffn_w_down']
TWIN_OUTPUTS = ['loss', 'grad_x', 'grad_ada_w', 'grad_ada_b', 'grad_pre_g', 'grad_post_g', 'grad_pool_w', 'grad_pool_scale', 'grad_cv_w_pw1', 'grad_cv_b_pw1', 'grad_cv_w_dw', 'grad_cv_b_dw', 'grad_cv_ln_g', 'grad_cv_ln_b', 'grad_cv_w_pw2', 'grad_cv_b_pw2', 'grad_ffn_w_up', 'grad_ffn_w_dw', 'grad_ffn_w_down', 'delta_ada_w', 'delta_ada_b', 'delta_pre_g', 'delta_post_g', 'delta_pool_w', 'delta_pool_scale', 'delta_cv_w_pw1', 'delta_cv_b_pw1', 'delta_cv_w_dw', 'delta_cv_b_dw', 'delta_cv_ln_g', 'delta_cv_ln_b', 'delta_cv_w_pw2', 'delta_cv_b_pw2', 'delta_ffn_w_up', 'delta_ffn_w_dw', 'delta_ffn_w_down', 'new_m_ada_w', 'new_m_ada_b', 'new_m_pre_g', 'new_m_post_g', 'new_m_pool_w', 'new_m_pool_scale', 'new_m_cv_w_pw1', 'new_m_cv_b_pw1', 'new_m_cv_w_dw', 'new_m_cv_b_dw', 'new_m_cv_ln_g', 'new_m_cv_ln_b', 'new_m_cv_w_pw2', 'new_m_cv_b_pw2', 'new_m_ffn_w_up', 'new_m_ffn_w_dw', 'new_m_ffn_w_down', 'new_v_ada_w', 'new_v_ada_b', 'new_v_pre_g', 'new_v_post_g', 'new_v_pool_w', 'new_v_pool_scale', 'new_v_cv_w_pw1', 'new_v_cv_b_pw1', 'new_v_cv_w_dw', 'new_v_cv_b_dw', 'new_v_cv_ln_g', 'new_v_cv_ln_b', 'new_v_cv_w_pw2', 'new_v_cv_b_pw2', 'new_v_ffn_w_up', 'new_v_ffn_w_dw', 'new_v_ffn_w_down']
TWIN_LEAF_KINDS = {'loss': 'loss', 'grad_x': 'grad_x', 'grad_ada_w': 'grad_w', 'grad_ada_b': 'grad_w', 'grad_pre_g': 'grad_w', 'grad_post_g': 'grad_w', 'grad_pool_w': 'grad_w', 'grad_pool_scale': 'grad_w', 'grad_cv_w_pw1': 'grad_w', 'grad_cv_b_pw1': 'grad_w', 'grad_cv_w_dw': 'grad_w', 'grad_cv_b_dw': 'grad_w', 'grad_cv_ln_g': 'grad_w', 'grad_cv_ln_b': 'grad_w', 'grad_cv_w_pw2': 'grad_w', 'grad_cv_b_pw2': 'grad_w', 'grad_ffn_w_up': 'grad_w', 'grad_ffn_w_dw': 'grad_w', 'grad_ffn_w_down': 'grad_w', 'delta_ada_w': 'delta_w', 'delta_ada_b': 'delta_w', 'delta_pre_g': 'delta_w', 'delta_post_g': 'delta_w', 'delta_pool_w': 'delta_w', 'delta_pool_scale': 'delta_w', 'delta_cv_w_pw1': 'delta_w', 'delta_cv_b_pw1': 'delta_w', 'delta_cv_w_dw': 'delta_w', 'delta_cv_b_dw': 'delta_w', 'delta_cv_ln_g': 'delta_w', 'delta_cv_ln_b': 'delta_w', 'delta_cv_w_pw2': 'delta_w', 'delta_cv_b_pw2': 'delta_w', 'delta_ffn_w_up': 'delta_w', 'delta_ffn_w_dw': 'delta_w', 'delta_ffn_w_down': 'delta_w', 'new_m_ada_w': 'new_m', 'new_m_ada_b': 'new_m', 'new_m_pre_g': 'new_m', 'new_m_post_g': 'new_m', 'new_m_pool_w': 'new_m', 'new_m_pool_scale': 'new_m', 'new_m_cv_w_pw1': 'new_m', 'new_m_cv_b_pw1': 'new_m', 'new_m_cv_w_dw': 'new_m', 'new_m_cv_b_dw': 'new_m', 'new_m_cv_ln_g': 'new_m', 'new_m_cv_ln_b': 'new_m', 'new_m_cv_w_pw2': 'new_m', 'new_m_cv_b_pw2': 'new_m', 'new_m_ffn_w_up': 'new_m', 'new_m_ffn_w_dw': 'new_m', 'new_m_ffn_w_down': 'new_m', 'new_v_ada_w': 'new_v', 'new_v_ada_b': 'new_v', 'new_v_pre_g': 'new_v', 'new_v_post_g': 'new_v', 'new_v_pool_w': 'new_v', 'new_v_pool_scale': 'new_v', 'new_v_cv_w_pw1': 'new_v', 'new_v_cv_b_pw1': 'new_v', 'new_v_cv_w_dw': 'new_v', 'new_v_cv_b_dw': 'new_v', 'new_v_cv_ln_g': 'new_v', 'new_v_cv_ln_b': 'new_v', 'new_v_cv_w_pw2': 'new_v', 'new_v_cv_b_pw2': 'new_v', 'new_v_ffn_w_up': 'new_v', 'new_v_ffn_w_dw': 'new_v', 'new_v_ffn_w_down': 'new_v'}


def _forward(args):
    return _fwd_reference(*[args[k] for k in FWD_PARAMS])


def _output_shape():
    out = _jax.eval_shape(lambda: _forward(_fwd_setup_inputs(0)))
    return out.shape, out.dtype

N_MICROBATCH = 1
ADAM_LR = 0.001
ADAM_B1 = 0.9
ADAM_B2 = 0.999
ADAM_EPS = 1e-08
ADAM_WD = 0.01
ADAM_STEP = 10
PER_EXAMPLE_BATCH_AXIS = {'x': 0, 'c': 0, 'loss_target': 0}
SHARED_INPUTS = []
_WEIGHT_DTYPES = {'ada_w': _jnp.float32, 'ada_b': _jnp.float32, 'pre_g': _jnp.float32, 'post_g': _jnp.float32, 'pool_w': _jnp.float32, 'pool_scale': _jnp.float32, 'cv_w_pw1': _jnp.float32, 'cv_b_pw1': _jnp.float32, 'cv_w_dw': _jnp.float32, 'cv_b_dw': _jnp.float32, 'cv_ln_g': _jnp.float32, 'cv_ln_b': _jnp.float32, 'cv_w_pw2': _jnp.float32, 'cv_b_pw2': _jnp.float32, 'ffn_w_up': _jnp.float32, 'ffn_w_dw': _jnp.float32, 'ffn_w_down': _jnp.float32}
MOMENT_SCALE = {'ada_w': 3.467663e+00, 'ada_b': 6.658983e+00, 'pre_g': 2.605679e-01, 'post_g': 7.591533e+00, 'pool_w': 4.479049e-01, 'pool_scale': 4.894070e+00, 'cv_w_pw1': 1.936587e-01, 'cv_b_pw1': 7.838011e-01, 'cv_w_dw': 2.769421e-01, 'cv_b_dw': 1.850683e+00, 'cv_ln_g': 8.076851e-01, 'cv_ln_b': 1.214720e+00, 'cv_w_pw2': 5.186133e-01, 'cv_b_pw2': 2.675889e+00, 'ffn_w_up': 9.366315e-02, 'ffn_w_dw': 9.812444e-02, 'ffn_w_down': 1.766184e-01}


def _to_microbatches(a, axis):
    t = _jnp.moveaxis(a, axis, 0)
    t = t.reshape((N_MICROBATCH, t.shape[0] // N_MICROBATCH) + t.shape[1:])
    return _jnp.moveaxis(t, 1, axis + 1)


def setup_inputs(seed: int = 0) -> dict:
    inp = _fwd_setup_inputs(seed)
    key = _jax.random.fold_in(_jax.random.key(seed), 7919)
    shape, _ = _output_shape()
    out = dict(inp)
    out["loss_target"] = _jax.random.normal(_jax.random.fold_in(key, 0), shape, _jnp.float32)
    for i, name in enumerate(TWIN_WEIGHTS):
        w = inp[name].astype(_jnp.float32)
        if MOMENT_SCALE is None:
            s = _jnp.sqrt(_jnp.mean(_jnp.square(w)) + 1e-30)
        else:
            s = MOMENT_SCALE[name]
        km, kv = _jax.random.split(_jax.random.fold_in(key, i + 1))
        out[name] = w
        out["m_" + name] = s * _jax.random.normal(km, w.shape, _jnp.float32)
        out["v_" + name] = (s * s) * _jax.random.uniform(kv, w.shape, _jnp.float32, 0.5, 1.5)
    if N_MICROBATCH > 1:
        for name, axis in PER_EXAMPLE_BATCH_AXIS.items():
            out[name] = _to_microbatches(out[name], axis)
    return {'x': out['x'], 'c': out['c'], 'ada_w': out['ada_w'], 'ada_b': out['ada_b'], 'pre_g': out['pre_g'], 'post_g': out['post_g'], 'pool_w': out['pool_w'], 'pool_scale': out['pool_scale'], 'cv_w_pw1': out['cv_w_pw1'], 'cv_b_pw1': out['cv_b_pw1'], 'cv_w_dw': out['cv_w_dw'], 'cv_b_dw': out['cv_b_dw'], 'cv_ln_g': out['cv_ln_g'], 'cv_ln_b': out['cv_ln_b'], 'cv_w_pw2': out['cv_w_pw2'], 'cv_b_pw2': out['cv_b_pw2'], 'ffn_w_up': out['ffn_w_up'], 'ffn_w_dw': out['ffn_w_dw'], 'ffn_w_down': out['ffn_w_down'], 'loss_target': out['loss_target'], 'm_ada_w': out['m_ada_w'], 'm_ada_b': out['m_ada_b'], 'm_pre_g': out['m_pre_g'], 'm_post_g': out['m_post_g'], 'm_pool_w': out['m_pool_w'], 'm_pool_scale': out['m_pool_scale'], 'm_cv_w_pw1': out['m_cv_w_pw1'], 'm_cv_b_pw1': out['m_cv_b_pw1'], 'm_cv_w_dw': out['m_cv_w_dw'], 'm_cv_b_dw': out['m_cv_b_dw'], 'm_cv_ln_g': out['m_cv_ln_g'], 'm_cv_ln_b': out['m_cv_ln_b'], 'm_cv_w_pw2': out['m_cv_w_pw2'], 'm_cv_b_pw2': out['m_cv_b_pw2'], 'm_ffn_w_up': out['m_ffn_w_up'], 'm_ffn_w_dw': out['m_ffn_w_dw'], 'm_ffn_w_down': out['m_ffn_w_down'], 'v_ada_w': out['v_ada_w'], 'v_ada_b': out['v_ada_b'], 'v_pre_g': out['v_pre_g'], 'v_post_g': out['v_post_g'], 'v_pool_w': out['v_pool_w'], 'v_pool_scale': out['v_pool_scale'], 'v_cv_w_pw1': out['v_cv_w_pw1'], 'v_cv_b_pw1': out['v_cv_b_pw1'], 'v_cv_w_dw': out['v_cv_w_dw'], 'v_cv_b_dw': out['v_cv_b_dw'], 'v_cv_ln_g': out['v_cv_ln_g'], 'v_cv_ln_b': out['v_cv_ln_b'], 'v_cv_w_pw2': out['v_cv_w_pw2'], 'v_cv_b_pw2': out['v_cv_b_pw2'], 'v_ffn_w_up': out['v_ffn_w_up'], 'v_ffn_w_dw': out['v_ffn_w_dw'], 'v_ffn_w_down': out['v_ffn_w_down']}


def _loss(weights, diff, rest, loss_target):
    with _jax.named_scope("forward"):
        args = {**rest, TWIN_DIFF_INPUT: diff, **{k: w.astype(_WEIGHT_DTYPES[k]) for k, w in weights.items()}}
        y = _forward(args)
    with _jax.named_scope("loss_head"):
        err = _jnp.square(y.astype(_jnp.float32) - loss_target)
        return 0.5 * _jnp.sum(_jnp.mean(err, axis=-1)) if err.ndim else 0.5 * err


def _adamw(w, g, m, v):
    m = ADAM_B1 * m + (1.0 - ADAM_B1) * g
    v = ADAM_B2 * v + (1.0 - ADAM_B2) * _jnp.square(g)
    m_hat = m / (1.0 - ADAM_B1 ** ADAM_STEP)
    v_hat = v / (1.0 - ADAM_B2 ** ADAM_STEP)
    delta = -ADAM_LR * (m_hat / (_jnp.sqrt(v_hat) + ADAM_EPS) + ADAM_WD * w)
    return delta, m, v


def reference(x, c, ada_w, ada_b, pre_g, post_g, pool_w, pool_scale, cv_w_pw1, cv_b_pw1, cv_w_dw, cv_b_dw, cv_ln_g, cv_ln_b, cv_w_pw2, cv_b_pw2, ffn_w_up, ffn_w_dw, ffn_w_down, loss_target, m_ada_w, m_ada_b, m_pre_g, m_post_g, m_pool_w, m_pool_scale, m_cv_w_pw1, m_cv_b_pw1, m_cv_w_dw, m_cv_b_dw, m_cv_ln_g, m_cv_ln_b, m_cv_w_pw2, m_cv_b_pw2, m_ffn_w_up, m_ffn_w_dw, m_ffn_w_down, v_ada_w, v_ada_b, v_pre_g, v_post_g, v_pool_w, v_pool_scale, v_cv_w_pw1, v_cv_b_pw1, v_cv_w_dw, v_cv_b_dw, v_cv_ln_g, v_cv_ln_b, v_cv_w_pw2, v_cv_b_pw2, v_ffn_w_up, v_ffn_w_dw, v_ffn_w_down):
    given = dict(x=x, c=c, ada_w=ada_w, ada_b=ada_b, pre_g=pre_g, post_g=post_g, pool_w=pool_w, pool_scale=pool_scale, cv_w_pw1=cv_w_pw1, cv_b_pw1=cv_b_pw1, cv_w_dw=cv_w_dw, cv_b_dw=cv_b_dw, cv_ln_g=cv_ln_g, cv_ln_b=cv_ln_b, cv_w_pw2=cv_w_pw2, cv_b_pw2=cv_b_pw2, ffn_w_up=ffn_w_up, ffn_w_dw=ffn_w_dw, ffn_w_down=ffn_w_down, loss_target=loss_target, m_ada_w=m_ada_w, m_ada_b=m_ada_b, m_pre_g=m_pre_g, m_post_g=m_post_g, m_pool_w=m_pool_w, m_pool_scale=m_pool_scale, m_cv_w_pw1=m_cv_w_pw1, m_cv_b_pw1=m_cv_b_pw1, m_cv_w_dw=m_cv_w_dw, m_cv_b_dw=m_cv_b_dw, m_cv_ln_g=m_cv_ln_g, m_cv_ln_b=m_cv_ln_b, m_cv_w_pw2=m_cv_w_pw2, m_cv_b_pw2=m_cv_b_pw2, m_ffn_w_up=m_ffn_w_up, m_ffn_w_dw=m_ffn_w_dw, m_ffn_w_down=m_ffn_w_down, v_ada_w=v_ada_w, v_ada_b=v_ada_b, v_pre_g=v_pre_g, v_post_g=v_post_g, v_pool_w=v_pool_w, v_pool_scale=v_pool_scale, v_cv_w_pw1=v_cv_w_pw1, v_cv_b_pw1=v_cv_b_pw1, v_cv_w_dw=v_cv_w_dw, v_cv_b_dw=v_cv_b_dw, v_cv_ln_g=v_cv_ln_g, v_cv_ln_b=v_cv_ln_b, v_cv_w_pw2=v_cv_w_pw2, v_cv_b_pw2=v_cv_b_pw2, v_ffn_w_up=v_ffn_w_up, v_ffn_w_dw=v_ffn_w_dw, v_ffn_w_down=v_ffn_w_down)
    weights = {n: given[n] for n in TWIN_WEIGHTS}
    shared = {n: given[n] for n in SHARED_INPUTS}
    per_example = {n: given[n] for n in ['x', 'c']}
    grad_fn = _jax.value_and_grad(_loss, argnums=(0, 1))

    def one_microbatch(ex, loss_target):
        ex = dict(ex)
        diff = ex.pop(TWIN_DIFF_INPUT)
        return grad_fn(weights, diff, {**shared, **ex}, loss_target)

    if N_MICROBATCH == 1:
        loss, (grad_w, grad_x) = one_microbatch(per_example, given["loss_target"])
    else:
        def body(carry, xs):
            loss_sum, grad_sum = carry
            l_k, (gw_k, gx_k) = one_microbatch(xs[0], xs[1])
            with _jax.named_scope("update"):
                return (loss_sum + l_k, _jax.tree.map(_jnp.add, grad_sum, gw_k)), gx_k

        init = (_jnp.zeros((), _jnp.float32), _jax.tree.map(_jnp.zeros_like, weights))
        (loss, grad_w), grad_x = _jax.lax.scan(body, init, (per_example, given["loss_target"]))
    with _jax.named_scope("update"):
        delta_w, new_m, new_v = {}, {}, {}
        for n in TWIN_WEIGHTS:
            delta_w[n], new_m[n], new_v[n] = _adamw(weights[n], grad_w[n], given["m_" + n], given["v_" + n])
    return (loss, grad_x, *[grad_w[n] for n in TWIN_WEIGHTS], *[delta_w[n] for n in TWIN_WEIGHTS],
            *[new_m[n] for n in TWIN_WEIGHTS], *[new_v[n] for n in TWIN_WEIGHTS])
```

```python
import functools

import jax
import jax.numpy as jnp
from jax import lax
from jax.experimental import pallas as pl
from jax.experimental.pallas import tpu as pltpu

F32, BF16 = jnp.float32, jnp.bfloat16
MESH_AXES = ("x", "y", "c")
N_DEV = 8
NORM_EPS = 1e-6
ADAM_LR, ADAM_B1, ADAM_B2, ADAM_EPS, ADAM_WD, ADAM_STEP = 0.001, 0.9, 0.999, 1e-08, 0.01, 10
POOL_WINDOWS = (2, 4, 8, 16)
CONV_TAPS = 31
FFN_TAPS = 3
N_MOD = 6

SUBLANES = 8
LANES = 128
VMEM_LIMIT_BYTES = 56 * 1024 * 1024
POOL_HALO = 16
CONV_HALO = 32
FFN_HALO = 8
TM_POOL, TM_FFN, TM_FFN_BWD, TM_CONV, TM_LOSS = 512, 512, 256, 256, 1024
CONV_ROWS = 32
ADAM_BLOCK_BYTES = 512 * 1024

PV_GPRE, PV_SC1, PV_SH, PV_GPOST, PV_GT = 0, 1, 2, 3, 4
PG_GPRE, PG_SC, PG_SH, PG_GPOST, PG_GT, PG_EXTRA = 0, 1, 2, 3, 4, 5


def _params(sem):
    return pltpu.CompilerParams(dimension_semantics=sem, vmem_limit_bytes=VMEM_LIMIT_BYTES)


def _dot(a, b):
    return jnp.dot(a, b, preferred_element_type=F32)


def _dot_nt(a, b):
    return lax.dot_general(a, b, (((1,), (1,)), ((), ())), preferred_element_type=F32)


def _dot_tn(a, b):
    return lax.dot_general(a, b, (((0,), (0,)), ((), ())), preferred_element_type=F32)


def _sigmoid(x):
    return 1.0 / (1.0 + jnp.exp(-x))


def _rms(x):
    return lax.rsqrt(jnp.mean(x * x, axis=-1, keepdims=True) + NORM_EPS)


def _colsum(v):
    return jnp.sum(v, axis=0, keepdims=True)


def _prenorm(x, pv):
    r = _rms(x)
    xn = x * r
    return xn * (pv[PV_GPRE:PV_GPRE + 1] * pv[PV_SC1:PV_SC1 + 1]) + pv[PV_SH:PV_SH + 1], xn, r


def _post(x, y, pv):
    return x + (pv[PV_GT:PV_GT + 1] * pv[PV_GPOST:PV_GPOST + 1]) * (y * _rms(y))


def _post_bwd(dxo, y, pv, pg_ref):
    ry = _rms(y)
    yn = y * ry
    gt, gpost = pv[PV_GT:PV_GT + 1], pv[PV_GPOST:PV_GPOST + 1]
    dyn = dxo * (gt * gpost)
    dy = ry * (dyn - yn * jnp.mean(dyn * yn, axis=-1, keepdims=True))
    s = _colsum(dxo * yn)
    pg_ref[PG_GPOST:PG_GPOST + 1, :] += s * gt
    pg_ref[PG_GT:PG_GT + 1, :] += s * gpost
    return dy


def _pre_bwd(dh, x, pv, pg_ref):
    r = _rms(x)
    xn = x * r
    gpre, sc1 = pv[PV_GPRE:PV_GPRE + 1], pv[PV_SC1:PV_SC1 + 1]
    dxn = dh * (gpre * sc1)
    dx = r * (dxn - xn * jnp.mean(dxn * xn, axis=-1, keepdims=True))
    p = _colsum(dh * xn)
    pg_ref[PG_GPRE:PG_GPRE + 1, :] += p * sc1
    pg_ref[PG_SC:PG_SC + 1, :] += p * gpre
    pg_ref[PG_SH:PG_SH + 1, :] += _colsum(dh)
    return dx


def _shift_down(a, k, prev):
    out = pltpu.roll(a, k, 0)
    row = lax.broadcasted_iota(jnp.int32, a.shape, 0)
    for q in range(k):
        out = jnp.where(row == q, prev[SUBLANES - k + q:SUBLANES - k + q + 1, :], out)
    return out


def _shift_up(a, k, nxt):
    rows = a.shape[0]
    out = pltpu.roll(a, rows - k, 0)
    row = lax.broadcasted_iota(jnp.int32, a.shape, 0)
    for q in range(k):
        out = jnp.where(row == rows - k + q, nxt[q:q + 1, :], out)
    return out


def _exchange(name, srcs, gathers):
    n = len(srcs)
    outs = [jax.ShapeDtypeStruct(((N_DEV,) + s.shape) if g else s.shape, s.dtype) for s, g in zip(srcs, gathers)]

    def body(*refs):
        src_refs, out_refs = refs[:n], refs[n:2 * n]
        send_sems, recv_sems, local_sems = refs[2 * n:]
        x, y, c = lax.axis_index("x"), lax.axis_index("y"), lax.axis_index("c")
        me = 4 * x + 2 * y + c
        copies, locals_ = [], []
        for a in range(n):
            mine = src_refs[a] if gathers[a] else src_refs[a].at[me]
            lc = pltpu.make_async_copy(mine, out_refs[a].at[me], local_sems.at[a])
            lc.start()
            locals_.append(lc)
        for d in range(1, N_DEV):
            px, py, pc = (x + (d >> 2)) % 2, (y + ((d >> 1) & 1)) % 2, (c + (d & 1)) % 2
            peer = 4 * px + 2 * py + pc
            for a in range(n):
                src = src_refs[a] if gathers[a] else src_refs[a].at[peer]
                cp = pltpu.make_async_remote_copy(
                    src_ref=src, dst_ref=out_refs[a].at[me], send_sem=send_sems.at[a, d - 1],
                    recv_sem=recv_sems.at[a, d - 1], device_id=(px, py, pc), device_id_type=pl.DeviceIdType.MESH)
                cp.start()
                copies.append(cp)
        for cp in copies:
            cp.wait()
        for lc in locals_:
            lc.wait()

    any_spec = pl.BlockSpec(memory_space=pl.ANY)
    return pl.pallas_call(
        body, name=name, out_shape=outs, in_specs=[any_spec] * n, out_specs=[any_spec] * n,
        scratch_shapes=[pltpu.SemaphoreType.DMA((n, N_DEV - 1)), pltpu.SemaphoreType.DMA((n, N_DEV - 1)),
                        pltpu.SemaphoreType.DMA((n,))],
    )(*srcs)


def _ada_fwd(c_all, ada_w):
    layers, d, cols = ada_w.shape

    def body(c_ref, w_ref, o_ref):
        c = c_ref[...]
        ca = (c * _sigmoid(c)).astype(BF16)
        for l in range(layers):
            o_ref[l] = _dot(ca, w_ref[l].astype(BF16))

    return pl.pallas_call(
        body, name="ada_fwd", out_shape=jax.ShapeDtypeStruct((layers, N_DEV, cols), F32),
        compiler_params=pltpu.CompilerParams(vmem_limit_bytes=VMEM_LIMIT_BYTES),
    )(c_all, ada_w)


def _ada_bwd(c_all, dmod_cols):
    layers, _, cols = dmod_cols.shape
    d = c_all.shape[1]

    def body(c_ref, g_ref, o_ref):
        c = c_ref[...]
        ca = (c * _sigmoid(c)).astype(BF16)
        for l in range(layers):
            o_ref[l] = _dot_tn(ca, g_ref[l].astype(BF16))

    return pl.pallas_call(
        body, name="ada_bwd", out_shape=jax.ShapeDtypeStruct((layers, d, cols), F32),
        compiler_params=pltpu.CompilerParams(vmem_limit_bytes=VMEM_LIMIT_BYTES),
    )(c_all, dmod_cols)


def _pooled(hbuf, h, t0, g, tm):
    gd = h.shape[1] // len(POOL_WINDOWS)
    cols = slice(g * gd, (g + 1) * gd)
    w = POOL_WINDOWS[g]
    hg = h[:, cols]
    s = hg
    for k in range(1, w):
        s = s + hbuf[POOL_HALO - k:POOL_HALO - k + tm, cols]
    t = t0 + lax.broadcasted_iota(jnp.int32, (tm, 1), 0)
    cnt = jnp.minimum(t + 1, w).astype(F32)
    return s / cnt - hg, cnt


def _pool_fwd(x, pv, wp, scale, *, tm):
    t_len, d = x.shape
    n_i = t_len // tm
    gd = d // len(POOL_WINDOWS)

    def body(x_ref, pv_ref, wp_ref, sc_ref, xo_ref, y_ref, hbuf):
        i = pl.program_id(0)

        @pl.when(i == 0)
        def _():
            hbuf[0:POOL_HALO, :] = jnp.zeros((POOL_HALO, d), F32)

        xv, pv_ = x_ref[...], pv_ref[...]
        h, _, _ = _prenorm(xv, pv_)
        hbuf[POOL_HALO:POOL_HALO + tm, :] = h
        for g in range(len(POOL_WINDOWS)):
            pooled, _ = _pooled(hbuf, h, i * tm, g, tm)
            y_ref[:, g * gd:(g + 1) * gd] = _dot(pooled.astype(BF16), wp_ref[g])
        xo_ref[...] = _post(xv, y_ref[...] * sc_ref[...], pv_)
        hbuf[0:POOL_HALO, :] = hbuf[tm:tm + POOL_HALO, :]

    row = pl.BlockSpec((tm, d), lambda i: (i, 0))
    return pl.pallas_call(
        body, name="pool_fwd", grid=(n_i,),
        in_specs=[row, pl.BlockSpec((SUBLANES, d), lambda i: (0, 0)), pl.BlockSpec(wp.shape, lambda i: (0, 0, 0)),
                  pl.BlockSpec((1, d), lambda i: (0, 0))],
        out_specs=[row, row],
        out_shape=[jax.ShapeDtypeStruct((t_len, d), F32), jax.ShapeDtypeStruct((t_len, d), F32)],
        scratch_shapes=[pltpu.VMEM((tm + POOL_HALO, d), F32)],
        compiler_params=_params(("arbitrary",)),
    )(x, pv, wp, scale)


def _pool_bwd(dxo, x, ypre, pv, wp, scale, *, tm):
    t_len, d = x.shape
    n_i = t_len // tm
    gd = d // len(POOL_WINDOWS)
    hb = tm // POOL_HALO

    def body(dxo_ref, x_ref, xh_ref, y_ref, pv_ref, wp_ref, sc_ref, dx_ref, pg_ref, dwp_ref, hbuf, qbuf):
        i = pl.program_id(0)
        ti = n_i - 1 - i

        @pl.when(i == 0)
        def _():
            pg_ref[...] = jnp.zeros_like(pg_ref)
            dwp_ref[...] = jnp.zeros_like(dwp_ref)
            qbuf[tm:tm + POOL_HALO, :] = jnp.zeros((POOL_HALO, d), F32)

        xv, pv_, dxo_v, yp, sc = x_ref[...], pv_ref[...], dxo_ref[...], y_ref[...], sc_ref[...]
        dy = _post_bwd(dxo_v, yp * sc, pv_, pg_ref)
        pg_ref[PG_EXTRA:PG_EXTRA + 1, :] += _colsum(dy * yp)
        dys = dy * sc
        h, _, _ = _prenorm(xv, pv_)
        hh, _, _ = _prenorm(xh_ref[...], pv_)
        hbuf[0:POOL_HALO, :] = jnp.where(ti > 0, hh, 0.0)
        hbuf[POOL_HALO:POOL_HALO + tm, :] = h
        for g in range(len(POOL_WINDOWS)):
            cols = slice(g * gd, (g + 1) * gd)
            pooled, cnt = _pooled(hbuf, h, ti * tm, g, tm)
            dyg = dys[:, cols].astype(BF16)
            dwp_ref[g] += _dot_tn(pooled.astype(BF16), dyg)
            dp = _dot_nt(dyg, wp_ref[g])
            qbuf[0:tm, cols] = dp / cnt
            dh = -dp
            for k in range(POOL_WINDOWS[g]):
                dh = dh + qbuf[k:k + tm, cols]
            hbuf[POOL_HALO:POOL_HALO + tm, cols] = dh
        dx_ref[...] = dxo_v + _pre_bwd(hbuf[POOL_HALO:POOL_HALO + tm, :], xv, pv_, pg_ref)
        qbuf[tm:tm + POOL_HALO, :] = qbuf[0:POOL_HALO, :]

    row = pl.BlockSpec((tm, d), lambda i: (n_i - 1 - i, 0))
    halo = pl.BlockSpec((POOL_HALO, d), lambda i: (jnp.maximum((n_i - 1 - i) * hb - 1, 0), 0))
    small = pl.BlockSpec((SUBLANES, d), lambda i: (0, 0))
    return pl.pallas_call(
        body, name="pool_bwd", grid=(n_i,),
        in_specs=[row, row, halo, row, small, pl.BlockSpec(wp.shape, lambda i: (0, 0, 0)),
                  pl.BlockSpec((1, d), lambda i: (0, 0))],
        out_specs=[row, small, pl.BlockSpec(wp.shape, lambda i: (0, 0, 0))],
        out_shape=[jax.ShapeDtypeStruct((t_len, d), F32), jax.ShapeDtypeStruct((SUBLANES, d), F32),
                   jax.ShapeDtypeStruct(wp.shape, F32)],
        scratch_shapes=[pltpu.VMEM((tm + POOL_HALO, d), F32), pltpu.VMEM((tm + POOL_HALO, d), F32)],
        compiler_params=_params(("arbitrary",)),
    )(dxo, x, x, ypre, pv, wp, scale)


def _ffn_conv(a, prev, w):
    return w[2:3] * a + w[1:2] * _shift_down(a, 1, prev) + w[0:1] * _shift_down(a, 2, prev)


def _ffn_fwd(x, pv, wup, wdw, wdn, *, tm):
    t_len, d = x.shape
    _, n_j, _, fc = wup.shape
    n_i = t_len // tm

    def body(x_ref, pv_ref, wup_ref, wdw_ref, wdn_ref, xo_ref, y_ref, h_ref, a_ref, h_s, yacc, carry):
        i, j = pl.program_id(0), pl.program_id(1)

        @pl.when(j == 0)
        def _():
            h, _, _ = _prenorm(x_ref[...], pv_ref[...])
            hb = h.astype(BF16)
            h_s[...] = hb
            h_ref[...] = hb
            yacc[...] = jnp.zeros_like(yacc)

        @pl.when((i == 0) & (j == 0))
        def _():
            carry[...] = jnp.zeros_like(carry)

        hb = h_s[...]
        conv = []
        for s in range(2):
            ab = _dot(hb, wup_ref[s, 0]).astype(BF16)
            a_ref[s, 0] = ab
            a = ab.astype(F32)
            conv.append(_ffn_conv(a, carry[s, j], wdw_ref[s, 0]))
            carry[s, j] = a[tm - FFN_HALO:tm, :]
        g, v = conv
        u = g * _sigmoid(g) * v
        yacc[...] += _dot(u.astype(BF16), wdn_ref[...])

        @pl.when(j == n_j - 1)
        def _():
            y = yacc[...]
            y_ref[...] = y
            xo_ref[...] = _post(x_ref[...], y, pv_ref[...])

    row = pl.BlockSpec((tm, d), lambda i, j: (i, 0))
    return pl.pallas_call(
        body, name="ffn_fwd", grid=(n_i, n_j),
        in_specs=[row, pl.BlockSpec((SUBLANES, d), lambda i, j: (0, 0)),
                  pl.BlockSpec((2, 1, d, fc), lambda i, j: (0, j, 0, 0)),
                  pl.BlockSpec((2, 1, SUBLANES, fc), lambda i, j: (0, j, 0, 0)),
                  pl.BlockSpec((fc, d), lambda i, j: (j, 0))],
        out_specs=[row, row, row, pl.BlockSpec((2, 1, tm, fc), lambda i, j: (0, j, i, 0))],
        out_shape=[jax.ShapeDtypeStruct((t_len, d), F32), jax.ShapeDtypeStruct((t_len, d), F32),
                   jax.ShapeDtypeStruct((t_len, d), BF16), jax.ShapeDtypeStruct((2, n_j, t_len, fc), BF16)],
        scratch_shapes=[pltpu.VMEM((tm, d), BF16), pltpu.VMEM((tm, d), F32), pltpu.VMEM((2, n_j, FFN_HALO, fc), F32)],
        compiler_params=_params(("arbitrary", "arbitrary")),
    )(x, pv, wup, wdw, wdn)


def _ffn_bwd_a(dxo, x, ypre, a_sav, pv, wup, wdw, wdn, *, tm):
    t_len, d = x.shape
    _, n_j, _, fc = wup.shape
    n_i = t_len // tm
    hb = tm // FFN_HALO

    def body(dxo_ref, x_ref, y_ref, a_ref, ah_ref, pv_ref, wup_ref, wdw_ref, wdn_ref,
             dx_ref, da_ref, u_ref, dy_ref, pg_ref, dwdw_ref, dy_s, dh_acc, carry):
        i, j = pl.program_id(0), pl.program_id(1)
        ti = n_i - 1 - i

        @pl.when((i == 0) & (j == 0))
        def _():
            pg_ref[...] = jnp.zeros_like(pg_ref)
            dwdw_ref[...] = jnp.zeros_like(dwdw_ref)
            carry[...] = jnp.zeros_like(carry)

        @pl.when(j == 0)
        def _():
            dyb = _post_bwd(dxo_ref[...], y_ref[...], pv_ref[...], pg_ref).astype(BF16)
            dy_s[...] = dyb
            dy_ref[...] = dyb
            dh_acc[...] = jnp.zeros_like(dh_acc)

        a, shifted, conv = [], [], []
        for s in range(2):
            a_s = a_ref[s, 0].astype(F32)
            prev = jnp.where(ti > 0, ah_ref[s, 0].astype(F32), 0.0)
            w = wdw_ref[s, 0]
            m1, m2 = _shift_down(a_s, 1, prev), _shift_down(a_s, 2, prev)
            a.append(a_s)
            shifted.append((m2, m1))
            conv.append(w[2:3] * a_s + w[1:2] * m1 + w[0:1] * m2)
        g, v = conv
        sg = _sigmoid(g)
        sl = g * sg
        ub = (sl * v).astype(BF16)
        u_ref[0] = ub
        dyb = dy_s[...]
        du = _dot_nt(dyb, wdn_ref[...])
        d2 = (du * v * (sg * (1.0 + g * (1.0 - sg))), du * sl)
        dh = dh_acc[...]
        for s in range(2):
            w = wdw_ref[s, 0]
            taps = (shifted[s][0], shifted[s][1], a[s])
            for k in range(FFN_TAPS):
                dwdw_ref[s, j, k:k + 1, :] += _colsum(d2[s] * taps[k])
            nxt = carry[s, j]
            da = w[2:3] * d2[s] + w[1:2] * _shift_up(d2[s], 1, nxt) + w[0:1] * _shift_up(d2[s], 2, nxt)
            carry[s, j] = d2[s][0:FFN_HALO, :]
            dab = da.astype(BF16)
            da_ref[s, 0] = dab
            dh = dh + _dot_nt(dab, wup_ref[s, 0])
        dh_acc[...] = dh

        @pl.when(j == n_j - 1)
        def _():
            dx_ref[...] = dxo_ref[...] + _pre_bwd(dh_acc[...], x_ref[...], pv_ref[...], pg_ref)

    row = pl.BlockSpec((tm, d), lambda i, j: (n_i - 1 - i, 0))
    small = pl.BlockSpec((SUBLANES, d), lambda i, j: (0, 0))
    return pl.pallas_call(
        body, name="ffn_bwd_a", grid=(n_i, n_j),
        in_specs=[row, row, row,
                  pl.BlockSpec((2, 1, tm, fc), lambda i, j: (0, j, n_i - 1 - i, 0)),
                  pl.BlockSpec((2, 1, FFN_HALO, fc), lambda i, j: (0, j, jnp.maximum((n_i - 1 - i) * hb - 1, 0), 0)),
                  small,
                  pl.BlockSpec((2, 1, d, fc), lambda i, j: (0, j, 0, 0)),
                  pl.BlockSpec((2, 1, SUBLANES, fc), lambda i, j: (0, j, 0, 0)),
                  pl.BlockSpec((fc, d), lambda i, j: (j, 0))],
        out_specs=[row,
                   pl.BlockSpec((2, 1, tm, fc), lambda i, j: (0, j, n_i - 1 - i, 0)),
                   pl.BlockSpec((1, tm, fc), lambda i, j: (j, n_i - 1 - i, 0)),
                   row, small,
                   pl.BlockSpec((2, n_j, SUBLANES, fc), lambda i, j: (0, 0, 0, 0))],
        out_shape=[jax.ShapeDtypeStruct((t_len, d), F32), jax.ShapeDtypeStruct((2, n_j, t_len, fc), BF16),
                   jax.ShapeDtypeStruct((n_j, t_len, fc), BF16), jax.ShapeDtypeStruct((t_len, d), BF16),
                   jax.ShapeDtypeStruct((SUBLANES, d), F32), jax.ShapeDtypeStruct((2, n_j, SUBLANES, fc), F32)],
        scratch_shapes=[pltpu.VMEM((tm, d), BF16), pltpu.VMEM((tm, d), F32), pltpu.VMEM((2, n_j, FFN_HALO, fc), F32)],
        compiler_params=_params(("arbitrary", "arbitrary")),
    )(dxo, x, ypre, a_sav, a_sav, pv, wup, wdw, wdn)


def _ffn_bwd_w(h, da, u, dy, *, tm):
    t_len, d = h.shape
    _, n_j, _, fc = da.shape
    n_i = t_len // tm

    def body(h_ref, da_ref, u_ref, dy_ref, gup_ref, gdn_ref, acc_up, acc_dn):
        i = pl.program_id(1)

        @pl.when(i == 0)
        def _():
            acc_up[...] = jnp.zeros_like(acc_up)
            acc_dn[...] = jnp.zeros_like(acc_dn)

        hb = h_ref[...]
        for s in range(2):
            acc_up[s] += _dot_tn(hb, da_ref[s, 0])
        acc_dn[...] += _dot_tn(u_ref[0], dy_ref[...])

        @pl.when(i == n_i - 1)
        def _():
            gup_ref[:, 0] = acc_up[...].astype(BF16)
            gdn_ref[...] = acc_dn[...].astype(BF16)

    row = pl.BlockSpec((tm, d), lambda j, i: (i, 0))
    return pl.pallas_call(
        body, name="ffn_bwd_w", grid=(n_j, n_i),
        in_specs=[row, pl.BlockSpec((2, 1, tm, fc), lambda j, i: (0, j, i, 0)),
                  pl.BlockSpec((1, tm, fc), lambda j, i: (j, i, 0)), row],
        out_specs=[pl.BlockSpec((2, 1, d, fc), lambda j, i: (0, j, 0, 0)), pl.BlockSpec((fc, d), lambda j, i: (j, 0))],
        out_shape=[jax.ShapeDtypeStruct((2, n_j, d, fc), BF16), jax.ShapeDtypeStruct((n_j * fc, d), BF16)],
        scratch_shapes=[pltpu.VMEM((2, d, fc), F32), pltpu.VMEM((fc, d), F32)],
        compiler_params=_params(("arbitrary", "arbitrary")),
    )(h, da, u, dy)


CV_B1, CV_BDW, CV_LNG, CV_LNB, CV_B2 = 0, 1, 2, 3, 4


def _depthwise(buf, w_ref, out_ref, offs, tm, d):
    def chunk(r, carry):
        r0 = pl.multiple_of(r * CONV_ROWS, CONV_ROWS)
        for cb in range(d // LANES):
            cols = slice(cb * LANES, (cb + 1) * LANES)
            win = buf[pl.ds(r0, CONV_ROWS + CONV_HALO), cols]
            acc = jnp.zeros((CONV_ROWS, LANES), F32)
            for k, off in enumerate(offs):
                acc = acc + win[off:off + CONV_ROWS, :] * w_ref[k:k + 1, cols]
            out_ref[pl.ds(r0, CONV_ROWS), cols] = acc
        return carry

    lax.fori_loop(0, tm // CONV_ROWS, chunk, 0)


def _depthwise_wgrad(dbuf, ubuf, dw_ref, tm, d):
    for cb in range(d // LANES):
        cols = slice(cb * LANES, (cb + 1) * LANES)

        def chunk(r, acc):
            r0 = pl.multiple_of(r * SUBLANES, SUBLANES)
            dv = dbuf[pl.ds(r0, SUBLANES), cols]
            win = ubuf[pl.ds(r0, SUBLANES + CONV_HALO), cols]
            return tuple(acc[k] + dv * win[2 + k:2 + k + SUBLANES, :] for k in range(CONV_TAPS))

        acc = lax.fori_loop(0, tm // SUBLANES, chunk, tuple(jnp.zeros((SUBLANES, LANES), F32) for _ in range(CONV_TAPS)))
        for k in range(CONV_TAPS):
            dw_ref[k:k + 1, cols] += _colsum(acc[k])


def _layer_norm_parts(c1):
    mu = jnp.mean(c1, axis=-1, keepdims=True)
    cen = c1 - mu
    rstd = lax.rsqrt(jnp.mean(cen * cen, axis=-1, keepdims=True) + NORM_EPS)
    return cen * rstd, rstd


def _conv_fwd(x, pv, w1, w2, wdw, vec, *, tm):
    t_len, d = x.shape
    n_i = t_len // tm
    n_q = w1.shape[0] // 2
    qc = w1.shape[2]

    def body(x_ref, pv_ref, w1_ref, w2_ref, wdw_ref, vec_ref, xo_ref, y_ref, a_ref, c1_ref, ubuf):
        i = pl.program_id(0)

        @pl.when(i == 0)
        def _():
            ubuf[0:CONV_HALO, :] = jnp.zeros((CONV_HALO, d), F32)

        xv, pv_ = x_ref[...], pv_ref[...]
        h, _, _ = _prenorm(xv, pv_)
        hb = h.astype(BF16)
        for q in range(n_q):
            cols = slice(q * qc, (q + 1) * qc)
            gcols = slice(d + q * qc, d + (q + 1) * qc)
            val = (_dot(hb, w1_ref[q]) + vec_ref[CV_B1:CV_B1 + 1, cols]).astype(BF16)
            gate = (_dot(hb, w1_ref[n_q + q]) + vec_ref[CV_B1:CV_B1 + 1, gcols]).astype(BF16)
            a_ref[:, cols] = val
            a_ref[:, gcols] = gate
            ubuf[CONV_HALO:CONV_HALO + tm, cols] = val.astype(F32) * _sigmoid(gate.astype(F32))
        _depthwise(ubuf, wdw_ref, c1_ref, tuple(2 + k for k in range(CONV_TAPS)), tm, d)
        c1 = c1_ref[...] + vec_ref[CV_BDW:CV_BDW + 1, 0:d]
        c1_ref[...] = c1
        xhat, _ = _layer_norm_parts(c1)
        ln = xhat * vec_ref[CV_LNG:CV_LNG + 1, 0:d] + vec_ref[CV_LNB:CV_LNB + 1, 0:d]
        s = ln * _sigmoid(ln)
        y = _dot(s.astype(BF16), w2_ref[...]) + vec_ref[CV_B2:CV_B2 + 1, 0:d]
        y_ref[...] = y
        xo_ref[...] = _post(xv, y, pv_)
        ubuf[0:CONV_HALO, :] = ubuf[tm:tm + CONV_HALO, :]

    row = pl.BlockSpec((tm, d), lambda i: (i, 0))
    return pl.pallas_call(
        body, name="conv_fwd", grid=(n_i,),
        in_specs=[row, pl.BlockSpec((SUBLANES, d), lambda i: (0, 0)), pl.BlockSpec(w1.shape, lambda i: (0, 0, 0)),
                  pl.BlockSpec(w2.shape, lambda i: (0, 0)), pl.BlockSpec(wdw.shape, lambda i: (0, 0)),
                  pl.BlockSpec(vec.shape, lambda i: (0, 0))],
        out_specs=[row, row, pl.BlockSpec((tm, 2 * d), lambda i: (i, 0)), row],
        out_shape=[jax.ShapeDtypeStruct((t_len, d), F32), jax.ShapeDtypeStruct((t_len, d), F32),
                   jax.ShapeDtypeStruct((t_len, 2 * d), BF16), jax.ShapeDtypeStruct((t_len, d), F32)],
        scratch_shapes=[pltpu.VMEM((tm + CONV_HALO, d), F32)],
        compiler_params=_params(("arbitrary",)),
    )(x, pv, w1, w2, wdw, vec)


def _conv_bwd(dxo, x, ypre, a_sav, c1_sav, pv, w1, w2, wdw, vec, *, tm):
    t_len, d = x.shape
    n_i = t_len // tm
    n_q = w1.shape[0] // 2
    qc = w1.shape[2]
    hb_ = tm // CONV_HALO

    def body(dxo_ref, x_ref, y_ref, a_ref, ah_ref, c1_ref, pv_ref, w1_ref, w2_ref, wdw_ref, vec_ref,
             dx_ref, pg_ref, gw1_ref, gw2_ref, gvec_ref, gwdw_ref, ubuf, dcbuf, dubuf, acc1, acc2):
        i = pl.program_id(0)
        ti = n_i - 1 - i

        @pl.when(i == 0)
        def _():
            pg_ref[...] = jnp.zeros_like(pg_ref)
            gvec_ref[...] = jnp.zeros_like(gvec_ref)
            gwdw_ref[...] = jnp.zeros_like(gwdw_ref)
            acc1[...] = jnp.zeros_like(acc1)
            acc2[...] = jnp.zeros_like(acc2)
            dcbuf[tm:tm + CONV_HALO, :] = jnp.zeros((CONV_HALO, d), F32)

        xv, pv_, dxo_v = x_ref[...], pv_ref[...], dxo_ref[...]
        dy = _post_bwd(dxo_v, y_ref[...], pv_, pg_ref)
        gvec_ref[CV_B2:CV_B2 + 1, 0:d] += _colsum(dy)
        dyb = dy.astype(BF16)
        xhat, rstd = _layer_norm_parts(c1_ref[...])
        lng = vec_ref[CV_LNG:CV_LNG + 1, 0:d]
        ln = xhat * lng + vec_ref[CV_LNB:CV_LNB + 1, 0:d]
        sg = _sigmoid(ln)
        acc2[...] += _dot_tn((ln * sg).astype(BF16), dyb)
        dln = _dot_nt(dyb, w2_ref[...]) * (sg * (1.0 + ln * (1.0 - sg)))
        gvec_ref[CV_LNG:CV_LNG + 1, 0:d] += _colsum(dln * xhat)
        gvec_ref[CV_LNB:CV_LNB + 1, 0:d] += _colsum(dln)
        dxh = dln * lng
        dc1 = rstd * (dxh - jnp.mean(dxh, axis=-1, keepdims=True)
                      - xhat * jnp.mean(dxh * xhat, axis=-1, keepdims=True))
        gvec_ref[CV_BDW:CV_BDW + 1, 0:d] += _colsum(dc1)
        dcbuf[0:tm, :] = dc1
        for q in range(n_q):
            cols = slice(q * qc, (q + 1) * qc)
            gcols = slice(d + q * qc, d + (q + 1) * qc)
            ubuf[CONV_HALO:CONV_HALO + tm, cols] = a_ref[:, cols].astype(F32) * _sigmoid(a_ref[:, gcols].astype(F32))
            uh = ah_ref[:, cols].astype(F32) * _sigmoid(ah_ref[:, gcols].astype(F32))
            ubuf[0:CONV_HALO, cols] = jnp.where(ti > 0, uh, 0.0)
        _depthwise_wgrad(dcbuf, ubuf, gwdw_ref, tm, d)
        _depthwise(dcbuf, wdw_ref, dubuf, tuple(CONV_TAPS - 1 - k for k in range(CONV_TAPS)), tm, d)
        dcbuf[tm:tm + CONV_HALO, :] = dcbuf[0:CONV_HALO, :]
        h, _, _ = _prenorm(xv, pv_)
        hb = h.astype(BF16)
        dh = jnp.zeros((tm, d), F32)
        for q in range(n_q):
            cols = slice(q * qc, (q + 1) * qc)
            gcols = slice(d + q * qc, d + (q + 1) * qc)
            du = dubuf[:, cols]
            val, gate = a_ref[:, cols].astype(F32), a_ref[:, gcols].astype(F32)
            sgg = _sigmoid(gate)
            dval = du * sgg
            dgate = du * val * (sgg * (1.0 - sgg))
            gvec_ref[CV_B1:CV_B1 + 1, cols] += _colsum(dval)
            gvec_ref[CV_B1:CV_B1 + 1, gcols] += _colsum(dgate)
            dvb, dgb = dval.astype(BF16), dgate.astype(BF16)
            acc1[q] += _dot_tn(hb, dvb)
            acc1[n_q + q] += _dot_tn(hb, dgb)
            dh = dh + _dot_nt(dvb, w1_ref[q]) + _dot_nt(dgb, w1_ref[n_q + q])
        dx_ref[...] = dxo_v + _pre_bwd(dh, xv, pv_, pg_ref)

        @pl.when(i == n_i - 1)
        def _():
            gw1_ref[...] = acc1[...].astype(BF16)
            gw2_ref[...] = acc2[...].astype(BF16)

    row = pl.BlockSpec((tm, d), lambda i: (n_i - 1 - i, 0))
    small = pl.BlockSpec((SUBLANES, d), lambda i: (0, 0))
    whole2 = lambda shape: pl.BlockSpec(shape, lambda i: (0, 0))
    return pl.pallas_call(
        body, name="conv_bwd", grid=(n_i,),
        in_specs=[row, row, row,
                  pl.BlockSpec((tm, 2 * d), lambda i: (n_i - 1 - i, 0)),
                  pl.BlockSpec((CONV_HALO, 2 * d), lambda i: (jnp.maximum((n_i - 1 - i) * hb_ - 1, 0), 0)),
                  row, small, pl.BlockSpec(w1.shape, lambda i: (0, 0, 0)), whole2(w2.shape), whole2(wdw.shape),
                  whole2(vec.shape)],
        out_specs=[row, small, pl.BlockSpec(w1.shape, lambda i: (0, 0, 0)), whole2(w2.shape), whole2(vec.shape),
                   whole2(wdw.shape)],
        out_shape=[jax.ShapeDtypeStruct((t_len, d), F32), jax.ShapeDtypeStruct((SUBLANES, d), F32),
                   jax.ShapeDtypeStruct(w1.shape, BF16), jax.ShapeDtypeStruct(w2.shape, BF16),
                   jax.ShapeDtypeStruct(vec.shape, F32), jax.ShapeDtypeStruct(wdw.shape, F32)],
        scratch_shapes=[pltpu.VMEM((tm + CONV_HALO, d), F32), pltpu.VMEM((tm + CONV_HALO, d), F32),
                        pltpu.VMEM((tm, d), F32), pltpu.VMEM(w1.shape, F32), pltpu.VMEM(w2.shape, F32)],
        compiler_params=_params(("arbitrary",)),
    )(dxo, x, ypre, a_sav, a_sav, c1_sav, pv, w1, w2, wdw, vec)


def _loss_head(y, target, *, tm):
    t_len, d = y.shape
    n_i = t_len // tm

    def body(y_ref, t_ref, dy_ref, sq_ref):
        @pl.when(pl.program_id(0) == 0)
        def _():
            sq_ref[...] = jnp.zeros_like(sq_ref)

        err = y_ref[...] - t_ref[...]
        dy_ref[...] = err * (1.0 / d)
        sq_ref[...] += jnp.sum((err * err).reshape(tm // SUBLANES, SUBLANES, d), axis=0)

    row = pl.BlockSpec((tm, d), lambda i: (i, 0))
    return pl.pallas_call(
        body, name="loss_head", grid=(n_i,), in_specs=[row, row],
        out_specs=[row, pl.BlockSpec((SUBLANES, d), lambda i: (0, 0))],
        out_shape=[jax.ShapeDtypeStruct((t_len, d), F32), jax.ShapeDtypeStruct((SUBLANES, d), F32)],
        compiler_params=_params(("arbitrary",)),
    )(y, target)


def _adam(name, parts, w, m, v):
    n, rows, cols = parts.shape
    tr = rows
    if rows % SUBLANES == 0:
        cap = max(SUBLANES, ADAM_BLOCK_BYTES // (4 * cols))
        tr = max(t for t in range(SUBLANES, rows + 1, SUBLANES) if rows % t == 0 and (t <= cap or t == SUBLANES))
    c1 = 1.0 / (1.0 - ADAM_B1 ** ADAM_STEP)
    c2 = 1.0 / (1.0 - ADAM_B2 ** ADAM_STEP)

    def body(p_ref, w_ref, m_ref, v_ref, g_ref, d_ref, mo_ref, vo_ref):
        g = p_ref[0].astype(F32)
        for k in range(1, n):
            g = g + p_ref[k].astype(F32)
        m2 = ADAM_B1 * m_ref[...] + (1.0 - ADAM_B1) * g
        v2 = ADAM_B2 * v_ref[...] + (1.0 - ADAM_B2) * (g * g)
        g_ref[...] = g
        mo_ref[...] = m2
        vo_ref[...] = v2
        d_ref[...] = -ADAM_LR * ((m2 * c1) / (jnp.sqrt(v2 * c2) + ADAM_EPS) + ADAM_WD * w_ref[...])

    blk = pl.BlockSpec((tr, cols), lambda i: (i, 0))
    out = jax.ShapeDtypeStruct((rows, cols), F32)
    return pl.pallas_call(
        body, name=name, grid=(rows // tr,),
        in_specs=[pl.BlockSpec((n, tr, cols), lambda i: (0, i, 0)), blk, blk, blk],
        out_specs=[blk, blk, blk, blk], out_shape=[out, out, out, out],
        compiler_params=_params(("arbitrary",)),
    )(parts, w, m, v)


def _adam_nd(name, parts, w, m, v):
    shape = w.shape
    cols = shape[-1]
    rows = w.size // cols
    outs = _adam(name, parts.reshape(parts.shape[0], rows, cols), w.reshape(rows, cols), m.reshape(rows, cols),
                 v.reshape(rows, cols))
    return [o.reshape(shape) for o in outs]


def _small_pack(parts, size):
    flat = jnp.concatenate([p.reshape(-1) for p in parts])
    return jnp.pad(flat, (0, size - flat.shape[0]))


def _to_shards(full, axis):
    shp = full.shape
    split = full.reshape(shp[:axis] + (N_DEV, shp[axis] // N_DEV) + shp[axis + 1:])
    return jnp.moveaxis(split, axis, 0)


def _from_shards(sh, axis):
    moved = jnp.moveaxis(sh, 0, axis)
    shp = moved.shape
    return moved.reshape(shp[:axis] + (shp[axis] * shp[axis + 1],) + shp[axis + 2:])


def kernel(x, c, ada_w, ada_b, pre_g, post_g, pool_w, pool_scale, cv_w_pw1, cv_b_pw1, cv_w_dw, cv_b_dw, cv_ln_g, cv_ln_b, cv_w_pw2, cv_b_pw2, ffn_w_up, ffn_w_dw, ffn_w_down, loss_target, m_ada_w, m_ada_b, m_pre_g, m_post_g, m_pool_w, m_pool_scale, m_cv_w_pw1, m_cv_b_pw1, m_cv_w_dw, m_cv_b_dw, m_cv_ln_g, m_cv_ln_b, m_cv_w_pw2, m_cv_b_pw2, m_ffn_w_up, m_ffn_w_dw, m_ffn_w_down, v_ada_w, v_ada_b, v_pre_g, v_post_g, v_pool_w, v_pool_scale, v_cv_w_pw1, v_cv_b_pw1, v_cv_w_dw, v_cv_b_dw, v_cv_ln_g, v_cv_ln_b, v_cv_w_pw2, v_cv_b_pw2, v_ffn_w_up, v_ffn_w_dw, v_ffn_w_down):
    t_len, d = x.shape[1], x.shape[2]
    depth = ada_w.shape[0]
    fc = ffn_w_up.shape[2]
    n_j = N_DEV // 2
    me = 4 * lax.axis_index("x") + 2 * lax.axis_index("y") + lax.axis_index("c")

    small_w = [pre_g, post_g, cv_b_pw1, cv_w_dw, cv_b_dw, cv_ln_g, cv_ln_b, cv_b_pw2, ffn_w_dw]
    small_m = [m_pre_g, m_post_g, m_cv_b_pw1, m_cv_w_dw, m_cv_b_dw, m_cv_ln_g, m_cv_ln_b, m_cv_b_pw2, m_ffn_w_dw]
    small_v = [v_pre_g, v_post_g, v_cv_b_pw1, v_cv_w_dw, v_cv_b_dw, v_cv_ln_g, v_cv_ln_b, v_cv_b_pw2, v_ffn_w_dw]
    sizes = [p.size for p in small_w]
    offs = [sum(sizes[:k]) for k in range(len(sizes) + 1)]
    pack = -(-offs[-1] // (SUBLANES * LANES)) * SUBLANES * LANES

    srcs = [c, _small_pack(small_w, pack), pool_w[0].astype(BF16), cv_w_pw1[0].astype(BF16), cv_w_pw2[0].astype(BF16)]
    srcs += [ffn_w_up[l].astype(BF16) for l in range(depth)] + [ffn_w_down[l].astype(BF16) for l in range(depth)]
    got = _exchange("gather_params", srcs, [True] * len(srcs))
    c_all = got[0].reshape(N_DEV, d)
    smalls = [got[1][:, offs[k]:offs[k + 1]].reshape((N_DEV,) + small_w[k].shape) for k in range(len(small_w))]
    pre_g_f, post_g_f = _from_shards(smalls[0], 2), _from_shards(smalls[1], 2)
    b1_f = _from_shards(smalls[2], 1)[0]
    cvw_f = jnp.pad(_from_shards(smalls[3], 2)[0], ((0, CONV_HALO - CONV_TAPS), (0, 0)))
    bdw_f, lng_f, lnb_f, b2_f = [_from_shards(smalls[k], 1)[0] for k in (4, 5, 6, 7)]
    fdw = jnp.pad(smalls[8], ((0, 0), (0, 0), (0, SUBLANES - FFN_TAPS), (0, 0)))
    wp = jnp.swapaxes(got[2], 0, 1).reshape(pool_w.shape[1], -1, pool_w.shape[3])
    w1 = got[3]
    w2 = got[4].reshape(d, d)
    wup = [got[5 + l].reshape(2, n_j, d, fc) for l in range(depth)]
    wdn = [got[5 + depth + l].reshape(n_j * fc, d) for l in range(depth)]
    wdw = [fdw[:, l].reshape(2, n_j, SUBLANES, fc) for l in range(depth)]
    cvec = jnp.zeros((SUBLANES, 2 * d), F32)
    cvec = cvec.at[CV_B1].set(b1_f)
    for r, vrow in ((CV_BDW, bdw_f), (CV_LNG, lng_f), (CV_LNB, lnb_f), (CV_B2, b2_f)):
        cvec = cvec.at[r, :d].set(vrow)

    mod_cols = _ada_fwd(c_all, ada_w)
    (mod_all,) = _exchange("gather_mod", [mod_cols], [True])
    mod = lax.dynamic_index_in_dim(mod_all, me, axis=2, keepdims=False)
    mod = jnp.swapaxes(mod, 0, 1).reshape(depth, N_MOD, d) + ada_b.reshape(depth, N_MOD, d)

    def pv_of(l, s):
        rows = [pre_g_f[l, s], 1.0 + mod[l, 3 * s + 1], mod[l, 3 * s], post_g_f[l, s], mod[l, 3 * s + 2]]
        return jnp.concatenate([jnp.stack(rows), jnp.zeros((SUBLANES - len(rows), d), F32)])

    x0 = x[0]
    pv00, pv01, pv10, pv11 = pv_of(0, 0), pv_of(0, 1), pv_of(1, 0), pv_of(1, 1)
    x1, y0 = _pool_fwd(x0, pv00, wp, pool_scale, tm=min(TM_POOL, t_len))
    x2, y1, h1, a1 = _ffn_fwd(x1, pv01, wup[0], wdw[0], wdn[0], tm=min(TM_FFN, t_len))
    x3, y2, a2, c2 = _conv_fwd(x2, pv10, w1, w2, cvw_f, cvec, tm=min(TM_CONV, t_len))
    x4, y3, h3, a3 = _ffn_fwd(x3, pv11, wup[1], wdw[1], wdn[1], tm=min(TM_FFN, t_len))
    dx4, sq = _loss_head(x4, loss_target[0], tm=min(TM_LOSS, t_len))
    loss = lax.psum(jnp.sum(sq) * (0.5 / d), MESH_AXES)

    dx3, da3, u3, dy3, pg11, gfdw1 = _ffn_bwd_a(dx4, x3, y3, a3, pv11, wup[1], wdw[1], wdn[1], tm=min(TM_FFN_BWD, t_len))
    gup1, gdn1 = _ffn_bwd_w(h3, da3, u3, dy3, tm=min(TM_FFN, t_len))
    dx2, pg10, gw1, gw2, gcvec, gcvw = _conv_bwd(dx3, x2, y2, a2, c2, pv10, w1, w2, cvw_f, cvec, tm=min(TM_CONV, t_len))
    dx1, da1, u1, dy1, pg01, gfdw0 = _ffn_bwd_a(dx2, x1, y1, a1, pv01, wup[0], wdw[0], wdn[0], tm=min(TM_FFN_BWD, t_len))
    gup0, gdn0 = _ffn_bwd_w(h1, da1, u1, dy1, tm=min(TM_FFN, t_len))
    dx0, pg00, gwp = _pool_bwd(dx1, x0, y0, pv00, wp, pool_scale, tm=min(TM_POOL, t_len))

    pgs = [[pg00, pg01], [pg10, pg11]]
    g_pre = jnp.stack([jnp.stack([pgs[l][s][PG_GPRE] for s in range(2)]) for l in range(depth)])
    g_post = jnp.stack([jnp.stack([pgs[l][s][PG_GPOST] for s in range(2)]) for l in range(depth)])
    dmod = jnp.stack([jnp.concatenate([pgs[l][s][r] for s in range(2) for r in (PG_SH, PG_SC, PG_GT)])
                      for l in range(depth)])
    gfdw = jnp.stack([g.reshape(N_DEV, SUBLANES, fc)[:, :FFN_TAPS] for g in (gfdw0, gfdw1)], axis=1)
    small_g = [_to_shards(g_pre, 2), _to_shards(g_post, 2), _to_shards(gcvec[CV_B1][None], 1),
               _to_shards(gcvw[None, :CONV_TAPS], 2), _to_shards(gcvec[CV_BDW, :d][None], 1),
               _to_shards(gcvec[CV_LNG, :d][None], 1), _to_shards(gcvec[CV_LNB, :d][None], 1),
               _to_shards(gcvec[CV_B2, :d][None], 1), gfdw]
    small_send = jnp.concatenate([g.reshape(N_DEV, -1) for g in small_g], axis=1)
    small_send = jnp.pad(small_send, ((0, 0), (0, pack - small_send.shape[1])))
    gwp_send = jnp.swapaxes(gwp.reshape(gwp.shape[0], N_DEV, -1, gwp.shape[2]), 0, 1).astype(BF16)
    sends = [small_send, gwp_send, gw1, gw2.reshape(N_DEV, -1, d), gup0.reshape(N_DEV, d, fc), gup1.reshape(N_DEV, d, fc),
             gdn0.reshape(N_DEV, -1, d), gdn1.reshape(N_DEV, -1, d), dmod, pg00[PG_EXTRA][None]]
    rec = _exchange("scatter_grads", sends, [False] * 8 + [True, True])

    outs = {}

    def put(name, res, shape=None):
        outs[name] = [r if shape is None else r.reshape(shape) for r in res]

    small_res = _adam("adam_small", rec[0][:, None, :].reshape(N_DEV, -1, SUBLANES * LANES),
                      _small_pack(small_w, pack).reshape(-1, SUBLANES * LANES),
                      _small_pack(small_m, pack).reshape(-1, SUBLANES * LANES),
                      _small_pack(small_v, pack).reshape(-1, SUBLANES * LANES))
    small_names = ["pre_g", "post_g", "cv_b_pw1", "cv_w_dw", "cv_b_dw", "cv_ln_g", "cv_ln_b", "cv_b_pw2", "ffn_w_dw"]
    for k, nm in enumerate(small_names):
        outs[nm] = [r.reshape(-1)[offs[k]:offs[k + 1]].reshape(small_w[k].shape) for r in small_res]
    put("pool_w", _adam_nd("adam_pool_w", rec[1][:, None], pool_w, m_pool_w, v_pool_w))
    put("cv_w_pw1", _adam_nd("adam_cv_w_pw1", rec[2][:, None], cv_w_pw1, m_cv_w_pw1, v_cv_w_pw1))
    put("cv_w_pw2", _adam_nd("adam_cv_w_pw2", rec[3][:, None], cv_w_pw2, m_cv_w_pw2, v_cv_w_pw2))
    put("ffn_w_up", _adam_nd("adam_ffn_w_up", jnp.stack([rec[4], rec[5]], axis=1), ffn_w_up, m_ffn_w_up, v_ffn_w_up))
    put("ffn_w_down", _adam_nd("adam_ffn_w_down", jnp.stack([rec[6], rec[7]], axis=1), ffn_w_down, m_ffn_w_down, v_ffn_w_down))
    put("ada_b", _adam_nd("adam_ada_b", rec[8], ada_b, m_ada_b, v_ada_b))
    put("pool_scale", _adam_nd("adam_pool_scale", rec[9], pool_scale, m_pool_scale, v_pool_scale))
    cols = ada_w.shape[2]
    dmod_cols = jnp.swapaxes(lax.dynamic_slice_in_dim(rec[8], me * cols, cols, axis=2), 0, 1)
    put("ada_w", _adam_nd("adam_ada_w", _ada_bwd(c_all, dmod_cols)[None], ada_w, m_ada_w, v_ada_w))

    order = ["ada_w", "ada_b", "pre_g", "post_g", "pool_w", "pool_scale", "cv_w_pw1", "cv_b_pw1", "cv_w_dw", "cv_b_dw",
             "cv_ln_g", "cv_ln_b", "cv_w_pw2", "cv_b_pw2", "ffn_w_up", "ffn_w_dw", "ffn_w_down"]
    return (loss, dx0[None], *[outs[nm][0] for nm in order], *[outs[nm][1] for nm in order],
            *[outs[nm][2] for nm in order], *[outs[nm][3] for nm in order])
```

```python
import functools

import jax
import jax.numpy as jnp
from jax import lax
from jax.experimental import pallas as pl
from jax.experimental.pallas import tpu as pltpu

F32, BF16 = jnp.float32, jnp.bfloat16
MESH_AXES = ("x", "y", "c")
N_DEV = 8
NORM_EPS = 1e-6
ADAM_LR, ADAM_B1, ADAM_B2, ADAM_EPS, ADAM_WD, ADAM_STEP = 0.001, 0.9, 0.999, 1e-08, 0.01, 10
POOL_WINDOWS = (2, 4, 8, 16)
CONV_TAPS = 31
FFN_TAPS = 3
N_MOD = 6

SUBLANES = 8
LANES = 128
VMEM_LIMIT_BYTES = 56 * 1024 * 1024
POOL_HALO = 16
CONV_HALO = 32
FFN_HALO = 8
TM_POOL, TM_FFN, TM_FFN_BWD, TM_CONV, TM_LOSS = 512, 512, 256, 256, 1024
CONV_ROWS = 32
ADAM_BLOCK_BYTES = 512 * 1024

PV_GPRE, PV_SC1, PV_SH, PV_GPOST, PV_GT = 0, 1, 2, 3, 4
PG_GPRE, PG_SC, PG_SH, PG_GPOST, PG_GT, PG_EXTRA = 0, 1, 2, 3, 4, 5


def _params(sem):
    return pltpu.CompilerParams(dimension_semantics=sem, vmem_limit_bytes=VMEM_LIMIT_BYTES)


def _dot(a, b):
    return jnp.dot(a, b, preferred_element_type=F32)


def _dot_nt(a, b):
    return lax.dot_general(a, b, (((1,), (1,)), ((), ())), preferred_element_type=F32)


def _dot_tn(a, b):
    return lax.dot_general(a, b, (((0,), (0,)), ((), ())), preferred_element_type=F32)


def _sigmoid(x):
    return 1.0 / (1.0 + jnp.exp(-x))


def _rms(x):
    return lax.rsqrt(jnp.mean(x * x, axis=-1, keepdims=True) + NORM_EPS)


def _colsum(v):
    return jnp.sum(v, axis=0, keepdims=True)


def _prenorm(x, pv):
    r = _rms(x)
    xn = x * r
    return xn * (pv[PV_GPRE:PV_GPRE + 1] * pv[PV_SC1:PV_SC1 + 1]) + pv[PV_SH:PV_SH + 1], xn, r


def _post(x, y, pv):
    return x + (pv[PV_GT:PV_GT + 1] * pv[PV_GPOST:PV_GPOST + 1]) * (y * _rms(y))


def _post_bwd(dxo, y, pv, pg_ref):
    ry = _rms(y)
    yn = y * ry
    gt, gpost = pv[PV_GT:PV_GT + 1], pv[PV_GPOST:PV_GPOST + 1]
    dyn = dxo * (gt * gpost)
    dy = ry * (dyn - yn * jnp.mean(dyn * yn, axis=-1, keepdims=True))
    s = _colsum(dxo * yn)
    pg_ref[PG_GPOST:PG_GPOST + 1, :] += s * gt
    pg_ref[PG_GT:PG_GT + 1, :] += s * gpost
    return dy


def _pre_bwd(dh, x, pv, pg_ref):
    r = _rms(x)
    xn = x * r
    gpre, sc1 = pv[PV_GPRE:PV_GPRE + 1], pv[PV_SC1:PV_SC1 + 1]
    dxn = dh * (gpre * sc1)
    dx = r * (dxn - xn * jnp.mean(dxn * xn, axis=-1, keepdims=True))
    p = _colsum(dh * xn)
    pg_ref[PG_GPRE:PG_GPRE + 1, :] += p * sc1
    pg_ref[PG_SC:PG_SC + 1, :] += p * gpre
    pg_ref[PG_SH:PG_SH + 1, :] += _colsum(dh)
    return dx


def _shift_down(a, k, prev):
    out = pltpu.roll(a, k, 0)
    row = lax.broadcasted_iota(jnp.int32, a.shape, 0)
    for q in range(k):
        out = jnp.where(row == q, prev[SUBLANES - k + q:SUBLANES - k + q + 1, :], out)
    return out


def _shift_up(a, k, nxt):
    rows = a.shape[0]
    out = pltpu.roll(a, rows - k, 0)
    row = lax.broadcasted_iota(jnp.int32, a.shape, 0)
    for q in range(k):
        out = jnp.where(row == rows - k + q, nxt[q:q + 1, :], out)
    return out


class _Exchange:
    def __init__(self, srcs, gathers):
        self.srcs, self.gathers, self.n = list(srcs), list(gathers), len(srcs)
        self.out_shape = [jax.ShapeDtypeStruct(((N_DEV,) + s.shape) if g else s.shape, s.dtype)
                          for s, g in zip(srcs, gathers)]
        self.specs = [pl.BlockSpec(memory_space=pl.ANY)] * self.n
        self.scratch = [pltpu.SemaphoreType.DMA((self.n, N_DEV - 1)), pltpu.SemaphoreType.DMA((self.n, N_DEV - 1)),
                        pltpu.SemaphoreType.DMA((self.n,))]

    def _copies(self, src_refs, out_refs, sems):
        send_sems, recv_sems, local_sems = sems
        x, y, c = lax.axis_index("x"), lax.axis_index("y"), lax.axis_index("c")
        me = 4 * x + 2 * y + c
        copies = []
        for a in range(self.n):
            mine = src_refs[a] if self.gathers[a] else src_refs[a].at[me]
            copies.append(pltpu.make_async_copy(mine, out_refs[a].at[me], local_sems.at[a]))
        for d in range(1, N_DEV):
            px, py, pc = (x + (d >> 2)) % 2, (y + ((d >> 1) & 1)) % 2, (c + (d & 1)) % 2
            peer = 4 * px + 2 * py + pc
            for a in range(self.n):
                src = src_refs[a] if self.gathers[a] else src_refs[a].at[peer]
                copies.append(pltpu.make_async_remote_copy(
                    src_ref=src, dst_ref=out_refs[a].at[me], send_sem=send_sems.at[a, d - 1],
                    recv_sem=recv_sems.at[a, d - 1], device_id=(px, py, pc), device_id_type=pl.DeviceIdType.MESH))
        return copies

    def start(self, src_refs, out_refs, sems):
        for cp in self._copies(src_refs, out_refs, sems):
            cp.start()

    def wait(self, src_refs, out_refs, sems):
        for cp in self._copies(src_refs, out_refs, sems):
            cp.wait()


def _exchange(name, srcs, gathers):
    ex = _Exchange(srcs, gathers)

    def body(*refs):
        src_refs, out_refs, sems = refs[:ex.n], refs[ex.n:2 * ex.n], refs[2 * ex.n:]
        ex.start(src_refs, out_refs, sems)
        ex.wait(src_refs, out_refs, sems)

    return pl.pallas_call(body, name=name, out_shape=ex.out_shape, in_specs=ex.specs, out_specs=ex.specs,
                          scratch_shapes=ex.scratch)(*srcs)


def _launch(body, name, grid, in_specs, out_specs, out_shape, scratch_shapes, args, ex=None):
    sem = ("arbitrary",) * len(grid)
    if ex is None:
        outs = pl.pallas_call(body, name=name, grid=grid, in_specs=in_specs, out_specs=out_specs, out_shape=out_shape,
                              scratch_shapes=scratch_shapes, compiler_params=_params(sem))(*args)
        return outs, []
    n_in, n_out, n_scr = len(in_specs), len(out_specs), len(scratch_shapes)

    def hosted(*refs):
        ins, ex_in = refs[:n_in], refs[n_in:n_in + ex.n]
        outs = refs[n_in + ex.n:n_in + ex.n + n_out]
        ex_out = refs[n_in + ex.n + n_out:n_in + 2 * ex.n + n_out]
        rest = refs[n_in + 2 * ex.n + n_out:]
        scratch, sems = rest[:n_scr], rest[n_scr:]
        ids = [pl.program_id(a) for a in range(len(grid))]
        first = functools.reduce(lambda p, q: p & q, [i == 0 for i in ids])
        last = functools.reduce(lambda p, q: p & q, [i == g - 1 for i, g in zip(ids, grid)])

        @pl.when(first)
        def _():
            ex.start(ex_in, ex_out, sems)

        body(*ins, *outs, *scratch)

        @pl.when(last)
        def _():
            ex.wait(ex_in, ex_out, sems)

    outs = pl.pallas_call(
        hosted, name=name, grid=grid, in_specs=list(in_specs) + ex.specs, out_specs=list(out_specs) + ex.specs,
        out_shape=list(out_shape) + ex.out_shape, scratch_shapes=list(scratch_shapes) + ex.scratch,
        compiler_params=_params(sem))(*args, *ex.srcs)
    return outs[:n_out], outs[n_out:]


def _ada_fwd(c_all, ada_w):
    layers, d, cols = ada_w.shape

    def body(c_ref, w_ref, o_ref):
        c = c_ref[...]
        ca = (c * _sigmoid(c)).astype(BF16)
        for l in range(layers):
            o_ref[l] = _dot(ca, w_ref[l].astype(BF16))

    return pl.pallas_call(
        body, name="ada_fwd", out_shape=jax.ShapeDtypeStruct((layers, N_DEV, cols), F32),
        compiler_params=pltpu.CompilerParams(vmem_limit_bytes=VMEM_LIMIT_BYTES),
    )(c_all, ada_w)


def _ada_bwd(c_all, dmod_cols):
    layers, _, cols = dmod_cols.shape
    d = c_all.shape[1]

    def body(c_ref, g_ref, o_ref):
        c = c_ref[...]
        ca = (c * _sigmoid(c)).astype(BF16)
        for l in range(layers):
            o_ref[l] = _dot_tn(ca, g_ref[l].astype(BF16))

    return pl.pallas_call(
        body, name="ada_bwd", out_shape=jax.ShapeDtypeStruct((layers, d, cols), F32),
        compiler_params=pltpu.CompilerParams(vmem_limit_bytes=VMEM_LIMIT_BYTES),
    )(c_all, dmod_cols)


def _pooled(hbuf, h, t0, g, tm):
    gd = h.shape[1] // len(POOL_WINDOWS)
    cols = slice(g * gd, (g + 1) * gd)
    w = POOL_WINDOWS[g]
    hg = h[:, cols]
    s = hg
    for k in range(1, w):
        s = s + hbuf[POOL_HALO - k:POOL_HALO - k + tm, cols]
    t = t0 + lax.broadcasted_iota(jnp.int32, (tm, 1), 0)
    cnt = jnp.minimum(t + 1, w).astype(F32)
    return s / cnt - hg, cnt


def _pool_fwd(x, pv, wp, scale, *, tm, ex=None):
    t_len, d = x.shape
    n_i = t_len // tm
    gd = d // len(POOL_WINDOWS)

    def body(x_ref, pv_ref, wp_ref, sc_ref, xo_ref, y_ref, hbuf):
        i = pl.program_id(0)

        @pl.when(i == 0)
        def _():
            hbuf[0:POOL_HALO, :] = jnp.zeros((POOL_HALO, d), F32)

        xv, pv_ = x_ref[...], pv_ref[...]
        h, _, _ = _prenorm(xv, pv_)
        hbuf[POOL_HALO:POOL_HALO + tm, :] = h
        for g in range(len(POOL_WINDOWS)):
            pooled, _ = _pooled(hbuf, h, i * tm, g, tm)
            y_ref[:, g * gd:(g + 1) * gd] = _dot(pooled.astype(BF16), wp_ref[g])
        xo_ref[...] = _post(xv, y_ref[...] * sc_ref[...], pv_)
        hbuf[0:POOL_HALO, :] = hbuf[tm:tm + POOL_HALO, :]

    row = pl.BlockSpec((tm, d), lambda i: (i, 0))
    return _launch(
        body, "pool_fwd", (n_i,),
        [row, pl.BlockSpec((SUBLANES, d), lambda i: (0, 0)), pl.BlockSpec(wp.shape, lambda i: (0, 0, 0)),
         pl.BlockSpec((1, d), lambda i: (0, 0))],
        [row, row],
        [jax.ShapeDtypeStruct((t_len, d), F32), jax.ShapeDtypeStruct((t_len, d), F32)],
        [pltpu.VMEM((tm + POOL_HALO, d), F32)],
        (x, pv, wp, scale), ex)


def _pool_bwd(dxo, x, ypre, pv, wp, scale, *, tm, ex=None):
    t_len, d = x.shape
    n_i = t_len // tm
    gd = d // len(POOL_WINDOWS)
    hb = tm // POOL_HALO

    def body(dxo_ref, x_ref, xh_ref, y_ref, pv_ref, wp_ref, sc_ref, dx_ref, pg_ref, dwp_ref, hbuf, qbuf):
        i = pl.program_id(0)
        ti = n_i - 1 - i

        @pl.when(i == 0)
        def _():
            pg_ref[...] = jnp.zeros_like(pg_ref)
            dwp_ref[...] = jnp.zeros_like(dwp_ref)
            qbuf[tm:tm + POOL_HALO, :] = jnp.zeros((POOL_HALO, d), F32)

        xv, pv_, dxo_v, yp, sc = x_ref[...], pv_ref[...], dxo_ref[...], y_ref[...], sc_ref[...]
        dy = _post_bwd(dxo_v, yp * sc, pv_, pg_ref)
        pg_ref[PG_EXTRA:PG_EXTRA + 1, :] += _colsum(dy * yp)
        dys = dy * sc
        h, _, _ = _prenorm(xv, pv_)
        hh, _, _ = _prenorm(xh_ref[...], pv_)
        hbuf[0:POOL_HALO, :] = jnp.where(ti > 0, hh, 0.0)
        hbuf[POOL_HALO:POOL_HALO + tm, :] = h
        for g in range(len(POOL_WINDOWS)):
            cols = slice(g * gd, (g + 1) * gd)
            pooled, cnt = _pooled(hbuf, h, ti * tm, g, tm)
            dyg = dys[:, cols].astype(BF16)
            dwp_ref[g] += _dot_tn(pooled.astype(BF16), dyg)
            dp = _dot_nt(dyg, wp_ref[g])
            qbuf[0:tm, cols] = dp / cnt
            dh = -dp
            for k in range(POOL_WINDOWS[g]):
                dh = dh + qbuf[k:k + tm, cols]
            hbuf[POOL_HALO:POOL_HALO + tm, cols] = dh
        dx_ref[...] = dxo_v + _pre_bwd(hbuf[POOL_HALO:POOL_HALO + tm, :], xv, pv_, pg_ref)
        qbuf[tm:tm + POOL_HALO, :] = qbuf[0:POOL_HALO, :]

    row = pl.BlockSpec((tm, d), lambda i: (n_i - 1 - i, 0))
    halo = pl.BlockSpec((POOL_HALO, d), lambda i: (jnp.maximum((n_i - 1 - i) * hb - 1, 0), 0))
    small = pl.BlockSpec((SUBLANES, d), lambda i: (0, 0))
    return _launch(
        body, "pool_bwd", (n_i,),
        [row, row, halo, row, small, pl.BlockSpec(wp.shape, lambda i: (0, 0, 0)), pl.BlockSpec((1, d), lambda i: (0, 0))],
        [row, small, pl.BlockSpec(wp.shape, lambda i: (0, 0, 0))],
        [jax.ShapeDtypeStruct((t_len, d), F32), jax.ShapeDtypeStruct((SUBLANES, d), F32),
         jax.ShapeDtypeStruct(wp.shape, F32)],
        [pltpu.VMEM((tm + POOL_HALO, d), F32), pltpu.VMEM((tm + POOL_HALO, d), F32)],
        (dxo, x, x, ypre, pv, wp, scale), ex)


def _ffn_conv(a, prev, w):
    return w[2:3] * a + w[1:2] * _shift_down(a, 1, prev) + w[0:1] * _shift_down(a, 2, prev)


def _ffn_fwd(x, pv, wup, wdw, wdn, *, tm, ex=None):
    t_len, d = x.shape
    _, n_j, _, fc = wup.shape
    n_i = t_len // tm

    def body(x_ref, pv_ref, wup_ref, wdw_ref, wdn_ref, xo_ref, y_ref, h_ref, a_ref, h_s, yacc, carry):
        i, j = pl.program_id(0), pl.program_id(1)

        @pl.when(j == 0)
        def _():
            h, _, _ = _prenorm(x_ref[...], pv_ref[...])
            hb = h.astype(BF16)
            h_s[...] = hb
            h_ref[...] = hb
            yacc[...] = jnp.zeros_like(yacc)

        @pl.when((i == 0) & (j == 0))
        def _():
            carry[...] = jnp.zeros_like(carry)

        hb = h_s[...]
        conv = []
        for s in range(2):
            ab = _dot(hb, wup_ref[s, 0]).astype(BF16)
            a_ref[s, 0] = ab
            a = ab.astype(F32)
            conv.append(_ffn_conv(a, carry[s, j], wdw_ref[s, 0]))
            carry[s, j] = a[tm - FFN_HALO:tm, :]
        g, v = conv
        u = g * _sigmoid(g) * v
        yacc[...] += _dot(u.astype(BF16), wdn_ref[...])

        @pl.when(j == n_j - 1)
        def _():
            y = yacc[...]
            y_ref[...] = y
            xo_ref[...] = _post(x_ref[...], y, pv_ref[...])

    row = pl.BlockSpec((tm, d), lambda i, j: (i, 0))
    return _launch(
        body, "ffn_fwd", (n_i, n_j),
        [row, pl.BlockSpec((SUBLANES, d), lambda i, j: (0, 0)),
         pl.BlockSpec((2, 1, d, fc), lambda i, j: (0, j, 0, 0)),
         pl.BlockSpec((2, 1, SUBLANES, fc), lambda i, j: (0, j, 0, 0)),
         pl.BlockSpec((fc, d), lambda i, j: (j, 0))],
        [row, row, row, pl.BlockSpec((2, 1, tm, fc), lambda i, j: (0, j, i, 0))],
        [jax.ShapeDtypeStruct((t_len, d), F32), jax.ShapeDtypeStruct((t_len, d), F32),
         jax.ShapeDtypeStruct((t_len, d), BF16), jax.ShapeDtypeStruct((2, n_j, t_len, fc), BF16)],
        [pltpu.VMEM((tm, d), BF16), pltpu.VMEM((tm, d), F32), pltpu.VMEM((2, n_j, FFN_HALO, fc), F32)],
        (x, pv, wup, wdw, wdn), ex)


def _ffn_bwd_a(dxo, x, ypre, a_sav, pv, wup, wdw, wdn, *, tm, ex=None):
    t_len, d = x.shape
    _, n_j, _, fc = wup.shape
    n_i = t_len // tm
    hb = tm // FFN_HALO

    def body(dxo_ref, x_ref, y_ref, a_ref, ah_ref, pv_ref, wup_ref, wdw_ref, wdn_ref,
             dx_ref, da_ref, u_ref, dy_ref, pg_ref, dwdw_ref, dy_s, dh_acc, carry):
        i, j = pl.program_id(0), pl.program_id(1)
        ti = n_i - 1 - i

        @pl.when((i == 0) & (j == 0))
        def _():
            pg_ref[...] = jnp.zeros_like(pg_ref)
            dwdw_ref[...] = jnp.zeros_like(dwdw_ref)
            carry[...] = jnp.zeros_like(carry)

        @pl.when(j == 0)
        def _():
            dyb = _post_bwd(dxo_ref[...], y_ref[...], pv_ref[...], pg_ref).astype(BF16)
            dy_s[...] = dyb
            dy_ref[...] = dyb
            dh_acc[...] = jnp.zeros_like(dh_acc)

        a, shifted, conv = [], [], []
        for s in range(2):
            a_s = a_ref[s, 0].astype(F32)
            prev = jnp.where(ti > 0, ah_ref[s, 0].astype(F32), 0.0)
            w = wdw_ref[s, 0]
            m1, m2 = _shift_down(a_s, 1, prev), _shift_down(a_s, 2, prev)
            a.append(a_s)
            shifted.append((m2, m1))
            conv.append(w[2:3] * a_s + w[1:2] * m1 + w[0:1] * m2)
        g, v = conv
        sg = _sigmoid(g)
        sl = g * sg
        ub = (sl * v).astype(BF16)
        u_ref[0] = ub
        dyb = dy_s[...]
        du = _dot_nt(dyb, wdn_ref[...])
        d2 = (du * v * (sg * (1.0 + g * (1.0 - sg))), du * sl)
        dh = dh_acc[...]
        for s in range(2):
            w = wdw_ref[s, 0]
            taps = (shifted[s][0], shifted[s][1], a[s])
            for k in range(FFN_TAPS):
                dwdw_ref[s, j, k:k + 1, :] += _colsum(d2[s] * taps[k])
            nxt = carry[s, j]
            da = w[2:3] * d2[s] + w[1:2] * _shift_up(d2[s], 1, nxt) + w[0:1] * _shift_up(d2[s], 2, nxt)
            carry[s, j] = d2[s][0:FFN_HALO, :]
            dab = da.astype(BF16)
            da_ref[s, 0] = dab
            dh = dh + _dot_nt(dab, wup_ref[s, 0])
        dh_acc[...] = dh

        @pl.when(j == n_j - 1)
        def _():
            dx_ref[...] = dxo_ref[...] + _pre_bwd(dh_acc[...], x_ref[...], pv_ref[...], pg_ref)

    row = pl.BlockSpec((tm, d), lambda i, j: (n_i - 1 - i, 0))
    small = pl.BlockSpec((SUBLANES, d), lambda i, j: (0, 0))
    return _launch(
        body, "ffn_bwd_a", (n_i, n_j),
        [row, row, row,
         pl.BlockSpec((2, 1, tm, fc), lambda i, j: (0, j, n_i - 1 - i, 0)),
         pl.BlockSpec((2, 1, FFN_HALO, fc), lambda i, j: (0, j, jnp.maximum((n_i - 1 - i) * hb - 1, 0), 0)),
         small,
         pl.BlockSpec((2, 1, d, fc), lambda i, j: (0, j, 0, 0)),
         pl.BlockSpec((2, 1, SUBLANES, fc), lambda i, j: (0, j, 0, 0)),
         pl.BlockSpec((fc, d), lambda i, j: (j, 0))],
        [row,
         pl.BlockSpec((2, 1, tm, fc), lambda i, j: (0, j, n_i - 1 - i, 0)),
         pl.BlockSpec((1, tm, fc), lambda i, j: (j, n_i - 1 - i, 0)),
         row, small,
         pl.BlockSpec((2, n_j, SUBLANES, fc), lambda i, j: (0, 0, 0, 0))],
        [jax.ShapeDtypeStruct((t_len, d), F32), jax.ShapeDtypeStruct((2, n_j, t_len, fc), BF16),
         jax.ShapeDtypeStruct((n_j, t_len, fc), BF16), jax.ShapeDtypeStruct((t_len, d), BF16),
         jax.ShapeDtypeStruct((SUBLANES, d), F32), jax.ShapeDtypeStruct((2, n_j, SUBLANES, fc), F32)],
        [pltpu.VMEM((tm, d), BF16), pltpu.VMEM((tm, d), F32), pltpu.VMEM((2, n_j, FFN_HALO, fc), F32)],
        (dxo, x, ypre, a_sav, a_sav, pv, wup, wdw, wdn), ex)


def _ffn_bwd_wup(h, da, *, tm, ex=None):
    t_len, d = h.shape
    _, n_j, _, fc = da.shape
    n_i = t_len // tm

    def body(h_ref, da_ref, gup_ref, acc):
        i = pl.program_id(1)

        @pl.when(i == 0)
        def _():
            acc[...] = jnp.zeros_like(acc)

        hb = h_ref[...]
        for s in range(2):
            acc[s] += _dot_tn(hb, da_ref[s, 0])

        @pl.when(i == n_i - 1)
        def _():
            gup_ref[:, 0] = acc[...].astype(BF16)

    return _launch(
        body, "ffn_bwd_wup", (n_j, n_i),
        [pl.BlockSpec((tm, d), lambda j, i: (i, 0)), pl.BlockSpec((2, 1, tm, fc), lambda j, i: (0, j, i, 0))],
        [pl.BlockSpec((2, 1, d, fc), lambda j, i: (0, j, 0, 0))],
        [jax.ShapeDtypeStruct((2, n_j, d, fc), BF16)],
        [pltpu.VMEM((2, d, fc), F32)],
        (h, da), ex)


def _ffn_bwd_wdn(u, dy, *, tm, ex=None):
    n_j, t_len, fc = u.shape
    d = dy.shape[1]
    n_i = t_len // tm

    def body(u_ref, dy_ref, gdn_ref, acc):
        i = pl.program_id(1)

        @pl.when(i == 0)
        def _():
            acc[...] = jnp.zeros_like(acc)

        acc[...] += _dot_tn(u_ref[0], dy_ref[...])

        @pl.when(i == n_i - 1)
        def _():
            gdn_ref[...] = acc[...].astype(BF16)

    return _launch(
        body, "ffn_bwd_wdn", (n_j, n_i),
        [pl.BlockSpec((1, tm, fc), lambda j, i: (j, i, 0)), pl.BlockSpec((tm, d), lambda j, i: (i, 0))],
        [pl.BlockSpec((fc, d), lambda j, i: (j, 0))],
        [jax.ShapeDtypeStruct((n_j * fc, d), BF16)],
        [pltpu.VMEM((fc, d), F32)],
        (u, dy), ex)


CV_B1, CV_BDW, CV_LNG, CV_LNB, CV_B2 = 0, 1, 2, 3, 4


def _depthwise(buf, w_ref, out_ref, offs, tm, d):
    def chunk(r, carry):
        r0 = pl.multiple_of(r * CONV_ROWS, CONV_ROWS)
        for cb in range(d // LANES):
            cols = slice(cb * LANES, (cb + 1) * LANES)
            win = buf[pl.ds(r0, CONV_ROWS + CONV_HALO), cols]
            acc = jnp.zeros((CONV_ROWS, LANES), F32)
            for k, off in enumerate(offs):
                acc = acc + win[off:off + CONV_ROWS, :] * w_ref[k:k + 1, cols]
            out_ref[pl.ds(r0, CONV_ROWS), cols] = acc
        return carry

    lax.fori_loop(0, tm // CONV_ROWS, chunk, 0)


def _depthwise_wgrad(dbuf, ubuf, dw_ref, tm, d):
    for cb in range(d // LANES):
        cols = slice(cb * LANES, (cb + 1) * LANES)

        def chunk(r, acc):
            r0 = pl.multiple_of(r * SUBLANES, SUBLANES)
            dv = dbuf[pl.ds(r0, SUBLANES), cols]
            win = ubuf[pl.ds(r0, SUBLANES + CONV_HALO), cols]
            return tuple(acc[k] + dv * win[2 + k:2 + k + SUBLANES, :] for k in range(CONV_TAPS))

        acc = lax.fori_loop(0, tm // SUBLANES, chunk, tuple(jnp.zeros((SUBLANES, LANES), F32) for _ in range(CONV_TAPS)))
        for k in range(CONV_TAPS):
            dw_ref[k:k + 1, cols] += _colsum(acc[k])


def _layer_norm_parts(c1):
    mu = jnp.mean(c1, axis=-1, keepdims=True)
    cen = c1 - mu
    rstd = lax.rsqrt(jnp.mean(cen * cen, axis=-1, keepdims=True) + NORM_EPS)
    return cen * rstd, rstd


def _conv_fwd(x, pv, w1, w2, wdw, vec, *, tm):
    t_len, d = x.shape
    n_i = t_len // tm
    n_q = w1.shape[0] // 2
    qc = w1.shape[2]

    def body(x_ref, pv_ref, w1_ref, w2_ref, wdw_ref, vec_ref, xo_ref, y_ref, a_ref, c1_ref, ubuf):
        i = pl.program_id(0)

        @pl.when(i == 0)
        def _():
            ubuf[0:CONV_HALO, :] = jnp.zeros((CONV_HALO, d), F32)

        xv, pv_ = x_ref[...], pv_ref[...]
        h, _, _ = _prenorm(xv, pv_)
        hb = h.astype(BF16)
        for q in range(n_q):
            cols = slice(q * qc, (q + 1) * qc)
            gcols = slice(d + q * qc, d + (q + 1) * qc)
            val = (_dot(hb, w1_ref[q]) + vec_ref[CV_B1:CV_B1 + 1, cols]).astype(BF16)
            gate = (_dot(hb, w1_ref[n_q + q]) + vec_ref[CV_B1:CV_B1 + 1, gcols]).astype(BF16)
            a_ref[:, cols] = val
            a_ref[:, gcols] = gate
            ubuf[CONV_HALO:CONV_HALO + tm, cols] = val.astype(F32) * _sigmoid(gate.astype(F32))
        _depthwise(ubuf, wdw_ref, c1_ref, tuple(2 + k for k in range(CONV_TAPS)), tm, d)
        c1 = c1_ref[...] + vec_ref[CV_BDW:CV_BDW + 1, 0:d]
        c1_ref[...] = c1
        xhat, _ = _layer_norm_parts(c1)
        ln = xhat * vec_ref[CV_LNG:CV_LNG + 1, 0:d] + vec_ref[CV_LNB:CV_LNB + 1, 0:d]
        s = ln * _sigmoid(ln)
        y = _dot(s.astype(BF16), w2_ref[...]) + vec_ref[CV_B2:CV_B2 + 1, 0:d]
        y_ref[...] = y
        xo_ref[...] = _post(xv, y, pv_)
        ubuf[0:CONV_HALO, :] = ubuf[tm:tm + CONV_HALO, :]

    row = pl.BlockSpec((tm, d), lambda i: (i, 0))
    return pl.pallas_call(
        body, name="conv_fwd", grid=(n_i,),
        in_specs=[row, pl.BlockSpec((SUBLANES, d), lambda i: (0, 0)), pl.BlockSpec(w1.shape, lambda i: (0, 0, 0)),
                  pl.BlockSpec(w2.shape, lambda i: (0, 0)), pl.BlockSpec(wdw.shape, lambda i: (0, 0)),
                  pl.BlockSpec(vec.shape, lambda i: (0, 0))],
        out_specs=[row, row, pl.BlockSpec((tm, 2 * d), lambda i: (i, 0)), row],
        out_shape=[jax.ShapeDtypeStruct((t_len, d), F32), jax.ShapeDtypeStruct((t_len, d), F32),
                   jax.ShapeDtypeStruct((t_len, 2 * d), BF16), jax.ShapeDtypeStruct((t_len, d), F32)],
        scratch_shapes=[pltpu.VMEM((tm + CONV_HALO, d), F32)],
        compiler_params=_params(("arbitrary",)),
    )(x, pv, w1, w2, wdw, vec)


def _conv_bwd(dxo, x, ypre, a_sav, c1_sav, pv, w1, w2, wdw, vec, *, tm, ex=None):
    t_len, d = x.shape
    n_i = t_len // tm
    n_q = w1.shape[0] // 2
    qc = w1.shape[2]
    hb_ = tm // CONV_HALO

    def body(dxo_ref, x_ref, y_ref, a_ref, ah_ref, c1_ref, pv_ref, w1_ref, w2_ref, wdw_ref, vec_ref,
             dx_ref, pg_ref, gw1_ref, gw2_ref, gvec_ref, gwdw_ref, ubuf, dcbuf, dubuf, acc1, acc2):
        i = pl.program_id(0)
        ti = n_i - 1 - i

        @pl.when(i == 0)
        def _():
            pg_ref[...] = jnp.zeros_like(pg_ref)
            gvec_ref[...] = jnp.zeros_like(gvec_ref)
            gwdw_ref[...] = jnp.zeros_like(gwdw_ref)
            acc1[...] = jnp.zeros_like(acc1)
            acc2[...] = jnp.zeros_like(acc2)
            dcbuf[tm:tm + CONV_HALO, :] = jnp.zeros((CONV_HALO, d), F32)

        xv, pv_, dxo_v = x_ref[...], pv_ref[...], dxo_ref[...]
        dy = _post_bwd(dxo_v, y_ref[...], pv_, pg_ref)
        gvec_ref[CV_B2:CV_B2 + 1, 0:d] += _colsum(dy)
        dyb = dy.astype(BF16)
        xhat, rstd = _layer_norm_parts(c1_ref[...])
        lng = vec_ref[CV_LNG:CV_LNG + 1, 0:d]
        ln = xhat * lng + vec_ref[CV_LNB:CV_LNB + 1, 0:d]
        sg = _sigmoid(ln)
        acc2[...] += _dot_tn((ln * sg).astype(BF16), dyb)
        dln = _dot_nt(dyb, w2_ref[...]) * (sg * (1.0 + ln * (1.0 - sg)))
        gvec_ref[CV_LNG:CV_LNG + 1, 0:d] += _colsum(dln * xhat)
        gvec_ref[CV_LNB:CV_LNB + 1, 0:d] += _colsum(dln)
        dxh = dln * lng
        dc1 = rstd * (dxh - jnp.mean(dxh, axis=-1, keepdims=True)
                      - xhat * jnp.mean(dxh * xhat, axis=-1, keepdims=True))
        gvec_ref[CV_BDW:CV_BDW + 1, 0:d] += _colsum(dc1)
        dcbuf[0:tm, :] = dc1
        for q in range(n_q):
            cols = slice(q * qc, (q + 1) * qc)
            gcols = slice(d + q * qc, d + (q + 1) * qc)
            ubuf[CONV_HALO:CONV_HALO + tm, cols] = a_ref[:, cols].astype(F32) * _sigmoid(a_ref[:, gcols].astype(F32))
            uh = ah_ref[:, cols].astype(F32) * _sigmoid(ah_ref[:, gcols].astype(F32))
            ubuf[0:CONV_HALO, cols] = jnp.where(ti > 0, uh, 0.0)
        _depthwise_wgrad(dcbuf, ubuf, gwdw_ref, tm, d)
        _depthwise(dcbuf, wdw_ref, dubuf, tuple(CONV_TAPS - 1 - k for k in range(CONV_TAPS)), tm, d)
        dcbuf[tm:tm + CONV_HALO, :] = dcbuf[0:CONV_HALO, :]
        h, _, _ = _prenorm(xv, pv_)
        hb = h.astype(BF16)
        dh = jnp.zeros((tm, d), F32)
        for q in range(n_q):
            cols = slice(q * qc, (q + 1) * qc)
            gcols = slice(d + q * qc, d + (q + 1) * qc)
            du = dubuf[:, cols]
            val, gate = a_ref[:, cols].astype(F32), a_ref[:, gcols].astype(F32)
            sgg = _sigmoid(gate)
            dval = du * sgg
            dgate = du * val * (sgg * (1.0 - sgg))
            gvec_ref[CV_B1:CV_B1 + 1, cols] += _colsum(dval)
            gvec_ref[CV_B1:CV_B1 + 1, gcols] += _colsum(dgate)
            dvb, dgb = dval.astype(BF16), dgate.astype(BF16)
            acc1[q] += _dot_tn(hb, dvb)
            acc1[n_q + q] += _dot_tn(hb, dgb)
            dh = dh + _dot_nt(dvb, w1_ref[q]) + _dot_nt(dgb, w1_ref[n_q + q])
        dx_ref[...] = dxo_v + _pre_bwd(dh, xv, pv_, pg_ref)

        @pl.when(i == n_i - 1)
        def _():
            gw1_ref[...] = acc1[...].astype(BF16)
            gw2_ref[...] = acc2[...].astype(BF16)

    row = pl.BlockSpec((tm, d), lambda i: (n_i - 1 - i, 0))
    small = pl.BlockSpec((SUBLANES, d), lambda i: (0, 0))
    whole2 = lambda shape: pl.BlockSpec(shape, lambda i: (0, 0))
    return _launch(
        body, "conv_bwd", (n_i,),
        [row, row, row,
         pl.BlockSpec((tm, 2 * d), lambda i: (n_i - 1 - i, 0)),
         pl.BlockSpec((CONV_HALO, 2 * d), lambda i: (jnp.maximum((n_i - 1 - i) * hb_ - 1, 0), 0)),
         row, small, pl.BlockSpec(w1.shape, lambda i: (0, 0, 0)), whole2(w2.shape), whole2(wdw.shape),
         whole2(vec.shape)],
        [row, small, pl.BlockSpec(w1.shape, lambda i: (0, 0, 0)), whole2(w2.shape), whole2(vec.shape),
         whole2(wdw.shape)],
        [jax.ShapeDtypeStruct((t_len, d), F32), jax.ShapeDtypeStruct((SUBLANES, d), F32),
         jax.ShapeDtypeStruct(w1.shape, BF16), jax.ShapeDtypeStruct(w2.shape, BF16),
         jax.ShapeDtypeStruct(vec.shape, F32), jax.ShapeDtypeStruct(wdw.shape, F32)],
        [pltpu.VMEM((tm + CONV_HALO, d), F32), pltpu.VMEM((tm + CONV_HALO, d), F32),
         pltpu.VMEM((tm, d), F32), pltpu.VMEM(w1.shape, F32), pltpu.VMEM(w2.shape, F32)],
        (dxo, x, ypre, a_sav, a_sav, c1_sav, pv, w1, w2, wdw, vec), ex)


def _loss_head(y, target, *, tm):
    t_len, d = y.shape
    n_i = t_len // tm

    def body(y_ref, t_ref, dy_ref, sq_ref):
        @pl.when(pl.program_id(0) == 0)
        def _():
            sq_ref[...] = jnp.zeros_like(sq_ref)

        err = y_ref[...] - t_ref[...]
        dy_ref[...] = err * (1.0 / d)
        sq_ref[...] += jnp.sum((err * err).reshape(tm // SUBLANES, SUBLANES, d), axis=0)

    row = pl.BlockSpec((tm, d), lambda i: (i, 0))
    return pl.pallas_call(
        body, name="loss_head", grid=(n_i,), in_specs=[row, row],
        out_specs=[row, pl.BlockSpec((SUBLANES, d), lambda i: (0, 0))],
        out_shape=[jax.ShapeDtypeStruct((t_len, d), F32), jax.ShapeDtypeStruct((SUBLANES, d), F32)],
        compiler_params=_params(("arbitrary",)),
    )(y, target)


def _adam(name, parts, w, m, v):
    n, rows, cols = parts.shape
    tr = rows
    if rows % SUBLANES == 0:
        cap = max(SUBLANES, ADAM_BLOCK_BYTES // (4 * cols))
        tr = max(t for t in range(SUBLANES, rows + 1, SUBLANES) if rows % t == 0 and (t <= cap or t == SUBLANES))
    c1 = 1.0 / (1.0 - ADAM_B1 ** ADAM_STEP)
    c2 = 1.0 / (1.0 - ADAM_B2 ** ADAM_STEP)

    def body(p_ref, w_ref, m_ref, v_ref, g_ref, d_ref, mo_ref, vo_ref):
        g = p_ref[0].astype(F32)
        for k in range(1, n):
            g = g + p_ref[k].astype(F32)
        m2 = ADAM_B1 * m_ref[...] + (1.0 - ADAM_B1) * g
        v2 = ADAM_B2 * v_ref[...] + (1.0 - ADAM_B2) * (g * g)
        g_ref[...] = g
        mo_ref[...] = m2
        vo_ref[...] = v2
        d_ref[...] = -ADAM_LR * ((m2 * c1) / (jnp.sqrt(v2 * c2) + ADAM_EPS) + ADAM_WD * w_ref[...])

    blk = pl.BlockSpec((tr, cols), lambda i: (i, 0))
    out = jax.ShapeDtypeStruct((rows, cols), F32)
    return pl.pallas_call(
        body, name=name, grid=(rows // tr,),
        in_specs=[pl.BlockSpec((n, tr, cols), lambda i: (0, i, 0)), blk, blk, blk],
        out_specs=[blk, blk, blk, blk], out_shape=[out, out, out, out],
        compiler_params=_params(("arbitrary",)),
    )(parts, w, m, v)


def _adam_nd(name, parts, w, m, v):
    shape = w.shape
    cols = shape[-1]
    rows = w.size // cols
    outs = _adam(name, parts.reshape(parts.shape[0], rows, cols), w.reshape(rows, cols), m.reshape(rows, cols),
                 v.reshape(rows, cols))
    return [o.reshape(shape) for o in outs]


def _small_pack(parts, size):
    flat = jnp.concatenate([p.reshape(-1) for p in parts])
    return jnp.pad(flat, (0, size - flat.shape[0]))


def _to_shards(full, axis):
    shp = full.shape
    split = full.reshape(shp[:axis] + (N_DEV, shp[axis] // N_DEV) + shp[axis + 1:])
    return jnp.moveaxis(split, axis, 0)


def _from_shards(sh, axis):
    moved = jnp.moveaxis(sh, 0, axis)
    shp = moved.shape
    return moved.reshape(shp[:axis] + (shp[axis] * shp[axis + 1],) + shp[axis + 2:])


def kernel(x, c, ada_w, ada_b, pre_g, post_g, pool_w, pool_scale, cv_w_pw1, cv_b_pw1, cv_w_dw, cv_b_dw, cv_ln_g, cv_ln_b, cv_w_pw2, cv_b_pw2, ffn_w_up, ffn_w_dw, ffn_w_down, loss_target, m_ada_w, m_ada_b, m_pre_g, m_post_g, m_pool_w, m_pool_scale, m_cv_w_pw1, m_cv_b_pw1, m_cv_w_dw, m_cv_b_dw, m_cv_ln_g, m_cv_ln_b, m_cv_w_pw2, m_cv_b_pw2, m_ffn_w_up, m_ffn_w_dw, m_ffn_w_down, v_ada_w, v_ada_b, v_pre_g, v_post_g, v_pool_w, v_pool_scale, v_cv_w_pw1, v_cv_b_pw1, v_cv_w_dw, v_cv_b_dw, v_cv_ln_g, v_cv_ln_b, v_cv_w_pw2, v_cv_b_pw2, v_ffn_w_up, v_ffn_w_dw, v_ffn_w_down):
    t_len, d = x.shape[1], x.shape[2]
    depth = ada_w.shape[0]
    fc = ffn_w_up.shape[2]
    n_j = N_DEV // 2
    me = 4 * lax.axis_index("x") + 2 * lax.axis_index("y") + lax.axis_index("c")

    small_w = [pre_g, post_g, cv_b_pw1, cv_w_dw, cv_b_dw, cv_ln_g, cv_ln_b, cv_b_pw2, ffn_w_dw]
    small_m = [m_pre_g, m_post_g, m_cv_b_pw1, m_cv_w_dw, m_cv_b_dw, m_cv_ln_g, m_cv_ln_b, m_cv_b_pw2, m_ffn_w_dw]
    small_v = [v_pre_g, v_post_g, v_cv_b_pw1, v_cv_w_dw, v_cv_b_dw, v_cv_ln_g, v_cv_ln_b, v_cv_b_pw2, v_ffn_w_dw]
    sizes = [p.size for p in small_w]
    offs = [sum(sizes[:k]) for k in range(len(sizes) + 1)]
    pack = -(-offs[-1] // (SUBLANES * LANES)) * SUBLANES * LANES

    got = _exchange("gather_small", [c, _small_pack(small_w, pack), pool_w[0].astype(BF16)], [True] * 3)
    c_all = got[0].reshape(N_DEV, d)
    smalls = [got[1][:, offs[k]:offs[k + 1]].reshape((N_DEV,) + small_w[k].shape) for k in range(len(small_w))]
    pre_g_f, post_g_f = _from_shards(smalls[0], 2), _from_shards(smalls[1], 2)
    b1_f = _from_shards(smalls[2], 1)[0]
    cvw_f = jnp.pad(_from_shards(smalls[3], 2)[0], ((0, CONV_HALO - CONV_TAPS), (0, 0)))
    bdw_f, lng_f, lnb_f, b2_f = [_from_shards(smalls[k], 1)[0] for k in (4, 5, 6, 7)]
    fdw = jnp.pad(smalls[8], ((0, 0), (0, 0), (0, SUBLANES - FFN_TAPS), (0, 0)))
    wp = jnp.swapaxes(got[2], 0, 1).reshape(pool_w.shape[1], -1, pool_w.shape[3])
    wdw = [fdw[:, l].reshape(2, n_j, SUBLANES, fc) for l in range(depth)]
    cvec = jnp.zeros((SUBLANES, 2 * d), F32)
    cvec = cvec.at[CV_B1].set(b1_f)
    for r, vrow in ((CV_BDW, bdw_f), (CV_LNG, lng_f), (CV_LNB, lnb_f), (CV_B2, b2_f)):
        cvec = cvec.at[r, :d].set(vrow)

    mod_cols = _ada_fwd(c_all, ada_w)
    (mod_all,) = _exchange("gather_mod", [mod_cols], [True])
    mod = lax.dynamic_index_in_dim(mod_all, me, axis=2, keepdims=False)
    mod = jnp.swapaxes(mod, 0, 1).reshape(depth, N_MOD, d) + ada_b.reshape(depth, N_MOD, d)

    def pv_of(l, s):
        rows = [pre_g_f[l, s], 1.0 + mod[l, 3 * s + 1], mod[l, 3 * s], post_g_f[l, s], mod[l, 3 * s + 2]]
        return jnp.concatenate([jnp.stack(rows), jnp.zeros((SUBLANES - len(rows), d), F32)])

    x0 = x[0]
    pv00, pv01, pv10, pv11 = pv_of(0, 0), pv_of(0, 1), pv_of(1, 0), pv_of(1, 1)
    tm_pool, tm_ffn, tm_bwd, tm_conv = min(TM_POOL, t_len), min(TM_FFN, t_len), min(TM_FFN_BWD, t_len), min(TM_CONV, t_len)
    ex = _Exchange([ffn_w_up[0].astype(BF16), ffn_w_down[0].astype(BF16)], [True, True])
    (x1, y0), (wup0, wdn0) = _pool_fwd(x0, pv00, wp, pool_scale, tm=tm_pool, ex=ex)
    ex = _Exchange([cv_w_pw1[0].astype(BF16), cv_w_pw2[0].astype(BF16), ffn_w_up[1].astype(BF16),
                    ffn_w_down[1].astype(BF16)], [True] * 4)
    (x2, y1, h1, a1), (w1, w2, wup1, wdn1) = _ffn_fwd(
        x1, pv01, wup0.reshape(2, n_j, d, fc), wdw[0], wdn0.reshape(n_j * fc, d), tm=tm_ffn, ex=ex)
    w2 = w2.reshape(d, d)
    wup = [wup0.reshape(2, n_j, d, fc), wup1.reshape(2, n_j, d, fc)]
    wdn = [wdn0.reshape(n_j * fc, d), wdn1.reshape(n_j * fc, d)]
    x3, y2, a2, c2 = _conv_fwd(x2, pv10, w1, w2, cvw_f, cvec, tm=tm_conv)
    (x4, y3, h3, a3), _ = _ffn_fwd(x3, pv11, wup[1], wdw[1], wdn[1], tm=tm_ffn)
    dx4, sq = _loss_head(x4, loss_target[0], tm=min(TM_LOSS, t_len))
    loss = lax.psum(jnp.sum(sq) * (0.5 / d), MESH_AXES)

    (dx3, da3, u3, dy3, pg11, gfdw1), _ = _ffn_bwd_a(dx4, x3, y3, a3, pv11, wup[1], wdw[1], wdn[1], tm=tm_bwd)
    (gup1,), _ = _ffn_bwd_wup(h3, da3, tm=tm_ffn)
    (gdn1,), _ = _ffn_bwd_wdn(u3, dy3, tm=tm_ffn)
    ex = _Exchange([gup1.reshape(N_DEV, d, fc), gdn1.reshape(N_DEV, -1, d)], [False, False])
    (dx2, pg10, gw1, gw2, gcvec, gcvw), (rup1, rdn1) = _conv_bwd(
        dx3, x2, y2, a2, c2, pv10, w1, w2, cvw_f, cvec, tm=tm_conv, ex=ex)
    ex = _Exchange([gw1, gw2.reshape(N_DEV, -1, d)], [False, False])
    (dx1, da1, u1, dy1, pg01, gfdw0), (rw1, rw2) = _ffn_bwd_a(
        dx2, x1, y1, a1, pv01, wup[0], wdw[0], wdn[0], tm=tm_bwd, ex=ex)
    (gdn0,), _ = _ffn_bwd_wdn(u1, dy1, tm=tm_ffn)
    ex = _Exchange([gdn0.reshape(N_DEV, -1, d)], [False])
    (gup0,), (rdn0,) = _ffn_bwd_wup(h1, da1, tm=tm_ffn, ex=ex)
    ex = _Exchange([gup0.reshape(N_DEV, d, fc)], [False])
    (dx0, pg00, gwp), (rup0,) = _pool_bwd(dx1, x0, y0, pv00, wp, pool_scale, tm=tm_pool, ex=ex)

    pgs = [[pg00, pg01], [pg10, pg11]]
    g_pre = jnp.stack([jnp.stack([pgs[l][s][PG_GPRE] for s in range(2)]) for l in range(depth)])
    g_post = jnp.stack([jnp.stack([pgs[l][s][PG_GPOST] for s in range(2)]) for l in range(depth)])
    dmod = jnp.stack([jnp.concatenate([pgs[l][s][r] for s in range(2) for r in (PG_SH, PG_SC, PG_GT)])
                      for l in range(depth)])
    gfdw = jnp.stack([g.reshape(N_DEV, SUBLANES, fc)[:, :FFN_TAPS] for g in (gfdw0, gfdw1)], axis=1)
    small_g = [_to_shards(g_pre, 2), _to_shards(g_post, 2), _to_shards(gcvec[CV_B1][None], 1),
               _to_shards(gcvw[None, :CONV_TAPS], 2), _to_shards(gcvec[CV_BDW, :d][None], 1),
               _to_shards(gcvec[CV_LNG, :d][None], 1), _to_shards(gcvec[CV_LNB, :d][None], 1),
               _to_shards(gcvec[CV_B2, :d][None], 1), gfdw]
    small_send = jnp.concatenate([g.reshape(N_DEV, -1) for g in small_g], axis=1)
    small_send = jnp.pad(small_send, ((0, 0), (0, pack - small_send.shape[1])))
    gwp_send = jnp.swapaxes(gwp.reshape(gwp.shape[0], N_DEV, -1, gwp.shape[2]), 0, 1)
    rsmall, rwp, rmod, rscale = _exchange("scatter_small", [small_send, gwp_send, dmod, pg00[PG_EXTRA][None]],
                                          [False, False, True, True])

    outs = {}

    def put(name, res, shape=None):
        outs[name] = [r if shape is None else r.reshape(shape) for r in res]

    small_res = _adam("adam_small", rsmall.reshape(N_DEV, -1, SUBLANES * LANES),
                      _small_pack(small_w, pack).reshape(-1, SUBLANES * LANES),
                      _small_pack(small_m, pack).reshape(-1, SUBLANES * LANES),
                      _small_pack(small_v, pack).reshape(-1, SUBLANES * LANES))
    small_names = ["pre_g", "post_g", "cv_b_pw1", "cv_w_dw", "cv_b_dw", "cv_ln_g", "cv_ln_b", "cv_b_pw2", "ffn_w_dw"]
    for k, nm in enumerate(small_names):
        outs[nm] = [r.reshape(-1)[offs[k]:offs[k + 1]].reshape(small_w[k].shape) for r in small_res]
    put("pool_w", _adam_nd("adam_pool_w", rwp[:, None], pool_w, m_pool_w, v_pool_w))
    put("cv_w_pw1", _adam_nd("adam_cv_w_pw1", rw1[:, None], cv_w_pw1, m_cv_w_pw1, v_cv_w_pw1))
    put("cv_w_pw2", _adam_nd("adam_cv_w_pw2", rw2[:, None], cv_w_pw2, m_cv_w_pw2, v_cv_w_pw2))
    put("ffn_w_up", _adam_nd("adam_ffn_w_up", jnp.stack([rup0, rup1], axis=1), ffn_w_up, m_ffn_w_up, v_ffn_w_up))
    put("ffn_w_down", _adam_nd("adam_ffn_w_down", jnp.stack([rdn0, rdn1], axis=1), ffn_w_down, m_ffn_w_down, v_ffn_w_down))
    put("ada_b", _adam_nd("adam_ada_b", rmod, ada_b, m_ada_b, v_ada_b))
    put("pool_scale", _adam_nd("adam_pool_scale", rscale, pool_scale, m_pool_scale, v_pool_scale))
    cols = ada_w.shape[2]
    dmod_cols = jnp.swapaxes(lax.dynamic_slice_in_dim(rmod, me * cols, cols, axis=2), 0, 1)
    put("ada_w", _adam_nd("adam_ada_w", _ada_bwd(c_all, dmod_cols)[None], ada_w, m_ada_w, v_ada_w))

    order = ["ada_w", "ada_b", "pre_g", "post_g", "pool_w", "pool_scale", "cv_w_pw1", "cv_b_pw1", "cv_w_dw", "cv_b_dw",
             "cv_ln_g", "cv_ln_b", "cv_w_pw2", "cv_b_pw2", "ffn_w_up", "ffn_w_dw", "ffn_w_down"]
    return (loss, dx0[None], *[outs[nm][0] for nm in order], *[outs[nm][1] for nm in order],
            *[outs[nm][2] for nm in order], *[outs[nm][3] for nm in order])
```

```python
import functools

import jax
import jax.numpy as jnp
from jax import lax
from jax.experimental import pallas as pl
from jax.experimental.pallas import tpu as pltpu

F32, BF16 = jnp.float32, jnp.bfloat16
MESH_AXES = ("x", "y", "c")
N_DEV = 8
NORM_EPS = 1e-6
ADAM_LR, ADAM_B1, ADAM_B2, ADAM_EPS, ADAM_WD, ADAM_STEP = 0.001, 0.9, 0.999, 1e-08, 0.01, 10
POOL_WINDOWS = (2, 4, 8, 16)
CONV_TAPS = 31
FFN_TAPS = 3
N_MOD = 6

SUBLANES = 8
LANES = 128
VMEM_LIMIT_BYTES = 56 * 1024 * 1024
POOL_HALO = 16
CONV_HALO = 32
FFN_HALO = 8
TM_POOL, TM_FFN, TM_FFN_BWD, TM_FFN_W, TM_CONV, TM_LOSS = 512, 512, 256, 1024, 256, 1024
CONV_ROWS = 32
ADAM_BLOCK_BYTES = 512 * 1024

PV_GPRE, PV_SC1, PV_SH, PV_GPOST, PV_GT = 0, 1, 2, 3, 4
PG_GPRE, PG_SC, PG_SH, PG_GPOST, PG_GT, PG_EXTRA = 0, 1, 2, 3, 4, 5


def _params(sem):
    return pltpu.CompilerParams(dimension_semantics=sem, vmem_limit_bytes=VMEM_LIMIT_BYTES)


def _dot(a, b):
    return jnp.dot(a, b, preferred_element_type=F32)


def _dot_nt(a, b):
    return lax.dot_general(a, b, (((1,), (1,)), ((), ())), preferred_element_type=F32)


def _dot_tn(a, b):
    return lax.dot_general(a, b, (((0,), (0,)), ((), ())), preferred_element_type=F32)


def _sigmoid(x):
    return 1.0 / (1.0 + jnp.exp(-x))


def _rms(x):
    return lax.rsqrt(jnp.mean(x * x, axis=-1, keepdims=True) + NORM_EPS)


def _colsum(v):
    return jnp.sum(v, axis=0, keepdims=True)


def _prenorm(x, pv):
    r = _rms(x)
    xn = x * r
    return xn * (pv[PV_GPRE:PV_GPRE + 1] * pv[PV_SC1:PV_SC1 + 1]) + pv[PV_SH:PV_SH + 1], xn, r


def _post(x, y, pv):
    return x + (pv[PV_GT:PV_GT + 1] * pv[PV_GPOST:PV_GPOST + 1]) * (y * _rms(y))


def _post_bwd(dxo, y, pv, pg_ref):
    ry = _rms(y)
    yn = y * ry
    gt, gpost = pv[PV_GT:PV_GT + 1], pv[PV_GPOST:PV_GPOST + 1]
    dyn = dxo * (gt * gpost)
    dy = ry * (dyn - yn * jnp.mean(dyn * yn, axis=-1, keepdims=True))
    s = _colsum(dxo * yn)
    pg_ref[PG_GPOST:PG_GPOST + 1, :] += s * gt
    pg_ref[PG_GT:PG_GT + 1, :] += s * gpost
    return dy


def _pre_bwd(dh, x, pv, pg_ref):
    r = _rms(x)
    xn = x * r
    gpre, sc1 = pv[PV_GPRE:PV_GPRE + 1], pv[PV_SC1:PV_SC1 + 1]
    dxn = dh * (gpre * sc1)
    dx = r * (dxn - xn * jnp.mean(dxn * xn, axis=-1, keepdims=True))
    p = _colsum(dh * xn)
    pg_ref[PG_GPRE:PG_GPRE + 1, :] += p * sc1
    pg_ref[PG_SC:PG_SC + 1, :] += p * gpre
    pg_ref[PG_SH:PG_SH + 1, :] += _colsum(dh)
    return dx


def _shift_down(a, k, prev):
    out = pltpu.roll(a, k, 0)
    row = lax.broadcasted_iota(jnp.int32, a.shape, 0)
    for q in range(k):
        out = jnp.where(row == q, prev[SUBLANES - k + q:SUBLANES - k + q + 1, :], out)
    return out


def _shift_up(a, k, nxt):
    rows = a.shape[0]
    out = pltpu.roll(a, rows - k, 0)
    row = lax.broadcasted_iota(jnp.int32, a.shape, 0)
    for q in range(k):
        out = jnp.where(row == rows - k + q, nxt[q:q + 1, :], out)
    return out


class _Exchange:
    def __init__(self, srcs, gathers):
        self.srcs, self.gathers, self.n = list(srcs), list(gathers), len(srcs)
        self.out_shape = [jax.ShapeDtypeStruct(((N_DEV,) + s.shape) if g else s.shape, s.dtype)
                          for s, g in zip(srcs, gathers)]
        self.specs = [pl.BlockSpec(memory_space=pl.ANY)] * self.n
        self.scratch = [pltpu.SemaphoreType.DMA((self.n, N_DEV - 1)), pltpu.SemaphoreType.DMA((self.n, N_DEV - 1)),
                        pltpu.SemaphoreType.DMA((self.n,))]

    def _copies(self, src_refs, out_refs, sems):
        send_sems, recv_sems, local_sems = sems
        x, y, c = lax.axis_index("x"), lax.axis_index("y"), lax.axis_index("c")
        me = 4 * x + 2 * y + c
        copies = []
        for a in range(self.n):
            mine = src_refs[a] if self.gathers[a] else src_refs[a].at[me]
            copies.append(pltpu.make_async_copy(mine, out_refs[a].at[me], local_sems.at[a]))
        for d in range(1, N_DEV):
            px, py, pc = (x + (d >> 2)) % 2, (y + ((d >> 1) & 1)) % 2, (c + (d & 1)) % 2
            peer = 4 * px + 2 * py + pc
            for a in range(self.n):
                src = src_refs[a] if self.gathers[a] else src_refs[a].at[peer]
                copies.append(pltpu.make_async_remote_copy(
                    src_ref=src, dst_ref=out_refs[a].at[me], send_sem=send_sems.at[a, d - 1],
                    recv_sem=recv_sems.at[a, d - 1], device_id=(px, py, pc), device_id_type=pl.DeviceIdType.MESH))
        return copies

    def start(self, src_refs, out_refs, sems):
        for cp in self._copies(src_refs, out_refs, sems):
            cp.start()

    def wait(self, src_refs, out_refs, sems):
        for cp in self._copies(src_refs, out_refs, sems):
            cp.wait()


def _exchange(name, srcs, gathers):
    ex = _Exchange(srcs, gathers)

    def body(*refs):
        src_refs, out_refs, sems = refs[:ex.n], refs[ex.n:2 * ex.n], refs[2 * ex.n:]
        ex.start(src_refs, out_refs, sems)
        ex.wait(src_refs, out_refs, sems)

    return pl.pallas_call(body, name=name, out_shape=ex.out_shape, in_specs=ex.specs, out_specs=ex.specs,
                          scratch_shapes=ex.scratch)(*srcs)


def _launch(body, name, grid, in_specs, out_specs, out_shape, scratch_shapes, args, ex=None):
    sem = ("arbitrary",) * len(grid)
    if ex is None:
        outs = pl.pallas_call(body, name=name, grid=grid, in_specs=in_specs, out_specs=out_specs, out_shape=out_shape,
                              scratch_shapes=scratch_shapes, compiler_params=_params(sem))(*args)
        return outs, []
    n_in, n_out, n_scr = len(in_specs), len(out_specs), len(scratch_shapes)

    def hosted(*refs):
        ins, ex_in = refs[:n_in], refs[n_in:n_in + ex.n]
        outs = refs[n_in + ex.n:n_in + ex.n + n_out]
        ex_out = refs[n_in + ex.n + n_out:n_in + 2 * ex.n + n_out]
        rest = refs[n_in + 2 * ex.n + n_out:]
        scratch, sems = rest[:n_scr], rest[n_scr:]
        ids = [pl.program_id(a) for a in range(len(grid))]
        first = functools.reduce(lambda p, q: p & q, [i == 0 for i in ids])
        last = functools.reduce(lambda p, q: p & q, [i == g - 1 for i, g in zip(ids, grid)])

        @pl.when(first)
        def _():
            ex.start(ex_in, ex_out, sems)

        body(*ins, *outs, *scratch)

        @pl.when(last)
        def _():
            ex.wait(ex_in, ex_out, sems)

    outs = pl.pallas_call(
        hosted, name=name, grid=grid, in_specs=list(in_specs) + ex.specs, out_specs=list(out_specs) + ex.specs,
        out_shape=list(out_shape) + ex.out_shape, scratch_shapes=list(scratch_shapes) + ex.scratch,
        compiler_params=_params(sem))(*args, *ex.srcs)
    return outs[:n_out], outs[n_out:]


def _ada_fwd(c_all, ada_w):
    layers, d, cols = ada_w.shape

    def body(c_ref, w_ref, o_ref):
        c = c_ref[...]
        ca = (c * _sigmoid(c)).astype(BF16)
        for l in range(layers):
            o_ref[l] = _dot(ca, w_ref[l].astype(BF16))

    return pl.pallas_call(
        body, name="ada_fwd", out_shape=jax.ShapeDtypeStruct((layers, N_DEV, cols), F32),
        compiler_params=pltpu.CompilerParams(vmem_limit_bytes=VMEM_LIMIT_BYTES),
    )(c_all, ada_w)


def _ada_bwd(c_all, dmod_cols):
    layers, _, cols = dmod_cols.shape
    d = c_all.shape[1]

    def body(c_ref, g_ref, o_ref):
        c = c_ref[...]
        ca = (c * _sigmoid(c)).astype(BF16)
        for l in range(layers):
            o_ref[l] = _dot_tn(ca, g_ref[l].astype(BF16))

    return pl.pallas_call(
        body, name="ada_bwd", out_shape=jax.ShapeDtypeStruct((layers, d, cols), F32),
        compiler_params=pltpu.CompilerParams(vmem_limit_bytes=VMEM_LIMIT_BYTES),
    )(c_all, dmod_cols)


def _pooled(hbuf, h, t0, g, tm):
    gd = h.shape[1] // len(POOL_WINDOWS)
    cols = slice(g * gd, (g + 1) * gd)
    w = POOL_WINDOWS[g]
    hg = h[:, cols]
    s = hg
    for k in range(1, w):
        s = s + hbuf[POOL_HALO - k:POOL_HALO - k + tm, cols]
    t = t0 + lax.broadcasted_iota(jnp.int32, (tm, 1), 0)
    cnt = jnp.minimum(t + 1, w).astype(F32)
    return s / cnt - hg, cnt


def _pool_fwd(x, pv, wp, scale, *, tm, ex=None):
    t_len, d = x.shape
    n_i = t_len // tm
    gd = d // len(POOL_WINDOWS)

    def body(x_ref, pv_ref, wp_ref, sc_ref, xo_ref, y_ref, hbuf):
        i = pl.program_id(0)

        @pl.when(i == 0)
        def _():
            hbuf[0:POOL_HALO, :] = jnp.zeros((POOL_HALO, d), F32)

        xv, pv_ = x_ref[...], pv_ref[...]
        h, _, _ = _prenorm(xv, pv_)
        hbuf[POOL_HALO:POOL_HALO + tm, :] = h
        for g in range(len(POOL_WINDOWS)):
            pooled, _ = _pooled(hbuf, h, i * tm, g, tm)
            y_ref[:, g * gd:(g + 1) * gd] = _dot(pooled.astype(BF16), wp_ref[g])
        xo_ref[...] = _post(xv, y_ref[...] * sc_ref[...], pv_)
        hbuf[0:POOL_HALO, :] = hbuf[tm:tm + POOL_HALO, :]

    row = pl.BlockSpec((tm, d), lambda i: (i, 0))
    return _launch(
        body, "pool_fwd", (n_i,),
        [row, pl.BlockSpec((SUBLANES, d), lambda i: (0, 0)), pl.BlockSpec(wp.shape, lambda i: (0, 0, 0)),
         pl.BlockSpec((1, d), lambda i: (0, 0))],
        [row, row],
        [jax.ShapeDtypeStruct((t_len, d), F32), jax.ShapeDtypeStruct((t_len, d), F32)],
        [pltpu.VMEM((tm + POOL_HALO, d), F32)],
        (x, pv, wp, scale), ex)


def _pool_bwd(dxo, x, ypre, pv, wp, scale, *, tm, ex=None):
    t_len, d = x.shape
    n_i = t_len // tm
    gd = d // len(POOL_WINDOWS)
    hb = tm // POOL_HALO

    def body(dxo_ref, x_ref, xh_ref, y_ref, pv_ref, wp_ref, sc_ref, dx_ref, pg_ref, dwp_ref, hbuf, qbuf):
        i = pl.program_id(0)
        ti = n_i - 1 - i

        @pl.when(i == 0)
        def _():
            pg_ref[...] = jnp.zeros_like(pg_ref)
            dwp_ref[...] = jnp.zeros_like(dwp_ref)
            qbuf[tm:tm + POOL_HALO, :] = jnp.zeros((POOL_HALO, d), F32)

        xv, pv_, dxo_v, yp, sc = x_ref[...], pv_ref[...], dxo_ref[...], y_ref[...], sc_ref[...]
        dy = _post_bwd(dxo_v, yp * sc, pv_, pg_ref)
        pg_ref[PG_EXTRA:PG_EXTRA + 1, :] += _colsum(dy * yp)
        dys = dy * sc
        h, _, _ = _prenorm(xv, pv_)
        hh, _, _ = _prenorm(xh_ref[...], pv_)
        hbuf[0:POOL_HALO, :] = jnp.where(ti > 0, hh, 0.0)
        hbuf[POOL_HALO:POOL_HALO + tm, :] = h
        for g in range(len(POOL_WINDOWS)):
            cols = slice(g * gd, (g + 1) * gd)
            pooled, cnt = _pooled(hbuf, h, ti * tm, g, tm)
            dyg = dys[:, cols].astype(BF16)
            dwp_ref[g] += _dot_tn(pooled.astype(BF16), dyg)
            dp = _dot_nt(dyg, wp_ref[g])
            qbuf[0:tm, cols] = dp / cnt
            dh = -dp
            for k in range(POOL_WINDOWS[g]):
                dh = dh + qbuf[k:k + tm, cols]
            hbuf[POOL_HALO:POOL_HALO + tm, cols] = dh
        dx_ref[...] = dxo_v + _pre_bwd(hbuf[POOL_HALO:POOL_HALO + tm, :], xv, pv_, pg_ref)
        qbuf[tm:tm + POOL_HALO, :] = qbuf[0:POOL_HALO, :]

    row = pl.BlockSpec((tm, d), lambda i: (n_i - 1 - i, 0))
    halo = pl.BlockSpec((POOL_HALO, d), lambda i: (jnp.maximum((n_i - 1 - i) * hb - 1, 0), 0))
    small = pl.BlockSpec((SUBLANES, d), lambda i: (0, 0))
    return _launch(
        body, "pool_bwd", (n_i,),
        [row, row, halo, row, small, pl.BlockSpec(wp.shape, lambda i: (0, 0, 0)), pl.BlockSpec((1, d), lambda i: (0, 0))],
        [row, small, pl.BlockSpec(wp.shape, lambda i: (0, 0, 0))],
        [jax.ShapeDtypeStruct((t_len, d), F32), jax.ShapeDtypeStruct((SUBLANES, d), F32),
         jax.ShapeDtypeStruct(wp.shape, F32)],
        [pltpu.VMEM((tm + POOL_HALO, d), F32), pltpu.VMEM((tm + POOL_HALO, d), F32)],
        (dxo, x, x, ypre, pv, wp, scale), ex)


def _ffn_conv(a, prev, w):
    return w[2:3] * a + w[1:2] * _shift_down(a, 1, prev) + w[0:1] * _shift_down(a, 2, prev)


def _ffn_fwd(x, pv, wup, wdw, wdn, *, tm, ex=None):
    t_len, d = x.shape
    _, n_j, _, fc = wup.shape
    n_i = t_len // tm

    def body(x_ref, pv_ref, wup_ref, wdw_ref, wdn_ref, xo_ref, y_ref, h_ref, a_ref, h_s, yacc, carry):
        i, j = pl.program_id(0), pl.program_id(1)

        @pl.when(j == 0)
        def _():
            h, _, _ = _prenorm(x_ref[...], pv_ref[...])
            hb = h.astype(BF16)
            h_s[...] = hb
            h_ref[...] = hb
            yacc[...] = jnp.zeros_like(yacc)

        @pl.when((i == 0) & (j == 0))
        def _():
            carry[...] = jnp.zeros_like(carry)

        hb = h_s[...]
        conv = []
        for s in range(2):
            ab = _dot(hb, wup_ref[s, 0]).astype(BF16)
            a_ref[s, 0] = ab
            a = ab.astype(F32)
            conv.append(_ffn_conv(a, carry[s, j], wdw_ref[s, 0]))
            carry[s, j] = a[tm - FFN_HALO:tm, :]
        g, v = conv
        u = g * _sigmoid(g) * v
        yacc[...] += _dot(u.astype(BF16), wdn_ref[...])

        @pl.when(j == n_j - 1)
        def _():
            y = yacc[...]
            y_ref[...] = y
            xo_ref[...] = _post(x_ref[...], y, pv_ref[...])

    row = pl.BlockSpec((tm, d), lambda i, j: (i, 0))
    return _launch(
        body, "ffn_fwd", (n_i, n_j),
        [row, pl.BlockSpec((SUBLANES, d), lambda i, j: (0, 0)),
         pl.BlockSpec((2, 1, d, fc), lambda i, j: (0, j, 0, 0)),
         pl.BlockSpec((2, 1, SUBLANES, fc), lambda i, j: (0, j, 0, 0)),
         pl.BlockSpec((fc, d), lambda i, j: (j, 0))],
        [row, row, row, pl.BlockSpec((2, 1, tm, fc), lambda i, j: (0, j, i, 0))],
        [jax.ShapeDtypeStruct((t_len, d), F32), jax.ShapeDtypeStruct((t_len, d), F32),
         jax.ShapeDtypeStruct((t_len, d), BF16), jax.ShapeDtypeStruct((2, n_j, t_len, fc), BF16)],
        [pltpu.VMEM((tm, d), BF16), pltpu.VMEM((tm, d), F32), pltpu.VMEM((2, n_j, FFN_HALO, fc), F32)],
        (x, pv, wup, wdw, wdn), ex)


def _ffn_bwd_a(dxo, x, ypre, a_sav, pv, wup, wdw, wdn, *, tm, ex=None):
    t_len, d = x.shape
    _, n_j, _, fc = wup.shape
    n_i = t_len // tm
    hb = tm // FFN_HALO

    def body(dxo_ref, x_ref, y_ref, a_ref, ah_ref, pv_ref, wup_ref, wdw_ref, wdn_ref,
             dx_ref, da_ref, u_ref, dy_ref, pg_ref, dwdw_ref, dy_s, dh_acc, carry):
        i, j = pl.program_id(0), pl.program_id(1)
        ti = n_i - 1 - i

        @pl.when((i == 0) & (j == 0))
        def _():
            pg_ref[...] = jnp.zeros_like(pg_ref)
            dwdw_ref[...] = jnp.zeros_like(dwdw_ref)
            carry[...] = jnp.zeros_like(carry)

        @pl.when(j == 0)
        def _():
            dyb = _post_bwd(dxo_ref[...], y_ref[...], pv_ref[...], pg_ref).astype(BF16)
            dy_s[...] = dyb
            dy_ref[...] = dyb
            dh_acc[...] = jnp.zeros_like(dh_acc)

        a, shifted, conv = [], [], []
        for s in range(2):
            a_s = a_ref[s, 0].astype(F32)
            prev = jnp.where(ti > 0, ah_ref[s, 0].astype(F32), 0.0)
            w = wdw_ref[s, 0]
            m1, m2 = _shift_down(a_s, 1, prev), _shift_down(a_s, 2, prev)
            a.append(a_s)
            shifted.append((m2, m1))
            conv.append(w[2:3] * a_s + w[1:2] * m1 + w[0:1] * m2)
        g, v = conv
        sg = _sigmoid(g)
        sl = g * sg
        ub = (sl * v).astype(BF16)
        u_ref[0] = ub
        dyb = dy_s[...]
        du = _dot_nt(dyb, wdn_ref[...])
        d2 = (du * v * (sg * (1.0 + g * (1.0 - sg))), du * sl)
        dh = dh_acc[...]
        for s in range(2):
            w = wdw_ref[s, 0]
            taps = (shifted[s][0], shifted[s][1], a[s])
            for k in range(FFN_TAPS):
                dwdw_ref[s, j, k:k + 1, :] += _colsum(d2[s] * taps[k])
            nxt = carry[s, j]
            da = w[2:3] * d2[s] + w[1:2] * _shift_up(d2[s], 1, nxt) + w[0:1] * _shift_up(d2[s], 2, nxt)
            carry[s, j] = d2[s][0:FFN_HALO, :]
            dab = da.astype(BF16)
            da_ref[s, 0] = dab
            dh = dh + _dot_nt(dab, wup_ref[s, 0])
        dh_acc[...] = dh

        @pl.when(j == n_j - 1)
        def _():
            dx_ref[...] = dxo_ref[...] + _pre_bwd(dh_acc[...], x_ref[...], pv_ref[...], pg_ref)

    row = pl.BlockSpec((tm, d), lambda i, j: (n_i - 1 - i, 0))
    small = pl.BlockSpec((SUBLANES, d), lambda i, j: (0, 0))
    return _launch(
        body, "ffn_bwd_a", (n_i, n_j),
        [row, row, row,
         pl.BlockSpec((2, 1, tm, fc), lambda i, j: (0, j, n_i - 1 - i, 0)),
         pl.BlockSpec((2, 1, FFN_HALO, fc), lambda i, j: (0, j, jnp.maximum((n_i - 1 - i) * hb - 1, 0), 0)),
         small,
         pl.BlockSpec((2, 1, d, fc), lambda i, j: (0, j, 0, 0)),
         pl.BlockSpec((2, 1, SUBLANES, fc), lambda i, j: (0, j, 0, 0)),
         pl.BlockSpec((fc, d), lambda i, j: (j, 0))],
        [row,
         pl.BlockSpec((2, 1, tm, fc), lambda i, j: (0, j, n_i - 1 - i, 0)),
         pl.BlockSpec((1, tm, fc), lambda i, j: (j, n_i - 1 - i, 0)),
         row, small,
         pl.BlockSpec((2, n_j, SUBLANES, fc), lambda i, j: (0, 0, 0, 0))],
        [jax.ShapeDtypeStruct((t_len, d), F32), jax.ShapeDtypeStruct((2, n_j, t_len, fc), BF16),
         jax.ShapeDtypeStruct((n_j, t_len, fc), BF16), jax.ShapeDtypeStruct((t_len, d), BF16),
         jax.ShapeDtypeStruct((SUBLANES, d), F32), jax.ShapeDtypeStruct((2, n_j, SUBLANES, fc), F32)],
        [pltpu.VMEM((tm, d), BF16), pltpu.VMEM((tm, d), F32), pltpu.VMEM((2, n_j, FFN_HALO, fc), F32)],
        (dxo, x, ypre, a_sav, a_sav, pv, wup, wdw, wdn), ex)


def _ffn_bwd_wup(h, da, *, tm, ex=None):
    t_len, d = h.shape
    _, n_j, _, fc = da.shape
    n_i = t_len // tm

    def body(h_ref, da_ref, gup_ref, acc):
        i = pl.program_id(1)

        @pl.when(i == 0)
        def _():
            acc[...] = jnp.zeros_like(acc)

        hb = h_ref[...]
        for s in range(2):
            acc[s] += _dot_tn(hb, da_ref[s, 0])

        @pl.when(i == n_i - 1)
        def _():
            gup_ref[:, 0] = acc[...].astype(BF16)

    return _launch(
        body, "ffn_bwd_wup", (n_j, n_i),
        [pl.BlockSpec((tm, d), lambda j, i: (i, 0)), pl.BlockSpec((2, 1, tm, fc), lambda j, i: (0, j, i, 0))],
        [pl.BlockSpec((2, 1, d, fc), lambda j, i: (0, j, 0, 0))],
        [jax.ShapeDtypeStruct((2, n_j, d, fc), BF16)],
        [pltpu.VMEM((2, d, fc), F32)],
        (h, da), ex)


def _ffn_bwd_wdn(u, dy, *, tm, ex=None):
    n_j, t_len, fc = u.shape
    d = dy.shape[1]
    n_i = t_len // tm

    def body(u_ref, dy_ref, gdn_ref, acc):
        i = pl.program_id(1)

        @pl.when(i == 0)
        def _():
            acc[...] = jnp.zeros_like(acc)

        acc[...] += _dot_tn(u_ref[0], dy_ref[...])

        @pl.when(i == n_i - 1)
        def _():
            gdn_ref[...] = acc[...].astype(BF16)

    return _launch(
        body, "ffn_bwd_wdn", (n_j, n_i),
        [pl.BlockSpec((1, tm, fc), lambda j, i: (j, i, 0)), pl.BlockSpec((tm, d), lambda j, i: (i, 0))],
        [pl.BlockSpec((fc, d), lambda j, i: (j, 0))],
        [jax.ShapeDtypeStruct((n_j * fc, d), BF16)],
        [pltpu.VMEM((fc, d), F32)],
        (u, dy), ex)


CV_B1, CV_BDW, CV_LNG, CV_LNB, CV_B2 = 0, 1, 2, 3, 4


def _taps_by_residue(offs):
    groups = {}
    for k, off in enumerate(offs):
        groups.setdefault(off % SUBLANES, []).append((k, off // SUBLANES))
    return sorted(groups.items())


def _depthwise(buf, w_ref, out_ref, offs, tm, d):
    taps_of = _taps_by_residue(offs)

    def chunk(r, carry):
        r0 = pl.multiple_of(r * CONV_ROWS, CONV_ROWS)
        for cb in range(d // LANES):
            cols = slice(cb * LANES, (cb + 1) * LANES)
            win = buf[pl.ds(r0, CONV_ROWS + CONV_HALO), cols]
            acc = jnp.zeros((CONV_ROWS, LANES), F32)
            for b, taps in taps_of:
                wb = win if b == 0 else pltpu.roll(win, CONV_ROWS + CONV_HALO - b, 0)
                for k, a in taps:
                    acc = acc + wb[SUBLANES * a:SUBLANES * a + CONV_ROWS, :] * w_ref[k:k + 1, cols]
            out_ref[pl.ds(r0, CONV_ROWS), cols] = acc
        return carry

    lax.fori_loop(0, tm // CONV_ROWS, chunk, 0)


def _depthwise_wgrad(dbuf, ubuf, dw_ref, tm, d):
    taps_of = _taps_by_residue(tuple(2 + k for k in range(CONV_TAPS)))
    for cb in range(d // LANES):
        cols = slice(cb * LANES, (cb + 1) * LANES)

        def chunk(r, acc):
            r0 = pl.multiple_of(r * SUBLANES, SUBLANES)
            dv = dbuf[pl.ds(r0, SUBLANES), cols]
            win = ubuf[pl.ds(r0, SUBLANES + CONV_HALO), cols]
            new = list(acc)
            for b, taps in taps_of:
                wb = win if b == 0 else pltpu.roll(win, SUBLANES + CONV_HALO - b, 0)
                for k, a in taps:
                    new[k] = acc[k] + dv * wb[SUBLANES * a:SUBLANES * (a + 1), :]
            return tuple(new)

        acc = lax.fori_loop(0, tm // SUBLANES, chunk, tuple(jnp.zeros((SUBLANES, LANES), F32) for _ in range(CONV_TAPS)))
        for k in range(CONV_TAPS):
            dw_ref[k:k + 1, cols] += _colsum(acc[k])


def _layer_norm_parts(c1):
    mu = jnp.mean(c1, axis=-1, keepdims=True)
    cen = c1 - mu
    rstd = lax.rsqrt(jnp.mean(cen * cen, axis=-1, keepdims=True) + NORM_EPS)
    return cen * rstd, rstd


def _conv_fwd(x, pv, w1, w2, wdw, vec, *, tm):
    t_len, d = x.shape
    n_i = t_len // tm
    n_q = w1.shape[0] // 2
    qc = w1.shape[2]

    def body(x_ref, pv_ref, w1_ref, w2_ref, wdw_ref, vec_ref, xo_ref, y_ref, a_ref, c1_ref, ubuf):
        i = pl.program_id(0)

        @pl.when(i == 0)
        def _():
            ubuf[0:CONV_HALO, :] = jnp.zeros((CONV_HALO, d), F32)

        xv, pv_ = x_ref[...], pv_ref[...]
        h, _, _ = _prenorm(xv, pv_)
        hb = h.astype(BF16)
        for q in range(n_q):
            cols = slice(q * qc, (q + 1) * qc)
            gcols = slice(d + q * qc, d + (q + 1) * qc)
            val = (_dot(hb, w1_ref[q]) + vec_ref[CV_B1:CV_B1 + 1, cols]).astype(BF16)
            gate = (_dot(hb, w1_ref[n_q + q]) + vec_ref[CV_B1:CV_B1 + 1, gcols]).astype(BF16)
            a_ref[:, cols] = val
            a_ref[:, gcols] = gate
            ubuf[CONV_HALO:CONV_HALO + tm, cols] = val.astype(F32) * _sigmoid(gate.astype(F32))
        _depthwise(ubuf, wdw_ref, c1_ref, tuple(2 + k for k in range(CONV_TAPS)), tm, d)
        c1 = c1_ref[...] + vec_ref[CV_BDW:CV_BDW + 1, 0:d]
        c1_ref[...] = c1
        xhat, _ = _layer_norm_parts(c1)
        ln = xhat * vec_ref[CV_LNG:CV_LNG + 1, 0:d] + vec_ref[CV_LNB:CV_LNB + 1, 0:d]
        s = ln * _sigmoid(ln)
        y = _dot(s.astype(BF16), w2_ref[...]) + vec_ref[CV_B2:CV_B2 + 1, 0:d]
        y_ref[...] = y
        xo_ref[...] = _post(xv, y, pv_)
        ubuf[0:CONV_HALO, :] = ubuf[tm:tm + CONV_HALO, :]

    row = pl.BlockSpec((tm, d), lambda i: (i, 0))
    return pl.pallas_call(
        body, name="conv_fwd", grid=(n_i,),
        in_specs=[row, pl.BlockSpec((SUBLANES, d), lambda i: (0, 0)), pl.BlockSpec(w1.shape, lambda i: (0, 0, 0)),
                  pl.BlockSpec(w2.shape, lambda i: (0, 0)), pl.BlockSpec(wdw.shape, lambda i: (0, 0)),
                  pl.BlockSpec(vec.shape, lambda i: (0, 0))],
        out_specs=[row, row, pl.BlockSpec((tm, 2 * d), lambda i: (i, 0)), row],
        out_shape=[jax.ShapeDtypeStruct((t_len, d), F32), jax.ShapeDtypeStruct((t_len, d), F32),
                   jax.ShapeDtypeStruct((t_len, 2 * d), BF16), jax.ShapeDtypeStruct((t_len, d), F32)],
        scratch_shapes=[pltpu.VMEM((tm + CONV_HALO, d), F32)],
        compiler_params=_params(("arbitrary",)),
    )(x, pv, w1, w2, wdw, vec)


def _conv_bwd(dxo, x, ypre, a_sav, c1_sav, pv, w1, w2, wdw, vec, *, tm, ex=None):
    t_len, d = x.shape
    n_i = t_len // tm
    n_q = w1.shape[0] // 2
    qc = w1.shape[2]
    hb_ = tm // CONV_HALO

    def body(dxo_ref, x_ref, y_ref, a_ref, ah_ref, c1_ref, pv_ref, w1_ref, w2_ref, wdw_ref, vec_ref,
             dx_ref, pg_ref, gw1_ref, gw2_ref, gvec_ref, gwdw_ref, ubuf, dcbuf, dubuf, acc1, acc2):
        i = pl.program_id(0)
        ti = n_i - 1 - i

        @pl.when(i == 0)
        def _():
            pg_ref[...] = jnp.zeros_like(pg_ref)
            gvec_ref[...] = jnp.zeros_like(gvec_ref)
            gwdw_ref[...] = jnp.zeros_like(gwdw_ref)
            acc1[...] = jnp.zeros_like(acc1)
            acc2[...] = jnp.zeros_like(acc2)
            dcbuf[tm:tm + CONV_HALO, :] = jnp.zeros((CONV_HALO, d), F32)

        xv, pv_, dxo_v = x_ref[...], pv_ref[...], dxo_ref[...]
        dy = _post_bwd(dxo_v, y_ref[...], pv_, pg_ref)
        gvec_ref[CV_B2:CV_B2 + 1, 0:d] += _colsum(dy)
        dyb = dy.astype(BF16)
        xhat, rstd = _layer_norm_parts(c1_ref[...])
        lng = vec_ref[CV_LNG:CV_LNG + 1, 0:d]
        ln = xhat * lng + vec_ref[CV_LNB:CV_LNB + 1, 0:d]
        sg = _sigmoid(ln)
        acc2[...] += _dot_tn((ln * sg).astype(BF16), dyb)
        dln = _dot_nt(dyb, w2_ref[...]) * (sg * (1.0 + ln * (1.0 - sg)))
        gvec_ref[CV_LNG:CV_LNG + 1, 0:d] += _colsum(dln * xhat)
        gvec_ref[CV_LNB:CV_LNB + 1, 0:d] += _colsum(dln)
        dxh = dln * lng
        dc1 = rstd * (dxh - jnp.mean(dxh, axis=-1, keepdims=True)
                      - xhat * jnp.mean(dxh * xhat, axis=-1, keepdims=True))
        gvec_ref[CV_BDW:CV_BDW + 1, 0:d] += _colsum(dc1)
        dcbuf[0:tm, :] = dc1
        for q in range(n_q):
            cols = slice(q * qc, (q + 1) * qc)
            gcols = slice(d + q * qc, d + (q + 1) * qc)
            ubuf[CONV_HALO:CONV_HALO + tm, cols] = a_ref[:, cols].astype(F32) * _sigmoid(a_ref[:, gcols].astype(F32))
            uh = ah_ref[:, cols].astype(F32) * _sigmoid(ah_ref[:, gcols].astype(F32))
            ubuf[0:CONV_HALO, cols] = jnp.where(ti > 0, uh, 0.0)
        _depthwise_wgrad(dcbuf, ubuf, gwdw_ref, tm, d)
        _depthwise(dcbuf, wdw_ref, dubuf, tuple(CONV_TAPS - 1 - k for k in range(CONV_TAPS)), tm, d)
        dcbuf[tm:tm + CONV_HALO, :] = dcbuf[0:CONV_HALO, :]
        h, _, _ = _prenorm(xv, pv_)
        hb = h.astype(BF16)
        dh = jnp.zeros((tm, d), F32)
        for q in range(n_q):
            cols = slice(q * qc, (q + 1) * qc)
            gcols = slice(d + q * qc, d + (q + 1) * qc)
            du = dubuf[:, cols]
            val, gate = a_ref[:, cols].astype(F32), a_ref[:, gcols].astype(F32)
            sgg = _sigmoid(gate)
            dval = du * sgg
            dgate = du * val * (sgg * (1.0 - sgg))
            gvec_ref[CV_B1:CV_B1 + 1, cols] += _colsum(dval)
            gvec_ref[CV_B1:CV_B1 + 1, gcols] += _colsum(dgate)
            dvb, dgb = dval.astype(BF16), dgate.astype(BF16)
            acc1[q] += _dot_tn(hb, dvb)
            acc1[n_q + q] += _dot_tn(hb, dgb)
            dh = dh + _dot_nt(dvb, w1_ref[q]) + _dot_nt(dgb, w1_ref[n_q + q])
        dx_ref[...] = dxo_v + _pre_bwd(dh, xv, pv_, pg_ref)

        @pl.when(i == n_i - 1)
        def _():
            gw1_ref[...] = acc1[...].astype(BF16)
            gw2_ref[...] = acc2[...].astype(BF16)

    row = pl.BlockSpec((tm, d), lambda i: (n_i - 1 - i, 0))
    small = pl.BlockSpec((SUBLANES, d), lambda i: (0, 0))
    whole2 = lambda shape: pl.BlockSpec(shape, lambda i: (0, 0))
    return _launch(
        body, "conv_bwd", (n_i,),
        [row, row, row,
         pl.BlockSpec((tm, 2 * d), lambda i: (n_i - 1 - i, 0)),
         pl.BlockSpec((CONV_HALO, 2 * d), lambda i: (jnp.maximum((n_i - 1 - i) * hb_ - 1, 0), 0)),
         row, small, pl.BlockSpec(w1.shape, lambda i: (0, 0, 0)), whole2(w2.shape), whole2(wdw.shape),
         whole2(vec.shape)],
        [row, small, pl.BlockSpec(w1.shape, lambda i: (0, 0, 0)), whole2(w2.shape), whole2(vec.shape),
         whole2(wdw.shape)],
        [jax.ShapeDtypeStruct((t_len, d), F32), jax.ShapeDtypeStruct((SUBLANES, d), F32),
         jax.ShapeDtypeStruct(w1.shape, BF16), jax.ShapeDtypeStruct(w2.shape, BF16),
         jax.ShapeDtypeStruct(vec.shape, F32), jax.ShapeDtypeStruct(wdw.shape, F32)],
        [pltpu.VMEM((tm + CONV_HALO, d), F32), pltpu.VMEM((tm + CONV_HALO, d), F32),
         pltpu.VMEM((tm, d), F32), pltpu.VMEM(w1.shape, F32), pltpu.VMEM(w2.shape, F32)],
        (dxo, x, ypre, a_sav, a_sav, c1_sav, pv, w1, w2, wdw, vec), ex)


def _loss_head(y, target, *, tm):
    t_len, d = y.shape
    n_i = t_len // tm

    def body(y_ref, t_ref, dy_ref, sq_ref):
        @pl.when(pl.program_id(0) == 0)
        def _():
            sq_ref[...] = jnp.zeros_like(sq_ref)

        err = y_ref[...] - t_ref[...]
        dy_ref[...] = err * (1.0 / d)
        sq_ref[...] += jnp.sum((err * err).reshape(tm // SUBLANES, SUBLANES, d), axis=0)

    row = pl.BlockSpec((tm, d), lambda i: (i, 0))
    return pl.pallas_call(
        body, name="loss_head", grid=(n_i,), in_specs=[row, row],
        out_specs=[row, pl.BlockSpec((SUBLANES, d), lambda i: (0, 0))],
        out_shape=[jax.ShapeDtypeStruct((t_len, d), F32), jax.ShapeDtypeStruct((SUBLANES, d), F32)],
        compiler_params=_params(("arbitrary",)),
    )(y, target)


def _adam(name, parts, w, m, v):
    layers = len(parts)
    n, rows, cols = parts[0].shape
    tr = rows
    if rows % SUBLANES == 0:
        cap = max(SUBLANES, ADAM_BLOCK_BYTES // (4 * cols))
        tr = max(t for t in range(SUBLANES, rows + 1, SUBLANES) if rows % t == 0 and (t <= cap or t == SUBLANES))
    c1 = 1.0 / (1.0 - ADAM_B1 ** ADAM_STEP)
    c2 = 1.0 / (1.0 - ADAM_B2 ** ADAM_STEP)

    def body(*refs):
        p_refs = refs[:layers]
        w_ref, m_ref, v_ref, g_ref, d_ref, mo_ref, vo_ref, g_s = refs[layers:]
        for l in range(layers):
            @pl.when(pl.program_id(0) == l)
            def _():
                g = p_refs[l][0].astype(F32)
                for k in range(1, n):
                    g = g + p_refs[l][k].astype(F32)
                g_s[...] = g

        g = g_s[...]
        m2 = ADAM_B1 * m_ref[0] + (1.0 - ADAM_B1) * g
        v2 = ADAM_B2 * v_ref[0] + (1.0 - ADAM_B2) * (g * g)
        g_ref[0] = g
        mo_ref[0] = m2
        vo_ref[0] = v2
        d_ref[0] = -ADAM_LR * ((m2 * c1) / (jnp.sqrt(v2 * c2) + ADAM_EPS) + ADAM_WD * w_ref[0])

    def part_spec(l):
        return pl.BlockSpec((n, tr, cols), lambda ll, i: (0, jnp.where(ll == l, i, 0), 0))

    blk = pl.BlockSpec((1, tr, cols), lambda ll, i: (ll, i, 0))
    out = jax.ShapeDtypeStruct((layers, rows, cols), F32)
    return pl.pallas_call(
        body, name=name, grid=(layers, rows // tr),
        in_specs=[part_spec(l) for l in range(layers)] + [blk, blk, blk],
        out_specs=[blk, blk, blk, blk], out_shape=[out, out, out, out],
        scratch_shapes=[pltpu.VMEM((tr, cols), F32)],
        compiler_params=_params(("arbitrary", "arbitrary")),
    )(*parts, w, m, v)


def _adam_nd(name, parts, w, m, v):
    shape = w.shape
    cols = shape[-1]
    if isinstance(parts, (list, tuple)):
        layers = len(parts)
    else:
        layers, parts = 1, [parts]
    rows = w.size // (cols * layers)
    flat = lambda t: t.reshape(layers, rows, cols)
    outs = _adam(name, [p.reshape(p.shape[0], rows, cols) for p in parts], flat(w), flat(m), flat(v))
    return [o.reshape(shape) for o in outs]


def _small_pack(parts, size):
    flat = jnp.concatenate([p.reshape(-1) for p in parts])
    return jnp.pad(flat, (0, size - flat.shape[0]))


def _to_shards(full, axis):
    shp = full.shape
    split = full.reshape(shp[:axis] + (N_DEV, shp[axis] // N_DEV) + shp[axis + 1:])
    return jnp.moveaxis(split, axis, 0)


def _from_shards(sh, axis):
    moved = jnp.moveaxis(sh, 0, axis)
    shp = moved.shape
    return moved.reshape(shp[:axis] + (shp[axis] * shp[axis + 1],) + shp[axis + 2:])


def kernel(x, c, ada_w, ada_b, pre_g, post_g, pool_w, pool_scale, cv_w_pw1, cv_b_pw1, cv_w_dw, cv_b_dw, cv_ln_g, cv_ln_b, cv_w_pw2, cv_b_pw2, ffn_w_up, ffn_w_dw, ffn_w_down, loss_target, m_ada_w, m_ada_b, m_pre_g, m_post_g, m_pool_w, m_pool_scale, m_cv_w_pw1, m_cv_b_pw1, m_cv_w_dw, m_cv_b_dw, m_cv_ln_g, m_cv_ln_b, m_cv_w_pw2, m_cv_b_pw2, m_ffn_w_up, m_ffn_w_dw, m_ffn_w_down, v_ada_w, v_ada_b, v_pre_g, v_post_g, v_pool_w, v_pool_scale, v_cv_w_pw1, v_cv_b_pw1, v_cv_w_dw, v_cv_b_dw, v_cv_ln_g, v_cv_ln_b, v_cv_w_pw2, v_cv_b_pw2, v_ffn_w_up, v_ffn_w_dw, v_ffn_w_down):
    t_len, d = x.shape[1], x.shape[2]
    depth = ada_w.shape[0]
    fc = ffn_w_up.shape[2]
    n_j = N_DEV // 2
    me = 4 * lax.axis_index("x") + 2 * lax.axis_index("y") + lax.axis_index("c")

    small_w = [pre_g, post_g, cv_b_pw1, cv_w_dw, cv_b_dw, cv_ln_g, cv_ln_b, cv_b_pw2, ffn_w_dw]
    small_m = [m_pre_g, m_post_g, m_cv_b_pw1, m_cv_w_dw, m_cv_b_dw, m_cv_ln_g, m_cv_ln_b, m_cv_b_pw2, m_ffn_w_dw]
    small_v = [v_pre_g, v_post_g, v_cv_b_pw1, v_cv_w_dw, v_cv_b_dw, v_cv_ln_g, v_cv_ln_b, v_cv_b_pw2, v_ffn_w_dw]
    sizes = [p.size for p in small_w]
    offs = [sum(sizes[:k]) for k in range(len(sizes) + 1)]
    pack = -(-offs[-1] // (SUBLANES * LANES)) * SUBLANES * LANES

    got = _exchange("gather_small", [c, _small_pack(small_w, pack), pool_w[0].astype(BF16)], [True] * 3)
    c_all = got[0].reshape(N_DEV, d)
    smalls = [got[1][:, offs[k]:offs[k + 1]].reshape((N_DEV,) + small_w[k].shape) for k in range(len(small_w))]
    pre_g_f, post_g_f = _from_shards(smalls[0], 2), _from_shards(smalls[1], 2)
    b1_f = _from_shards(smalls[2], 1)[0]
    cvw_f = jnp.pad(_from_shards(smalls[3], 2)[0], ((0, CONV_HALO - CONV_TAPS), (0, 0)))
    bdw_f, lng_f, lnb_f, b2_f = [_from_shards(smalls[k], 1)[0] for k in (4, 5, 6, 7)]
    fdw = jnp.pad(smalls[8], ((0, 0), (0, 0), (0, SUBLANES - FFN_TAPS), (0, 0)))
    wp = jnp.swapaxes(got[2], 0, 1).reshape(pool_w.shape[1], -1, pool_w.shape[3])
    wdw = [fdw[:, l].reshape(2, n_j, SUBLANES, fc) for l in range(depth)]
    cvec = jnp.zeros((SUBLANES, 2 * d), F32)
    cvec = cvec.at[CV_B1].set(b1_f)
    for r, vrow in ((CV_BDW, bdw_f), (CV_LNG, lng_f), (CV_LNB, lnb_f), (CV_B2, b2_f)):
        cvec = cvec.at[r, :d].set(vrow)

    mod_cols = _ada_fwd(c_all, ada_w)
    (mod_all,) = _exchange("gather_mod", [mod_cols], [True])
    mod = lax.dynamic_index_in_dim(mod_all, me, axis=2, keepdims=False)
    mod = jnp.swapaxes(mod, 0, 1).reshape(depth, N_MOD, d) + ada_b.reshape(depth, N_MOD, d)

    def pv_of(l, s):
        rows = [pre_g_f[l, s], 1.0 + mod[l, 3 * s + 1], mod[l, 3 * s], post_g_f[l, s], mod[l, 3 * s + 2]]
        return jnp.concatenate([jnp.stack(rows), jnp.zeros((SUBLANES - len(rows), d), F32)])

    x0 = x[0]
    pv00, pv01, pv10, pv11 = pv_of(0, 0), pv_of(0, 1), pv_of(1, 0), pv_of(1, 1)
    tm_pool, tm_ffn, tm_bwd, tm_conv = min(TM_POOL, t_len), min(TM_FFN, t_len), min(TM_FFN_BWD, t_len), min(TM_CONV, t_len)
    tm_w = min(TM_FFN_W, t_len)
    ex = _Exchange([ffn_w_up[0].astype(BF16), ffn_w_down[0].astype(BF16)], [True, True])
    (x1, y0), (wup0, wdn0) = _pool_fwd(x0, pv00, wp, pool_scale, tm=tm_pool, ex=ex)
    ex = _Exchange([cv_w_pw1[0].astype(BF16), cv_w_pw2[0].astype(BF16), ffn_w_up[1].astype(BF16),
                    ffn_w_down[1].astype(BF16)], [True] * 4)
    (x2, y1, h1, a1), (w1, w2, wup1, wdn1) = _ffn_fwd(
        x1, pv01, wup0.reshape(2, n_j, d, fc), wdw[0], wdn0.reshape(n_j * fc, d), tm=tm_ffn, ex=ex)
    w2 = w2.reshape(d, d)
    wup = [wup0.reshape(2, n_j, d, fc), wup1.reshape(2, n_j, d, fc)]
    wdn = [wdn0.reshape(n_j * fc, d), wdn1.reshape(n_j * fc, d)]
    x3, y2, a2, c2 = _conv_fwd(x2, pv10, w1, w2, cvw_f, cvec, tm=tm_conv)
    (x4, y3, h3, a3), _ = _ffn_fwd(x3, pv11, wup[1], wdw[1], wdn[1], tm=tm_ffn)
    dx4, sq = _loss_head(x4, loss_target[0], tm=min(TM_LOSS, t_len))
    loss = lax.psum(jnp.sum(sq) * (0.5 / d), MESH_AXES)

    (dx3, da3, u3, dy3, pg11, gfdw1), _ = _ffn_bwd_a(dx4, x3, y3, a3, pv11, wup[1], wdw[1], wdn[1], tm=tm_bwd)
    (gup1,), _ = _ffn_bwd_wup(h3, da3, tm=tm_w)
    (gdn1,), _ = _ffn_bwd_wdn(u3, dy3, tm=tm_w)
    ex = _Exchange([gup1.reshape(N_DEV, d, fc), gdn1.reshape(N_DEV, -1, d)], [False, False])
    (dx2, pg10, gw1, gw2, gcvec, gcvw), (rup1, rdn1) = _conv_bwd(
        dx3, x2, y2, a2, c2, pv10, w1, w2, cvw_f, cvec, tm=tm_conv, ex=ex)
    ex = _Exchange([gw1, gw2.reshape(N_DEV, -1, d)], [False, False])
    (dx1, da1, u1, dy1, pg01, gfdw0), (rw1, rw2) = _ffn_bwd_a(
        dx2, x1, y1, a1, pv01, wup[0], wdw[0], wdn[0], tm=tm_bwd, ex=ex)
    (gdn0,), _ = _ffn_bwd_wdn(u1, dy1, tm=tm_w)
    ex = _Exchange([gdn0.reshape(N_DEV, -1, d)], [False])
    (gup0,), (rdn0,) = _ffn_bwd_wup(h1, da1, tm=tm_w, ex=ex)
    ex = _Exchange([gup0.reshape(N_DEV, d, fc)], [False])
    (dx0, pg00, gwp), (rup0,) = _pool_bwd(dx1, x0, y0, pv00, wp, pool_scale, tm=tm_pool, ex=ex)

    pgs = [[pg00, pg01], [pg10, pg11]]
    g_pre = jnp.stack([jnp.stack([pgs[l][s][PG_GPRE] for s in range(2)]) for l in range(depth)])
    g_post = jnp.stack([jnp.stack([pgs[l][s][PG_GPOST] for s in range(2)]) for l in range(depth)])
    dmod = jnp.stack([jnp.concatenate([pgs[l][s][r] for s in range(2) for r in (PG_SH, PG_SC, PG_GT)])
                      for l in range(depth)])
    gfdw = jnp.stack([g.reshape(N_DEV, SUBLANES, fc)[:, :FFN_TAPS] for g in (gfdw0, gfdw1)], axis=1)
    small_g = [_to_shards(g_pre, 2), _to_shards(g_post, 2), _to_shards(gcvec[CV_B1][None], 1),
               _to_shards(gcvw[None, :CONV_TAPS], 2), _to_shards(gcvec[CV_BDW, :d][None], 1),
               _to_shards(gcvec[CV_LNG, :d][None], 1), _to_shards(gcvec[CV_LNB, :d][None], 1),
               _to_shards(gcvec[CV_B2, :d][None], 1), gfdw]
    small_send = jnp.concatenate([g.reshape(N_DEV, -1) for g in small_g], axis=1)
    small_send = jnp.pad(small_send, ((0, 0), (0, pack - small_send.shape[1])))
    gwp_send = jnp.swapaxes(gwp.reshape(gwp.shape[0], N_DEV, -1, gwp.shape[2]), 0, 1)
    rsmall, rwp, rmod, rscale = _exchange("scatter_small", [small_send, gwp_send, dmod, pg00[PG_EXTRA][None]],
                                          [False, False, True, True])

    outs = {}

    def put(name, res, shape=None):
        outs[name] = [r if shape is None else r.reshape(shape) for r in res]

    small_res = _adam_nd("adam_small", rsmall.reshape(N_DEV, -1, SUBLANES * LANES),
                         _small_pack(small_w, pack).reshape(-1, SUBLANES * LANES),
                         _small_pack(small_m, pack).reshape(-1, SUBLANES * LANES),
                         _small_pack(small_v, pack).reshape(-1, SUBLANES * LANES))
    small_names = ["pre_g", "post_g", "cv_b_pw1", "cv_w_dw", "cv_b_dw", "cv_ln_g", "cv_ln_b", "cv_b_pw2", "ffn_w_dw"]
    for k, nm in enumerate(small_names):
        outs[nm] = [r.reshape(-1)[offs[k]:offs[k + 1]].reshape(small_w[k].shape) for r in small_res]
    put("pool_w", _adam_nd("adam_pool_w", rwp[:, None], pool_w, m_pool_w, v_pool_w))
    put("cv_w_pw1", _adam_nd("adam_cv_w_pw1", rw1[:, None], cv_w_pw1, m_cv_w_pw1, v_cv_w_pw1))
    put("cv_w_pw2", _adam_nd("adam_cv_w_pw2", rw2[:, None], cv_w_pw2, m_cv_w_pw2, v_cv_w_pw2))
    put("ffn_w_up", _adam_nd("adam_ffn_w_up", [rup0, rup1], ffn_w_up, m_ffn_w_up, v_ffn_w_up))
    put("ffn_w_down", _adam_nd("adam_ffn_w_down", [rdn0, rdn1], ffn_w_down, m_ffn_w_down, v_ffn_w_down))
    put("ada_b", _adam_nd("adam_ada_b", rmod, ada_b, m_ada_b, v_ada_b))
    put("pool_scale", _adam_nd("adam_pool_scale", rscale, pool_scale, m_pool_scale, v_pool_scale))
    cols = ada_w.shape[2]
    dmod_cols = jnp.swapaxes(lax.dynamic_slice_in_dim(rmod, me * cols, cols, axis=2), 0, 1)
    put("ada_w", _adam_nd("adam_ada_w", _ada_bwd(c_all, dmod_cols)[None], ada_w, m_ada_w, v_ada_w))

    order = ["ada_w", "ada_b", "pre_g", "post_g", "pool_w", "pool_scale", "cv_w_pw1", "cv_b_pw1", "cv_w_dw", "cv_b_dw",
             "cv_ln_g", "cv_ln_b", "cv_w_pw2", "cv_b_pw2", "ffn_w_up", "ffn_w_dw", "ffn_w_down"]
    return (loss, dx0[None], *[outs[nm][0] for nm in order], *[outs[nm][1] for nm in order],
            *[outs[nm][2] for nm in order], *[outs[nm][3] for nm in order])
```

```python
import functools

import jax
import jax.numpy as jnp
from jax import lax
from jax.experimental import pallas as pl
from jax.experimental.pallas import tpu as pltpu

F32, BF16 = jnp.float32, jnp.bfloat16
MESH_AXES = ("x", "y", "c")
N_DEV = 8
NORM_EPS = 1e-6
ADAM_LR, ADAM_B1, ADAM_B2, ADAM_EPS, ADAM_WD, ADAM_STEP = 0.001, 0.9, 0.999, 1e-08, 0.01, 10
POOL_WINDOWS = (2, 4, 8, 16)
CONV_TAPS = 31
FFN_TAPS = 3
N_MOD = 6

SUBLANES = 8
LANES = 128
VMEM_LIMIT_BYTES = 56 * 1024 * 1024
POOL_HALO = 16
CONV_HALO = 32
FFN_HALO = 8
TM_POOL, TM_FFN, TM_FFN_BWD, TM_FFN_W, TM_CONV, TM_LOSS = 512, 512, 256, 1024, 256, 1024
CONV_ROWS = 32
ADAM_BLOCK_BYTES = 512 * 1024

PV_GPRE, PV_SC1, PV_SH, PV_GPOST, PV_GT = 0, 1, 2, 3, 4
PG_GPRE, PG_SC, PG_SH, PG_GPOST, PG_GT, PG_EXTRA = 0, 1, 2, 3, 4, 5


def _params(sem):
    return pltpu.CompilerParams(dimension_semantics=sem, vmem_limit_bytes=VMEM_LIMIT_BYTES)


def _dot(a, b):
    return jnp.dot(a, b, preferred_element_type=F32)


def _dot_nt(a, b):
    return lax.dot_general(a, b, (((1,), (1,)), ((), ())), preferred_element_type=F32)


def _dot_tn(a, b):
    return lax.dot_general(a, b, (((0,), (0,)), ((), ())), preferred_element_type=F32)


def _sigmoid(x):
    return 1.0 / (1.0 + jnp.exp(-x))


def _rms(x):
    return lax.rsqrt(jnp.mean(x * x, axis=-1, keepdims=True) + NORM_EPS)


def _colsum(v):
    return jnp.sum(v, axis=0, keepdims=True)


def _prenorm(x, pv):
    r = _rms(x)
    xn = x * r
    return xn * (pv[PV_GPRE:PV_GPRE + 1] * pv[PV_SC1:PV_SC1 + 1]) + pv[PV_SH:PV_SH + 1], xn, r


def _post(x, y, pv):
    return x + (pv[PV_GT:PV_GT + 1] * pv[PV_GPOST:PV_GPOST + 1]) * (y * _rms(y))


def _post_bwd(dxo, y, pv, pg_ref):
    ry = _rms(y)
    yn = y * ry
    gt, gpost = pv[PV_GT:PV_GT + 1], pv[PV_GPOST:PV_GPOST + 1]
    dyn = dxo * (gt * gpost)
    dy = ry * (dyn - yn * jnp.mean(dyn * yn, axis=-1, keepdims=True))
    s = _colsum(dxo * yn)
    pg_ref[PG_GPOST:PG_GPOST + 1, :] += s * gt
    pg_ref[PG_GT:PG_GT + 1, :] += s * gpost
    return dy


def _pre_bwd(dh, x, pv, pg_ref):
    r = _rms(x)
    xn = x * r
    gpre, sc1 = pv[PV_GPRE:PV_GPRE + 1], pv[PV_SC1:PV_SC1 + 1]
    dxn = dh * (gpre * sc1)
    dx = r * (dxn - xn * jnp.mean(dxn * xn, axis=-1, keepdims=True))
    p = _colsum(dh * xn)
    pg_ref[PG_GPRE:PG_GPRE + 1, :] += p * sc1
    pg_ref[PG_SC:PG_SC + 1, :] += p * gpre
    pg_ref[PG_SH:PG_SH + 1, :] += _colsum(dh)
    return dx


def _shift_down(a, k, prev):
    out = pltpu.roll(a, k, 0)
    row = lax.broadcasted_iota(jnp.int32, a.shape, 0)
    for q in range(k):
        out = jnp.where(row == q, prev[SUBLANES - k + q:SUBLANES - k + q + 1, :], out)
    return out


def _shift_up(a, k, nxt):
    rows = a.shape[0]
    out = pltpu.roll(a, rows - k, 0)
    row = lax.broadcasted_iota(jnp.int32, a.shape, 0)
    for q in range(k):
        out = jnp.where(row == rows - k + q, nxt[q:q + 1, :], out)
    return out


class _Exchange:
    def __init__(self, srcs, gathers):
        self.srcs, self.gathers, self.n = list(srcs), list(gathers), len(srcs)
        self.out_shape = [jax.ShapeDtypeStruct(((N_DEV,) + s.shape) if g else s.shape, s.dtype)
                          for s, g in zip(srcs, gathers)]
        self.specs = [pl.BlockSpec(memory_space=pl.ANY)] * self.n
        self.scratch = [pltpu.SemaphoreType.DMA((self.n, N_DEV - 1)), pltpu.SemaphoreType.DMA((self.n, N_DEV - 1)),
                        pltpu.SemaphoreType.DMA((self.n,))]

    def _copies(self, src_refs, out_refs, sems):
        send_sems, recv_sems, local_sems = sems
        x, y, c = lax.axis_index("x"), lax.axis_index("y"), lax.axis_index("c")
        me = 4 * x + 2 * y + c
        copies = []
        for a in range(self.n):
            mine = src_refs[a] if self.gathers[a] else src_refs[a].at[me]
            copies.append(pltpu.make_async_copy(mine, out_refs[a].at[me], local_sems.at[a]))
        for d in range(1, N_DEV):
            px, py, pc = (x + (d >> 2)) % 2, (y + ((d >> 1) & 1)) % 2, (c + (d & 1)) % 2
            peer = 4 * px + 2 * py + pc
            for a in range(self.n):
                src = src_refs[a] if self.gathers[a] else src_refs[a].at[peer]
                copies.append(pltpu.make_async_remote_copy(
                    src_ref=src, dst_ref=out_refs[a].at[me], send_sem=send_sems.at[a, d - 1],
                    recv_sem=recv_sems.at[a, d - 1], device_id=(px, py, pc), device_id_type=pl.DeviceIdType.MESH))
        return copies

    def start(self, src_refs, out_refs, sems):
        for cp in self._copies(src_refs, out_refs, sems):
            cp.start()

    def wait(self, src_refs, out_refs, sems):
        for cp in self._copies(src_refs, out_refs, sems):
            cp.wait()


class _TwoLevelGather:
    def __init__(self, srcs):
        self.srcs, self.n = list(srcs), len(srcs)
        self.out_shape = [jax.ShapeDtypeStruct((N_DEV,) + s.shape, s.dtype) for s in srcs]
        self.specs = [pl.BlockSpec(memory_space=pl.ANY)] * self.n
        self.scratch = [pltpu.SemaphoreType.DMA((self.n, N_DEV - 1)), pltpu.SemaphoreType.DMA((self.n, N_DEV - 1)),
                        pltpu.SemaphoreType.DMA((self.n,))]

    def _places(self):
        x, y, c = lax.axis_index("x"), lax.axis_index("y"), lax.axis_index("c")
        return (x, y, c), (x, y, 1 - c), [(1 - x, y), (x, 1 - y), (1 - x, 1 - y)]

    def _copy(self, a, k, block, to, src_refs, out_refs, sems, from_src=False):
        slot = 4 * block[0] + 2 * block[1] + block[2]
        return pltpu.make_async_remote_copy(
            src_ref=src_refs[a] if from_src else out_refs[a].at[slot], dst_ref=out_refs[a].at[slot],
            send_sem=sems[0].at[a, k], recv_sem=sems[1].at[a, k], device_id=to, device_id_type=pl.DeviceIdType.MESH)

    def _local(self, a, src_refs, out_refs, sems):
        me, _, _ = self._places()
        return pltpu.make_async_copy(src_refs[a], out_refs[a].at[4 * me[0] + 2 * me[1] + me[2]], sems[2].at[a])

    def start(self, src_refs, out_refs, sems):
        me, sibling, chips = self._places()
        for a in range(self.n):
            self._local(a, src_refs, out_refs, sems).start()
            self._copy(a, 0, me, sibling, src_refs, out_refs, sems, from_src=True).start()
        for j, chip in enumerate(chips):
            for a in range(self.n):
                self._copy(a, 1 + j, me, (*chip, me[2]), src_refs, out_refs, sems, from_src=True).start()

    def wait(self, src_refs, out_refs, sems):
        me, sibling, chips = self._places()
        for j, chip in enumerate(chips):
            for a in range(self.n):
                self._copy(a, 1 + j, (*chip, me[2]), me, src_refs, out_refs, sems).wait_recv()
                self._copy(a, 4 + j, (*chip, me[2]), sibling, src_refs, out_refs, sems).start()
        for a in range(self.n):
            self._copy(a, 0, sibling, me, src_refs, out_refs, sems).wait_recv()
            for j, chip in enumerate(chips):
                self._copy(a, 4 + j, (*chip, sibling[2]), me, src_refs, out_refs, sems).wait_recv()
        for a in range(self.n):
            self._copy(a, 0, me, sibling, src_refs, out_refs, sems, from_src=True).wait_send()
            for j, chip in enumerate(chips):
                self._copy(a, 1 + j, me, (*chip, me[2]), src_refs, out_refs, sems, from_src=True).wait_send()
                self._copy(a, 4 + j, (*chip, me[2]), sibling, src_refs, out_refs, sems).wait_send()
            self._local(a, src_refs, out_refs, sems).wait()


def _exchange(name, srcs, gathers):
    ex = _Exchange(srcs, gathers)

    def body(*refs):
        src_refs, out_refs, sems = refs[:ex.n], refs[ex.n:2 * ex.n], refs[2 * ex.n:]
        ex.start(src_refs, out_refs, sems)
        ex.wait(src_refs, out_refs, sems)

    return pl.pallas_call(body, name=name, out_shape=ex.out_shape, in_specs=ex.specs, out_specs=ex.specs,
                          scratch_shapes=ex.scratch)(*srcs)


def _launch(body, name, grid, in_specs, out_specs, out_shape, scratch_shapes, args, ex=None):
    sem = ("arbitrary",) * len(grid)
    if ex is None:
        outs = pl.pallas_call(body, name=name, grid=grid, in_specs=in_specs, out_specs=out_specs, out_shape=out_shape,
                              scratch_shapes=scratch_shapes, compiler_params=_params(sem))(*args)
        return outs, []
    n_in, n_out, n_scr = len(in_specs), len(out_specs), len(scratch_shapes)

    def hosted(*refs):
        ins, ex_in = refs[:n_in], refs[n_in:n_in + ex.n]
        outs = refs[n_in + ex.n:n_in + ex.n + n_out]
        ex_out = refs[n_in + ex.n + n_out:n_in + 2 * ex.n + n_out]
        rest = refs[n_in + 2 * ex.n + n_out:]
        scratch, sems = rest[:n_scr], rest[n_scr:]
        ids = [pl.program_id(a) for a in range(len(grid))]
        first = functools.reduce(lambda p, q: p & q, [i == 0 for i in ids])
        last = functools.reduce(lambda p, q: p & q, [i == g - 1 for i, g in zip(ids, grid)])

        @pl.when(first)
        def _():
            ex.start(ex_in, ex_out, sems)

        body(*ins, *outs, *scratch)

        @pl.when(last)
        def _():
            ex.wait(ex_in, ex_out, sems)

    outs = pl.pallas_call(
        hosted, name=name, grid=grid, in_specs=list(in_specs) + ex.specs, out_specs=list(out_specs) + ex.specs,
        out_shape=list(out_shape) + ex.out_shape, scratch_shapes=list(scratch_shapes) + ex.scratch,
        compiler_params=_params(sem))(*args, *ex.srcs)
    return outs[:n_out], outs[n_out:]


def _ada_fwd(c_all, ada_w):
    layers, d, cols = ada_w.shape

    def body(c_ref, w_ref, o_ref):
        c = c_ref[...]
        ca = (c * _sigmoid(c)).astype(BF16)
        for l in range(layers):
            o_ref[l] = _dot(ca, w_ref[l].astype(BF16))

    return pl.pallas_call(
        body, name="ada_fwd", out_shape=jax.ShapeDtypeStruct((layers, N_DEV, cols), F32),
        compiler_params=pltpu.CompilerParams(vmem_limit_bytes=VMEM_LIMIT_BYTES),
    )(c_all, ada_w)


def _ada_bwd(c_all, dmod_cols):
    layers, _, cols = dmod_cols.shape
    d = c_all.shape[1]

    def body(c_ref, g_ref, o_ref):
        c = c_ref[...]
        ca = (c * _sigmoid(c)).astype(BF16)
        for l in range(layers):
            o_ref[l] = _dot_tn(ca, g_ref[l].astype(BF16))

    return pl.pallas_call(
        body, name="ada_bwd", out_shape=jax.ShapeDtypeStruct((layers, d, cols), F32),
        compiler_params=pltpu.CompilerParams(vmem_limit_bytes=VMEM_LIMIT_BYTES),
    )(c_all, dmod_cols)


def _pooled(hbuf, h, t0, g, tm):
    gd = h.shape[1] // len(POOL_WINDOWS)
    cols = slice(g * gd, (g + 1) * gd)
    w = POOL_WINDOWS[g]
    hg = h[:, cols]
    s = hg
    for k in range(1, w):
        s = s + hbuf[POOL_HALO - k:POOL_HALO - k + tm, cols]
    t = t0 + lax.broadcasted_iota(jnp.int32, (tm, 1), 0)
    cnt = jnp.minimum(t + 1, w).astype(F32)
    return s / cnt - hg, cnt


def _pool_fwd(x, pv, wp, scale, *, tm, ex=None):
    t_len, d = x.shape
    n_i = t_len // tm
    gd = d // len(POOL_WINDOWS)

    def body(x_ref, pv_ref, wp_ref, sc_ref, xo_ref, y_ref, hbuf):
        i = pl.program_id(0)

        @pl.when(i == 0)
        def _():
            hbuf[0:POOL_HALO, :] = jnp.zeros((POOL_HALO, d), F32)

        xv, pv_ = x_ref[...], pv_ref[...]
        h, _, _ = _prenorm(xv, pv_)
        hbuf[POOL_HALO:POOL_HALO + tm, :] = h
        for g in range(len(POOL_WINDOWS)):
            pooled, _ = _pooled(hbuf, h, i * tm, g, tm)
            y_ref[:, g * gd:(g + 1) * gd] = _dot(pooled.astype(BF16), wp_ref[g])
        xo_ref[...] = _post(xv, y_ref[...] * sc_ref[...], pv_)
        hbuf[0:POOL_HALO, :] = hbuf[tm:tm + POOL_HALO, :]

    row = pl.BlockSpec((tm, d), lambda i: (i, 0))
    return _launch(
        body, "pool_fwd", (n_i,),
        [row, pl.BlockSpec((SUBLANES, d), lambda i: (0, 0)), pl.BlockSpec(wp.shape, lambda i: (0, 0, 0)),
         pl.BlockSpec((1, d), lambda i: (0, 0))],
        [row, row],
        [jax.ShapeDtypeStruct((t_len, d), F32), jax.ShapeDtypeStruct((t_len, d), F32)],
        [pltpu.VMEM((tm + POOL_HALO, d), F32)],
        (x, pv, wp, scale), ex)


def _pool_bwd(dxo, x, ypre, pv, wp, scale, *, tm, ex=None):
    t_len, d = x.shape
    n_i = t_len // tm
    gd = d // len(POOL_WINDOWS)
    hb = tm // POOL_HALO

    def body(dxo_ref, x_ref, xh_ref, y_ref, pv_ref, wp_ref, sc_ref, dx_ref, pg_ref, dwp_ref, hbuf, qbuf):
        i = pl.program_id(0)
        ti = n_i - 1 - i

        @pl.when(i == 0)
        def _():
            pg_ref[...] = jnp.zeros_like(pg_ref)
            dwp_ref[...] = jnp.zeros_like(dwp_ref)
            qbuf[tm:tm + POOL_HALO, :] = jnp.zeros((POOL_HALO, d), F32)

        xv, pv_, dxo_v, yp, sc = x_ref[...], pv_ref[...], dxo_ref[...], y_ref[...], sc_ref[...]
        dy = _post_bwd(dxo_v, yp * sc, pv_, pg_ref)
        pg_ref[PG_EXTRA:PG_EXTRA + 1, :] += _colsum(dy * yp)
        dys = dy * sc
        h, _, _ = _prenorm(xv, pv_)
        hh, _, _ = _prenorm(xh_ref[...], pv_)
        hbuf[0:POOL_HALO, :] = jnp.where(ti > 0, hh, 0.0)
        hbuf[POOL_HALO:POOL_HALO + tm, :] = h
        for g in range(len(POOL_WINDOWS)):
            cols = slice(g * gd, (g + 1) * gd)
            pooled, cnt = _pooled(hbuf, h, ti * tm, g, tm)
            dyg = dys[:, cols].astype(BF16)
            dwp_ref[g] += _dot_tn(pooled.astype(BF16), dyg)
            dp = _dot_nt(dyg, wp_ref[g])
            qbuf[0:tm, cols] = dp / cnt
            dh = -dp
            for k in range(POOL_WINDOWS[g]):
                dh = dh + qbuf[k:k + tm, cols]
            hbuf[POOL_HALO:POOL_HALO + tm, cols] = dh
        dx_ref[...] = dxo_v + _pre_bwd(hbuf[POOL_HALO:POOL_HALO + tm, :], xv, pv_, pg_ref)
        qbuf[tm:tm + POOL_HALO, :] = qbuf[0:POOL_HALO, :]

    row = pl.BlockSpec((tm, d), lambda i: (n_i - 1 - i, 0))
    halo = pl.BlockSpec((POOL_HALO, d), lambda i: (jnp.maximum((n_i - 1 - i) * hb - 1, 0), 0))
    small = pl.BlockSpec((SUBLANES, d), lambda i: (0, 0))
    return _launch(
        body, "pool_bwd", (n_i,),
        [row, row, halo, row, small, pl.BlockSpec(wp.shape, lambda i: (0, 0, 0)), pl.BlockSpec((1, d), lambda i: (0, 0))],
        [row, small, pl.BlockSpec(wp.shape, lambda i: (0, 0, 0))],
        [jax.ShapeDtypeStruct((t_len, d), F32), jax.ShapeDtypeStruct((SUBLANES, d), F32),
         jax.ShapeDtypeStruct(wp.shape, F32)],
        [pltpu.VMEM((tm + POOL_HALO, d), F32), pltpu.VMEM((tm + POOL_HALO, d), F32)],
        (dxo, x, x, ypre, pv, wp, scale), ex)


def _ffn_conv(a, prev, w):
    return w[2:3] * a + w[1:2] * _shift_down(a, 1, prev) + w[0:1] * _shift_down(a, 2, prev)


def _ffn_fwd(x, pv, wup, wdw, wdn, *, tm, ex=None):
    t_len, d = x.shape
    _, n_j, _, fc = wup.shape
    n_i = t_len // tm

    def body(x_ref, pv_ref, wup_ref, wdw_ref, wdn_ref, xo_ref, y_ref, h_ref, a_ref, c_ref, h_s, yacc, carry):
        i, j = pl.program_id(0), pl.program_id(1)

        @pl.when(j == 0)
        def _():
            h, _, _ = _prenorm(x_ref[...], pv_ref[...])
            hb = h.astype(BF16)
            h_s[...] = hb
            h_ref[...] = hb
            yacc[...] = jnp.zeros_like(yacc)

        @pl.when((i == 0) & (j == 0))
        def _():
            carry[...] = jnp.zeros_like(carry)

        hb = h_s[...]
        conv = []
        for s in range(2):
            ab = _dot(hb, wup_ref[s, 0]).astype(BF16)
            a_ref[s, 0] = ab
            a = ab.astype(F32)
            cv = _ffn_conv(a, carry[s, j], wdw_ref[s, 0])
            c_ref[s, 0] = cv.astype(BF16)
            conv.append(cv)
            carry[s, j] = a[tm - FFN_HALO:tm, :]
        g, v = conv
        u = g * _sigmoid(g) * v
        yacc[...] += _dot(u.astype(BF16), wdn_ref[...])

        @pl.when(j == n_j - 1)
        def _():
            y = yacc[...]
            y_ref[...] = y
            xo_ref[...] = _post(x_ref[...], y, pv_ref[...])

    row = pl.BlockSpec((tm, d), lambda i, j: (i, 0))
    return _launch(
        body, "ffn_fwd", (n_i, n_j),
        [row, pl.BlockSpec((SUBLANES, d), lambda i, j: (0, 0)),
         pl.BlockSpec((2, 1, d, fc), lambda i, j: (0, j, 0, 0)),
         pl.BlockSpec((2, 1, SUBLANES, fc), lambda i, j: (0, j, 0, 0)),
         pl.BlockSpec((fc, d), lambda i, j: (j, 0))],
        [row, row, row, pl.BlockSpec((2, 1, tm, fc), lambda i, j: (0, j, i, 0)),
         pl.BlockSpec((2, 1, tm, fc), lambda i, j: (0, j, i, 0))],
        [jax.ShapeDtypeStruct((t_len, d), F32), jax.ShapeDtypeStruct((t_len, d), F32),
         jax.ShapeDtypeStruct((t_len, d), BF16), jax.ShapeDtypeStruct((2, n_j, t_len, fc), BF16),
         jax.ShapeDtypeStruct((2, n_j, t_len, fc), BF16)],
        [pltpu.VMEM((tm, d), BF16), pltpu.VMEM((tm, d), F32), pltpu.VMEM((2, n_j, FFN_HALO, fc), F32)],
        (x, pv, wup, wdw, wdn), ex)


def _ffn_bwd(dy, h, a_sav, c_sav, wup, wdw, wdn, *, tm, ex=None):
    t_len, d = dy.shape
    _, n_j, _, fc = wup.shape
    n_i = t_len // tm

    def cur(i):
        return n_i - 1 - jnp.minimum(i, n_i - 1)

    def lag(i):
        return jnp.minimum(n_i - i, n_i - 1)

    def body(dyc_ref, dyl_ref, hl_ref, a_ref, c_ref, wup_ref, wdw_ref, wdn_ref,
             dhp_ref, gup_ref, gdn_ref, dwdw_ref, da_0, da_1, u_0, u_1, acc_up, acc_dn, carry):
        j, i = pl.program_id(0), pl.program_id(1)

        @pl.when((j == 0) & (i == 0))
        def _():
            dwdw_ref[...] = jnp.zeros_like(dwdw_ref)

        @pl.when(i == 0)
        def _():
            acc_up[...] = jnp.zeros_like(acc_up)
            acc_dn[...] = jnp.zeros_like(acc_dn)
            carry[...] = jnp.zeros_like(carry)
            da_1[...] = jnp.zeros_like(da_1)
            u_1[...] = jnp.zeros_like(u_1)

        def step(da_w, u_w, da_r, u_r):
            hl = hl_ref[...]
            live = (i < n_i).astype(F32)
            du = _dot_nt(dyc_ref[...], wdn_ref[...])
            g, v = c_ref[0, 0].astype(F32), c_ref[1, 0].astype(F32)
            sg = _sigmoid(g)
            sl = g * sg
            u_w[...] = (sl * v).astype(BF16)
            dh0 = _dot_nt(da_r[0], wup_ref[0, 0])
            d2 = (du * v * (sg * (1.0 + g * (1.0 - sg))), du * sl)
            dhp_ref[0] = (dh0 + _dot_nt(da_r[1], wup_ref[1, 0])).astype(BF16)
            for s in range(2):
                w = wdw_ref[s, 0]
                nxt = carry[s]
                p1, p2 = _shift_up(d2[s], 1, nxt), _shift_up(d2[s], 2, nxt)
                carry[s] = d2[s][0:FFN_HALO, :]
                da_w[s] = (w[2:3] * d2[s] + w[1:2] * p1 + w[0:1] * p2).astype(BF16)
                acc_up[s] += _dot_tn(hl, da_r[s])
                a_s = a_ref[s, 0].astype(F32)
                for k, sh in ((2, d2[s]), (1, p1), (0, p2)):
                    dwdw_ref[s, j, k:k + 1, :] += live * _colsum(a_s * sh)
            acc_dn[...] += _dot_tn(u_r[...], dyl_ref[...])

        @pl.when(i % 2 == 0)
        def _():
            step(da_0, u_0, da_1, u_1)

        @pl.when(i % 2 == 1)
        def _():
            step(da_1, u_1, da_0, u_0)

        @pl.when(i == n_i)
        def _():
            gup_ref[:, 0] = acc_up[...].astype(BF16)
            gdn_ref[...] = acc_dn[...].astype(BF16)

    chunk = lambda rows: pl.BlockSpec((2, 1, rows, fc), lambda j, i: (0, j, 0, 0))
    tile = pl.BlockSpec((2, 1, tm, fc), lambda j, i: (0, j, cur(i), 0))
    return _launch(
        body, "ffn_bwd", (n_j, n_i + 1),
        [pl.BlockSpec((tm, d), lambda j, i: (cur(i), 0)), pl.BlockSpec((tm, d), lambda j, i: (lag(i), 0)),
         pl.BlockSpec((tm, d), lambda j, i: (lag(i), 0)), tile, tile, chunk(d), chunk(SUBLANES),
         pl.BlockSpec((fc, d), lambda j, i: (j, 0))],
        [pl.BlockSpec((1, tm, d), lambda j, i: (j, lag(i), 0)), chunk(d), pl.BlockSpec((fc, d), lambda j, i: (j, 0)),
         pl.BlockSpec((2, n_j, SUBLANES, fc), lambda j, i: (0, 0, 0, 0))],
        [jax.ShapeDtypeStruct((n_j, t_len, d), BF16), jax.ShapeDtypeStruct((2, n_j, d, fc), BF16),
         jax.ShapeDtypeStruct((n_j * fc, d), BF16), jax.ShapeDtypeStruct((2, n_j, SUBLANES, fc), F32)],
        [pltpu.VMEM((2, tm, fc), BF16), pltpu.VMEM((2, tm, fc), BF16), pltpu.VMEM((tm, fc), BF16),
         pltpu.VMEM((tm, fc), BF16), pltpu.VMEM((2, d, fc), F32), pltpu.VMEM((fc, d), F32),
         pltpu.VMEM((2, FFN_HALO, fc), F32)],
        (dy, dy, h, a_sav, c_sav, wup, wdw, wdn), ex)


def _sub_post_bwd(dxo, ypre, pv, *, tm, ex=None):
    t_len, d = dxo.shape
    n_i = t_len // tm

    def body(dxo_ref, y_ref, pv_ref, dy_ref, pg_ref):
        @pl.when(pl.program_id(0) == 0)
        def _():
            pg_ref[...] = jnp.zeros_like(pg_ref)

        dy_ref[...] = _post_bwd(dxo_ref[...], y_ref[...], pv_ref[...], pg_ref).astype(BF16)

    row = pl.BlockSpec((tm, d), lambda i: (i, 0))
    small = pl.BlockSpec((SUBLANES, d), lambda i: (0, 0))
    return _launch(body, "sub_post_bwd", (n_i,), [row, row, small], [row, small],
                   [jax.ShapeDtypeStruct((t_len, d), BF16), jax.ShapeDtypeStruct((SUBLANES, d), F32)], [],
                   (dxo, ypre, pv), ex)


def _sub_pre_bwd(dhp, x, dxo, pv, *, tm, ex=None):
    n_p, t_len, d = dhp.shape
    n_i = t_len // tm

    def body(dhp_ref, x_ref, dxo_ref, pv_ref, dx_ref, pg_ref):
        @pl.when(pl.program_id(0) == 0)
        def _():
            pg_ref[...] = jnp.zeros_like(pg_ref)

        dh = dhp_ref[0].astype(F32)
        for p in range(1, n_p):
            dh = dh + dhp_ref[p].astype(F32)
        dx_ref[...] = dxo_ref[...] + _pre_bwd(dh, x_ref[...], pv_ref[...], pg_ref)

    row = pl.BlockSpec((tm, d), lambda i: (i, 0))
    small = pl.BlockSpec((SUBLANES, d), lambda i: (0, 0))
    return _launch(body, "sub_pre_bwd", (n_i,), [pl.BlockSpec((n_p, tm, d), lambda i: (0, i, 0)), row, row, small],
                   [row, small], [jax.ShapeDtypeStruct((t_len, d), F32), jax.ShapeDtypeStruct((SUBLANES, d), F32)], [],
                   (dhp, x, dxo, pv), ex)


CV_B1, CV_BDW, CV_LNG, CV_LNB, CV_B2 = 0, 1, 2, 3, 4


def _taps_by_residue(offs):
    groups = {}
    for k, off in enumerate(offs):
        groups.setdefault(off % SUBLANES, []).append((k, off // SUBLANES))
    return sorted(groups.items())


def _depthwise(buf, w_ref, out_ref, offs, tm, d):
    taps_of = _taps_by_residue(offs)

    def chunk(r, carry):
        r0 = pl.multiple_of(r * CONV_ROWS, CONV_ROWS)
        for cb in range(d // LANES):
            cols = slice(cb * LANES, (cb + 1) * LANES)
            win = buf[pl.ds(r0, CONV_ROWS + CONV_HALO), cols]
            acc = jnp.zeros((CONV_ROWS, LANES), F32)
            for b, taps in taps_of:
                wb = win if b == 0 else pltpu.roll(win, CONV_ROWS + CONV_HALO - b, 0)
                for k, a in taps:
                    acc = acc + wb[SUBLANES * a:SUBLANES * a + CONV_ROWS, :] * w_ref[k:k + 1, cols]
            out_ref[pl.ds(r0, CONV_ROWS), cols] = acc
        return carry

    lax.fori_loop(0, tm // CONV_ROWS, chunk, 0)


def _depthwise_wgrad(dbuf, ubuf, dw_ref, tm, d):
    taps_of = _taps_by_residue(tuple(2 + k for k in range(CONV_TAPS)))
    for cb in range(d // LANES):
        cols = slice(cb * LANES, (cb + 1) * LANES)

        def chunk(r, acc):
            r0 = pl.multiple_of(r * SUBLANES, SUBLANES)
            dv = dbuf[pl.ds(r0, SUBLANES), cols]
            win = ubuf[pl.ds(r0, SUBLANES + CONV_HALO), cols]
            new = list(acc)
            for b, taps in taps_of:
                wb = win if b == 0 else pltpu.roll(win, SUBLANES + CONV_HALO - b, 0)
                for k, a in taps:
                    new[k] = acc[k] + dv * wb[SUBLANES * a:SUBLANES * (a + 1), :]
            return tuple(new)

        acc = lax.fori_loop(0, tm // SUBLANES, chunk, tuple(jnp.zeros((SUBLANES, LANES), F32) for _ in range(CONV_TAPS)))
        for k in range(CONV_TAPS):
            dw_ref[k:k + 1, cols] += _colsum(acc[k])


def _layer_norm_parts(c1):
    mu = jnp.mean(c1, axis=-1, keepdims=True)
    cen = c1 - mu
    rstd = lax.rsqrt(jnp.mean(cen * cen, axis=-1, keepdims=True) + NORM_EPS)
    return cen * rstd, rstd


def _conv_fwd(x, pv, w1, w2, wdw, vec, *, tm):
    t_len, d = x.shape
    n_i = t_len // tm
    n_q = w1.shape[0] // 2
    qc = w1.shape[2]

    def body(x_ref, pv_ref, w1_ref, w2_ref, wdw_ref, vec_ref, xo_ref, y_ref, a_ref, c1_ref, ubuf):
        i = pl.program_id(0)

        @pl.when(i == 0)
        def _():
            ubuf[0:CONV_HALO, :] = jnp.zeros((CONV_HALO, d), F32)

        xv, pv_ = x_ref[...], pv_ref[...]
        h, _, _ = _prenorm(xv, pv_)
        hb = h.astype(BF16)
        for q in range(n_q):
            cols = slice(q * qc, (q + 1) * qc)
            gcols = slice(d + q * qc, d + (q + 1) * qc)
            val = (_dot(hb, w1_ref[q]) + vec_ref[CV_B1:CV_B1 + 1, cols]).astype(BF16)
            gate = (_dot(hb, w1_ref[n_q + q]) + vec_ref[CV_B1:CV_B1 + 1, gcols]).astype(BF16)
            a_ref[:, cols] = val
            a_ref[:, gcols] = gate
            ubuf[CONV_HALO:CONV_HALO + tm, cols] = val.astype(F32) * _sigmoid(gate.astype(F32))
        _depthwise(ubuf, wdw_ref, c1_ref, tuple(2 + k for k in range(CONV_TAPS)), tm, d)
        c1 = c1_ref[...] + vec_ref[CV_BDW:CV_BDW + 1, 0:d]
        c1_ref[...] = c1
        xhat, _ = _layer_norm_parts(c1)
        ln = xhat * vec_ref[CV_LNG:CV_LNG + 1, 0:d] + vec_ref[CV_LNB:CV_LNB + 1, 0:d]
        s = ln * _sigmoid(ln)
        y = _dot(s.astype(BF16), w2_ref[...]) + vec_ref[CV_B2:CV_B2 + 1, 0:d]
        y_ref[...] = y
        xo_ref[...] = _post(xv, y, pv_)
        ubuf[0:CONV_HALO, :] = ubuf[tm:tm + CONV_HALO, :]

    row = pl.BlockSpec((tm, d), lambda i: (i, 0))
    return pl.pallas_call(
        body, name="conv_fwd", grid=(n_i,),
        in_specs=[row, pl.BlockSpec((SUBLANES, d), lambda i: (0, 0)), pl.BlockSpec(w1.shape, lambda i: (0, 0, 0)),
                  pl.BlockSpec(w2.shape, lambda i: (0, 0)), pl.BlockSpec(wdw.shape, lambda i: (0, 0)),
                  pl.BlockSpec(vec.shape, lambda i: (0, 0))],
        out_specs=[row, row, pl.BlockSpec((tm, 2 * d), lambda i: (i, 0)), row],
        out_shape=[jax.ShapeDtypeStruct((t_len, d), F32), jax.ShapeDtypeStruct((t_len, d), F32),
                   jax.ShapeDtypeStruct((t_len, 2 * d), BF16), jax.ShapeDtypeStruct((t_len, d), F32)],
        scratch_shapes=[pltpu.VMEM((tm + CONV_HALO, d), F32)],
        compiler_params=_params(("arbitrary",)),
    )(x, pv, w1, w2, wdw, vec)


def _conv_bwd(dxo, x, ypre, a_sav, c1_sav, pv, w1, w2, wdw, vec, *, tm, ex=None):
    t_len, d = x.shape
    n_i = t_len // tm
    n_q = w1.shape[0] // 2
    qc = w1.shape[2]
    hb_ = tm // CONV_HALO

    def body(dxo_ref, x_ref, y_ref, a_ref, ah_ref, c1_ref, pv_ref, w1_ref, w2_ref, wdw_ref, vec_ref,
             dx_ref, pg_ref, gw1_ref, gw2_ref, gvec_ref, gwdw_ref, ubuf, dcbuf, dubuf, acc1, acc2):
        i = pl.program_id(0)
        ti = n_i - 1 - i

        @pl.when(i == 0)
        def _():
            pg_ref[...] = jnp.zeros_like(pg_ref)
            gvec_ref[...] = jnp.zeros_like(gvec_ref)
            gwdw_ref[...] = jnp.zeros_like(gwdw_ref)
            acc1[...] = jnp.zeros_like(acc1)
            acc2[...] = jnp.zeros_like(acc2)
            dcbuf[tm:tm + CONV_HALO, :] = jnp.zeros((CONV_HALO, d), F32)

        xv, pv_, dxo_v = x_ref[...], pv_ref[...], dxo_ref[...]
        dy = _post_bwd(dxo_v, y_ref[...], pv_, pg_ref)
        gvec_ref[CV_B2:CV_B2 + 1, 0:d] += _colsum(dy)
        dyb = dy.astype(BF16)
        xhat, rstd = _layer_norm_parts(c1_ref[...])
        lng = vec_ref[CV_LNG:CV_LNG + 1, 0:d]
        ln = xhat * lng + vec_ref[CV_LNB:CV_LNB + 1, 0:d]
        sg = _sigmoid(ln)
        acc2[...] += _dot_tn((ln * sg).astype(BF16), dyb)
        dln = _dot_nt(dyb, w2_ref[...]) * (sg * (1.0 + ln * (1.0 - sg)))
        gvec_ref[CV_LNG:CV_LNG + 1, 0:d] += _colsum(dln * xhat)
        gvec_ref[CV_LNB:CV_LNB + 1, 0:d] += _colsum(dln)
        dxh = dln * lng
        dc1 = rstd * (dxh - jnp.mean(dxh, axis=-1, keepdims=True)
                      - xhat * jnp.mean(dxh * xhat, axis=-1, keepdims=True))
        gvec_ref[CV_BDW:CV_BDW + 1, 0:d] += _colsum(dc1)
        dcbuf[0:tm, :] = dc1
        for q in range(n_q):
            cols = slice(q * qc, (q + 1) * qc)
            gcols = slice(d + q * qc, d + (q + 1) * qc)
            ubuf[CONV_HALO:CONV_HALO + tm, cols] = a_ref[:, cols].astype(F32) * _sigmoid(a_ref[:, gcols].astype(F32))
            uh = ah_ref[:, cols].astype(F32) * _sigmoid(ah_ref[:, gcols].astype(F32))
            ubuf[0:CONV_HALO, cols] = jnp.where(ti > 0, uh, 0.0)
        _depthwise_wgrad(dcbuf, ubuf, gwdw_ref, tm, d)
        _depthwise(dcbuf, wdw_ref, dubuf, tuple(CONV_TAPS - 1 - k for k in range(CONV_TAPS)), tm, d)
        dcbuf[tm:tm + CONV_HALO, :] = dcbuf[0:CONV_HALO, :]
        h, _, _ = _prenorm(xv, pv_)
        hb = h.astype(BF16)
        dh = jnp.zeros((tm, d), F32)
        for q in range(n_q):
            cols = slice(q * qc, (q + 1) * qc)
            gcols = slice(d + q * qc, d + (q + 1) * qc)
            du = dubuf[:, cols]
            val, gate = a_ref[:, cols].astype(F32), a_ref[:, gcols].astype(F32)
            sgg = _sigmoid(gate)
            dval = du * sgg
            dgate = du * val * (sgg * (1.0 - sgg))
            gvec_ref[CV_B1:CV_B1 + 1, cols] += _colsum(dval)
            gvec_ref[CV_B1:CV_B1 + 1, gcols] += _colsum(dgate)
            dvb, dgb = dval.astype(BF16), dgate.astype(BF16)
            acc1[q] += _dot_tn(hb, dvb)
            acc1[n_q + q] += _dot_tn(hb, dgb)
            dh = dh + _dot_nt(dvb, w1_ref[q]) + _dot_nt(dgb, w1_ref[n_q + q])
        dx_ref[...] = dxo_v + _pre_bwd(dh, xv, pv_, pg_ref)

        @pl.when(i == n_i - 1)
        def _():
            gw1_ref[...] = acc1[...].astype(BF16)
            gw2_ref[...] = acc2[...].astype(BF16)

    row = pl.BlockSpec((tm, d), lambda i: (n_i - 1 - i, 0))
    small = pl.BlockSpec((SUBLANES, d), lambda i: (0, 0))
    whole2 = lambda shape: pl.BlockSpec(shape, lambda i: (0, 0))
    return _launch(
        body, "conv_bwd", (n_i,),
        [row, row, row,
         pl.BlockSpec((tm, 2 * d), lambda i: (n_i - 1 - i, 0)),
         pl.BlockSpec((CONV_HALO, 2 * d), lambda i: (jnp.maximum((n_i - 1 - i) * hb_ - 1, 0), 0)),
         row, small, pl.BlockSpec(w1.shape, lambda i: (0, 0, 0)), whole2(w2.shape), whole2(wdw.shape),
         whole2(vec.shape)],
        [row, small, pl.BlockSpec(w1.shape, lambda i: (0, 0, 0)), whole2(w2.shape), whole2(vec.shape),
         whole2(wdw.shape)],
        [jax.ShapeDtypeStruct((t_len, d), F32), jax.ShapeDtypeStruct((SUBLANES, d), F32),
         jax.ShapeDtypeStruct(w1.shape, BF16), jax.ShapeDtypeStruct(w2.shape, BF16),
         jax.ShapeDtypeStruct(vec.shape, F32), jax.ShapeDtypeStruct(wdw.shape, F32)],
        [pltpu.VMEM((tm + CONV_HALO, d), F32), pltpu.VMEM((tm + CONV_HALO, d), F32),
         pltpu.VMEM((tm, d), F32), pltpu.VMEM(w1.shape, F32), pltpu.VMEM(w2.shape, F32)],
        (dxo, x, ypre, a_sav, a_sav, c1_sav, pv, w1, w2, wdw, vec), ex)


def _loss_head(y, target, *, tm):
    t_len, d = y.shape
    n_i = t_len // tm

    def body(y_ref, t_ref, dy_ref, sq_ref):
        @pl.when(pl.program_id(0) == 0)
        def _():
            sq_ref[...] = jnp.zeros_like(sq_ref)

        err = y_ref[...] - t_ref[...]
        dy_ref[...] = err * (1.0 / d)
        sq_ref[...] += jnp.sum((err * err).reshape(tm // SUBLANES, SUBLANES, d), axis=0)

    row = pl.BlockSpec((tm, d), lambda i: (i, 0))
    return pl.pallas_call(
        body, name="loss_head", grid=(n_i,), in_specs=[row, row],
        out_specs=[row, pl.BlockSpec((SUBLANES, d), lambda i: (0, 0))],
        out_shape=[jax.ShapeDtypeStruct((t_len, d), F32), jax.ShapeDtypeStruct((SUBLANES, d), F32)],
        compiler_params=_params(("arbitrary",)),
    )(y, target)


def _adam(name, parts, w, m, v):
    layers = len(parts)
    n, rows, cols = parts[0].shape
    tr = rows
    if rows % SUBLANES == 0:
        cap = max(SUBLANES, ADAM_BLOCK_BYTES // (4 * cols))
        tr = max(t for t in range(SUBLANES, rows + 1, SUBLANES) if rows % t == 0 and (t <= cap or t == SUBLANES))
    c1 = 1.0 / (1.0 - ADAM_B1 ** ADAM_STEP)
    c2 = 1.0 / (1.0 - ADAM_B2 ** ADAM_STEP)

    def body(*refs):
        p_refs = refs[:layers]
        w_ref, m_ref, v_ref, g_ref, d_ref, mo_ref, vo_ref, g_s = refs[layers:]
        for l in range(layers):
            @pl.when(pl.program_id(0) == l)
            def _():
                g = p_refs[l][0].astype(F32)
                for k in range(1, n):
                    g = g + p_refs[l][k].astype(F32)
                g_s[...] = g

        g = g_s[...]
        m2 = ADAM_B1 * m_ref[0] + (1.0 - ADAM_B1) * g
        v2 = ADAM_B2 * v_ref[0] + (1.0 - ADAM_B2) * (g * g)
        g_ref[0] = g
        mo_ref[0] = m2
        vo_ref[0] = v2
        d_ref[0] = -ADAM_LR * ((m2 * c1) / (jnp.sqrt(v2 * c2) + ADAM_EPS) + ADAM_WD * w_ref[0])

    def part_spec(l):
        return pl.BlockSpec((n, tr, cols), lambda ll, i: (0, jnp.where(ll == l, i, 0), 0))

    blk = pl.BlockSpec((1, tr, cols), lambda ll, i: (ll, i, 0))
    out = jax.ShapeDtypeStruct((layers, rows, cols), F32)
    return pl.pallas_call(
        body, name=name, grid=(layers, rows // tr),
        in_specs=[part_spec(l) for l in range(layers)] + [blk, blk, blk],
        out_specs=[blk, blk, blk, blk], out_shape=[out, out, out, out],
        scratch_shapes=[pltpu.VMEM((tr, cols), F32)],
        compiler_params=_params(("arbitrary", "arbitrary")),
    )(*parts, w, m, v)


def _adam_nd(name, parts, w, m, v):
    shape = w.shape
    cols = shape[-1]
    if isinstance(parts, (list, tuple)):
        layers = len(parts)
    else:
        layers, parts = 1, [parts]
    rows = w.size // (cols * layers)
    flat = lambda t: t.reshape(layers, rows, cols)
    outs = _adam(name, [p.reshape(p.shape[0], rows, cols) for p in parts], flat(w), flat(m), flat(v))
    return [o.reshape(shape) for o in outs]


def _small_pack(parts, size):
    flat = jnp.concatenate([p.reshape(-1) for p in parts])
    return jnp.pad(flat, (0, size - flat.shape[0]))


def _to_shards(full, axis):
    shp = full.shape
    split = full.reshape(shp[:axis] + (N_DEV, shp[axis] // N_DEV) + shp[axis + 1:])
    return jnp.moveaxis(split, axis, 0)


def _from_shards(sh, axis):
    moved = jnp.moveaxis(sh, 0, axis)
    shp = moved.shape
    return moved.reshape(shp[:axis] + (shp[axis] * shp[axis + 1],) + shp[axis + 2:])


def kernel(x, c, ada_w, ada_b, pre_g, post_g, pool_w, pool_scale, cv_w_pw1, cv_b_pw1, cv_w_dw, cv_b_dw, cv_ln_g, cv_ln_b, cv_w_pw2, cv_b_pw2, ffn_w_up, ffn_w_dw, ffn_w_down, loss_target, m_ada_w, m_ada_b, m_pre_g, m_post_g, m_pool_w, m_pool_scale, m_cv_w_pw1, m_cv_b_pw1, m_cv_w_dw, m_cv_b_dw, m_cv_ln_g, m_cv_ln_b, m_cv_w_pw2, m_cv_b_pw2, m_ffn_w_up, m_ffn_w_dw, m_ffn_w_down, v_ada_w, v_ada_b, v_pre_g, v_post_g, v_pool_w, v_pool_scale, v_cv_w_pw1, v_cv_b_pw1, v_cv_w_dw, v_cv_b_dw, v_cv_ln_g, v_cv_ln_b, v_cv_w_pw2, v_cv_b_pw2, v_ffn_w_up, v_ffn_w_dw, v_ffn_w_down):
    t_len, d = x.shape[1], x.shape[2]
    depth = ada_w.shape[0]
    fc = ffn_w_up.shape[2]
    n_j = N_DEV // 2
    me = 4 * lax.axis_index("x") + 2 * lax.axis_index("y") + lax.axis_index("c")

    small_w = [pre_g, post_g, cv_b_pw1, cv_w_dw, cv_b_dw, cv_ln_g, cv_ln_b, cv_b_pw2, ffn_w_dw]
    small_m = [m_pre_g, m_post_g, m_cv_b_pw1, m_cv_w_dw, m_cv_b_dw, m_cv_ln_g, m_cv_ln_b, m_cv_b_pw2, m_ffn_w_dw]
    small_v = [v_pre_g, v_post_g, v_cv_b_pw1, v_cv_w_dw, v_cv_b_dw, v_cv_ln_g, v_cv_ln_b, v_cv_b_pw2, v_ffn_w_dw]
    sizes = [p.size for p in small_w]
    offs = [sum(sizes[:k]) for k in range(len(sizes) + 1)]
    pack = -(-offs[-1] // (SUBLANES * LANES)) * SUBLANES * LANES

    got = _exchange("gather_small", [c, _small_pack(small_w, pack), pool_w[0].astype(BF16)], [True] * 3)
    c_all = got[0].reshape(N_DEV, d)
    smalls = [got[1][:, offs[k]:offs[k + 1]].reshape((N_DEV,) + small_w[k].shape) for k in range(len(small_w))]
    pre_g_f, post_g_f = _from_shards(smalls[0], 2), _from_shards(smalls[1], 2)
    b1_f = _from_shards(smalls[2], 1)[0]
    cvw_f = jnp.pad(_from_shards(smalls[3], 2)[0], ((0, CONV_HALO - CONV_TAPS), (0, 0)))
    bdw_f, lng_f, lnb_f, b2_f = [_from_shards(smalls[k], 1)[0] for k in (4, 5, 6, 7)]
    fdw = jnp.pad(smalls[8], ((0, 0), (0, 0), (0, SUBLANES - FFN_TAPS), (0, 0)))
    wp = jnp.swapaxes(got[2], 0, 1).reshape(pool_w.shape[1], -1, pool_w.shape[3])
    wdw = [fdw[:, l].reshape(2, n_j, SUBLANES, fc) for l in range(depth)]
    cvec = jnp.zeros((SUBLANES, 2 * d), F32)
    cvec = cvec.at[CV_B1].set(b1_f)
    for r, vrow in ((CV_BDW, bdw_f), (CV_LNG, lng_f), (CV_LNB, lnb_f), (CV_B2, b2_f)):
        cvec = cvec.at[r, :d].set(vrow)

    mod_cols = _ada_fwd(c_all, ada_w)
    (mod_all,) = _exchange("gather_mod", [mod_cols], [True])
    mod = lax.dynamic_index_in_dim(mod_all, me, axis=2, keepdims=False)
    mod = jnp.swapaxes(mod, 0, 1).reshape(depth, N_MOD, d) + ada_b.reshape(depth, N_MOD, d)

    def pv_of(l, s):
        rows = [pre_g_f[l, s], 1.0 + mod[l, 3 * s + 1], mod[l, 3 * s], post_g_f[l, s], mod[l, 3 * s + 2]]
        return jnp.concatenate([jnp.stack(rows), jnp.zeros((SUBLANES - len(rows), d), F32)])

    x0 = x[0]
    pv00, pv01, pv10, pv11 = pv_of(0, 0), pv_of(0, 1), pv_of(1, 0), pv_of(1, 1)
    tm_pool, tm_ffn, tm_bwd, tm_conv = min(TM_POOL, t_len), min(TM_FFN, t_len), min(TM_FFN_BWD, t_len), min(TM_CONV, t_len)
    tm_w = min(TM_FFN_W, t_len)
    ex = _TwoLevelGather([ffn_w_up[0].astype(BF16), ffn_w_down[0].astype(BF16)])
    (x1, y0), (wup0, wdn0) = _pool_fwd(x0, pv00, wp, pool_scale, tm=tm_pool, ex=ex)
    ex = _TwoLevelGather([cv_w_pw1[0].astype(BF16), cv_w_pw2[0].astype(BF16), ffn_w_up[1].astype(BF16),
                          ffn_w_down[1].astype(BF16)])
    (x2, y1, h1, a1, c1), (w1, w2, wup1, wdn1) = _ffn_fwd(
        x1, pv01, wup0.reshape(2, n_j, d, fc), wdw[0], wdn0.reshape(n_j * fc, d), tm=tm_ffn, ex=ex)
    w2 = w2.reshape(d, d)
    wup = [wup0.reshape(2, n_j, d, fc), wup1.reshape(2, n_j, d, fc)]
    wdn = [wdn0.reshape(n_j * fc, d), wdn1.reshape(n_j * fc, d)]
    x3, y2, a2, c2 = _conv_fwd(x2, pv10, w1, w2, cvw_f, cvec, tm=tm_conv)
    (x4, y3, h3, a3, c3), _ = _ffn_fwd(x3, pv11, wup[1], wdw[1], wdn[1], tm=tm_ffn)
    dx4, sq = _loss_head(x4, loss_target[0], tm=min(TM_LOSS, t_len))
    loss = lax.psum(jnp.sum(sq) * (0.5 / d), MESH_AXES)

    (dy3, pgq11), _ = _sub_post_bwd(dx4, y3, pv11, tm=tm_w)
    (dhp3, gup1, gdn1, gfdw1), _ = _ffn_bwd(dy3, h3, a3, c3, wup[1], wdw[1], wdn[1], tm=tm_bwd)
    (dx3, pgp11), _ = _sub_pre_bwd(dhp3, x3, dx4, pv11, tm=tm_ffn)
    ex = _Exchange([gup1.reshape(N_DEV, d, fc), gdn1.reshape(N_DEV, -1, d)], [False, False])
    (dx2, pg10, gw1, gw2, gcvec, gcvw), (rup1, rdn1) = _conv_bwd(
        dx3, x2, y2, a2, c2, pv10, w1, w2, cvw_f, cvec, tm=tm_conv, ex=ex)
    (dy1, pgq01), _ = _sub_post_bwd(dx2, y1, pv01, tm=tm_w)
    ex = _Exchange([gw1, gw2.reshape(N_DEV, -1, d)], [False, False])
    (dhp1, gup0, gdn0, gfdw0), (rw1, rw2) = _ffn_bwd(dy1, h1, a1, c1, wup[0], wdw[0], wdn[0], tm=tm_bwd, ex=ex)
    ex = _Exchange([gdn0.reshape(N_DEV, -1, d)], [False])
    (dx1, pgp01), (rdn0,) = _sub_pre_bwd(dhp1, x1, dx2, pv01, tm=tm_ffn, ex=ex)
    ex = _Exchange([gup0.reshape(N_DEV, d, fc)], [False])
    (dx0, pg00, gwp), (rup0,) = _pool_bwd(dx1, x0, y0, pv00, wp, pool_scale, tm=tm_pool, ex=ex)
    pg01, pg11 = pgq01 + pgp01, pgq11 + pgp11

    pgs = [[pg00, pg01], [pg10, pg11]]
    g_pre = jnp.stack([jnp.stack([pgs[l][s][PG_GPRE] for s in range(2)]) for l in range(depth)])
    g_post = jnp.stack([jnp.stack([pgs[l][s][PG_GPOST] for s in range(2)]) for l in range(depth)])
    dmod = jnp.stack([jnp.concatenate([pgs[l][s][r] for s in range(2) for r in (PG_SH, PG_SC, PG_GT)])
                      for l in range(depth)])
    gfdw = jnp.stack([g.reshape(N_DEV, SUBLANES, fc)[:, :FFN_TAPS] for g in (gfdw0, gfdw1)], axis=1)
    small_g = [_to_shards(g_pre, 2), _to_shards(g_post, 2), _to_shards(gcvec[CV_B1][None], 1),
               _to_shards(gcvw[None, :CONV_TAPS], 2), _to_shards(gcvec[CV_BDW, :d][None], 1),
               _to_shards(gcvec[CV_LNG, :d][None], 1), _to_shards(gcvec[CV_LNB, :d][None], 1),
               _to_shards(gcvec[CV_B2, :d][None], 1), gfdw]
    small_send = jnp.concatenate([g.reshape(N_DEV, -1) for g in small_g], axis=1)
    small_send = jnp.pad(small_send, ((0, 0), (0, pack - small_send.shape[1])))
    gwp_send = jnp.swapaxes(gwp.reshape(gwp.shape[0], N_DEV, -1, gwp.shape[2]), 0, 1)
    rsmall, rwp, rmod, rscale = _exchange("scatter_small", [small_send, gwp_send, dmod, pg00[PG_EXTRA][None]],
                                          [False, False, True, True])

    outs = {}

    def put(name, res, shape=None):
        outs[name] = [r if shape is None else r.reshape(shape) for r in res]

    small_res = _adam_nd("adam_small", rsmall.reshape(N_DEV, -1, SUBLANES * LANES),
                         _small_pack(small_w, pack).reshape(-1, SUBLANES * LANES),
                         _small_pack(small_m, pack).reshape(-1, SUBLANES * LANES),
                         _small_pack(small_v, pack).reshape(-1, SUBLANES * LANES))
    small_names = ["pre_g", "post_g", "cv_b_pw1", "cv_w_dw", "cv_b_dw", "cv_ln_g", "cv_ln_b", "cv_b_pw2", "ffn_w_dw"]
    for k, nm in enumerate(small_names):
        outs[nm] = [r.reshape(-1)[offs[k]:offs[k + 1]].reshape(small_w[k].shape) for r in small_res]
    put("pool_w", _adam_nd("adam_pool_w", rwp[:, None], pool_w, m_pool_w, v_pool_w))
    put("cv_w_pw1", _adam_nd("adam_cv_w_pw1", rw1[:, None], cv_w_pw1, m_cv_w_pw1, v_cv_w_pw1))
    put("cv_w_pw2", _adam_nd("adam_cv_w_pw2", rw2[:, None], cv_w_pw2, m_cv_w_pw2, v_cv_w_pw2))
    put("ffn_w_up", _adam_nd("adam_ffn_w_up", [rup0, rup1], ffn_w_up, m_ffn_w_up, v_ffn_w_up))
    put("ffn_w_down", _adam_nd("adam_ffn_w_down", [rdn0, rdn1], ffn_w_down, m_ffn_w_down, v_ffn_w_down))
    put("ada_b", _adam_nd("adam_ada_b", rmod, ada_b, m_ada_b, v_ada_b))
    put("pool_scale", _adam_nd("adam_pool_scale", rscale, pool_scale, m_pool_scale, v_pool_scale))
    cols = ada_w.shape[2]
    dmod_cols = jnp.swapaxes(lax.dynamic_slice_in_dim(rmod, me * cols, cols, axis=2), 0, 1)
    put("ada_w", _adam_nd("adam_ada_w", _ada_bwd(c_all, dmod_cols)[None], ada_w, m_ada_w, v_ada_w))

    order = ["ada_w", "ada_b", "pre_g", "post_g", "pool_w", "pool_scale", "cv_w_pw1", "cv_b_pw1", "cv_w_dw", "cv_b_dw",
             "cv_ln_g", "cv_ln_b", "cv_w_pw2", "cv_b_pw2", "ffn_w_up", "ffn_w_dw", "ffn_w_down"]
    return (loss, dx0[None], *[outs[nm][0] for nm in order], *[outs[nm][1] for nm in order],
            *[outs[nm][2] for nm in order], *[outs[nm][3] for nm in order])
```

```python
import functools

import jax
import jax.numpy as jnp
from jax import lax
from jax.experimental import pallas as pl
from jax.experimental.pallas import tpu as pltpu

F32, BF16 = jnp.float32, jnp.bfloat16
MESH_AXES = ("x", "y", "c")
N_DEV = 8
NORM_EPS = 1e-6
ADAM_LR, ADAM_B1, ADAM_B2, ADAM_EPS, ADAM_WD, ADAM_STEP = 0.001, 0.9, 0.999, 1e-08, 0.01, 10
POOL_WINDOWS = (2, 4, 8, 16)
CONV_TAPS = 31
FFN_TAPS = 3
N_MOD = 6

SUBLANES = 8
LANES = 128
VMEM_LIMIT_BYTES = 56 * 1024 * 1024
POOL_HALO = 16
CONV_HALO = 32
FFN_HALO = 8
TM_POOL, TM_FFN, TM_FFN_BWD, TM_FFN_W, TM_CONV, TM_LOSS = 512, 512, 512, 1024, 256, 1024
CONV_ROWS = 32
ADAM_BLOCK_BYTES = 512 * 1024

PV_GPRE, PV_SC1, PV_SH, PV_GPOST, PV_GT = 0, 1, 2, 3, 4
PG_GPRE, PG_SC, PG_SH, PG_GPOST, PG_GT, PG_EXTRA = 0, 1, 2, 3, 4, 5


def _params(sem):
    return pltpu.CompilerParams(dimension_semantics=sem, vmem_limit_bytes=VMEM_LIMIT_BYTES)


def _dot(a, b):
    return jnp.dot(a, b, preferred_element_type=F32)


def _dot_nt(a, b):
    return lax.dot_general(a, b, (((1,), (1,)), ((), ())), preferred_element_type=F32)


def _dot_tn(a, b):
    return lax.dot_general(a, b, (((0,), (0,)), ((), ())), preferred_element_type=F32)


def _sigmoid(x):
    return 1.0 / (1.0 + jnp.exp(-x))


def _rms(x):
    return lax.rsqrt(jnp.mean(x * x, axis=-1, keepdims=True) + NORM_EPS)


def _colsum(v):
    return jnp.sum(v, axis=0, keepdims=True)


def _prenorm(x, pv):
    r = _rms(x)
    xn = x * r
    return xn * (pv[PV_GPRE:PV_GPRE + 1] * pv[PV_SC1:PV_SC1 + 1]) + pv[PV_SH:PV_SH + 1], xn, r


def _post(x, y, pv):
    return x + (pv[PV_GT:PV_GT + 1] * pv[PV_GPOST:PV_GPOST + 1]) * (y * _rms(y))


def _post_bwd(dxo, y, pv, pg_ref):
    ry = _rms(y)
    yn = y * ry
    gt, gpost = pv[PV_GT:PV_GT + 1], pv[PV_GPOST:PV_GPOST + 1]
    dyn = dxo * (gt * gpost)
    dy = ry * (dyn - yn * jnp.mean(dyn * yn, axis=-1, keepdims=True))
    s = _colsum(dxo * yn)
    pg_ref[PG_GPOST:PG_GPOST + 1, :] += s * gt
    pg_ref[PG_GT:PG_GT + 1, :] += s * gpost
    return dy


def _pre_bwd(dh, x, pv, pg_ref):
    r = _rms(x)
    xn = x * r
    gpre, sc1 = pv[PV_GPRE:PV_GPRE + 1], pv[PV_SC1:PV_SC1 + 1]
    dxn = dh * (gpre * sc1)
    dx = r * (dxn - xn * jnp.mean(dxn * xn, axis=-1, keepdims=True))
    p = _colsum(dh * xn)
    pg_ref[PG_GPRE:PG_GPRE + 1, :] += p * sc1
    pg_ref[PG_SC:PG_SC + 1, :] += p * gpre
    pg_ref[PG_SH:PG_SH + 1, :] += _colsum(dh)
    return dx


def _shift_down(a, k, prev):
    out = pltpu.roll(a, k, 0)
    row = lax.broadcasted_iota(jnp.int32, a.shape, 0)
    for q in range(k):
        out = jnp.where(row == q, prev[SUBLANES - k + q:SUBLANES - k + q + 1, :], out)
    return out


def _shift_up(a, k, nxt):
    rows = a.shape[0]
    out = pltpu.roll(a, rows - k, 0)
    row = lax.broadcasted_iota(jnp.int32, a.shape, 0)
    for q in range(k):
        out = jnp.where(row == rows - k + q, nxt[q:q + 1, :], out)
    return out


class _Exchange:
    def __init__(self, srcs, gathers):
        self.srcs, self.gathers, self.n = list(srcs), list(gathers), len(srcs)
        self.out_shape = [jax.ShapeDtypeStruct(((N_DEV,) + s.shape) if g else s.shape, s.dtype)
                          for s, g in zip(srcs, gathers)]
        self.specs = [pl.BlockSpec(memory_space=pl.ANY)] * self.n
        self.scratch = [pltpu.SemaphoreType.DMA((self.n, N_DEV - 1)), pltpu.SemaphoreType.DMA((self.n, N_DEV - 1)),
                        pltpu.SemaphoreType.DMA((self.n,))]

    def _copies(self, src_refs, out_refs, sems):
        send_sems, recv_sems, local_sems = sems
        x, y, c = lax.axis_index("x"), lax.axis_index("y"), lax.axis_index("c")
        me = 4 * x + 2 * y + c
        copies = []
        for a in range(self.n):
            mine = src_refs[a] if self.gathers[a] else src_refs[a].at[me]
            copies.append(pltpu.make_async_copy(mine, out_refs[a].at[me], local_sems.at[a]))
        for d in range(1, N_DEV):
            px, py, pc = (x + (d >> 2)) % 2, (y + ((d >> 1) & 1)) % 2, (c + (d & 1)) % 2
            peer = 4 * px + 2 * py + pc
            for a in range(self.n):
                src = src_refs[a] if self.gathers[a] else src_refs[a].at[peer]
                copies.append(pltpu.make_async_remote_copy(
                    src_ref=src, dst_ref=out_refs[a].at[me], send_sem=send_sems.at[a, d - 1],
                    recv_sem=recv_sems.at[a, d - 1], device_id=(px, py, pc), device_id_type=pl.DeviceIdType.MESH))
        return copies

    def start(self, src_refs, out_refs, sems):
        for cp in self._copies(src_refs, out_refs, sems):
            cp.start()

    def wait(self, src_refs, out_refs, sems):
        for cp in self._copies(src_refs, out_refs, sems):
            cp.wait()


class _TwoLevelGather:
    def __init__(self, srcs):
        self.srcs, self.n = list(srcs), len(srcs)
        self.out_shape = [jax.ShapeDtypeStruct((N_DEV,) + s.shape, s.dtype) for s in srcs]
        self.specs = [pl.BlockSpec(memory_space=pl.ANY)] * self.n
        self.scratch = [pltpu.SemaphoreType.DMA((self.n, N_DEV - 1)), pltpu.SemaphoreType.DMA((self.n, N_DEV - 1)),
                        pltpu.SemaphoreType.DMA((self.n,))]

    def _places(self):
        x, y, c = lax.axis_index("x"), lax.axis_index("y"), lax.axis_index("c")
        return (x, y, c), (x, y, 1 - c), [(1 - x, y), (x, 1 - y), (1 - x, 1 - y)]

    def _copy(self, a, k, block, to, src_refs, out_refs, sems, from_src=False):
        slot = 4 * block[0] + 2 * block[1] + block[2]
        return pltpu.make_async_remote_copy(
            src_ref=src_refs[a] if from_src else out_refs[a].at[slot], dst_ref=out_refs[a].at[slot],
            send_sem=sems[0].at[a, k], recv_sem=sems[1].at[a, k], device_id=to, device_id_type=pl.DeviceIdType.MESH)

    def _local(self, a, src_refs, out_refs, sems):
        me, _, _ = self._places()
        return pltpu.make_async_copy(src_refs[a], out_refs[a].at[4 * me[0] + 2 * me[1] + me[2]], sems[2].at[a])

    def start(self, src_refs, out_refs, sems):
        me, sibling, chips = self._places()
        for a in range(self.n):
            self._local(a, src_refs, out_refs, sems).start()
            self._copy(a, 0, me, sibling, src_refs, out_refs, sems, from_src=True).start()
        for j, chip in enumerate(chips):
            for a in range(self.n):
                self._copy(a, 1 + j, me, (*chip, me[2]), src_refs, out_refs, sems, from_src=True).start()

    def wait(self, src_refs, out_refs, sems):
        me, sibling, chips = self._places()
        for j, chip in enumerate(chips):
            for a in range(self.n):
                self._copy(a, 1 + j, (*chip, me[2]), me, src_refs, out_refs, sems).wait_recv()
                self._copy(a, 4 + j, (*chip, me[2]), sibling, src_refs, out_refs, sems).start()
        for a in range(self.n):
            self._copy(a, 0, sibling, me, src_refs, out_refs, sems).wait_recv()
            for j, chip in enumerate(chips):
                self._copy(a, 4 + j, (*chip, sibling[2]), me, src_refs, out_refs, sems).wait_recv()
        for a in range(self.n):
            self._copy(a, 0, me, sibling, src_refs, out_refs, sems, from_src=True).wait_send()
            for j, chip in enumerate(chips):
                self._copy(a, 1 + j, me, (*chip, me[2]), src_refs, out_refs, sems, from_src=True).wait_send()
                self._copy(a, 4 + j, (*chip, me[2]), sibling, src_refs, out_refs, sems).wait_send()
            self._local(a, src_refs, out_refs, sems).wait()


def _exchange(name, srcs, gathers):
    ex = _Exchange(srcs, gathers)

    def body(*refs):
        src_refs, out_refs, sems = refs[:ex.n], refs[ex.n:2 * ex.n], refs[2 * ex.n:]
        ex.start(src_refs, out_refs, sems)
        ex.wait(src_refs, out_refs, sems)

    return pl.pallas_call(body, name=name, out_shape=ex.out_shape, in_specs=ex.specs, out_specs=ex.specs,
                          scratch_shapes=ex.scratch)(*srcs)


def _launch(body, name, grid, in_specs, out_specs, out_shape, scratch_shapes, args, ex=None):
    sem = ("arbitrary",) * len(grid)
    if ex is None:
        outs = pl.pallas_call(body, name=name, grid=grid, in_specs=in_specs, out_specs=out_specs, out_shape=out_shape,
                              scratch_shapes=scratch_shapes, compiler_params=_params(sem))(*args)
        return outs, []
    n_in, n_out, n_scr = len(in_specs), len(out_specs), len(scratch_shapes)

    def hosted(*refs):
        ins, ex_in = refs[:n_in], refs[n_in:n_in + ex.n]
        outs = refs[n_in + ex.n:n_in + ex.n + n_out]
        ex_out = refs[n_in + ex.n + n_out:n_in + 2 * ex.n + n_out]
        rest = refs[n_in + 2 * ex.n + n_out:]
        scratch, sems = rest[:n_scr], rest[n_scr:]
        ids = [pl.program_id(a) for a in range(len(grid))]
        first = functools.reduce(lambda p, q: p & q, [i == 0 for i in ids])
        last = functools.reduce(lambda p, q: p & q, [i == g - 1 for i, g in zip(ids, grid)])

        @pl.when(first)
        def _():
            ex.start(ex_in, ex_out, sems)

        body(*ins, *outs, *scratch)

        @pl.when(last)
        def _():
            ex.wait(ex_in, ex_out, sems)

    outs = pl.pallas_call(
        hosted, name=name, grid=grid, in_specs=list(in_specs) + ex.specs, out_specs=list(out_specs) + ex.specs,
        out_shape=list(out_shape) + ex.out_shape, scratch_shapes=list(scratch_shapes) + ex.scratch,
        compiler_params=_params(sem))(*args, *ex.srcs)
    return outs[:n_out], outs[n_out:]


def _ada_fwd(c_all, ada_w):
    layers, d, cols = ada_w.shape

    def body(c_ref, w_ref, o_ref):
        c = c_ref[...]
        ca = (c * _sigmoid(c)).astype(BF16)
        for l in range(layers):
            o_ref[l] = _dot(ca, w_ref[l].astype(BF16))

    return pl.pallas_call(
        body, name="ada_fwd", out_shape=jax.ShapeDtypeStruct((layers, N_DEV, cols), F32),
        compiler_params=pltpu.CompilerParams(vmem_limit_bytes=VMEM_LIMIT_BYTES),
    )(c_all, ada_w)


def _ada_bwd(c_all, dmod_cols):
    layers, _, cols = dmod_cols.shape
    d = c_all.shape[1]

    def body(c_ref, g_ref, o_ref):
        c = c_ref[...]
        ca = (c * _sigmoid(c)).astype(BF16)
        for l in range(layers):
            o_ref[l] = _dot_tn(ca, g_ref[l].astype(BF16))

    return pl.pallas_call(
        body, name="ada_bwd", out_shape=jax.ShapeDtypeStruct((layers, d, cols), F32),
        compiler_params=pltpu.CompilerParams(vmem_limit_bytes=VMEM_LIMIT_BYTES),
    )(c_all, dmod_cols)


def _pooled(hbuf, h, t0, g, tm):
    gd = h.shape[1] // len(POOL_WINDOWS)
    cols = slice(g * gd, (g + 1) * gd)
    w = POOL_WINDOWS[g]
    hg = h[:, cols]
    s = hg
    for k in range(1, w):
        s = s + hbuf[POOL_HALO - k:POOL_HALO - k + tm, cols]
    t = t0 + lax.broadcasted_iota(jnp.int32, (tm, 1), 0)
    cnt = jnp.minimum(t + 1, w).astype(F32)
    return s / cnt - hg, cnt


def _pool_fwd(x, pv, wp, scale, *, tm, ex=None):
    t_len, d = x.shape
    n_i = t_len // tm
    gd = d // len(POOL_WINDOWS)

    def body(x_ref, pv_ref, wp_ref, sc_ref, xo_ref, y_ref, hbuf):
        i = pl.program_id(0)

        @pl.when(i == 0)
        def _():
            hbuf[0:POOL_HALO, :] = jnp.zeros((POOL_HALO, d), F32)

        xv, pv_ = x_ref[...], pv_ref[...]
        h, _, _ = _prenorm(xv, pv_)
        hbuf[POOL_HALO:POOL_HALO + tm, :] = h
        for g in range(len(POOL_WINDOWS)):
            pooled, _ = _pooled(hbuf, h, i * tm, g, tm)
            y_ref[:, g * gd:(g + 1) * gd] = _dot(pooled.astype(BF16), wp_ref[g])
        xo_ref[...] = _post(xv, y_ref[...] * sc_ref[...], pv_)
        hbuf[0:POOL_HALO, :] = hbuf[tm:tm + POOL_HALO, :]

    row = pl.BlockSpec((tm, d), lambda i: (i, 0))
    return _launch(
        body, "pool_fwd", (n_i,),
        [row, pl.BlockSpec((SUBLANES, d), lambda i: (0, 0)), pl.BlockSpec(wp.shape, lambda i: (0, 0, 0)),
         pl.BlockSpec((1, d), lambda i: (0, 0))],
        [row, row],
        [jax.ShapeDtypeStruct((t_len, d), F32), jax.ShapeDtypeStruct((t_len, d), F32)],
        [pltpu.VMEM((tm + POOL_HALO, d), F32)],
        (x, pv, wp, scale), ex)


def _pool_bwd(dxo, x, ypre, pv, wp, scale, *, tm, ex=None):
    t_len, d = x.shape
    n_i = t_len // tm
    gd = d // len(POOL_WINDOWS)
    hb = tm // POOL_HALO

    def body(dxo_ref, x_ref, xh_ref, y_ref, pv_ref, wp_ref, sc_ref, dx_ref, pg_ref, dwp_ref, hbuf, qbuf):
        i = pl.program_id(0)
        ti = n_i - 1 - i

        @pl.when(i == 0)
        def _():
            pg_ref[...] = jnp.zeros_like(pg_ref)
            dwp_ref[...] = jnp.zeros_like(dwp_ref)
            qbuf[tm:tm + POOL_HALO, :] = jnp.zeros((POOL_HALO, d), F32)

        xv, pv_, dxo_v, yp, sc = x_ref[...], pv_ref[...], dxo_ref[...], y_ref[...], sc_ref[...]
        dy = _post_bwd(dxo_v, yp * sc, pv_, pg_ref)
        pg_ref[PG_EXTRA:PG_EXTRA + 1, :] += _colsum(dy * yp)
        dys = dy * sc
        h, _, _ = _prenorm(xv, pv_)
        hh, _, _ = _prenorm(xh_ref[...], pv_)
        hbuf[0:POOL_HALO, :] = jnp.where(ti > 0, hh, 0.0)
        hbuf[POOL_HALO:POOL_HALO + tm, :] = h
        for g in range(len(POOL_WINDOWS)):
            cols = slice(g * gd, (g + 1) * gd)
            pooled, cnt = _pooled(hbuf, h, ti * tm, g, tm)
            dyg = dys[:, cols].astype(BF16)
            dwp_ref[g] += _dot_tn(pooled.astype(BF16), dyg)
            dp = _dot_nt(dyg, wp_ref[g])
            qbuf[0:tm, cols] = dp / cnt
            dh = -dp
            for k in range(POOL_WINDOWS[g]):
                dh = dh + qbuf[k:k + tm, cols]
            hbuf[POOL_HALO:POOL_HALO + tm, cols] = dh
        dx_ref[...] = dxo_v + _pre_bwd(hbuf[POOL_HALO:POOL_HALO + tm, :], xv, pv_, pg_ref)
        qbuf[tm:tm + POOL_HALO, :] = qbuf[0:POOL_HALO, :]

    row = pl.BlockSpec((tm, d), lambda i: (n_i - 1 - i, 0))
    halo = pl.BlockSpec((POOL_HALO, d), lambda i: (jnp.maximum((n_i - 1 - i) * hb - 1, 0), 0))
    small = pl.BlockSpec((SUBLANES, d), lambda i: (0, 0))
    return _launch(
        body, "pool_bwd", (n_i,),
        [row, row, halo, row, small, pl.BlockSpec(wp.shape, lambda i: (0, 0, 0)), pl.BlockSpec((1, d), lambda i: (0, 0))],
        [row, small, pl.BlockSpec(wp.shape, lambda i: (0, 0, 0))],
        [jax.ShapeDtypeStruct((t_len, d), F32), jax.ShapeDtypeStruct((SUBLANES, d), F32),
         jax.ShapeDtypeStruct(wp.shape, F32)],
        [pltpu.VMEM((tm + POOL_HALO, d), F32), pltpu.VMEM((tm + POOL_HALO, d), F32)],
        (dxo, x, x, ypre, pv, wp, scale), ex)


def _ffn_conv(a, prev, w):
    return w[2:3] * a + w[1:2] * _shift_down(a, 1, prev) + w[0:1] * _shift_down(a, 2, prev)


def _ffn_fwd(x, pv, wup, wdw, wdn, *, tm, ex=None):
    t_len, d = x.shape
    _, n_j, fc, _ = wup.shape
    n_i = t_len // tm

    def body(x_ref, pv_ref, wup_ref, wdw_ref, wdn_ref, xo_ref, y_ref, h_ref, a_ref, c_ref, h_s, yacc, carry):
        i, j = pl.program_id(0), pl.program_id(1)

        @pl.when(j == 0)
        def _():
            h, _, _ = _prenorm(x_ref[...], pv_ref[...])
            hb = h.astype(BF16)
            h_s[...] = hb
            h_ref[...] = hb
            yacc[...] = jnp.zeros_like(yacc)

        @pl.when((i == 0) & (j == 0))
        def _():
            carry[...] = jnp.zeros_like(carry)

        hb = h_s[...]
        conv = []
        for s in range(2):
            ab = _dot_nt(hb, wup_ref[s, 0]).astype(BF16)
            a_ref[s, 0] = ab
            a = ab.astype(F32)
            cv = _ffn_conv(a, carry[s, j], wdw_ref[s, 0])
            c_ref[s, 0] = cv.astype(BF16)
            conv.append(cv)
            carry[s, j] = a[tm - FFN_HALO:tm, :]
        g, v = conv
        u = g * _sigmoid(g) * v
        yacc[...] += _dot(u.astype(BF16), wdn_ref[...])

        @pl.when(j == n_j - 1)
        def _():
            y = yacc[...]
            y_ref[...] = y
            xo_ref[...] = _post(x_ref[...], y, pv_ref[...])

    row = pl.BlockSpec((tm, d), lambda i, j: (i, 0))
    return _launch(
        body, "ffn_fwd", (n_i, n_j),
        [row, pl.BlockSpec((SUBLANES, d), lambda i, j: (0, 0)),
         pl.BlockSpec((2, 1, fc, d), lambda i, j: (0, j, 0, 0)),
         pl.BlockSpec((2, 1, SUBLANES, fc), lambda i, j: (0, j, 0, 0)),
         pl.BlockSpec((fc, d), lambda i, j: (j, 0))],
        [row, row, row, pl.BlockSpec((2, 1, tm, fc), lambda i, j: (0, j, i, 0)),
         pl.BlockSpec((2, 1, tm, fc), lambda i, j: (0, j, i, 0))],
        [jax.ShapeDtypeStruct((t_len, d), F32), jax.ShapeDtypeStruct((t_len, d), F32),
         jax.ShapeDtypeStruct((t_len, d), BF16), jax.ShapeDtypeStruct((2, n_j, t_len, fc), BF16),
         jax.ShapeDtypeStruct((2, n_j, t_len, fc), BF16)],
        [pltpu.VMEM((tm, d), BF16), pltpu.VMEM((tm, d), F32), pltpu.VMEM((2, n_j, FFN_HALO, fc), F32)],
        (x, pv, wup, wdw, wdn), ex)


def _ffn_bwd(dy, h, a_sav, c_sav, wup, wdw, wdn, *, tm, ex=None):
    t_len, d = dy.shape
    _, n_j, fc, _ = wup.shape
    n_i = t_len // tm

    def cur(i):
        return n_i - 1 - jnp.minimum(i, n_i - 1)

    def lag(i):
        return jnp.minimum(n_i - i, n_i - 1)

    def body(dyc_ref, dyl_ref, hl_ref, a_ref, c_ref, wup_ref, wdw_ref, wdn_ref,
             dhp_ref, gup_ref, gdn_ref, dwdw_ref, da_0, da_1, u_0, u_1, acc_up, acc_dn, carry):
        j, i = pl.program_id(0), pl.program_id(1)

        @pl.when((j == 0) & (i == 0))
        def _():
            dwdw_ref[...] = jnp.zeros_like(dwdw_ref)

        @pl.when(i == 0)
        def _():
            acc_up[...] = jnp.zeros_like(acc_up)
            acc_dn[...] = jnp.zeros_like(acc_dn)
            carry[...] = jnp.zeros_like(carry)
            da_1[...] = jnp.zeros_like(da_1)
            u_1[...] = jnp.zeros_like(u_1)

        def step(da_w, u_w, da_r, u_r):
            hl = hl_ref[...]
            live = (i < n_i).astype(F32)
            du = _dot_nt(dyc_ref[...], wdn_ref[...])
            g, v = c_ref[0, 0].astype(F32), c_ref[1, 0].astype(F32)
            sg = _sigmoid(g)
            sl = g * sg
            u_w[...] = (sl * v).astype(BF16)
            dh0 = _dot(da_r[0], wup_ref[0, 0])
            d2 = (du * v * (sg * (1.0 + g * (1.0 - sg))), du * sl)
            dhp_ref[0] = (dh0 + _dot(da_r[1], wup_ref[1, 0])).astype(BF16)
            for s in range(2):
                w = wdw_ref[s, 0]
                nxt = carry[s]
                p1, p2 = _shift_up(d2[s], 1, nxt), _shift_up(d2[s], 2, nxt)
                carry[s] = d2[s][0:FFN_HALO, :]
                da_w[s] = (w[2:3] * d2[s] + w[1:2] * p1 + w[0:1] * p2).astype(BF16)
                acc_up[s] += _dot_tn(da_r[s], hl)
                a_s = a_ref[s, 0].astype(F32)
                for k, sh in ((2, d2[s]), (1, p1), (0, p2)):
                    dwdw_ref[s, j, k:k + 1, :] += live * _colsum(a_s * sh)
            acc_dn[...] += _dot_tn(u_r[...], dyl_ref[...])

        @pl.when(i % 2 == 0)
        def _():
            step(da_0, u_0, da_1, u_1)

        @pl.when(i % 2 == 1)
        def _():
            step(da_1, u_1, da_0, u_0)

        @pl.when(i == n_i)
        def _():
            gup_ref[:, 0] = acc_up[...].astype(BF16)
            gdn_ref[...] = acc_dn[...].astype(BF16)

    chunk = lambda shape: pl.BlockSpec((2, 1) + shape, lambda j, i: (0, j, 0, 0))
    tile = pl.BlockSpec((2, 1, tm, fc), lambda j, i: (0, j, cur(i), 0))
    return _launch(
        body, "ffn_bwd", (n_j, n_i + 1),
        [pl.BlockSpec((tm, d), lambda j, i: (cur(i), 0)), pl.BlockSpec((tm, d), lambda j, i: (lag(i), 0)),
         pl.BlockSpec((tm, d), lambda j, i: (lag(i), 0)), tile, tile, chunk((fc, d)), chunk((SUBLANES, fc)),
         pl.BlockSpec((fc, d), lambda j, i: (j, 0))],
        [pl.BlockSpec((1, tm, d), lambda j, i: (j, lag(i), 0)), chunk((fc, d)),
         pl.BlockSpec((fc, d), lambda j, i: (j, 0)), pl.BlockSpec((2, n_j, SUBLANES, fc), lambda j, i: (0, 0, 0, 0))],
        [jax.ShapeDtypeStruct((n_j, t_len, d), BF16), jax.ShapeDtypeStruct((2, n_j, fc, d), BF16),
         jax.ShapeDtypeStruct((n_j * fc, d), BF16), jax.ShapeDtypeStruct((2, n_j, SUBLANES, fc), F32)],
        [pltpu.VMEM((2, tm, fc), BF16), pltpu.VMEM((2, tm, fc), BF16), pltpu.VMEM((tm, fc), BF16),
         pltpu.VMEM((tm, fc), BF16), pltpu.VMEM((2, fc, d), F32), pltpu.VMEM((fc, d), F32),
         pltpu.VMEM((2, FFN_HALO, fc), F32)],
        (dy, dy, h, a_sav, c_sav, wup, wdw, wdn), ex)


def _sub_post_bwd(dxo, ypre, pv, *, tm, ex=None):
    t_len, d = dxo.shape
    n_i = t_len // tm

    def body(dxo_ref, y_ref, pv_ref, dy_ref, pg_ref):
        @pl.when(pl.program_id(0) == 0)
        def _():
            pg_ref[...] = jnp.zeros_like(pg_ref)

        dy_ref[...] = _post_bwd(dxo_ref[...], y_ref[...], pv_ref[...], pg_ref).astype(BF16)

    row = pl.BlockSpec((tm, d), lambda i: (i, 0))
    small = pl.BlockSpec((SUBLANES, d), lambda i: (0, 0))
    return _launch(body, "sub_post_bwd", (n_i,), [row, row, small], [row, small],
                   [jax.ShapeDtypeStruct((t_len, d), BF16), jax.ShapeDtypeStruct((SUBLANES, d), F32)], [],
                   (dxo, ypre, pv), ex)


def _sub_pre_bwd(dhp, x, dxo, pv, *, tm, ex=None):
    n_p, t_len, d = dhp.shape
    n_i = t_len // tm

    def body(dhp_ref, x_ref, dxo_ref, pv_ref, dx_ref, pg_ref):
        @pl.when(pl.program_id(0) == 0)
        def _():
            pg_ref[...] = jnp.zeros_like(pg_ref)

        dh = dhp_ref[0].astype(F32)
        for p in range(1, n_p):
            dh = dh + dhp_ref[p].astype(F32)
        dx_ref[...] = dxo_ref[...] + _pre_bwd(dh, x_ref[...], pv_ref[...], pg_ref)

    row = pl.BlockSpec((tm, d), lambda i: (i, 0))
    small = pl.BlockSpec((SUBLANES, d), lambda i: (0, 0))
    return _launch(body, "sub_pre_bwd", (n_i,), [pl.BlockSpec((n_p, tm, d), lambda i: (0, i, 0)), row, row, small],
                   [row, small], [jax.ShapeDtypeStruct((t_len, d), F32), jax.ShapeDtypeStruct((SUBLANES, d), F32)], [],
                   (dhp, x, dxo, pv), ex)


CV_B1, CV_BDW, CV_LNG, CV_LNB, CV_B2 = 0, 1, 2, 3, 4


def _taps_by_residue(offs):
    groups = {}
    for k, off in enumerate(offs):
        groups.setdefault(off % SUBLANES, []).append((k, off // SUBLANES))
    return sorted(groups.items())


def _depthwise(buf, w_ref, out_ref, offs, tm, d):
    taps_of = _taps_by_residue(offs)

    def chunk(r, carry):
        r0 = pl.multiple_of(r * CONV_ROWS, CONV_ROWS)
        for cb in range(d // LANES):
            cols = slice(cb * LANES, (cb + 1) * LANES)
            win = buf[pl.ds(r0, CONV_ROWS + CONV_HALO), cols]
            acc = jnp.zeros((CONV_ROWS, LANES), F32)
            for b, taps in taps_of:
                wb = win if b == 0 else pltpu.roll(win, CONV_ROWS + CONV_HALO - b, 0)
                for k, a in taps:
                    acc = acc + wb[SUBLANES * a:SUBLANES * a + CONV_ROWS, :] * w_ref[k:k + 1, cols]
            out_ref[pl.ds(r0, CONV_ROWS), cols] = acc
        return carry

    lax.fori_loop(0, tm // CONV_ROWS, chunk, 0)


def _depthwise_wgrad(dbuf, ubuf, dw_ref, tm, d):
    taps_of = _taps_by_residue(tuple(2 + k for k in range(CONV_TAPS)))
    for cb in range(d // LANES):
        cols = slice(cb * LANES, (cb + 1) * LANES)

        def chunk(r, acc):
            r0 = pl.multiple_of(r * SUBLANES, SUBLANES)
            dv = dbuf[pl.ds(r0, SUBLANES), cols]
            win = ubuf[pl.ds(r0, SUBLANES + CONV_HALO), cols]
            new = list(acc)
            for b, taps in taps_of:
                wb = win if b == 0 else pltpu.roll(win, SUBLANES + CONV_HALO - b, 0)
                for k, a in taps:
                    new[k] = acc[k] + dv * wb[SUBLANES * a:SUBLANES * (a + 1), :]
            return tuple(new)

        acc = lax.fori_loop(0, tm // SUBLANES, chunk, tuple(jnp.zeros((SUBLANES, LANES), F32) for _ in range(CONV_TAPS)))
        for k in range(CONV_TAPS):
            dw_ref[k:k + 1, cols] += _colsum(acc[k])


def _layer_norm_parts(c1):
    mu = jnp.mean(c1, axis=-1, keepdims=True)
    cen = c1 - mu
    rstd = lax.rsqrt(jnp.mean(cen * cen, axis=-1, keepdims=True) + NORM_EPS)
    return cen * rstd, rstd


def _conv_fwd(x, pv, w1, w2, wdw, vec, *, tm):
    t_len, d = x.shape
    n_i = t_len // tm
    n_q = w1.shape[0] // 2
    qc = w1.shape[2]

    def body(x_ref, pv_ref, w1_ref, w2_ref, wdw_ref, vec_ref, xo_ref, y_ref, a_ref, c1_ref, ubuf):
        i = pl.program_id(0)

        @pl.when(i == 0)
        def _():
            ubuf[0:CONV_HALO, :] = jnp.zeros((CONV_HALO, d), F32)

        xv, pv_ = x_ref[...], pv_ref[...]
        h, _, _ = _prenorm(xv, pv_)
        hb = h.astype(BF16)
        for q in range(n_q):
            cols = slice(q * qc, (q + 1) * qc)
            gcols = slice(d + q * qc, d + (q + 1) * qc)
            val = (_dot(hb, w1_ref[q]) + vec_ref[CV_B1:CV_B1 + 1, cols]).astype(BF16)
            gate = (_dot(hb, w1_ref[n_q + q]) + vec_ref[CV_B1:CV_B1 + 1, gcols]).astype(BF16)
            a_ref[:, cols] = val
            a_ref[:, gcols] = gate
            ubuf[CONV_HALO:CONV_HALO + tm, cols] = val.astype(F32) * _sigmoid(gate.astype(F32))
        _depthwise(ubuf, wdw_ref, c1_ref, tuple(2 + k for k in range(CONV_TAPS)), tm, d)
        c1 = c1_ref[...] + vec_ref[CV_BDW:CV_BDW + 1, 0:d]
        c1_ref[...] = c1
        xhat, _ = _layer_norm_parts(c1)
        ln = xhat * vec_ref[CV_LNG:CV_LNG + 1, 0:d] + vec_ref[CV_LNB:CV_LNB + 1, 0:d]
        s = ln * _sigmoid(ln)
        y = _dot(s.astype(BF16), w2_ref[...]) + vec_ref[CV_B2:CV_B2 + 1, 0:d]
        y_ref[...] = y
        xo_ref[...] = _post(xv, y, pv_)
        ubuf[0:CONV_HALO, :] = ubuf[tm:tm + CONV_HALO, :]

    row = pl.BlockSpec((tm, d), lambda i: (i, 0))
    return pl.pallas_call(
        body, name="conv_fwd", grid=(n_i,),
        in_specs=[row, pl.BlockSpec((SUBLANES, d), lambda i: (0, 0)), pl.BlockSpec(w1.shape, lambda i: (0, 0, 0)),
                  pl.BlockSpec(w2.shape, lambda i: (0, 0)), pl.BlockSpec(wdw.shape, lambda i: (0, 0)),
                  pl.BlockSpec(vec.shape, lambda i: (0, 0))],
        out_specs=[row, row, pl.BlockSpec((tm, 2 * d), lambda i: (i, 0)), row],
        out_shape=[jax.ShapeDtypeStruct((t_len, d), F32), jax.ShapeDtypeStruct((t_len, d), F32),
                   jax.ShapeDtypeStruct((t_len, 2 * d), BF16), jax.ShapeDtypeStruct((t_len, d), F32)],
        scratch_shapes=[pltpu.VMEM((tm + CONV_HALO, d), F32)],
        compiler_params=_params(("arbitrary",)),
    )(x, pv, w1, w2, wdw, vec)


def _conv_bwd(dxo, x, ypre, a_sav, c1_sav, pv, w1, w2, wdw, vec, *, tm, ex=None):
    t_len, d = x.shape
    n_i = t_len // tm
    n_q = w1.shape[0] // 2
    qc = w1.shape[2]
    hb_ = tm // CONV_HALO

    def body(dxo_ref, x_ref, y_ref, a_ref, ah_ref, c1_ref, pv_ref, w1_ref, w2_ref, wdw_ref, vec_ref,
             dx_ref, pg_ref, gw1_ref, gw2_ref, gvec_ref, gwdw_ref, ubuf, dcbuf, dubuf, acc1, acc2):
        i = pl.program_id(0)
        ti = n_i - 1 - i

        @pl.when(i == 0)
        def _():
            pg_ref[...] = jnp.zeros_like(pg_ref)
            gvec_ref[...] = jnp.zeros_like(gvec_ref)
            gwdw_ref[...] = jnp.zeros_like(gwdw_ref)
            acc1[...] = jnp.zeros_like(acc1)
            acc2[...] = jnp.zeros_like(acc2)
            dcbuf[tm:tm + CONV_HALO, :] = jnp.zeros((CONV_HALO, d), F32)

        xv, pv_, dxo_v = x_ref[...], pv_ref[...], dxo_ref[...]
        dy = _post_bwd(dxo_v, y_ref[...], pv_, pg_ref)
        gvec_ref[CV_B2:CV_B2 + 1, 0:d] += _colsum(dy)
        dyb = dy.astype(BF16)
        xhat, rstd = _layer_norm_parts(c1_ref[...])
        lng = vec_ref[CV_LNG:CV_LNG + 1, 0:d]
        ln = xhat * lng + vec_ref[CV_LNB:CV_LNB + 1, 0:d]
        sg = _sigmoid(ln)
        acc2[...] += _dot_tn((ln * sg).astype(BF16), dyb)
        dln = _dot_nt(dyb, w2_ref[...]) * (sg * (1.0 + ln * (1.0 - sg)))
        gvec_ref[CV_LNG:CV_LNG + 1, 0:d] += _colsum(dln * xhat)
        gvec_ref[CV_LNB:CV_LNB + 1, 0:d] += _colsum(dln)
        dxh = dln * lng
        dc1 = rstd * (dxh - jnp.mean(dxh, axis=-1, keepdims=True)
                      - xhat * jnp.mean(dxh * xhat, axis=-1, keepdims=True))
        gvec_ref[CV_BDW:CV_BDW + 1, 0:d] += _colsum(dc1)
        dcbuf[0:tm, :] = dc1
        for q in range(n_q):
            cols = slice(q * qc, (q + 1) * qc)
            gcols = slice(d + q * qc, d + (q + 1) * qc)
            ubuf[CONV_HALO:CONV_HALO + tm, cols] = a_ref[:, cols].astype(F32) * _sigmoid(a_ref[:, gcols].astype(F32))
            uh = ah_ref[:, cols].astype(F32) * _sigmoid(ah_ref[:, gcols].astype(F32))
            ubuf[0:CONV_HALO, cols] = jnp.where(ti > 0, uh, 0.0)
        _depthwise_wgrad(dcbuf, ubuf, gwdw_ref, tm, d)
        _depthwise(dcbuf, wdw_ref, dubuf, tuple(CONV_TAPS - 1 - k for k in range(CONV_TAPS)), tm, d)
        dcbuf[tm:tm + CONV_HALO, :] = dcbuf[0:CONV_HALO, :]
        h, _, _ = _prenorm(xv, pv_)
        hb = h.astype(BF16)
        dh = jnp.zeros((tm, d), F32)
        for q in range(n_q):
            cols = slice(q * qc, (q + 1) * qc)
            gcols = slice(d + q * qc, d + (q + 1) * qc)
            du = dubuf[:, cols]
            val, gate = a_ref[:, cols].astype(F32), a_ref[:, gcols].astype(F32)
            sgg = _sigmoid(gate)
            dval = du * sgg
            dgate = du * val * (sgg * (1.0 - sgg))
            gvec_ref[CV_B1:CV_B1 + 1, cols] += _colsum(dval)
            gvec_ref[CV_B1:CV_B1 + 1, gcols] += _colsum(dgate)
            dvb, dgb = dval.astype(BF16), dgate.astype(BF16)
            acc1[q] += _dot_tn(hb, dvb)
            acc1[n_q + q] += _dot_tn(hb, dgb)
            dh = dh + _dot_nt(dvb, w1_ref[q]) + _dot_nt(dgb, w1_ref[n_q + q])
        dx_ref[...] = dxo_v + _pre_bwd(dh, xv, pv_, pg_ref)

        @pl.when(i == n_i - 1)
        def _():
            gw1_ref[...] = acc1[...].astype(BF16)
            gw2_ref[...] = acc2[...].astype(BF16)

    row = pl.BlockSpec((tm, d), lambda i: (n_i - 1 - i, 0))
    small = pl.BlockSpec((SUBLANES, d), lambda i: (0, 0))
    whole2 = lambda shape: pl.BlockSpec(shape, lambda i: (0, 0))
    return _launch(
        body, "conv_bwd", (n_i,),
        [row, row, row,
         pl.BlockSpec((tm, 2 * d), lambda i: (n_i - 1 - i, 0)),
         pl.BlockSpec((CONV_HALO, 2 * d), lambda i: (jnp.maximum((n_i - 1 - i) * hb_ - 1, 0), 0)),
         row, small, pl.BlockSpec(w1.shape, lambda i: (0, 0, 0)), whole2(w2.shape), whole2(wdw.shape),
         whole2(vec.shape)],
        [row, small, pl.BlockSpec(w1.shape, lambda i: (0, 0, 0)), whole2(w2.shape), whole2(vec.shape),
         whole2(wdw.shape)],
        [jax.ShapeDtypeStruct((t_len, d), F32), jax.ShapeDtypeStruct((SUBLANES, d), F32),
         jax.ShapeDtypeStruct(w1.shape, BF16), jax.ShapeDtypeStruct(w2.shape, BF16),
         jax.ShapeDtypeStruct(vec.shape, F32), jax.ShapeDtypeStruct(wdw.shape, F32)],
        [pltpu.VMEM((tm + CONV_HALO, d), F32), pltpu.VMEM((tm + CONV_HALO, d), F32),
         pltpu.VMEM((tm, d), F32), pltpu.VMEM(w1.shape, F32), pltpu.VMEM(w2.shape, F32)],
        (dxo, x, ypre, a_sav, a_sav, c1_sav, pv, w1, w2, wdw, vec), ex)


def _loss_head(y, target, *, tm):
    t_len, d = y.shape
    n_i = t_len // tm

    def body(y_ref, t_ref, dy_ref, sq_ref):
        @pl.when(pl.program_id(0) == 0)
        def _():
            sq_ref[...] = jnp.zeros_like(sq_ref)

        err = y_ref[...] - t_ref[...]
        dy_ref[...] = err * (1.0 / d)
        sq_ref[...] += jnp.sum((err * err).reshape(tm // SUBLANES, SUBLANES, d), axis=0)

    row = pl.BlockSpec((tm, d), lambda i: (i, 0))
    return pl.pallas_call(
        body, name="loss_head", grid=(n_i,), in_specs=[row, row],
        out_specs=[row, pl.BlockSpec((SUBLANES, d), lambda i: (0, 0))],
        out_shape=[jax.ShapeDtypeStruct((t_len, d), F32), jax.ShapeDtypeStruct((SUBLANES, d), F32)],
        compiler_params=_params(("arbitrary",)),
    )(y, target)


def _adam(name, parts, w, m, v):
    layers = len(parts)
    n, rows, cols = parts[0].shape
    tr = rows
    if rows % SUBLANES == 0:
        cap = max(SUBLANES, ADAM_BLOCK_BYTES // (4 * cols))
        tr = max(t for t in range(SUBLANES, rows + 1, SUBLANES) if rows % t == 0 and (t <= cap or t == SUBLANES))
    c1 = 1.0 / (1.0 - ADAM_B1 ** ADAM_STEP)
    c2 = 1.0 / (1.0 - ADAM_B2 ** ADAM_STEP)

    def body(*refs):
        p_refs = refs[:layers]
        w_ref, m_ref, v_ref, g_ref, d_ref, mo_ref, vo_ref, g_s = refs[layers:]
        for l in range(layers):
            @pl.when(pl.program_id(0) == l)
            def _():
                g = p_refs[l][0].astype(F32)
                for k in range(1, n):
                    g = g + p_refs[l][k].astype(F32)
                g_s[...] = g

        g = g_s[...]
        m2 = ADAM_B1 * m_ref[0] + (1.0 - ADAM_B1) * g
        v2 = ADAM_B2 * v_ref[0] + (1.0 - ADAM_B2) * (g * g)
        g_ref[0] = g
        mo_ref[0] = m2
        vo_ref[0] = v2
        d_ref[0] = -ADAM_LR * ((m2 * c1) / (jnp.sqrt(v2 * c2) + ADAM_EPS) + ADAM_WD * w_ref[0])

    def part_spec(l):
        return pl.BlockSpec((n, tr, cols), lambda ll, i: (0, jnp.where(ll == l, i, 0), 0))

    blk = pl.BlockSpec((1, tr, cols), lambda ll, i: (ll, i, 0))
    out = jax.ShapeDtypeStruct((layers, rows, cols), F32)
    return pl.pallas_call(
        body, name=name, grid=(layers, rows // tr),
        in_specs=[part_spec(l) for l in range(layers)] + [blk, blk, blk],
        out_specs=[blk, blk, blk, blk], out_shape=[out, out, out, out],
        scratch_shapes=[pltpu.VMEM((tr, cols), F32)],
        compiler_params=_params(("arbitrary", "arbitrary")),
    )(*parts, w, m, v)


def _adam_nd(name, parts, w, m, v):
    shape = w.shape
    cols = shape[-1]
    if isinstance(parts, (list, tuple)):
        layers = len(parts)
    else:
        layers, parts = 1, [parts]
    rows = w.size // (cols * layers)
    flat = lambda t: t.reshape(layers, rows, cols)
    outs = _adam(name, [p.reshape(p.shape[0], rows, cols) for p in parts], flat(w), flat(m), flat(v))
    return [o.reshape(shape) for o in outs]


def _small_pack(parts, size):
    flat = jnp.concatenate([p.reshape(-1) for p in parts])
    return jnp.pad(flat, (0, size - flat.shape[0]))


def _to_shards(full, axis):
    shp = full.shape
    split = full.reshape(shp[:axis] + (N_DEV, shp[axis] // N_DEV) + shp[axis + 1:])
    return jnp.moveaxis(split, axis, 0)


def _from_shards(sh, axis):
    moved = jnp.moveaxis(sh, 0, axis)
    shp = moved.shape
    return moved.reshape(shp[:axis] + (shp[axis] * shp[axis + 1],) + shp[axis + 2:])


def kernel(x, c, ada_w, ada_b, pre_g, post_g, pool_w, pool_scale, cv_w_pw1, cv_b_pw1, cv_w_dw, cv_b_dw, cv_ln_g, cv_ln_b, cv_w_pw2, cv_b_pw2, ffn_w_up, ffn_w_dw, ffn_w_down, loss_target, m_ada_w, m_ada_b, m_pre_g, m_post_g, m_pool_w, m_pool_scale, m_cv_w_pw1, m_cv_b_pw1, m_cv_w_dw, m_cv_b_dw, m_cv_ln_g, m_cv_ln_b, m_cv_w_pw2, m_cv_b_pw2, m_ffn_w_up, m_ffn_w_dw, m_ffn_w_down, v_ada_w, v_ada_b, v_pre_g, v_post_g, v_pool_w, v_pool_scale, v_cv_w_pw1, v_cv_b_pw1, v_cv_w_dw, v_cv_b_dw, v_cv_ln_g, v_cv_ln_b, v_cv_w_pw2, v_cv_b_pw2, v_ffn_w_up, v_ffn_w_dw, v_ffn_w_down):
    t_len, d = x.shape[1], x.shape[2]
    depth = ada_w.shape[0]
    fc = ffn_w_up.shape[2]
    n_j = N_DEV // 2
    me = 4 * lax.axis_index("x") + 2 * lax.axis_index("y") + lax.axis_index("c")

    small_w = [pre_g, post_g, cv_b_pw1, cv_w_dw, cv_b_dw, cv_ln_g, cv_ln_b, cv_b_pw2, ffn_w_dw]
    small_m = [m_pre_g, m_post_g, m_cv_b_pw1, m_cv_w_dw, m_cv_b_dw, m_cv_ln_g, m_cv_ln_b, m_cv_b_pw2, m_ffn_w_dw]
    small_v = [v_pre_g, v_post_g, v_cv_b_pw1, v_cv_w_dw, v_cv_b_dw, v_cv_ln_g, v_cv_ln_b, v_cv_b_pw2, v_ffn_w_dw]
    sizes = [p.size for p in small_w]
    offs = [sum(sizes[:k]) for k in range(len(sizes) + 1)]
    pack = -(-offs[-1] // (SUBLANES * LANES)) * SUBLANES * LANES

    got = _exchange("gather_small", [c, _small_pack(small_w, pack), pool_w[0].astype(BF16)], [True] * 3)
    c_all = got[0].reshape(N_DEV, d)
    smalls = [got[1][:, offs[k]:offs[k + 1]].reshape((N_DEV,) + small_w[k].shape) for k in range(len(small_w))]
    pre_g_f, post_g_f = _from_shards(smalls[0], 2), _from_shards(smalls[1], 2)
    b1_f = _from_shards(smalls[2], 1)[0]
    cvw_f = jnp.pad(_from_shards(smalls[3], 2)[0], ((0, CONV_HALO - CONV_TAPS), (0, 0)))
    bdw_f, lng_f, lnb_f, b2_f = [_from_shards(smalls[k], 1)[0] for k in (4, 5, 6, 7)]
    fdw = jnp.pad(smalls[8], ((0, 0), (0, 0), (0, SUBLANES - FFN_TAPS), (0, 0)))
    wp = jnp.swapaxes(got[2], 0, 1).reshape(pool_w.shape[1], -1, pool_w.shape[3])
    wdw = [fdw[:, l].reshape(2, n_j, SUBLANES, fc) for l in range(depth)]
    cvec = jnp.zeros((SUBLANES, 2 * d), F32)
    cvec = cvec.at[CV_B1].set(b1_f)
    for r, vrow in ((CV_BDW, bdw_f), (CV_LNG, lng_f), (CV_LNB, lnb_f), (CV_B2, b2_f)):
        cvec = cvec.at[r, :d].set(vrow)

    mod_cols = _ada_fwd(c_all, ada_w)
    (mod_all,) = _exchange("gather_mod", [mod_cols], [True])
    mod = lax.dynamic_index_in_dim(mod_all, me, axis=2, keepdims=False)
    mod = jnp.swapaxes(mod, 0, 1).reshape(depth, N_MOD, d) + ada_b.reshape(depth, N_MOD, d)

    def pv_of(l, s):
        rows = [pre_g_f[l, s], 1.0 + mod[l, 3 * s + 1], mod[l, 3 * s], post_g_f[l, s], mod[l, 3 * s + 2]]
        return jnp.concatenate([jnp.stack(rows), jnp.zeros((SUBLANES - len(rows), d), F32)])

    x0 = x[0]
    pv00, pv01, pv10, pv11 = pv_of(0, 0), pv_of(0, 1), pv_of(1, 0), pv_of(1, 1)
    tm_pool, tm_ffn, tm_bwd, tm_conv = min(TM_POOL, t_len), min(TM_FFN, t_len), min(TM_FFN_BWD, t_len), min(TM_CONV, t_len)
    tm_w = min(TM_FFN_W, t_len)
    wup_t = jnp.swapaxes(ffn_w_up, 1, 2)
    ex = _TwoLevelGather([wup_t[0].astype(BF16), ffn_w_down[0].astype(BF16)])
    (x1, y0), (wup0, wdn0) = _pool_fwd(x0, pv00, wp, pool_scale, tm=tm_pool, ex=ex)
    ex = _TwoLevelGather([cv_w_pw1[0].astype(BF16), cv_w_pw2[0].astype(BF16), wup_t[1].astype(BF16),
                          ffn_w_down[1].astype(BF16)])
    (x2, y1, h1, a1, c1), (w1, w2, wup1, wdn1) = _ffn_fwd(
        x1, pv01, wup0.reshape(2, n_j, fc, d), wdw[0], wdn0.reshape(n_j * fc, d), tm=tm_ffn, ex=ex)
    w2 = w2.reshape(d, d)
    wup = [wup0.reshape(2, n_j, fc, d), wup1.reshape(2, n_j, fc, d)]
    wdn = [wdn0.reshape(n_j * fc, d), wdn1.reshape(n_j * fc, d)]
    x3, y2, a2, c2 = _conv_fwd(x2, pv10, w1, w2, cvw_f, cvec, tm=tm_conv)
    (x4, y3, h3, a3, c3), _ = _ffn_fwd(x3, pv11, wup[1], wdw[1], wdn[1], tm=tm_ffn)
    dx4, sq = _loss_head(x4, loss_target[0], tm=min(TM_LOSS, t_len))
    loss = lax.psum(jnp.sum(sq) * (0.5 / d), MESH_AXES)

    (dy3, pgq11), _ = _sub_post_bwd(dx4, y3, pv11, tm=tm_w)
    (dhp3, gup1, gdn1, gfdw1), _ = _ffn_bwd(dy3, h3, a3, c3, wup[1], wdw[1], wdn[1], tm=tm_bwd)
    (dx3, pgp11), _ = _sub_pre_bwd(dhp3, x3, dx4, pv11, tm=tm_ffn)
    ex = _Exchange([gup1.reshape(N_DEV, fc, d), gdn1.reshape(N_DEV, -1, d)], [False, False])
    (dx2, pg10, gw1, gw2, gcvec, gcvw), (rup1, rdn1) = _conv_bwd(
        dx3, x2, y2, a2, c2, pv10, w1, w2, cvw_f, cvec, tm=tm_conv, ex=ex)
    (dy1, pgq01), _ = _sub_post_bwd(dx2, y1, pv01, tm=tm_w)
    ex = _Exchange([gw1, gw2.reshape(N_DEV, -1, d)], [False, False])
    (dhp1, gup0, gdn0, gfdw0), (rw1, rw2) = _ffn_bwd(dy1, h1, a1, c1, wup[0], wdw[0], wdn[0], tm=tm_bwd, ex=ex)
    ex = _Exchange([gdn0.reshape(N_DEV, -1, d)], [False])
    (dx1, pgp01), (rdn0,) = _sub_pre_bwd(dhp1, x1, dx2, pv01, tm=tm_ffn, ex=ex)
    ex = _Exchange([gup0.reshape(N_DEV, fc, d)], [False])
    (dx0, pg00, gwp), (rup0,) = _pool_bwd(dx1, x0, y0, pv00, wp, pool_scale, tm=tm_pool, ex=ex)
    pg01, pg11 = pgq01 + pgp01, pgq11 + pgp11

    pgs = [[pg00, pg01], [pg10, pg11]]
    g_pre = jnp.stack([jnp.stack([pgs[l][s][PG_GPRE] for s in range(2)]) for l in range(depth)])
    g_post = jnp.stack([jnp.stack([pgs[l][s][PG_GPOST] for s in range(2)]) for l in range(depth)])
    dmod = jnp.stack([jnp.concatenate([pgs[l][s][r] for s in range(2) for r in (PG_SH, PG_SC, PG_GT)])
                      for l in range(depth)])
    gfdw = jnp.stack([g.reshape(N_DEV, SUBLANES, fc)[:, :FFN_TAPS] for g in (gfdw0, gfdw1)], axis=1)
    small_g = [_to_shards(g_pre, 2), _to_shards(g_post, 2), _to_shards(gcvec[CV_B1][None], 1),
               _to_shards(gcvw[None, :CONV_TAPS], 2), _to_shards(gcvec[CV_BDW, :d][None], 1),
               _to_shards(gcvec[CV_LNG, :d][None], 1), _to_shards(gcvec[CV_LNB, :d][None], 1),
               _to_shards(gcvec[CV_B2, :d][None], 1), gfdw]
    small_send = jnp.concatenate([g.reshape(N_DEV, -1) for g in small_g], axis=1)
    small_send = jnp.pad(small_send, ((0, 0), (0, pack - small_send.shape[1])))
    gwp_send = jnp.swapaxes(gwp.reshape(gwp.shape[0], N_DEV, -1, gwp.shape[2]), 0, 1)
    rsmall, rwp, rmod, rscale = _exchange("scatter_small", [small_send, gwp_send, dmod, pg00[PG_EXTRA][None]],
                                          [False, False, True, True])

    outs = {}

    def put(name, res, shape=None):
        outs[name] = [r if shape is None else r.reshape(shape) for r in res]

    small_res = _adam_nd("adam_small", rsmall.reshape(N_DEV, -1, SUBLANES * LANES),
                         _small_pack(small_w, pack).reshape(-1, SUBLANES * LANES),
                         _small_pack(small_m, pack).reshape(-1, SUBLANES * LANES),
                         _small_pack(small_v, pack).reshape(-1, SUBLANES * LANES))
    small_names = ["pre_g", "post_g", "cv_b_pw1", "cv_w_dw", "cv_b_dw", "cv_ln_g", "cv_ln_b", "cv_b_pw2", "ffn_w_dw"]
    for k, nm in enumerate(small_names):
        outs[nm] = [r.reshape(-1)[offs[k]:offs[k + 1]].reshape(small_w[k].shape) for r in small_res]
    put("pool_w", _adam_nd("adam_pool_w", rwp[:, None], pool_w, m_pool_w, v_pool_w))
    put("cv_w_pw1", _adam_nd("adam_cv_w_pw1", rw1[:, None], cv_w_pw1, m_cv_w_pw1, v_cv_w_pw1))
    put("cv_w_pw2", _adam_nd("adam_cv_w_pw2", rw2[:, None], cv_w_pw2, m_cv_w_pw2, v_cv_w_pw2))
    outs["ffn_w_up"] = [jnp.swapaxes(r, 1, 2) for r in _adam_nd(
        "adam_ffn_w_up", [rup0, rup1], wup_t, jnp.swapaxes(m_ffn_w_up, 1, 2), jnp.swapaxes(v_ffn_w_up, 1, 2))]
    put("ffn_w_down", _adam_nd("adam_ffn_w_down", [rdn0, rdn1], ffn_w_down, m_ffn_w_down, v_ffn_w_down))
    put("ada_b", _adam_nd("adam_ada_b", rmod, ada_b, m_ada_b, v_ada_b))
    put("pool_scale", _adam_nd("adam_pool_scale", rscale, pool_scale, m_pool_scale, v_pool_scale))
    cols = ada_w.shape[2]
    dmod_cols = jnp.swapaxes(lax.dynamic_slice_in_dim(rmod, me * cols, cols, axis=2), 0, 1)
    put("ada_w", _adam_nd("adam_ada_w", _ada_bwd(c_all, dmod_cols)[None], ada_w, m_ada_w, v_ada_w))

    order = ["ada_w", "ada_b", "pre_g", "post_g", "pool_w", "pool_scale", "cv_w_pw1", "cv_b_pw1", "cv_w_dw", "cv_b_dw",
             "cv_ln_g", "cv_ln_b", "cv_w_pw2", "cv_b_pw2", "ffn_w_up", "ffn_w_dw", "ffn_w_down"]
    return (loss, dx0[None], *[outs[nm][0] for nm in order], *[outs[nm][1] for nm in order],
            *[outs[nm][2] for nm in order], *[outs[nm][3] for nm in order])
```

```python
import functools

import jax
import jax.numpy as jnp
from jax import lax
from jax.experimental import pallas as pl
from jax.experimental.pallas import tpu as pltpu

F32, BF16 = jnp.float32, jnp.bfloat16
MESH_AXES = ("x", "y", "c")
N_DEV = 8
NORM_EPS = 1e-6
ADAM_LR, ADAM_B1, ADAM_B2, ADAM_EPS, ADAM_WD, ADAM_STEP = 0.001, 0.9, 0.999, 1e-08, 0.01, 10
POOL_WINDOWS = (2, 4, 8, 16)
CONV_TAPS = 31
FFN_TAPS = 3
N_MOD = 6

SUBLANES = 8
LANES = 128
VMEM_LIMIT_BYTES = 56 * 1024 * 1024
POOL_HALO = 16
CONV_HALO = 32
FFN_HALO = 8
TM_POOL, TM_FFN, TM_FFN_BWD, TM_FFN_W, TM_CONV, TM_LOSS = 512, 512, 512, 1024, 256, 1024
CONV_ROWS = 32
ADAM_BLOCK_BYTES = 512 * 1024

PV_GPRE, PV_SC1, PV_SH, PV_GPOST, PV_GT = 0, 1, 2, 3, 4
PG_GPRE, PG_SC, PG_SH, PG_GPOST, PG_GT, PG_EXTRA = 0, 1, 2, 3, 4, 5


def _params(sem):
    return pltpu.CompilerParams(dimension_semantics=sem, vmem_limit_bytes=VMEM_LIMIT_BYTES)


def _dot(a, b):
    return jnp.dot(a, b, preferred_element_type=F32)


def _dot_nt(a, b):
    return lax.dot_general(a, b, (((1,), (1,)), ((), ())), preferred_element_type=F32)


def _dot_tn(a, b):
    return lax.dot_general(a, b, (((0,), (0,)), ((), ())), preferred_element_type=F32)


def _sigmoid(x):
    return 1.0 / (1.0 + jnp.exp(-x))


def _rms(x):
    return lax.rsqrt(jnp.mean(x * x, axis=-1, keepdims=True) + NORM_EPS)


def _colsum(v):
    return jnp.sum(v, axis=0, keepdims=True)


def _prenorm(x, pv):
    r = _rms(x)
    xn = x * r
    return xn * (pv[PV_GPRE:PV_GPRE + 1] * pv[PV_SC1:PV_SC1 + 1]) + pv[PV_SH:PV_SH + 1], xn, r


def _post(x, y, pv):
    return x + (pv[PV_GT:PV_GT + 1] * pv[PV_GPOST:PV_GPOST + 1]) * (y * _rms(y))


def _post_bwd(dxo, y, pv, pg_ref):
    ry = _rms(y)
    yn = y * ry
    gt, gpost = pv[PV_GT:PV_GT + 1], pv[PV_GPOST:PV_GPOST + 1]
    dyn = dxo * (gt * gpost)
    dy = ry * (dyn - yn * jnp.mean(dyn * yn, axis=-1, keepdims=True))
    s = _colsum(dxo * yn)
    pg_ref[PG_GPOST:PG_GPOST + 1, :] += s * gt
    pg_ref[PG_GT:PG_GT + 1, :] += s * gpost
    return dy


def _pre_bwd(dh, x, pv, pg_ref):
    r = _rms(x)
    xn = x * r
    gpre, sc1 = pv[PV_GPRE:PV_GPRE + 1], pv[PV_SC1:PV_SC1 + 1]
    dxn = dh * (gpre * sc1)
    dx = r * (dxn - xn * jnp.mean(dxn * xn, axis=-1, keepdims=True))
    p = _colsum(dh * xn)
    pg_ref[PG_GPRE:PG_GPRE + 1, :] += p * sc1
    pg_ref[PG_SC:PG_SC + 1, :] += p * gpre
    pg_ref[PG_SH:PG_SH + 1, :] += _colsum(dh)
    return dx


def _shift_down(a, k, prev):
    out = pltpu.roll(a, k, 0)
    row = lax.broadcasted_iota(jnp.int32, a.shape, 0)
    for q in range(k):
        out = jnp.where(row == q, prev[SUBLANES - k + q:SUBLANES - k + q + 1, :], out)
    return out


def _shift_up(a, k, nxt):
    rows = a.shape[0]
    out = pltpu.roll(a, rows - k, 0)
    row = lax.broadcasted_iota(jnp.int32, a.shape, 0)
    for q in range(k):
        out = jnp.where(row == rows - k + q, nxt[q:q + 1, :], out)
    return out


class _Exchange:
    def __init__(self, srcs, gathers):
        self.srcs, self.gathers, self.n = list(srcs), list(gathers), len(srcs)
        self.out_shape = [jax.ShapeDtypeStruct(((N_DEV,) + s.shape) if g else s.shape, s.dtype)
                          for s, g in zip(srcs, gathers)]
        self.specs = [pl.BlockSpec(memory_space=pl.ANY)] * self.n
        self.scratch = [pltpu.SemaphoreType.DMA((self.n, N_DEV - 1)), pltpu.SemaphoreType.DMA((self.n, N_DEV - 1)),
                        pltpu.SemaphoreType.DMA((self.n,))]

    def _copies(self, src_refs, out_refs, sems):
        send_sems, recv_sems, local_sems = sems
        x, y, c = lax.axis_index("x"), lax.axis_index("y"), lax.axis_index("c")
        me = 4 * x + 2 * y + c
        copies = []
        for a in range(self.n):
            mine = src_refs[a] if self.gathers[a] else src_refs[a].at[me]
            copies.append(pltpu.make_async_copy(mine, out_refs[a].at[me], local_sems.at[a]))
        for d in range(1, N_DEV):
            px, py, pc = (x + (d >> 2)) % 2, (y + ((d >> 1) & 1)) % 2, (c + (d & 1)) % 2
            peer = 4 * px + 2 * py + pc
            for a in range(self.n):
                src = src_refs[a] if self.gathers[a] else src_refs[a].at[peer]
                copies.append(pltpu.make_async_remote_copy(
                    src_ref=src, dst_ref=out_refs[a].at[me], send_sem=send_sems.at[a, d - 1],
                    recv_sem=recv_sems.at[a, d - 1], device_id=(px, py, pc), device_id_type=pl.DeviceIdType.MESH))
        return copies

    def start(self, src_refs, out_refs, sems):
        for cp in self._copies(src_refs, out_refs, sems):
            cp.start()

    def wait(self, src_refs, out_refs, sems):
        for cp in self._copies(src_refs, out_refs, sems):
            cp.wait()


class _TwoLevelGather:
    def __init__(self, srcs):
        self.srcs, self.n = list(srcs), len(srcs)
        self.out_shape = [jax.ShapeDtypeStruct((N_DEV,) + s.shape, s.dtype) for s in srcs]
        self.specs = [pl.BlockSpec(memory_space=pl.ANY)] * self.n
        self.scratch = [pltpu.SemaphoreType.DMA((self.n, N_DEV - 1)), pltpu.SemaphoreType.DMA((self.n, N_DEV - 1)),
                        pltpu.SemaphoreType.DMA((self.n,))]

    def _places(self):
        x, y, c = lax.axis_index("x"), lax.axis_index("y"), lax.axis_index("c")
        return (x, y, c), (x, y, 1 - c), [(1 - x, y), (x, 1 - y), (1 - x, 1 - y)]

    def _copy(self, a, k, block, to, src_refs, out_refs, sems, from_src=False):
        slot = 4 * block[0] + 2 * block[1] + block[2]
        return pltpu.make_async_remote_copy(
            src_ref=src_refs[a] if from_src else out_refs[a].at[slot], dst_ref=out_refs[a].at[slot],
            send_sem=sems[0].at[a, k], recv_sem=sems[1].at[a, k], device_id=to, device_id_type=pl.DeviceIdType.MESH)

    def _local(self, a, src_refs, out_refs, sems):
        me, _, _ = self._places()
        return pltpu.make_async_copy(src_refs[a], out_refs[a].at[4 * me[0] + 2 * me[1] + me[2]], sems[2].at[a])

    def start(self, src_refs, out_refs, sems):
        me, sibling, chips = self._places()
        for a in range(self.n):
            self._local(a, src_refs, out_refs, sems).start()
            self._copy(a, 0, me, sibling, src_refs, out_refs, sems, from_src=True).start()
        for j, chip in enumerate(chips):
            for a in range(self.n):
                self._copy(a, 1 + j, me, (*chip, me[2]), src_refs, out_refs, sems, from_src=True).start()

    def wait(self, src_refs, out_refs, sems):
        me, sibling, chips = self._places()
        for j, chip in enumerate(chips):
            for a in range(self.n):
                self._copy(a, 1 + j, (*chip, me[2]), me, src_refs, out_refs, sems).wait_recv()
                self._copy(a, 4 + j, (*chip, me[2]), sibling, src_refs, out_refs, sems).start()
        for a in range(self.n):
            self._copy(a, 0, sibling, me, src_refs, out_refs, sems).wait_recv()
            for j, chip in enumerate(chips):
                self._copy(a, 4 + j, (*chip, sibling[2]), me, src_refs, out_refs, sems).wait_recv()
        for a in range(self.n):
            self._copy(a, 0, me, sibling, src_refs, out_refs, sems, from_src=True).wait_send()
            for j, chip in enumerate(chips):
                self._copy(a, 1 + j, me, (*chip, me[2]), src_refs, out_refs, sems, from_src=True).wait_send()
                self._copy(a, 4 + j, (*chip, me[2]), sibling, src_refs, out_refs, sems).wait_send()
            self._local(a, src_refs, out_refs, sems).wait()


def _exchange(name, srcs, gathers):
    ex = _Exchange(srcs, gathers)

    def body(*refs):
        src_refs, out_refs, sems = refs[:ex.n], refs[ex.n:2 * ex.n], refs[2 * ex.n:]
        ex.start(src_refs, out_refs, sems)
        ex.wait(src_refs, out_refs, sems)

    return pl.pallas_call(body, name=name, out_shape=ex.out_shape, in_specs=ex.specs, out_specs=ex.specs,
                          scratch_shapes=ex.scratch)(*srcs)


def _launch(body, name, grid, in_specs, out_specs, out_shape, scratch_shapes, args, ex=None):
    sem = ("arbitrary",) * len(grid)
    if ex is None:
        outs = pl.pallas_call(body, name=name, grid=grid, in_specs=in_specs, out_specs=out_specs, out_shape=out_shape,
                              scratch_shapes=scratch_shapes, compiler_params=_params(sem))(*args)
        return outs, []
    n_in, n_out, n_scr = len(in_specs), len(out_specs), len(scratch_shapes)

    def hosted(*refs):
        ins, ex_in = refs[:n_in], refs[n_in:n_in + ex.n]
        outs = refs[n_in + ex.n:n_in + ex.n + n_out]
        ex_out = refs[n_in + ex.n + n_out:n_in + 2 * ex.n + n_out]
        rest = refs[n_in + 2 * ex.n + n_out:]
        scratch, sems = rest[:n_scr], rest[n_scr:]
        ids = [pl.program_id(a) for a in range(len(grid))]
        first = functools.reduce(lambda p, q: p & q, [i == 0 for i in ids])
        last = functools.reduce(lambda p, q: p & q, [i == g - 1 for i, g in zip(ids, grid)])

        @pl.when(first)
        def _():
            ex.start(ex_in, ex_out, sems)

        body(*ins, *outs, *scratch)

        @pl.when(last)
        def _():
            ex.wait(ex_in, ex_out, sems)

    outs = pl.pallas_call(
        hosted, name=name, grid=grid, in_specs=list(in_specs) + ex.specs, out_specs=list(out_specs) + ex.specs,
        out_shape=list(out_shape) + ex.out_shape, scratch_shapes=list(scratch_shapes) + ex.scratch,
        compiler_params=_params(sem))(*args, *ex.srcs)
    return outs[:n_out], outs[n_out:]


def _ada_fwd(c_all, ada_w):
    layers, d, cols = ada_w.shape

    def body(c_ref, w_ref, o_ref):
        c = c_ref[...]
        ca = (c * _sigmoid(c)).astype(BF16)
        for l in range(layers):
            o_ref[l] = _dot(ca, w_ref[l].astype(BF16))

    return pl.pallas_call(
        body, name="ada_fwd", out_shape=jax.ShapeDtypeStruct((layers, N_DEV, cols), F32),
        compiler_params=pltpu.CompilerParams(vmem_limit_bytes=VMEM_LIMIT_BYTES),
    )(c_all, ada_w)


def _ada_bwd(c_all, dmod_cols):
    layers, _, cols = dmod_cols.shape
    d = c_all.shape[1]

    def body(c_ref, g_ref, o_ref):
        c = c_ref[...]
        ca = (c * _sigmoid(c)).astype(BF16)
        for l in range(layers):
            o_ref[l] = _dot_tn(ca, g_ref[l].astype(BF16))

    return pl.pallas_call(
        body, name="ada_bwd", out_shape=jax.ShapeDtypeStruct((layers, d, cols), F32),
        compiler_params=pltpu.CompilerParams(vmem_limit_bytes=VMEM_LIMIT_BYTES),
    )(c_all, dmod_cols)


def _pooled(hbuf, h, t0, g, tm):
    gd = h.shape[1] // len(POOL_WINDOWS)
    cols = slice(g * gd, (g + 1) * gd)
    w = POOL_WINDOWS[g]
    hg = h[:, cols]
    s = hg
    for k in range(1, w):
        s = s + hbuf[POOL_HALO - k:POOL_HALO - k + tm, cols]
    t = t0 + lax.broadcasted_iota(jnp.int32, (tm, 1), 0)
    cnt = jnp.minimum(t + 1, w).astype(F32)
    return s / cnt - hg, cnt


def _pool_fwd(x, pv, wp, scale, *, tm, ex=None):
    t_len, d = x.shape
    n_i = t_len // tm
    gd = d // len(POOL_WINDOWS)

    def body(x_ref, pv_ref, wp_ref, sc_ref, xo_ref, y_ref, hbuf):
        i = pl.program_id(0)

        @pl.when(i == 0)
        def _():
            hbuf[0:POOL_HALO, :] = jnp.zeros((POOL_HALO, d), F32)

        xv, pv_ = x_ref[...], pv_ref[...]
        h, _, _ = _prenorm(xv, pv_)
        hbuf[POOL_HALO:POOL_HALO + tm, :] = h
        for g in range(len(POOL_WINDOWS)):
            pooled, _ = _pooled(hbuf, h, i * tm, g, tm)
            y_ref[:, g * gd:(g + 1) * gd] = _dot(pooled.astype(BF16), wp_ref[g])
        xo_ref[...] = _post(xv, y_ref[...] * sc_ref[...], pv_)
        hbuf[0:POOL_HALO, :] = hbuf[tm:tm + POOL_HALO, :]

    row = pl.BlockSpec((tm, d), lambda i: (i, 0))
    return _launch(
        body, "pool_fwd", (n_i,),
        [row, pl.BlockSpec((SUBLANES, d), lambda i: (0, 0)), pl.BlockSpec(wp.shape, lambda i: (0, 0, 0)),
         pl.BlockSpec((1, d), lambda i: (0, 0))],
        [row, row],
        [jax.ShapeDtypeStruct((t_len, d), F32), jax.ShapeDtypeStruct((t_len, d), F32)],
        [pltpu.VMEM((tm + POOL_HALO, d), F32)],
        (x, pv, wp, scale), ex)


def _pool_bwd(dxo, x, ypre, pv, wp, scale, *, tm, ex=None):
    t_len, d = x.shape
    n_i = t_len // tm
    gd = d // len(POOL_WINDOWS)
    hb = tm // POOL_HALO

    def body(dxo_ref, x_ref, xh_ref, y_ref, pv_ref, wp_ref, sc_ref, dx_ref, pg_ref, dwp_ref, hbuf, qbuf):
        i = pl.program_id(0)
        ti = n_i - 1 - i

        @pl.when(i == 0)
        def _():
            pg_ref[...] = jnp.zeros_like(pg_ref)
            dwp_ref[...] = jnp.zeros_like(dwp_ref)
            qbuf[tm:tm + POOL_HALO, :] = jnp.zeros((POOL_HALO, d), F32)

        xv, pv_, dxo_v, yp, sc = x_ref[...], pv_ref[...], dxo_ref[...], y_ref[...], sc_ref[...]
        dy = _post_bwd(dxo_v, yp * sc, pv_, pg_ref)
        pg_ref[PG_EXTRA:PG_EXTRA + 1, :] += _colsum(dy * yp)
        dys = dy * sc
        h, _, _ = _prenorm(xv, pv_)
        hh, _, _ = _prenorm(xh_ref[...], pv_)
        hbuf[0:POOL_HALO, :] = jnp.where(ti > 0, hh, 0.0)
        hbuf[POOL_HALO:POOL_HALO + tm, :] = h
        for g in range(len(POOL_WINDOWS)):
            cols = slice(g * gd, (g + 1) * gd)
            pooled, cnt = _pooled(hbuf, h, ti * tm, g, tm)
            dyg = dys[:, cols].astype(BF16)
            dwp_ref[g] += _dot_tn(pooled.astype(BF16), dyg)
            dp = _dot_nt(dyg, wp_ref[g])
            qbuf[0:tm, cols] = dp / cnt
            dh = -dp
            for k in range(POOL_WINDOWS[g]):
                dh = dh + qbuf[k:k + tm, cols]
            hbuf[POOL_HALO:POOL_HALO + tm, cols] = dh
        dx_ref[...] = dxo_v + _pre_bwd(hbuf[POOL_HALO:POOL_HALO + tm, :], xv, pv_, pg_ref)
        qbuf[tm:tm + POOL_HALO, :] = qbuf[0:POOL_HALO, :]

    row = pl.BlockSpec((tm, d), lambda i: (n_i - 1 - i, 0))
    halo = pl.BlockSpec((POOL_HALO, d), lambda i: (jnp.maximum((n_i - 1 - i) * hb - 1, 0), 0))
    small = pl.BlockSpec((SUBLANES, d), lambda i: (0, 0))
    return _launch(
        body, "pool_bwd", (n_i,),
        [row, row, halo, row, small, pl.BlockSpec(wp.shape, lambda i: (0, 0, 0)), pl.BlockSpec((1, d), lambda i: (0, 0))],
        [row, small, pl.BlockSpec(wp.shape, lambda i: (0, 0, 0))],
        [jax.ShapeDtypeStruct((t_len, d), F32), jax.ShapeDtypeStruct((SUBLANES, d), F32),
         jax.ShapeDtypeStruct(wp.shape, F32)],
        [pltpu.VMEM((tm + POOL_HALO, d), F32), pltpu.VMEM((tm + POOL_HALO, d), F32)],
        (dxo, x, x, ypre, pv, wp, scale), ex)


def _ffn_conv(a, prev, w):
    return w[2:3] * a + w[1:2] * _shift_down(a, 1, prev) + w[0:1] * _shift_down(a, 2, prev)


def _ffn_fwd(x, pv, wup, wdw, wdn, *, tm, ex=None):
    t_len, d = x.shape
    _, n_j, fc, _ = wup.shape
    n_i = t_len // tm

    def body(x_ref, pv_ref, wup_ref, wdw_ref, wdn_ref, xo_ref, y_ref, h_ref, a_ref, c_ref, h_s, yacc, carry):
        i, j = pl.program_id(0), pl.program_id(1)

        @pl.when(j == 0)
        def _():
            h, _, _ = _prenorm(x_ref[...], pv_ref[...])
            hb = h.astype(BF16)
            h_s[...] = hb
            h_ref[...] = hb
            yacc[...] = jnp.zeros_like(yacc)

        @pl.when((i == 0) & (j == 0))
        def _():
            carry[...] = jnp.zeros_like(carry)

        hb = h_s[...]
        conv = []
        for s in range(2):
            a = _dot_nt(hb, wup_ref[s, 0])
            a_ref[s, 0] = a.astype(BF16)
            cv = _ffn_conv(a, carry[s, j], wdw_ref[s, 0])
            c_ref[s, 0] = cv.astype(BF16)
            conv.append(cv)
            carry[s, j] = a[tm - FFN_HALO:tm, :]
        g, v = conv
        u = g * _sigmoid(g) * v
        yacc[...] += _dot(u.astype(BF16), wdn_ref[...])

        @pl.when(j == n_j - 1)
        def _():
            y = yacc[...]
            y_ref[...] = y
            xo_ref[...] = _post(x_ref[...], y, pv_ref[...])

    row = pl.BlockSpec((tm, d), lambda i, j: (i, 0))
    return _launch(
        body, "ffn_fwd", (n_i, n_j),
        [row, pl.BlockSpec((SUBLANES, d), lambda i, j: (0, 0)),
         pl.BlockSpec((2, 1, fc, d), lambda i, j: (0, j, 0, 0)),
         pl.BlockSpec((2, 1, SUBLANES, fc), lambda i, j: (0, j, 0, 0)),
         pl.BlockSpec((fc, d), lambda i, j: (j, 0))],
        [row, row, row, pl.BlockSpec((2, 1, tm, fc), lambda i, j: (0, j, i, 0)),
         pl.BlockSpec((2, 1, tm, fc), lambda i, j: (0, j, i, 0))],
        [jax.ShapeDtypeStruct((t_len, d), F32), jax.ShapeDtypeStruct((t_len, d), F32),
         jax.ShapeDtypeStruct((t_len, d), BF16), jax.ShapeDtypeStruct((2, n_j, t_len, fc), BF16),
         jax.ShapeDtypeStruct((2, n_j, t_len, fc), BF16)],
        [pltpu.VMEM((tm, d), BF16), pltpu.VMEM((tm, d), F32), pltpu.VMEM((2, n_j, FFN_HALO, fc), F32)],
        (x, pv, wup, wdw, wdn), ex)


def _ffn_bwd(dy, h, a_sav, c_sav, wup, wdw, wdn, *, tm, ex=None):
    t_len, d = dy.shape
    _, n_j, fc, _ = wup.shape
    n_i = t_len // tm

    def cur(i):
        return n_i - 1 - jnp.minimum(i, n_i - 1)

    def lag(i):
        return jnp.minimum(n_i - i, n_i - 1)

    def body(dyc_ref, dyl_ref, hl_ref, a_ref, c_ref, wup_ref, wdw_ref, wdn_ref,
             dhp_ref, gup_ref, gdn_ref, dwdw_ref, da_0, da_1, u_0, u_1, acc_up, acc_dn, carry):
        j, i = pl.program_id(0), pl.program_id(1)

        @pl.when((j == 0) & (i == 0))
        def _():
            dwdw_ref[...] = jnp.zeros_like(dwdw_ref)

        @pl.when(i == 0)
        def _():
            acc_up[...] = jnp.zeros_like(acc_up)
            acc_dn[...] = jnp.zeros_like(acc_dn)
            carry[...] = jnp.zeros_like(carry)
            da_1[...] = jnp.zeros_like(da_1)
            u_1[...] = jnp.zeros_like(u_1)

        def step(da_w, u_w, da_r, u_r):
            hl = hl_ref[...]
            live = (i < n_i).astype(F32)
            du = _dot_nt(dyc_ref[...], wdn_ref[...])
            g, v = c_ref[0, 0].astype(F32), c_ref[1, 0].astype(F32)
            sg = _sigmoid(g)
            sl = g * sg
            u_w[...] = (sl * v).astype(BF16)
            dh0 = _dot(da_r[0], wup_ref[0, 0])
            d2 = (du * v * (sg * (1.0 + g * (1.0 - sg))), du * sl)
            dhp_ref[0] = (dh0 + _dot(da_r[1], wup_ref[1, 0])).astype(BF16)
            for s in range(2):
                w = wdw_ref[s, 0]
                nxt = carry[s]
                p1, p2 = _shift_up(d2[s], 1, nxt), _shift_up(d2[s], 2, nxt)
                carry[s] = d2[s][0:FFN_HALO, :]
                da_w[s] = (w[2:3] * d2[s] + w[1:2] * p1 + w[0:1] * p2).astype(BF16)
                acc_up[s] += _dot_tn(da_r[s], hl)
                a_s = a_ref[s, 0].astype(F32)
                for k, sh in ((2, d2[s]), (1, p1), (0, p2)):
                    dwdw_ref[s, j, k:k + 1, :] += live * _colsum(a_s * sh)
            acc_dn[...] += _dot_tn(u_r[...], dyl_ref[...])

        @pl.when(i % 2 == 0)
        def _():
            step(da_0, u_0, da_1, u_1)

        @pl.when(i % 2 == 1)
        def _():
            step(da_1, u_1, da_0, u_0)

        @pl.when(i == n_i)
        def _():
            gup_ref[:, 0] = acc_up[...].astype(BF16)
            gdn_ref[...] = acc_dn[...].astype(BF16)

    chunk = lambda shape: pl.BlockSpec((2, 1) + shape, lambda j, i: (0, j, 0, 0))
    tile = pl.BlockSpec((2, 1, tm, fc), lambda j, i: (0, j, cur(i), 0))
    return _launch(
        body, "ffn_bwd", (n_j, n_i + 1),
        [pl.BlockSpec((tm, d), lambda j, i: (cur(i), 0)), pl.BlockSpec((tm, d), lambda j, i: (lag(i), 0)),
         pl.BlockSpec((tm, d), lambda j, i: (lag(i), 0)), tile, tile, chunk((fc, d)), chunk((SUBLANES, fc)),
         pl.BlockSpec((fc, d), lambda j, i: (j, 0))],
        [pl.BlockSpec((1, tm, d), lambda j, i: (j, lag(i), 0)), chunk((fc, d)),
         pl.BlockSpec((fc, d), lambda j, i: (j, 0)), pl.BlockSpec((2, n_j, SUBLANES, fc), lambda j, i: (0, 0, 0, 0))],
        [jax.ShapeDtypeStruct((n_j, t_len, d), BF16), jax.ShapeDtypeStruct((2, n_j, fc, d), BF16),
         jax.ShapeDtypeStruct((n_j * fc, d), BF16), jax.ShapeDtypeStruct((2, n_j, SUBLANES, fc), F32)],
        [pltpu.VMEM((2, tm, fc), BF16), pltpu.VMEM((2, tm, fc), BF16), pltpu.VMEM((tm, fc), BF16),
         pltpu.VMEM((tm, fc), BF16), pltpu.VMEM((2, fc, d), F32), pltpu.VMEM((fc, d), F32),
         pltpu.VMEM((2, FFN_HALO, fc), F32)],
        (dy, dy, h, a_sav, c_sav, wup, wdw, wdn), ex)


def _sub_post_bwd(dxo, ypre, pv, *, tm, ex=None):
    t_len, d = dxo.shape
    n_i = t_len // tm

    def body(dxo_ref, y_ref, pv_ref, dy_ref, pg_ref):
        @pl.when(pl.program_id(0) == 0)
        def _():
            pg_ref[...] = jnp.zeros_like(pg_ref)

        dy_ref[...] = _post_bwd(dxo_ref[...], y_ref[...], pv_ref[...], pg_ref).astype(BF16)

    row = pl.BlockSpec((tm, d), lambda i: (i, 0))
    small = pl.BlockSpec((SUBLANES, d), lambda i: (0, 0))
    return _launch(body, "sub_post_bwd", (n_i,), [row, row, small], [row, small],
                   [jax.ShapeDtypeStruct((t_len, d), BF16), jax.ShapeDtypeStruct((SUBLANES, d), F32)], [],
                   (dxo, ypre, pv), ex)


def _sub_pre_bwd(dhp, x, dxo, pv, *, tm, ex=None):
    n_p, t_len, d = dhp.shape
    n_i = t_len // tm

    def body(dhp_ref, x_ref, dxo_ref, pv_ref, dx_ref, pg_ref):
        @pl.when(pl.program_id(0) == 0)
        def _():
            pg_ref[...] = jnp.zeros_like(pg_ref)

        dh = dhp_ref[0].astype(F32)
        for p in range(1, n_p):
            dh = dh + dhp_ref[p].astype(F32)
        dx_ref[...] = dxo_ref[...] + _pre_bwd(dh, x_ref[...], pv_ref[...], pg_ref)

    row = pl.BlockSpec((tm, d), lambda i: (i, 0))
    small = pl.BlockSpec((SUBLANES, d), lambda i: (0, 0))
    return _launch(body, "sub_pre_bwd", (n_i,), [pl.BlockSpec((n_p, tm, d), lambda i: (0, i, 0)), row, row, small],
                   [row, small], [jax.ShapeDtypeStruct((t_len, d), F32), jax.ShapeDtypeStruct((SUBLANES, d), F32)], [],
                   (dhp, x, dxo, pv), ex)


CV_B1, CV_BDW, CV_LNG, CV_LNB, CV_B2 = 0, 1, 2, 3, 4


def _taps_by_residue(offs):
    groups = {}
    for k, off in enumerate(offs):
        groups.setdefault(off % SUBLANES, []).append((k, off // SUBLANES))
    return sorted(groups.items())


def _depthwise(buf, w_ref, out_ref, offs, tm, d):
    taps_of = _taps_by_residue(offs)

    def chunk(r, carry):
        r0 = pl.multiple_of(r * CONV_ROWS, CONV_ROWS)
        for cb in range(d // LANES):
            cols = slice(cb * LANES, (cb + 1) * LANES)
            win = buf[pl.ds(r0, CONV_ROWS + CONV_HALO), cols]
            acc = jnp.zeros((CONV_ROWS, LANES), F32)
            for b, taps in taps_of:
                wb = win if b == 0 else pltpu.roll(win, CONV_ROWS + CONV_HALO - b, 0)
                for k, a in taps:
                    acc = acc + wb[SUBLANES * a:SUBLANES * a + CONV_ROWS, :] * w_ref[k:k + 1, cols]
            out_ref[pl.ds(r0, CONV_ROWS), cols] = acc
        return carry

    lax.fori_loop(0, tm // CONV_ROWS, chunk, 0)


def _depthwise_wgrad(dbuf, ubuf, dw_ref, tm, d):
    taps_of = _taps_by_residue(tuple(2 + k for k in range(CONV_TAPS)))
    for cb in range(d // LANES):
        cols = slice(cb * LANES, (cb + 1) * LANES)

        def chunk(r, acc):
            r0 = pl.multiple_of(r * CONV_ROWS, CONV_ROWS)
            dv = dbuf[pl.ds(r0, CONV_ROWS), cols]
            win = ubuf[pl.ds(r0, CONV_ROWS + CONV_HALO), cols]
            new = list(acc)
            for b, taps in taps_of:
                wb = win if b == 0 else pltpu.roll(win, CONV_ROWS + CONV_HALO - b, 0)
                for k, a in taps:
                    for q in range(CONV_ROWS // SUBLANES):
                        new[k] = new[k] + dv[SUBLANES * q:SUBLANES * (q + 1), :] * wb[SUBLANES * (a + q):SUBLANES * (a + q + 1), :]
            return tuple(new)

        acc = lax.fori_loop(0, tm // CONV_ROWS, chunk, tuple(jnp.zeros((SUBLANES, LANES), F32) for _ in range(CONV_TAPS)))
        for k in range(CONV_TAPS):
            dw_ref[k:k + 1, cols] += _colsum(acc[k])


def _layer_norm_parts(c1):
    mu = jnp.mean(c1, axis=-1, keepdims=True)
    cen = c1 - mu
    rstd = lax.rsqrt(jnp.mean(cen * cen, axis=-1, keepdims=True) + NORM_EPS)
    return cen * rstd, rstd


def _conv_fwd(x, pv, w1, w2, wdw, vec, *, tm):
    t_len, d = x.shape
    n_i = t_len // tm
    n_q = w1.shape[0] // 2
    qc = w1.shape[2]

    def body(x_ref, pv_ref, w1_ref, w2_ref, wdw_ref, vec_ref, xo_ref, y_ref, a_ref, c1_ref, ubuf):
        i = pl.program_id(0)

        @pl.when(i == 0)
        def _():
            ubuf[0:CONV_HALO, :] = jnp.zeros((CONV_HALO, d), F32)

        xv, pv_ = x_ref[...], pv_ref[...]
        h, _, _ = _prenorm(xv, pv_)
        hb = h.astype(BF16)
        for q in range(n_q):
            cols = slice(q * qc, (q + 1) * qc)
            gcols = slice(d + q * qc, d + (q + 1) * qc)
            val = (_dot(hb, w1_ref[q]) + vec_ref[CV_B1:CV_B1 + 1, cols]).astype(BF16)
            gate = (_dot(hb, w1_ref[n_q + q]) + vec_ref[CV_B1:CV_B1 + 1, gcols]).astype(BF16)
            a_ref[:, cols] = val
            a_ref[:, gcols] = gate
            ubuf[CONV_HALO:CONV_HALO + tm, cols] = val.astype(F32) * _sigmoid(gate.astype(F32))
        _depthwise(ubuf, wdw_ref, c1_ref, tuple(2 + k for k in range(CONV_TAPS)), tm, d)
        c1 = c1_ref[...] + vec_ref[CV_BDW:CV_BDW + 1, 0:d]
        c1_ref[...] = c1
        xhat, _ = _layer_norm_parts(c1)
        ln = xhat * vec_ref[CV_LNG:CV_LNG + 1, 0:d] + vec_ref[CV_LNB:CV_LNB + 1, 0:d]
        s = ln * _sigmoid(ln)
        y = _dot(s.astype(BF16), w2_ref[...]) + vec_ref[CV_B2:CV_B2 + 1, 0:d]
        y_ref[...] = y
        xo_ref[...] = _post(xv, y, pv_)
        ubuf[0:CONV_HALO, :] = ubuf[tm:tm + CONV_HALO, :]

    row = pl.BlockSpec((tm, d), lambda i: (i, 0))
    return pl.pallas_call(
        body, name="conv_fwd", grid=(n_i,),
        in_specs=[row, pl.BlockSpec((SUBLANES, d), lambda i: (0, 0)), pl.BlockSpec(w1.shape, lambda i: (0, 0, 0)),
                  pl.BlockSpec(w2.shape, lambda i: (0, 0)), pl.BlockSpec(wdw.shape, lambda i: (0, 0)),
                  pl.BlockSpec(vec.shape, lambda i: (0, 0))],
        out_specs=[row, row, pl.BlockSpec((tm, 2 * d), lambda i: (i, 0)), row],
        out_shape=[jax.ShapeDtypeStruct((t_len, d), F32), jax.ShapeDtypeStruct((t_len, d), F32),
                   jax.ShapeDtypeStruct((t_len, 2 * d), BF16), jax.ShapeDtypeStruct((t_len, d), F32)],
        scratch_shapes=[pltpu.VMEM((tm + CONV_HALO, d), F32)],
        compiler_params=_params(("arbitrary",)),
    )(x, pv, w1, w2, wdw, vec)


def _conv_bwd(dxo, x, ypre, a_sav, c1_sav, pv, w1, w2, wdw, vec, *, tm, ex=None):
    t_len, d = x.shape
    n_i = t_len // tm
    n_q = w1.shape[0] // 2
    qc = w1.shape[2]
    hb_ = tm // CONV_HALO

    def body(dxo_ref, x_ref, y_ref, a_ref, ah_ref, c1_ref, pv_ref, w1_ref, w2_ref, wdw_ref, vec_ref,
             dx_ref, pg_ref, gw1_ref, gw2_ref, gvec_ref, gwdw_ref, ubuf, dcbuf, dubuf, acc1, acc2):
        i = pl.program_id(0)
        ti = n_i - 1 - i

        @pl.when(i == 0)
        def _():
            pg_ref[...] = jnp.zeros_like(pg_ref)
            gvec_ref[...] = jnp.zeros_like(gvec_ref)
            gwdw_ref[...] = jnp.zeros_like(gwdw_ref)
            acc1[...] = jnp.zeros_like(acc1)
            acc2[...] = jnp.zeros_like(acc2)
            dcbuf[tm:tm + CONV_HALO, :] = jnp.zeros((CONV_HALO, d), F32)

        xv, pv_, dxo_v = x_ref[...], pv_ref[...], dxo_ref[...]
        dy = _post_bwd(dxo_v, y_ref[...], pv_, pg_ref)
        gvec_ref[CV_B2:CV_B2 + 1, 0:d] += _colsum(dy)
        dyb = dy.astype(BF16)
        xhat, rstd = _layer_norm_parts(c1_ref[...])
        lng = vec_ref[CV_LNG:CV_LNG + 1, 0:d]
        ln = xhat * lng + vec_ref[CV_LNB:CV_LNB + 1, 0:d]
        sg = _sigmoid(ln)
        acc2[...] += _dot_tn((ln * sg).astype(BF16), dyb)
        dln = _dot_nt(dyb, w2_ref[...]) * (sg * (1.0 + ln * (1.0 - sg)))
        gvec_ref[CV_LNG:CV_LNG + 1, 0:d] += _colsum(dln * xhat)
        gvec_ref[CV_LNB:CV_LNB + 1, 0:d] += _colsum(dln)
        dxh = dln * lng
        dc1 = rstd * (dxh - jnp.mean(dxh, axis=-1, keepdims=True)
                      - xhat * jnp.mean(dxh * xhat, axis=-1, keepdims=True))
        gvec_ref[CV_BDW:CV_BDW + 1, 0:d] += _colsum(dc1)
        dcbuf[0:tm, :] = dc1
        for q in range(n_q):
            cols = slice(q * qc, (q + 1) * qc)
            gcols = slice(d + q * qc, d + (q + 1) * qc)
            ubuf[CONV_HALO:CONV_HALO + tm, cols] = a_ref[:, cols].astype(F32) * _sigmoid(a_ref[:, gcols].astype(F32))
            uh = ah_ref[:, cols].astype(F32) * _sigmoid(ah_ref[:, gcols].astype(F32))
            ubuf[0:CONV_HALO, cols] = jnp.where(ti > 0, uh, 0.0)
        _depthwise_wgrad(dcbuf, ubuf, gwdw_ref, tm, d)
        _depthwise(dcbuf, wdw_ref, dubuf, tuple(CONV_TAPS - 1 - k for k in range(CONV_TAPS)), tm, d)
        dcbuf[tm:tm + CONV_HALO, :] = dcbuf[0:CONV_HALO, :]
        h, _, _ = _prenorm(xv, pv_)
        hb = h.astype(BF16)
        dh = jnp.zeros((tm, d), F32)
        for q in range(n_q):
            cols = slice(q * qc, (q + 1) * qc)
            gcols = slice(d + q * qc, d + (q + 1) * qc)
            du = dubuf[:, cols]
            val, gate = a_ref[:, cols].astype(F32), a_ref[:, gcols].astype(F32)
            sgg = _sigmoid(gate)
            dval = du * sgg
            dgate = du * val * (sgg * (1.0 - sgg))
            gvec_ref[CV_B1:CV_B1 + 1, cols] += _colsum(dval)
            gvec_ref[CV_B1:CV_B1 + 1, gcols] += _colsum(dgate)
            dvb, dgb = dval.astype(BF16), dgate.astype(BF16)
            acc1[q] += _dot_tn(hb, dvb)
            acc1[n_q + q] += _dot_tn(hb, dgb)
            dh = dh + _dot_nt(dvb, w1_ref[q]) + _dot_nt(dgb, w1_ref[n_q + q])
        dx_ref[...] = dxo_v + _pre_bwd(dh, xv, pv_, pg_ref)

        @pl.when(i == n_i - 1)
        def _():
            gw1_ref[...] = acc1[...].astype(BF16)
            gw2_ref[...] = acc2[...].astype(BF16)

    row = pl.BlockSpec((tm, d), lambda i: (n_i - 1 - i, 0))
    small = pl.BlockSpec((SUBLANES, d), lambda i: (0, 0))
    whole2 = lambda shape: pl.BlockSpec(shape, lambda i: (0, 0))
    return _launch(
        body, "conv_bwd", (n_i,),
        [row, row, row,
         pl.BlockSpec((tm, 2 * d), lambda i: (n_i - 1 - i, 0)),
         pl.BlockSpec((CONV_HALO, 2 * d), lambda i: (jnp.maximum((n_i - 1 - i) * hb_ - 1, 0), 0)),
         row, small, pl.BlockSpec(w1.shape, lambda i: (0, 0, 0)), whole2(w2.shape), whole2(wdw.shape),
         whole2(vec.shape)],
        [row, small, pl.BlockSpec(w1.shape, lambda i: (0, 0, 0)), whole2(w2.shape), whole2(vec.shape),
         whole2(wdw.shape)],
        [jax.ShapeDtypeStruct((t_len, d), F32), jax.ShapeDtypeStruct((SUBLANES, d), F32),
         jax.ShapeDtypeStruct(w1.shape, BF16), jax.ShapeDtypeStruct(w2.shape, BF16),
         jax.ShapeDtypeStruct(vec.shape, F32), jax.ShapeDtypeStruct(wdw.shape, F32)],
        [pltpu.VMEM((tm + CONV_HALO, d), F32), pltpu.VMEM((tm + CONV_HALO, d), F32),
         pltpu.VMEM((tm, d), F32), pltpu.VMEM(w1.shape, F32), pltpu.VMEM(w2.shape, F32)],
        (dxo, x, ypre, a_sav, a_sav, c1_sav, pv, w1, w2, wdw, vec), ex)


def _loss_head(y, target, *, tm):
    t_len, d = y.shape
    n_i = t_len // tm

    def body(y_ref, t_ref, dy_ref, sq_ref):
        @pl.when(pl.program_id(0) == 0)
        def _():
            sq_ref[...] = jnp.zeros_like(sq_ref)

        err = y_ref[...] - t_ref[...]
        dy_ref[...] = err * (1.0 / d)
        sq_ref[...] += jnp.sum((err * err).reshape(tm // SUBLANES, SUBLANES, d), axis=0)

    row = pl.BlockSpec((tm, d), lambda i: (i, 0))
    return pl.pallas_call(
        body, name="loss_head", grid=(n_i,), in_specs=[row, row],
        out_specs=[row, pl.BlockSpec((SUBLANES, d), lambda i: (0, 0))],
        out_shape=[jax.ShapeDtypeStruct((t_len, d), F32), jax.ShapeDtypeStruct((SUBLANES, d), F32)],
        compiler_params=_params(("arbitrary",)),
    )(y, target)


def _adam(name, parts, w, m, v):
    layers = len(parts)
    n, rows, cols = parts[0].shape
    tr = rows
    if rows % SUBLANES == 0:
        cap = max(SUBLANES, ADAM_BLOCK_BYTES // (4 * cols))
        tr = max(t for t in range(SUBLANES, rows + 1, SUBLANES) if rows % t == 0 and (t <= cap or t == SUBLANES))
    c1 = 1.0 / (1.0 - ADAM_B1 ** ADAM_STEP)
    c2 = 1.0 / (1.0 - ADAM_B2 ** ADAM_STEP)

    def body(*refs):
        p_refs = refs[:layers]
        w_ref, m_ref, v_ref, g_ref, d_ref, mo_ref, vo_ref, g_s = refs[layers:]
        for l in range(layers):
            @pl.when(pl.program_id(0) == l)
            def _():
                g = p_refs[l][0].astype(F32)
                for k in range(1, n):
                    g = g + p_refs[l][k].astype(F32)
                g_s[...] = g

        g = g_s[...]
        m2 = ADAM_B1 * m_ref[0] + (1.0 - ADAM_B1) * g
        v2 = ADAM_B2 * v_ref[0] + (1.0 - ADAM_B2) * (g * g)
        g_ref[0] = g
        mo_ref[0] = m2
        vo_ref[0] = v2
        d_ref[0] = -ADAM_LR * ((m2 * c1) / (jnp.sqrt(v2 * c2) + ADAM_EPS) + ADAM_WD * w_ref[0])

    def part_spec(l):
        return pl.BlockSpec((n, tr, cols), lambda ll, i: (0, jnp.where(ll == l, i, 0), 0))

    blk = pl.BlockSpec((1, tr, cols), lambda ll, i: (ll, i, 0))
    out = jax.ShapeDtypeStruct((layers, rows, cols), F32)
    return pl.pallas_call(
        body, name=name, grid=(layers, rows // tr),
        in_specs=[part_spec(l) for l in range(layers)] + [blk, blk, blk],
        out_specs=[blk, blk, blk, blk], out_shape=[out, out, out, out],
        scratch_shapes=[pltpu.VMEM((tr, cols), F32)],
        compiler_params=_params(("arbitrary", "arbitrary")),
    )(*parts, w, m, v)


def _adam_nd(name, parts, w, m, v):
    shape = w.shape
    cols = shape[-1]
    if isinstance(parts, (list, tuple)):
        layers = len(parts)
    else:
        layers, parts = 1, [parts]
    rows = w.size // (cols * layers)
    flat = lambda t: t.reshape(layers, rows, cols)
    outs = _adam(name, [p.reshape(p.shape[0], rows, cols) for p in parts], flat(w), flat(m), flat(v))
    return [o.reshape(shape) for o in outs]


def _small_pack(parts, size):
    flat = jnp.concatenate([p.reshape(-1) for p in parts])
    return jnp.pad(flat, (0, size - flat.shape[0]))


def _to_shards(full, axis):
    shp = full.shape
    split = full.reshape(shp[:axis] + (N_DEV, shp[axis] // N_DEV) + shp[axis + 1:])
    return jnp.moveaxis(split, axis, 0)


def _from_shards(sh, axis):
    moved = jnp.moveaxis(sh, 0, axis)
    shp = moved.shape
    return moved.reshape(shp[:axis] + (shp[axis] * shp[axis + 1],) + shp[axis + 2:])


def kernel(x, c, ada_w, ada_b, pre_g, post_g, pool_w, pool_scale, cv_w_pw1, cv_b_pw1, cv_w_dw, cv_b_dw, cv_ln_g, cv_ln_b, cv_w_pw2, cv_b_pw2, ffn_w_up, ffn_w_dw, ffn_w_down, loss_target, m_ada_w, m_ada_b, m_pre_g, m_post_g, m_pool_w, m_pool_scale, m_cv_w_pw1, m_cv_b_pw1, m_cv_w_dw, m_cv_b_dw, m_cv_ln_g, m_cv_ln_b, m_cv_w_pw2, m_cv_b_pw2, m_ffn_w_up, m_ffn_w_dw, m_ffn_w_down, v_ada_w, v_ada_b, v_pre_g, v_post_g, v_pool_w, v_pool_scale, v_cv_w_pw1, v_cv_b_pw1, v_cv_w_dw, v_cv_b_dw, v_cv_ln_g, v_cv_ln_b, v_cv_w_pw2, v_cv_b_pw2, v_ffn_w_up, v_ffn_w_dw, v_ffn_w_down):
    t_len, d = x.shape[1], x.shape[2]
    depth = ada_w.shape[0]
    fc = ffn_w_up.shape[2]
    n_j = N_DEV // 2
    me = 4 * lax.axis_index("x") + 2 * lax.axis_index("y") + lax.axis_index("c")

    small_w = [pre_g, post_g, cv_b_pw1, cv_w_dw, cv_b_dw, cv_ln_g, cv_ln_b, cv_b_pw2, ffn_w_dw]
    small_m = [m_pre_g, m_post_g, m_cv_b_pw1, m_cv_w_dw, m_cv_b_dw, m_cv_ln_g, m_cv_ln_b, m_cv_b_pw2, m_ffn_w_dw]
    small_v = [v_pre_g, v_post_g, v_cv_b_pw1, v_cv_w_dw, v_cv_b_dw, v_cv_ln_g, v_cv_ln_b, v_cv_b_pw2, v_ffn_w_dw]
    sizes = [p.size for p in small_w]
    offs = [sum(sizes[:k]) for k in range(len(sizes) + 1)]
    pack = -(-offs[-1] // (SUBLANES * LANES)) * SUBLANES * LANES

    got = _exchange("gather_small", [c, _small_pack(small_w, pack), pool_w[0].astype(BF16)], [True] * 3)
    c_all = got[0].reshape(N_DEV, d)
    smalls = [got[1][:, offs[k]:offs[k + 1]].reshape((N_DEV,) + small_w[k].shape) for k in range(len(small_w))]
    pre_g_f, post_g_f = _from_shards(smalls[0], 2), _from_shards(smalls[1], 2)
    b1_f = _from_shards(smalls[2], 1)[0]
    cvw_f = jnp.pad(_from_shards(smalls[3], 2)[0], ((0, CONV_HALO - CONV_TAPS), (0, 0)))
    bdw_f, lng_f, lnb_f, b2_f = [_from_shards(smalls[k], 1)[0] for k in (4, 5, 6, 7)]
    fdw = jnp.pad(smalls[8], ((0, 0), (0, 0), (0, SUBLANES - FFN_TAPS), (0, 0)))
    wp = jnp.swapaxes(got[2], 0, 1).reshape(pool_w.shape[1], -1, pool_w.shape[3])
    wdw = [fdw[:, l].reshape(2, n_j, SUBLANES, fc) for l in range(depth)]
    cvec = jnp.zeros((SUBLANES, 2 * d), F32)
    cvec = cvec.at[CV_B1].set(b1_f)
    for r, vrow in ((CV_BDW, bdw_f), (CV_LNG, lng_f), (CV_LNB, lnb_f), (CV_B2, b2_f)):
        cvec = cvec.at[r, :d].set(vrow)

    mod_cols = _ada_fwd(c_all, ada_w)
    (mod_all,) = _exchange("gather_mod", [mod_cols], [True])
    mod = lax.dynamic_index_in_dim(mod_all, me, axis=2, keepdims=False)
    mod = jnp.swapaxes(mod, 0, 1).reshape(depth, N_MOD, d) + ada_b.reshape(depth, N_MOD, d)

    def pv_of(l, s):
        rows = [pre_g_f[l, s], 1.0 + mod[l, 3 * s + 1], mod[l, 3 * s], post_g_f[l, s], mod[l, 3 * s + 2]]
        return jnp.concatenate([jnp.stack(rows), jnp.zeros((SUBLANES - len(rows), d), F32)])

    x0 = x[0]
    pv00, pv01, pv10, pv11 = pv_of(0, 0), pv_of(0, 1), pv_of(1, 0), pv_of(1, 1)
    tm_pool, tm_ffn, tm_bwd, tm_conv = min(TM_POOL, t_len), min(TM_FFN, t_len), min(TM_FFN_BWD, t_len), min(TM_CONV, t_len)
    tm_w = min(TM_FFN_W, t_len)
    wup_t = jnp.swapaxes(ffn_w_up, 1, 2)
    ex = _TwoLevelGather([wup_t[0].astype(BF16), ffn_w_down[0].astype(BF16)])
    (x1, y0), (wup0, wdn0) = _pool_fwd(x0, pv00, wp, pool_scale, tm=tm_pool, ex=ex)
    ex = _TwoLevelGather([cv_w_pw1[0].astype(BF16), cv_w_pw2[0].astype(BF16), wup_t[1].astype(BF16),
                          ffn_w_down[1].astype(BF16)])
    (x2, y1, h1, a1, c1), (w1, w2, wup1, wdn1) = _ffn_fwd(
        x1, pv01, wup0.reshape(2, n_j, fc, d), wdw[0], wdn0.reshape(n_j * fc, d), tm=tm_ffn, ex=ex)
    w2 = w2.reshape(d, d)
    wup = [wup0.reshape(2, n_j, fc, d), wup1.reshape(2, n_j, fc, d)]
    wdn = [wdn0.reshape(n_j * fc, d), wdn1.reshape(n_j * fc, d)]
    x3, y2, a2, c2 = _conv_fwd(x2, pv10, w1, w2, cvw_f, cvec, tm=tm_conv)
    (x4, y3, h3, a3, c3), _ = _ffn_fwd(x3, pv11, wup[1], wdw[1], wdn[1], tm=tm_ffn)
    dx4, sq = _loss_head(x4, loss_target[0], tm=min(TM_LOSS, t_len))
    loss = lax.psum(jnp.sum(sq) * (0.5 / d), MESH_AXES)

    (dy3, pgq11), _ = _sub_post_bwd(dx4, y3, pv11, tm=tm_w)
    (dhp3, gup1, gdn1, gfdw1), _ = _ffn_bwd(dy3, h3, a3, c3, wup[1], wdw[1], wdn[1], tm=tm_bwd)
    (dx3, pgp11), _ = _sub_pre_bwd(dhp3, x3, dx4, pv11, tm=tm_ffn)
    ex = _Exchange([gup1.reshape(N_DEV, fc, d), gdn1.reshape(N_DEV, -1, d)], [False, False])
    (dx2, pg10, gw1, gw2, gcvec, gcvw), (rup1, rdn1) = _conv_bwd(
        dx3, x2, y2, a2, c2, pv10, w1, w2, cvw_f, cvec, tm=tm_conv, ex=ex)
    (dy1, pgq01), _ = _sub_post_bwd(dx2, y1, pv01, tm=tm_w)
    ex = _Exchange([gw1, gw2.reshape(N_DEV, -1, d)], [False, False])
    (dhp1, gup0, gdn0, gfdw0), (rw1, rw2) = _ffn_bwd(dy1, h1, a1, c1, wup[0], wdw[0], wdn[0], tm=tm_bwd, ex=ex)
    ex = _Exchange([gdn0.reshape(N_DEV, -1, d)], [False])
    (dx1, pgp01), (rdn0,) = _sub_pre_bwd(dhp1, x1, dx2, pv01, tm=tm_ffn, ex=ex)
    ex = _Exchange([gup0.reshape(N_DEV, fc, d)], [False])
    (dx0, pg00, gwp), (rup0,) = _pool_bwd(dx1, x0, y0, pv00, wp, pool_scale, tm=tm_pool, ex=ex)
    pg01, pg11 = pgq01 + pgp01, pgq11 + pgp11

    pgs = [[pg00, pg01], [pg10, pg11]]
    g_pre = jnp.stack([jnp.stack([pgs[l][s][PG_GPRE] for s in range(2)]) for l in range(depth)])
    g_post = jnp.stack([jnp.stack([pgs[l][s][PG_GPOST] for s in range(2)]) for l in range(depth)])
    dmod = jnp.stack([jnp.concatenate([pgs[l][s][r] for s in range(2) for r in (PG_SH, PG_SC, PG_GT)])
                      for l in range(depth)])
    gfdw = jnp.stack([g.reshape(N_DEV, SUBLANES, fc)[:, :FFN_TAPS] for g in (gfdw0, gfdw1)], axis=1)
    small_g = [_to_shards(g_pre, 2), _to_shards(g_post, 2), _to_shards(gcvec[CV_B1][None], 1),
               _to_shards(gcvw[None, :CONV_TAPS], 2), _to_shards(gcvec[CV_BDW, :d][None], 1),
               _to_shards(gcvec[CV_LNG, :d][None], 1), _to_shards(gcvec[CV_LNB, :d][None], 1),
               _to_shards(gcvec[CV_B2, :d][None], 1), gfdw]
    small_send = jnp.concatenate([g.reshape(N_DEV, -1) for g in small_g], axis=1)
    small_send = jnp.pad(small_send, ((0, 0), (0, pack - small_send.shape[1])))
    gwp_send = jnp.swapaxes(gwp.reshape(gwp.shape[0], N_DEV, -1, gwp.shape[2]), 0, 1)
    rsmall, rwp, rmod, rscale = _exchange("scatter_small", [small_send, gwp_send, dmod, pg00[PG_EXTRA][None]],
                                          [False, False, True, True])

    outs = {}

    def put(name, res, shape=None):
        outs[name] = [r if shape is None else r.reshape(shape) for r in res]

    small_res = _adam_nd("adam_small", rsmall.reshape(N_DEV, -1, SUBLANES * LANES),
                         _small_pack(small_w, pack).reshape(-1, SUBLANES * LANES),
                         _small_pack(small_m, pack).reshape(-1, SUBLANES * LANES),
                         _small_pack(small_v, pack).reshape(-1, SUBLANES * LANES))
    small_names = ["pre_g", "post_g", "cv_b_pw1", "cv_w_dw", "cv_b_dw", "cv_ln_g", "cv_ln_b", "cv_b_pw2", "ffn_w_dw"]
    for k, nm in enumerate(small_names):
        outs[nm] = [r.reshape(-1)[offs[k]:offs[k + 1]].reshape(small_w[k].shape) for r in small_res]
    put("pool_w", _adam_nd("adam_pool_w", rwp[:, None], pool_w, m_pool_w, v_pool_w))
    put("cv_w_pw1", _adam_nd("adam_cv_w_pw1", rw1[:, None], cv_w_pw1, m_cv_w_pw1, v_cv_w_pw1))
    put("cv_w_pw2", _adam_nd("adam_cv_w_pw2", rw2[:, None], cv_w_pw2, m_cv_w_pw2, v_cv_w_pw2))
    outs["ffn_w_up"] = [jnp.swapaxes(r, 1, 2) for r in _adam_nd(
        "adam_ffn_w_up", [rup0, rup1], wup_t, jnp.swapaxes(m_ffn_w_up, 1, 2), jnp.swapaxes(v_ffn_w_up, 1, 2))]
    put("ffn_w_down", _adam_nd("adam_ffn_w_down", [rdn0, rdn1], ffn_w_down, m_ffn_w_down, v_ffn_w_down))
    put("ada_b", _adam_nd("adam_ada_b", rmod, ada_b, m_ada_b, v_ada_b))
    put("pool_scale", _adam_nd("adam_pool_scale", rscale, pool_scale, m_pool_scale, v_pool_scale))
    cols = ada_w.shape[2]
    dmod_cols = jnp.swapaxes(lax.dynamic_slice_in_dim(rmod, me * cols, cols, axis=2), 0, 1)
    put("ada_w", _adam_nd("adam_ada_w", _ada_bwd(c_all, dmod_cols)[None], ada_w, m_ada_w, v_ada_w))

    order = ["ada_w", "ada_b", "pre_g", "post_g", "pool_w", "pool_scale", "cv_w_pw1", "cv_b_pw1", "cv_w_dw", "cv_b_dw",
             "cv_ln_g", "cv_ln_b", "cv_w_pw2", "cv_b_pw2", "ffn_w_up", "ffn_w_dw", "ffn_w_down"]
    return (loss, dx0[None], *[outs[nm][0] for nm in order], *[outs[nm][1] for nm in order],
            *[outs[nm][2] for nm in order], *[outs[nm][3] for nm in order])
```

```python
import functools

import jax
import jax.numpy as jnp
from jax import lax
from jax.experimental import pallas as pl
from jax.experimental.pallas import tpu as pltpu

F32, BF16 = jnp.float32, jnp.bfloat16
MESH_AXES = ("x", "y", "c")
N_DEV = 8
NORM_EPS = 1e-6
ADAM_LR, ADAM_B1, ADAM_B2, ADAM_EPS, ADAM_WD, ADAM_STEP = 0.001, 0.9, 0.999, 1e-08, 0.01, 10
POOL_WINDOWS = (2, 4, 8, 16)
CONV_TAPS = 31
FFN_TAPS = 3
N_MOD = 6

SUBLANES = 8
LANES = 128
VMEM_LIMIT_BYTES = 56 * 1024 * 1024
POOL_HALO = 16
CONV_HALO = 32
FFN_HALO = 8
TM_POOL, TM_FFN, TM_FFN_BWD, TM_FFN_W, TM_CONV = 512, 512, 512, 1024, 256
CONV_ROWS = 32
ADAM_BLOCK_BYTES = 512 * 1024

PV_GPRE, PV_SC1, PV_SH, PV_GPOST, PV_GT = 0, 1, 2, 3, 4
PG_GPRE, PG_SC, PG_SH, PG_GPOST, PG_GT, PG_EXTRA = 0, 1, 2, 3, 4, 5


def _params(sem):
    return pltpu.CompilerParams(dimension_semantics=sem, vmem_limit_bytes=VMEM_LIMIT_BYTES)


def _dot(a, b):
    return jnp.dot(a, b, preferred_element_type=F32)


def _dot_nt(a, b):
    return lax.dot_general(a, b, (((1,), (1,)), ((), ())), preferred_element_type=F32)


def _dot_tn(a, b):
    return lax.dot_general(a, b, (((0,), (0,)), ((), ())), preferred_element_type=F32)


def _sigmoid(x):
    return 1.0 / (1.0 + jnp.exp(-x))


def _rms(x):
    return lax.rsqrt(jnp.mean(x * x, axis=-1, keepdims=True) + NORM_EPS)


def _colsum(v):
    return jnp.sum(v, axis=0, keepdims=True)


def _prenorm(x, pv):
    r = _rms(x)
    xn = x * r
    return xn * (pv[PV_GPRE:PV_GPRE + 1] * pv[PV_SC1:PV_SC1 + 1]) + pv[PV_SH:PV_SH + 1], xn, r


def _post(x, y, pv):
    return x + (pv[PV_GT:PV_GT + 1] * pv[PV_GPOST:PV_GPOST + 1]) * (y * _rms(y))


def _post_bwd(dxo, y, pv, pg_ref):
    ry = _rms(y)
    yn = y * ry
    gt, gpost = pv[PV_GT:PV_GT + 1], pv[PV_GPOST:PV_GPOST + 1]
    dyn = dxo * (gt * gpost)
    dy = ry * (dyn - yn * jnp.mean(dyn * yn, axis=-1, keepdims=True))
    s = _colsum(dxo * yn)
    pg_ref[PG_GPOST:PG_GPOST + 1, :] += s * gt
    pg_ref[PG_GT:PG_GT + 1, :] += s * gpost
    return dy


def _pre_bwd(dh, x, pv, pg_ref):
    r = _rms(x)
    xn = x * r
    gpre, sc1 = pv[PV_GPRE:PV_GPRE + 1], pv[PV_SC1:PV_SC1 + 1]
    dxn = dh * (gpre * sc1)
    dx = r * (dxn - xn * jnp.mean(dxn * xn, axis=-1, keepdims=True))
    p = _colsum(dh * xn)
    pg_ref[PG_GPRE:PG_GPRE + 1, :] += p * sc1
    pg_ref[PG_SC:PG_SC + 1, :] += p * gpre
    pg_ref[PG_SH:PG_SH + 1, :] += _colsum(dh)
    return dx


def _shift_down(a, k, prev):
    out = pltpu.roll(a, k, 0)
    row = lax.broadcasted_iota(jnp.int32, a.shape, 0)
    for q in range(k):
        out = jnp.where(row == q, prev[SUBLANES - k + q:SUBLANES - k + q + 1, :], out)
    return out


def _shift_up(a, k, nxt):
    rows = a.shape[0]
    out = pltpu.roll(a, rows - k, 0)
    row = lax.broadcasted_iota(jnp.int32, a.shape, 0)
    for q in range(k):
        out = jnp.where(row == rows - k + q, nxt[q:q + 1, :], out)
    return out


class _Exchange:
    def __init__(self, srcs, gathers):
        self.srcs, self.gathers, self.n = list(srcs), list(gathers), len(srcs)
        self.out_shape = [jax.ShapeDtypeStruct(((N_DEV,) + s.shape) if g else s.shape, s.dtype)
                          for s, g in zip(srcs, gathers)]
        self.specs = [pl.BlockSpec(memory_space=pl.ANY)] * self.n
        self.scratch = [pltpu.SemaphoreType.DMA((self.n, N_DEV - 1)), pltpu.SemaphoreType.DMA((self.n, N_DEV - 1)),
                        pltpu.SemaphoreType.DMA((self.n,))]

    def _copies(self, src_refs, out_refs, sems):
        send_sems, recv_sems, local_sems = sems
        x, y, c = lax.axis_index("x"), lax.axis_index("y"), lax.axis_index("c")
        me = 4 * x + 2 * y + c
        copies = []
        for a in range(self.n):
            mine = src_refs[a] if self.gathers[a] else src_refs[a].at[me]
            copies.append(pltpu.make_async_copy(mine, out_refs[a].at[me], local_sems.at[a]))
        for d in range(1, N_DEV):
            px, py, pc = (x + (d >> 2)) % 2, (y + ((d >> 1) & 1)) % 2, (c + (d & 1)) % 2
            peer = 4 * px + 2 * py + pc
            for a in range(self.n):
                src = src_refs[a] if self.gathers[a] else src_refs[a].at[peer]
                copies.append(pltpu.make_async_remote_copy(
                    src_ref=src, dst_ref=out_refs[a].at[me], send_sem=send_sems.at[a, d - 1],
                    recv_sem=recv_sems.at[a, d - 1], device_id=(px, py, pc), device_id_type=pl.DeviceIdType.MESH))
        return copies

    def start(self, src_refs, out_refs, sems):
        for cp in self._copies(src_refs, out_refs, sems):
            cp.start()

    def wait(self, src_refs, out_refs, sems):
        for cp in self._copies(src_refs, out_refs, sems):
            cp.wait()


class _TwoLevelGather:
    def __init__(self, srcs):
        self.srcs, self.n = list(srcs), len(srcs)
        self.out_shape = [jax.ShapeDtypeStruct((N_DEV,) + s.shape, s.dtype) for s in srcs]
        self.specs = [pl.BlockSpec(memory_space=pl.ANY)] * self.n
        self.scratch = [pltpu.SemaphoreType.DMA((self.n, N_DEV - 1)), pltpu.SemaphoreType.DMA((self.n, N_DEV - 1)),
                        pltpu.SemaphoreType.DMA((self.n,))]

    def _places(self):
        x, y, c = lax.axis_index("x"), lax.axis_index("y"), lax.axis_index("c")
        return (x, y, c), (x, y, 1 - c), [(1 - x, y), (x, 1 - y), (1 - x, 1 - y)]

    def _copy(self, a, k, block, to, src_refs, out_refs, sems, from_src=False):
        slot = 4 * block[0] + 2 * block[1] + block[2]
        return pltpu.make_async_remote_copy(
            src_ref=src_refs[a] if from_src else out_refs[a].at[slot], dst_ref=out_refs[a].at[slot],
            send_sem=sems[0].at[a, k], recv_sem=sems[1].at[a, k], device_id=to, device_id_type=pl.DeviceIdType.MESH)

    def _local(self, a, src_refs, out_refs, sems):
        me, _, _ = self._places()
        return pltpu.make_async_copy(src_refs[a], out_refs[a].at[4 * me[0] + 2 * me[1] + me[2]], sems[2].at[a])

    def start(self, src_refs, out_refs, sems):
        me, sibling, chips = self._places()
        for a in range(self.n):
            self._local(a, src_refs, out_refs, sems).start()
            self._copy(a, 0, me, sibling, src_refs, out_refs, sems, from_src=True).start()
        for j, chip in enumerate(chips):
            for a in range(self.n):
                self._copy(a, 1 + j, me, (*chip, me[2]), src_refs, out_refs, sems, from_src=True).start()

    def forward(self, src_refs, out_refs, sems):
        me, sibling, chips = self._places()
        for j, chip in enumerate(chips):
            for a in range(self.n):
                self._copy(a, 1 + j, (*chip, me[2]), me, src_refs, out_refs, sems).wait_recv()
                self._copy(a, 4 + j, (*chip, me[2]), sibling, src_refs, out_refs, sems).start()

    def wait(self, src_refs, out_refs, sems):
        me, sibling, chips = self._places()
        for a in range(self.n):
            self._copy(a, 0, sibling, me, src_refs, out_refs, sems).wait_recv()
            for j, chip in enumerate(chips):
                self._copy(a, 4 + j, (*chip, sibling[2]), me, src_refs, out_refs, sems).wait_recv()
        for a in range(self.n):
            self._copy(a, 0, me, sibling, src_refs, out_refs, sems, from_src=True).wait_send()
            for j, chip in enumerate(chips):
                self._copy(a, 1 + j, me, (*chip, me[2]), src_refs, out_refs, sems, from_src=True).wait_send()
                self._copy(a, 4 + j, (*chip, me[2]), sibling, src_refs, out_refs, sems).wait_send()
            self._local(a, src_refs, out_refs, sems).wait()


def _exchange(name, srcs, gathers):
    ex = _Exchange(srcs, gathers)

    def body(*refs):
        src_refs, out_refs, sems = refs[:ex.n], refs[ex.n:2 * ex.n], refs[2 * ex.n:]
        ex.start(src_refs, out_refs, sems)
        ex.wait(src_refs, out_refs, sems)

    return pl.pallas_call(body, name=name, out_shape=ex.out_shape, in_specs=ex.specs, out_specs=ex.specs,
                          scratch_shapes=ex.scratch)(*srcs)


def _launch(body, name, grid, in_specs, out_specs, out_shape, scratch_shapes, args, ex=None):
    sem = ("arbitrary",) * len(grid)
    if ex is None:
        outs = pl.pallas_call(body, name=name, grid=grid, in_specs=in_specs, out_specs=out_specs, out_shape=out_shape,
                              scratch_shapes=scratch_shapes, compiler_params=_params(sem))(*args)
        return outs, []
    n_in, n_out, n_scr = len(in_specs), len(out_specs), len(scratch_shapes)

    def hosted(*refs):
        ins, ex_in = refs[:n_in], refs[n_in:n_in + ex.n]
        outs = refs[n_in + ex.n:n_in + ex.n + n_out]
        ex_out = refs[n_in + ex.n + n_out:n_in + 2 * ex.n + n_out]
        rest = refs[n_in + 2 * ex.n + n_out:]
        scratch, sems = rest[:n_scr], rest[n_scr:]
        ids = [pl.program_id(a) for a in range(len(grid))]
        first = functools.reduce(lambda p, q: p & q, [i == 0 for i in ids])
        last = functools.reduce(lambda p, q: p & q, [i == g - 1 for i, g in zip(ids, grid)])

        @pl.when(first)
        def _():
            ex.start(ex_in, ex_out, sems)

        if hasattr(ex, "forward"):
            @pl.when(functools.reduce(lambda p, q: p & q, [i == (3 * g // 4 if a == 0 else 0)
                                                           for a, (i, g) in enumerate(zip(ids, grid))]))
            def _():
                ex.forward(ex_in, ex_out, sems)

        body(*ins, *outs, *scratch)

        @pl.when(last)
        def _():
            ex.wait(ex_in, ex_out, sems)

    outs = pl.pallas_call(
        hosted, name=name, grid=grid, in_specs=list(in_specs) + ex.specs, out_specs=list(out_specs) + ex.specs,
        out_shape=list(out_shape) + ex.out_shape, scratch_shapes=list(scratch_shapes) + ex.scratch,
        compiler_params=_params(sem))(*args, *ex.srcs)
    return outs[:n_out], outs[n_out:]


def _ada_fwd(c_all, ada_w):
    layers, d, cols = ada_w.shape

    def body(c_ref, w_ref, o_ref):
        c = c_ref[...]
        ca = (c * _sigmoid(c)).astype(BF16)
        for l in range(layers):
            o_ref[l] = _dot(ca, w_ref[l].astype(BF16))

    return pl.pallas_call(
        body, name="ada_fwd", out_shape=jax.ShapeDtypeStruct((layers, N_DEV, cols), F32),
        compiler_params=pltpu.CompilerParams(vmem_limit_bytes=VMEM_LIMIT_BYTES),
    )(c_all, ada_w)


def _ada_bwd(c_all, dmod_cols):
    layers, _, cols = dmod_cols.shape
    d = c_all.shape[1]

    def body(c_ref, g_ref, o_ref):
        c = c_ref[...]
        ca = (c * _sigmoid(c)).astype(BF16)
        for l in range(layers):
            o_ref[l] = _dot_tn(ca, g_ref[l].astype(BF16))

    return pl.pallas_call(
        body, name="ada_bwd", out_shape=jax.ShapeDtypeStruct((layers, d, cols), F32),
        compiler_params=pltpu.CompilerParams(vmem_limit_bytes=VMEM_LIMIT_BYTES),
    )(c_all, dmod_cols)


def _pooled(hbuf, h, t0, g, tm):
    gd = h.shape[1] // len(POOL_WINDOWS)
    cols = slice(g * gd, (g + 1) * gd)
    w = POOL_WINDOWS[g]
    hg = h[:, cols]
    s = hg
    for k in range(1, w):
        s = s + hbuf[POOL_HALO - k:POOL_HALO - k + tm, cols]
    t = t0 + lax.broadcasted_iota(jnp.int32, (tm, 1), 0)
    cnt = jnp.minimum(t + 1, w).astype(F32)
    return s / cnt - hg, cnt


def _pool_fwd(x, pv, wp, scale, *, tm, ex=None):
    t_len, d = x.shape
    n_i = t_len // tm
    gd = d // len(POOL_WINDOWS)

    def body(x_ref, pv_ref, wp_ref, sc_ref, xo_ref, y_ref, hbuf):
        i = pl.program_id(0)

        @pl.when(i == 0)
        def _():
            hbuf[0:POOL_HALO, :] = jnp.zeros((POOL_HALO, d), F32)

        xv, pv_ = x_ref[...], pv_ref[...]
        h, _, _ = _prenorm(xv, pv_)
        hbuf[POOL_HALO:POOL_HALO + tm, :] = h
        for g in range(len(POOL_WINDOWS)):
            pooled, _ = _pooled(hbuf, h, i * tm, g, tm)
            y_ref[:, g * gd:(g + 1) * gd] = _dot(pooled.astype(BF16), wp_ref[g])
        xo_ref[...] = _post(xv, y_ref[...] * sc_ref[...], pv_)
        hbuf[0:POOL_HALO, :] = hbuf[tm:tm + POOL_HALO, :]

    row = pl.BlockSpec((tm, d), lambda i: (i, 0))
    return _launch(
        body, "pool_fwd", (n_i,),
        [row, pl.BlockSpec((SUBLANES, d), lambda i: (0, 0)), pl.BlockSpec(wp.shape, lambda i: (0, 0, 0)),
         pl.BlockSpec((1, d), lambda i: (0, 0))],
        [row, row],
        [jax.ShapeDtypeStruct((t_len, d), F32), jax.ShapeDtypeStruct((t_len, d), F32)],
        [pltpu.VMEM((tm + POOL_HALO, d), F32)],
        (x, pv, wp, scale), ex)


def _pool_bwd(dxo, x, ypre, pv, wp, scale, *, tm, ex=None):
    t_len, d = x.shape
    n_i = t_len // tm
    gd = d // len(POOL_WINDOWS)
    hb = tm // POOL_HALO

    def body(dxo_ref, x_ref, xh_ref, y_ref, pv_ref, wp_ref, sc_ref, dx_ref, pg_ref, dwp_ref, hbuf, qbuf):
        i = pl.program_id(0)
        ti = n_i - 1 - i

        @pl.when(i == 0)
        def _():
            pg_ref[...] = jnp.zeros_like(pg_ref)
            dwp_ref[...] = jnp.zeros_like(dwp_ref)
            qbuf[tm:tm + POOL_HALO, :] = jnp.zeros((POOL_HALO, d), F32)

        xv, pv_, dxo_v, yp, sc = x_ref[...], pv_ref[...], dxo_ref[...], y_ref[...], sc_ref[...]
        dy = _post_bwd(dxo_v, yp * sc, pv_, pg_ref)
        pg_ref[PG_EXTRA:PG_EXTRA + 1, :] += _colsum(dy * yp)
        dys = dy * sc
        h, _, _ = _prenorm(xv, pv_)
        hh, _, _ = _prenorm(xh_ref[...], pv_)
        hbuf[0:POOL_HALO, :] = jnp.where(ti > 0, hh, 0.0)
        hbuf[POOL_HALO:POOL_HALO + tm, :] = h
        for g in range(len(POOL_WINDOWS)):
            cols = slice(g * gd, (g + 1) * gd)
            pooled, cnt = _pooled(hbuf, h, ti * tm, g, tm)
            dyg = dys[:, cols].astype(BF16)
            dwp_ref[g] += _dot_tn(pooled.astype(BF16), dyg)
            dp = _dot_nt(dyg, wp_ref[g])
            qbuf[0:tm, cols] = dp / cnt
            dh = -dp
            for k in range(POOL_WINDOWS[g]):
                dh = dh + qbuf[k:k + tm, cols]
            hbuf[POOL_HALO:POOL_HALO + tm, cols] = dh
        dx_ref[...] = dxo_v + _pre_bwd(hbuf[POOL_HALO:POOL_HALO + tm, :], xv, pv_, pg_ref)
        qbuf[tm:tm + POOL_HALO, :] = qbuf[0:POOL_HALO, :]

    row = pl.BlockSpec((tm, d), lambda i: (n_i - 1 - i, 0))
    halo = pl.BlockSpec((POOL_HALO, d), lambda i: (jnp.maximum((n_i - 1 - i) * hb - 1, 0), 0))
    small = pl.BlockSpec((SUBLANES, d), lambda i: (0, 0))
    return _launch(
        body, "pool_bwd", (n_i,),
        [row, row, halo, row, small, pl.BlockSpec(wp.shape, lambda i: (0, 0, 0)), pl.BlockSpec((1, d), lambda i: (0, 0))],
        [row, small, pl.BlockSpec(wp.shape, lambda i: (0, 0, 0))],
        [jax.ShapeDtypeStruct((t_len, d), F32), jax.ShapeDtypeStruct((SUBLANES, d), F32),
         jax.ShapeDtypeStruct(wp.shape, F32)],
        [pltpu.VMEM((tm + POOL_HALO, d), F32), pltpu.VMEM((tm + POOL_HALO, d), F32)],
        (dxo, x, x, ypre, pv, wp, scale), ex)


def _ffn_conv(a, prev, w):
    return w[2:3] * a + w[1:2] * _shift_down(a, 1, prev) + w[0:1] * _shift_down(a, 2, prev)


def _ffn_fwd(x, pv, wup, wdw, wdn, *, tm, ex=None):
    t_len, d = x.shape
    _, n_j, fc, _ = wup.shape
    n_i = t_len // tm

    def body(x_ref, pv_ref, wup_ref, wdw_ref, wdn_ref, xo_ref, y_ref, h_ref, a_ref, c_ref, h_s, yacc, carry):
        i, j = pl.program_id(0), pl.program_id(1)

        @pl.when(j == 0)
        def _():
            h, _, _ = _prenorm(x_ref[...], pv_ref[...])
            hb = h.astype(BF16)
            h_s[...] = hb
            h_ref[...] = hb
            yacc[...] = jnp.zeros_like(yacc)

        @pl.when((i == 0) & (j == 0))
        def _():
            carry[...] = jnp.zeros_like(carry)

        hb = h_s[...]
        conv = []
        for s in range(2):
            a = _dot_nt(hb, wup_ref[s, 0])
            a_ref[s, 0] = a.astype(BF16)
            cv = _ffn_conv(a, carry[s, j], wdw_ref[s, 0])
            c_ref[s, 0] = cv.astype(BF16)
            conv.append(cv)
            carry[s, j] = a[tm - FFN_HALO:tm, :]
        g, v = conv
        u = g * _sigmoid(g) * v
        yacc[...] += _dot(u.astype(BF16), wdn_ref[...])

        @pl.when(j == n_j - 1)
        def _():
            y = yacc[...]
            y_ref[...] = y
            xo_ref[...] = _post(x_ref[...], y, pv_ref[...])

    row = pl.BlockSpec((tm, d), lambda i, j: (i, 0))
    return _launch(
        body, "ffn_fwd", (n_i, n_j),
        [row, pl.BlockSpec((SUBLANES, d), lambda i, j: (0, 0)),
         pl.BlockSpec((2, 1, fc, d), lambda i, j: (0, j, 0, 0)),
         pl.BlockSpec((2, 1, SUBLANES, fc), lambda i, j: (0, j, 0, 0)),
         pl.BlockSpec((fc, d), lambda i, j: (j, 0))],
        [row, row, row, pl.BlockSpec((2, 1, tm, fc), lambda i, j: (0, j, i, 0)),
         pl.BlockSpec((2, 1, tm, fc), lambda i, j: (0, j, i, 0))],
        [jax.ShapeDtypeStruct((t_len, d), F32), jax.ShapeDtypeStruct((t_len, d), F32),
         jax.ShapeDtypeStruct((t_len, d), BF16), jax.ShapeDtypeStruct((2, n_j, t_len, fc), BF16),
         jax.ShapeDtypeStruct((2, n_j, t_len, fc), BF16)],
        [pltpu.VMEM((tm, d), BF16), pltpu.VMEM((tm, d), F32), pltpu.VMEM((2, n_j, FFN_HALO, fc), F32)],
        (x, pv, wup, wdw, wdn), ex)


def _ffn_bwd(dy, h, a_sav, c_sav, wup, wdw, wdn, *, tm, ex=None):
    t_len, d = dy.shape
    _, n_j, fc, _ = wup.shape
    n_i = t_len // tm
    assert n_i % 2 == 0

    def body(dyp_ref, hp_ref, a_ref, c_ref, wup_ref, wdw_ref, wdn_ref,
             dhp_ref, gup_ref, gdn_ref, dwdw_ref, da_p, u_p, acc_up, acc_dn, carry):
        j, i = pl.program_id(0), pl.program_id(1)

        @pl.when((j == 0) & (i == 0))
        def _():
            dwdw_ref[...] = jnp.zeros_like(dwdw_ref)

        @pl.when(i == 0)
        def _():
            acc_up[...] = jnp.zeros_like(acc_up)
            acc_dn[...] = jnp.zeros_like(acc_dn)
            carry[...] = jnp.zeros_like(carry)

        def step(half):
            rows = slice(half * tm, (half + 1) * tm)
            du = _dot_nt(dyp_ref[rows, :], wdn_ref[...])
            g, v = c_ref[0, 0].astype(F32), c_ref[1, 0].astype(F32)
            sg = _sigmoid(g)
            sl = g * sg
            u_p[rows, :] = (sl * v).astype(BF16)
            d2 = (du * v * (sg * (1.0 + g * (1.0 - sg))), du * sl)
            dab = []
            for s in range(2):
                w = wdw_ref[s, 0]
                nxt = carry[s]
                p1, p2 = _shift_up(d2[s], 1, nxt), _shift_up(d2[s], 2, nxt)
                carry[s] = d2[s][0:FFN_HALO, :]
                dab.append((w[2:3] * d2[s] + w[1:2] * p1 + w[0:1] * p2).astype(BF16))
                da_p[s, rows, :] = dab[s]
                a_s = a_ref[s, 0].astype(F32)
                for k, sh in ((2, d2[s]), (1, p1), (0, p2)):
                    dwdw_ref[s, j, k:k + 1, :] += _colsum(a_s * sh)
            dhp_ref[0] = (_dot(dab[0], wup_ref[0, 0]) + _dot(dab[1], wup_ref[1, 0])).astype(BF16)
            if half == 0:
                hp = hp_ref[...]
                for s in range(2):
                    acc_up[s] += _dot_tn(da_p[s], hp)
                acc_dn[...] += _dot_tn(u_p[...], dyp_ref[...])

        @pl.when(i % 2 == 0)
        def _():
            step(1)

        @pl.when(i % 2 == 1)
        def _():
            step(0)

        @pl.when(i == n_i - 1)
        def _():
            gup_ref[:, 0] = acc_up[...].astype(BF16)
            gdn_ref[...] = acc_dn[...].astype(BF16)

    chunk = lambda shape: pl.BlockSpec((2, 1) + shape, lambda j, i: (0, j, 0, 0))
    tile = pl.BlockSpec((2, 1, tm, fc), lambda j, i: (0, j, n_i - 1 - i, 0))
    pair = pl.BlockSpec((2 * tm, d), lambda j, i: ((n_i - 1 - i) // 2, 0))
    return _launch(
        body, "ffn_bwd", (n_j, n_i),
        [pair, pair, tile, tile, chunk((fc, d)), chunk((SUBLANES, fc)), pl.BlockSpec((fc, d), lambda j, i: (j, 0))],
        [pl.BlockSpec((1, tm, d), lambda j, i: (j, n_i - 1 - i, 0)), chunk((fc, d)),
         pl.BlockSpec((fc, d), lambda j, i: (j, 0)), pl.BlockSpec((2, n_j, SUBLANES, fc), lambda j, i: (0, 0, 0, 0))],
        [jax.ShapeDtypeStruct((n_j, t_len, d), BF16), jax.ShapeDtypeStruct((2, n_j, fc, d), BF16),
         jax.ShapeDtypeStruct((n_j * fc, d), BF16), jax.ShapeDtypeStruct((2, n_j, SUBLANES, fc), F32)],
        [pltpu.VMEM((2, 2 * tm, fc), BF16), pltpu.VMEM((2 * tm, fc), BF16), pltpu.VMEM((2, fc, d), F32),
         pltpu.VMEM((fc, d), F32), pltpu.VMEM((2, FFN_HALO, fc), F32)],
        (dy, h, a_sav, c_sav, wup, wdw, wdn), ex)


def _sub_post_bwd(dxo, ypre, pv, *, tm, ex=None):
    t_len, d = dxo.shape
    n_i = t_len // tm

    def body(dxo_ref, y_ref, pv_ref, dy_ref, pg_ref):
        @pl.when(pl.program_id(0) == 0)
        def _():
            pg_ref[...] = jnp.zeros_like(pg_ref)

        dy_ref[...] = _post_bwd(dxo_ref[...], y_ref[...], pv_ref[...], pg_ref).astype(BF16)

    row = pl.BlockSpec((tm, d), lambda i: (i, 0))
    small = pl.BlockSpec((SUBLANES, d), lambda i: (0, 0))
    return _launch(body, "sub_post_bwd", (n_i,), [row, row, small], [row, small],
                   [jax.ShapeDtypeStruct((t_len, d), BF16), jax.ShapeDtypeStruct((SUBLANES, d), F32)], [],
                   (dxo, ypre, pv), ex)


def _sub_pre_bwd(dhp, x, dxo, pv, *, tm, ex=None):
    n_p, t_len, d = dhp.shape
    n_i = t_len // tm

    def body(dhp_ref, x_ref, dxo_ref, pv_ref, dx_ref, pg_ref):
        @pl.when(pl.program_id(0) == 0)
        def _():
            pg_ref[...] = jnp.zeros_like(pg_ref)

        dh = dhp_ref[0].astype(F32)
        for p in range(1, n_p):
            dh = dh + dhp_ref[p].astype(F32)
        dx_ref[...] = dxo_ref[...] + _pre_bwd(dh, x_ref[...], pv_ref[...], pg_ref)

    row = pl.BlockSpec((tm, d), lambda i: (i, 0))
    small = pl.BlockSpec((SUBLANES, d), lambda i: (0, 0))
    return _launch(body, "sub_pre_bwd", (n_i,), [pl.BlockSpec((n_p, tm, d), lambda i: (0, i, 0)), row, row, small],
                   [row, small], [jax.ShapeDtypeStruct((t_len, d), F32), jax.ShapeDtypeStruct((SUBLANES, d), F32)], [],
                   (dhp, x, dxo, pv), ex)


CV_B1, CV_BDW, CV_LNG, CV_LNB, CV_B2 = 0, 1, 2, 3, 4


def _taps_by_residue(offs):
    groups = {}
    for k, off in enumerate(offs):
        groups.setdefault(off % SUBLANES, []).append((k, off // SUBLANES))
    return sorted(groups.items())


def _depthwise(buf, w_ref, out_ref, offs, tm, d):
    taps_of = _taps_by_residue(offs)

    def chunk(r, carry):
        r0 = pl.multiple_of(r * CONV_ROWS, CONV_ROWS)
        for cb in range(d // LANES):
            cols = slice(cb * LANES, (cb + 1) * LANES)
            win = buf[pl.ds(r0, CONV_ROWS + CONV_HALO), cols]
            acc = jnp.zeros((CONV_ROWS, LANES), F32)
            for b, taps in taps_of:
                wb = win if b == 0 else pltpu.roll(win, CONV_ROWS + CONV_HALO - b, 0)
                for k, a in taps:
                    acc = acc + wb[SUBLANES * a:SUBLANES * a + CONV_ROWS, :] * w_ref[k:k + 1, cols]
            out_ref[pl.ds(r0, CONV_ROWS), cols] = acc
        return carry

    lax.fori_loop(0, tm // CONV_ROWS, chunk, 0)


def _depthwise_wgrad(dbuf, ubuf, dw_ref, tm, d):
    taps_of = _taps_by_residue(tuple(2 + k for k in range(CONV_TAPS)))
    for cb in range(d // LANES):
        cols = slice(cb * LANES, (cb + 1) * LANES)

        def chunk(r, acc):
            r0 = pl.multiple_of(r * CONV_ROWS, CONV_ROWS)
            dv = dbuf[pl.ds(r0, CONV_ROWS), cols]
            win = ubuf[pl.ds(r0, CONV_ROWS + CONV_HALO), cols]
            new = list(acc)
            for b, taps in taps_of:
                wb = win if b == 0 else pltpu.roll(win, CONV_ROWS + CONV_HALO - b, 0)
                for k, a in taps:
                    for q in range(CONV_ROWS // SUBLANES):
                        new[k] = new[k] + dv[SUBLANES * q:SUBLANES * (q + 1), :] * wb[SUBLANES * (a + q):SUBLANES * (a + q + 1), :]
            return tuple(new)

        acc = lax.fori_loop(0, tm // CONV_ROWS, chunk, tuple(jnp.zeros((SUBLANES, LANES), F32) for _ in range(CONV_TAPS)))
        for k in range(CONV_TAPS):
            dw_ref[k:k + 1, cols] += _colsum(acc[k])


def _layer_norm_parts(c1):
    mu = jnp.mean(c1, axis=-1, keepdims=True)
    cen = c1 - mu
    rstd = lax.rsqrt(jnp.mean(cen * cen, axis=-1, keepdims=True) + NORM_EPS)
    return cen * rstd, rstd


def _conv_fwd(x, pv, w1, w2, wdw, vec, *, tm):
    t_len, d = x.shape
    n_i = t_len // tm
    n_q = w1.shape[0] // 2
    qc = w1.shape[2]

    def body(x_ref, pv_ref, w1_ref, w2_ref, wdw_ref, vec_ref, xo_ref, y_ref, a_ref, c1_ref, ubuf):
        i = pl.program_id(0)

        @pl.when(i == 0)
        def _():
            ubuf[0:CONV_HALO, :] = jnp.zeros((CONV_HALO, d), F32)

        xv, pv_ = x_ref[...], pv_ref[...]
        h, _, _ = _prenorm(xv, pv_)
        hb = h.astype(BF16)
        for q in range(n_q):
            cols = slice(q * qc, (q + 1) * qc)
            gcols = slice(d + q * qc, d + (q + 1) * qc)
            val = (_dot(hb, w1_ref[q]) + vec_ref[CV_B1:CV_B1 + 1, cols]).astype(BF16)
            gate = (_dot(hb, w1_ref[n_q + q]) + vec_ref[CV_B1:CV_B1 + 1, gcols]).astype(BF16)
            a_ref[:, cols] = val
            a_ref[:, gcols] = gate
            ubuf[CONV_HALO:CONV_HALO + tm, cols] = val.astype(F32) * _sigmoid(gate.astype(F32))
        _depthwise(ubuf, wdw_ref, c1_ref, tuple(2 + k for k in range(CONV_TAPS)), tm, d)
        c1 = c1_ref[...] + vec_ref[CV_BDW:CV_BDW + 1, 0:d]
        c1_ref[...] = c1
        xhat, _ = _layer_norm_parts(c1)
        ln = xhat * vec_ref[CV_LNG:CV_LNG + 1, 0:d] + vec_ref[CV_LNB:CV_LNB + 1, 0:d]
        s = ln * _sigmoid(ln)
        y = _dot(s.astype(BF16), w2_ref[...]) + vec_ref[CV_B2:CV_B2 + 1, 0:d]
        y_ref[...] = y
        xo_ref[...] = _post(xv, y, pv_)
        ubuf[0:CONV_HALO, :] = ubuf[tm:tm + CONV_HALO, :]

    row = pl.BlockSpec((tm, d), lambda i: (i, 0))
    return pl.pallas_call(
        body, name="conv_fwd", grid=(n_i,),
        in_specs=[row, pl.BlockSpec((SUBLANES, d), lambda i: (0, 0)), pl.BlockSpec(w1.shape, lambda i: (0, 0, 0)),
                  pl.BlockSpec(w2.shape, lambda i: (0, 0)), pl.BlockSpec(wdw.shape, lambda i: (0, 0)),
                  pl.BlockSpec(vec.shape, lambda i: (0, 0))],
        out_specs=[row, row, pl.BlockSpec((tm, 2 * d), lambda i: (i, 0)), row],
        out_shape=[jax.ShapeDtypeStruct((t_len, d), F32), jax.ShapeDtypeStruct((t_len, d), F32),
                   jax.ShapeDtypeStruct((t_len, 2 * d), BF16), jax.ShapeDtypeStruct((t_len, d), F32)],
        scratch_shapes=[pltpu.VMEM((tm + CONV_HALO, d), F32)],
        compiler_params=_params(("arbitrary",)),
    )(x, pv, w1, w2, wdw, vec)


def _conv_bwd(dxo, x, ypre, a_sav, c1_sav, pv, w1, w2, wdw, vec, *, tm, ex=None):
    t_len, d = x.shape
    n_i = t_len // tm
    n_q = w1.shape[0] // 2
    qc = w1.shape[2]
    hb_ = tm // CONV_HALO

    def body(dxo_ref, x_ref, y_ref, a_ref, ah_ref, c1_ref, pv_ref, w1_ref, w2_ref, wdw_ref, vec_ref,
             dx_ref, pg_ref, gw1_ref, gw2_ref, gvec_ref, gwdw_ref, ubuf, dcbuf, dubuf, acc1, acc2):
        i = pl.program_id(0)
        ti = n_i - 1 - i

        @pl.when(i == 0)
        def _():
            pg_ref[...] = jnp.zeros_like(pg_ref)
            gvec_ref[...] = jnp.zeros_like(gvec_ref)
            gwdw_ref[...] = jnp.zeros_like(gwdw_ref)
            acc1[...] = jnp.zeros_like(acc1)
            acc2[...] = jnp.zeros_like(acc2)
            dcbuf[tm:tm + CONV_HALO, :] = jnp.zeros((CONV_HALO, d), F32)

        xv, pv_, dxo_v = x_ref[...], pv_ref[...], dxo_ref[...]
        dy = _post_bwd(dxo_v, y_ref[...], pv_, pg_ref)
        gvec_ref[CV_B2:CV_B2 + 1, 0:d] += _colsum(dy)
        dyb = dy.astype(BF16)
        xhat, rstd = _layer_norm_parts(c1_ref[...])
        lng = vec_ref[CV_LNG:CV_LNG + 1, 0:d]
        ln = xhat * lng + vec_ref[CV_LNB:CV_LNB + 1, 0:d]
        sg = _sigmoid(ln)
        acc2[...] += _dot_tn((ln * sg).astype(BF16), dyb)
        dln = _dot_nt(dyb, w2_ref[...]) * (sg * (1.0 + ln * (1.0 - sg)))
        gvec_ref[CV_LNG:CV_LNG + 1, 0:d] += _colsum(dln * xhat)
        gvec_ref[CV_LNB:CV_LNB + 1, 0:d] += _colsum(dln)
        dxh = dln * lng
        dc1 = rstd * (dxh - jnp.mean(dxh, axis=-1, keepdims=True)
                      - xhat * jnp.mean(dxh * xhat, axis=-1, keepdims=True))
        gvec_ref[CV_BDW:CV_BDW + 1, 0:d] += _colsum(dc1)
        dcbuf[0:tm, :] = dc1
        for q in range(n_q):
            cols = slice(q * qc, (q + 1) * qc)
            gcols = slice(d + q * qc, d + (q + 1) * qc)
            ubuf[CONV_HALO:CONV_HALO + tm, cols] = a_ref[:, cols].astype(F32) * _sigmoid(a_ref[:, gcols].astype(F32))
            uh = ah_ref[:, cols].astype(F32) * _sigmoid(ah_ref[:, gcols].astype(F32))
            ubuf[0:CONV_HALO, cols] = jnp.where(ti > 0, uh, 0.0)
        _depthwise_wgrad(dcbuf, ubuf, gwdw_ref, tm, d)
        _depthwise(dcbuf, wdw_ref, dubuf, tuple(CONV_TAPS - 1 - k for k in range(CONV_TAPS)), tm, d)
        dcbuf[tm:tm + CONV_HALO, :] = dcbuf[0:CONV_HALO, :]
        h, _, _ = _prenorm(xv, pv_)
        hb = h.astype(BF16)
        dh = jnp.zeros((tm, d), F32)
        for q in range(n_q):
            cols = slice(q * qc, (q + 1) * qc)
            gcols = slice(d + q * qc, d + (q + 1) * qc)
            du = dubuf[:, cols]
            val, gate = a_ref[:, cols].astype(F32), a_ref[:, gcols].astype(F32)
            sgg = _sigmoid(gate)
            dval = du * sgg
            dgate = du * val * (sgg * (1.0 - sgg))
            gvec_ref[CV_B1:CV_B1 + 1, cols] += _colsum(dval)
            gvec_ref[CV_B1:CV_B1 + 1, gcols] += _colsum(dgate)
            dvb, dgb = dval.astype(BF16), dgate.astype(BF16)
            acc1[q] += _dot_tn(hb, dvb)
            acc1[n_q + q] += _dot_tn(hb, dgb)
            dh = dh + _dot_nt(dvb, w1_ref[q]) + _dot_nt(dgb, w1_ref[n_q + q])
        dx_ref[...] = dxo_v + _pre_bwd(dh, xv, pv_, pg_ref)

        @pl.when(i == n_i - 1)
        def _():
            gw1_ref[...] = acc1[...].astype(BF16)
            gw2_ref[...] = acc2[...].astype(BF16)

    row = pl.BlockSpec((tm, d), lambda i: (n_i - 1 - i, 0))
    small = pl.BlockSpec((SUBLANES, d), lambda i: (0, 0))
    whole2 = lambda shape: pl.BlockSpec(shape, lambda i: (0, 0))
    return _launch(
        body, "conv_bwd", (n_i,),
        [row, row, row,
         pl.BlockSpec((tm, 2 * d), lambda i: (n_i - 1 - i, 0)),
         pl.BlockSpec((CONV_HALO, 2 * d), lambda i: (jnp.maximum((n_i - 1 - i) * hb_ - 1, 0), 0)),
         row, small, pl.BlockSpec(w1.shape, lambda i: (0, 0, 0)), whole2(w2.shape), whole2(wdw.shape),
         whole2(vec.shape)],
        [row, small, pl.BlockSpec(w1.shape, lambda i: (0, 0, 0)), whole2(w2.shape), whole2(vec.shape),
         whole2(wdw.shape)],
        [jax.ShapeDtypeStruct((t_len, d), F32), jax.ShapeDtypeStruct((SUBLANES, d), F32),
         jax.ShapeDtypeStruct(w1.shape, BF16), jax.ShapeDtypeStruct(w2.shape, BF16),
         jax.ShapeDtypeStruct(vec.shape, F32), jax.ShapeDtypeStruct(wdw.shape, F32)],
        [pltpu.VMEM((tm + CONV_HALO, d), F32), pltpu.VMEM((tm + CONV_HALO, d), F32),
         pltpu.VMEM((tm, d), F32), pltpu.VMEM(w1.shape, F32), pltpu.VMEM(w2.shape, F32)],
        (dxo, x, ypre, a_sav, a_sav, c1_sav, pv, w1, w2, wdw, vec), ex)


def _loss_head(xo, target, ypre, pv, *, tm):
    t_len, d = xo.shape
    n_i = t_len // tm

    def body(xo_ref, t_ref, y_ref, pv_ref, dxo_ref, sq_ref, dy_ref, pg_ref):
        @pl.when(pl.program_id(0) == 0)
        def _():
            sq_ref[...] = jnp.zeros_like(sq_ref)
            pg_ref[...] = jnp.zeros_like(pg_ref)

        err = xo_ref[...] - t_ref[...]
        dxo = err * (1.0 / d)
        dxo_ref[...] = dxo
        sq_ref[...] += jnp.sum((err * err).reshape(tm // SUBLANES, SUBLANES, d), axis=0)
        dy_ref[...] = _post_bwd(dxo, y_ref[...], pv_ref[...], pg_ref).astype(BF16)

    row = pl.BlockSpec((tm, d), lambda i: (i, 0))
    small = pl.BlockSpec((SUBLANES, d), lambda i: (0, 0))
    return pl.pallas_call(
        body, name="loss_head", grid=(n_i,), in_specs=[row, row, row, small],
        out_specs=[row, small, row, small],
        out_shape=[jax.ShapeDtypeStruct((t_len, d), F32), jax.ShapeDtypeStruct((SUBLANES, d), F32),
                   jax.ShapeDtypeStruct((t_len, d), BF16), jax.ShapeDtypeStruct((SUBLANES, d), F32)],
        compiler_params=_params(("arbitrary",)),
    )(xo, target, ypre, pv)


def _adam(name, parts, w, m, v):
    layers = len(parts)
    n, rows, cols = parts[0].shape
    tr = rows
    if rows % SUBLANES == 0:
        cap = max(SUBLANES, ADAM_BLOCK_BYTES // (4 * cols))
        tr = max(t for t in range(SUBLANES, rows + 1, SUBLANES) if rows % t == 0 and (t <= cap or t == SUBLANES))
    c1 = 1.0 / (1.0 - ADAM_B1 ** ADAM_STEP)
    c2 = 1.0 / (1.0 - ADAM_B2 ** ADAM_STEP)

    def body(*refs):
        p_refs = refs[:layers]
        w_ref, m_ref, v_ref, g_ref, d_ref, mo_ref, vo_ref, g_s = refs[layers:]
        for l in range(layers):
            @pl.when(pl.program_id(0) == l)
            def _():
                g = p_refs[l][0].astype(F32)
                for k in range(1, n):
                    g = g + p_refs[l][k].astype(F32)
                g_s[...] = g

        g = g_s[...]
        m2 = ADAM_B1 * m_ref[0] + (1.0 - ADAM_B1) * g
        v2 = ADAM_B2 * v_ref[0] + (1.0 - ADAM_B2) * (g * g)
        g_ref[0] = g
        mo_ref[0] = m2
        vo_ref[0] = v2
        d_ref[0] = -ADAM_LR * ((m2 * c1) / (jnp.sqrt(v2 * c2) + ADAM_EPS) + ADAM_WD * w_ref[0])

    def part_spec(l):
        return pl.BlockSpec((n, tr, cols), lambda ll, i: (0, jnp.where(ll == l, i, 0), 0))

    blk = pl.BlockSpec((1, tr, cols), lambda ll, i: (ll, i, 0))
    out = jax.ShapeDtypeStruct((layers, rows, cols), F32)
    return pl.pallas_call(
        body, name=name, grid=(layers, rows // tr),
        in_specs=[part_spec(l) for l in range(layers)] + [blk, blk, blk],
        out_specs=[blk, blk, blk, blk], out_shape=[out, out, out, out],
        scratch_shapes=[pltpu.VMEM((tr, cols), F32)],
        compiler_params=_params(("arbitrary", "arbitrary")),
    )(*parts, w, m, v)


def _adam_nd(name, parts, w, m, v):
    shape = w.shape
    cols = shape[-1]
    if isinstance(parts, (list, tuple)):
        layers = len(parts)
    else:
        layers, parts = 1, [parts]
    rows = w.size // (cols * layers)
    flat = lambda t: t.reshape(layers, rows, cols)
    outs = _adam(name, [p.reshape(p.shape[0], rows, cols) for p in parts], flat(w), flat(m), flat(v))
    return [o.reshape(shape) for o in outs]


def _small_pack(parts, size):
    flat = jnp.concatenate([p.reshape(-1) for p in parts])
    return jnp.pad(flat, (0, size - flat.shape[0]))


def _to_shards(full, axis):
    shp = full.shape
    split = full.reshape(shp[:axis] + (N_DEV, shp[axis] // N_DEV) + shp[axis + 1:])
    return jnp.moveaxis(split, axis, 0)


def _from_shards(sh, axis):
    moved = jnp.moveaxis(sh, 0, axis)
    shp = moved.shape
    return moved.reshape(shp[:axis] + (shp[axis] * shp[axis + 1],) + shp[axis + 2:])


def kernel(x, c, ada_w, ada_b, pre_g, post_g, pool_w, pool_scale, cv_w_pw1, cv_b_pw1, cv_w_dw, cv_b_dw, cv_ln_g, cv_ln_b, cv_w_pw2, cv_b_pw2, ffn_w_up, ffn_w_dw, ffn_w_down, loss_target, m_ada_w, m_ada_b, m_pre_g, m_post_g, m_pool_w, m_pool_scale, m_cv_w_pw1, m_cv_b_pw1, m_cv_w_dw, m_cv_b_dw, m_cv_ln_g, m_cv_ln_b, m_cv_w_pw2, m_cv_b_pw2, m_ffn_w_up, m_ffn_w_dw, m_ffn_w_down, v_ada_w, v_ada_b, v_pre_g, v_post_g, v_pool_w, v_pool_scale, v_cv_w_pw1, v_cv_b_pw1, v_cv_w_dw, v_cv_b_dw, v_cv_ln_g, v_cv_ln_b, v_cv_w_pw2, v_cv_b_pw2, v_ffn_w_up, v_ffn_w_dw, v_ffn_w_down):
    t_len, d = x.shape[1], x.shape[2]
    depth = ada_w.shape[0]
    fc = ffn_w_up.shape[2]
    n_j = N_DEV // 2
    me = 4 * lax.axis_index("x") + 2 * lax.axis_index("y") + lax.axis_index("c")

    small_w = [pre_g, post_g, cv_b_pw1, cv_w_dw, cv_b_dw, cv_ln_g, cv_ln_b, cv_b_pw2, ffn_w_dw]
    small_m = [m_pre_g, m_post_g, m_cv_b_pw1, m_cv_w_dw, m_cv_b_dw, m_cv_ln_g, m_cv_ln_b, m_cv_b_pw2, m_ffn_w_dw]
    small_v = [v_pre_g, v_post_g, v_cv_b_pw1, v_cv_w_dw, v_cv_b_dw, v_cv_ln_g, v_cv_ln_b, v_cv_b_pw2, v_ffn_w_dw]
    sizes = [p.size for p in small_w]
    offs = [sum(sizes[:k]) for k in range(len(sizes) + 1)]
    pack = -(-offs[-1] // (SUBLANES * LANES)) * SUBLANES * LANES

    got = _exchange("gather_small", [c, _small_pack(small_w, pack), pool_w[0].astype(BF16)], [True] * 3)
    c_all = got[0].reshape(N_DEV, d)
    smalls = [got[1][:, offs[k]:offs[k + 1]].reshape((N_DEV,) + small_w[k].shape) for k in range(len(small_w))]
    pre_g_f, post_g_f = _from_shards(smalls[0], 2), _from_shards(smalls[1], 2)
    b1_f = _from_shards(smalls[2], 1)[0]
    cvw_f = jnp.pad(_from_shards(smalls[3], 2)[0], ((0, CONV_HALO - CONV_TAPS), (0, 0)))
    bdw_f, lng_f, lnb_f, b2_f = [_from_shards(smalls[k], 1)[0] for k in (4, 5, 6, 7)]
    fdw = jnp.pad(smalls[8], ((0, 0), (0, 0), (0, SUBLANES - FFN_TAPS), (0, 0)))
    wp = jnp.swapaxes(got[2], 0, 1).reshape(pool_w.shape[1], -1, pool_w.shape[3])
    wdw = [fdw[:, l].reshape(2, n_j, SUBLANES, fc) for l in range(depth)]
    cvec = jnp.zeros((SUBLANES, 2 * d), F32)
    cvec = cvec.at[CV_B1].set(b1_f)
    for r, vrow in ((CV_BDW, bdw_f), (CV_LNG, lng_f), (CV_LNB, lnb_f), (CV_B2, b2_f)):
        cvec = cvec.at[r, :d].set(vrow)

    mod_cols = _ada_fwd(c_all, ada_w)
    (mod_all,) = _exchange("gather_mod", [mod_cols], [True])
    mod = lax.dynamic_index_in_dim(mod_all, me, axis=2, keepdims=False)
    mod = jnp.swapaxes(mod, 0, 1).reshape(depth, N_MOD, d) + ada_b.reshape(depth, N_MOD, d)

    def pv_of(l, s):
        rows = [pre_g_f[l, s], 1.0 + mod[l, 3 * s + 1], mod[l, 3 * s], post_g_f[l, s], mod[l, 3 * s + 2]]
        return jnp.concatenate([jnp.stack(rows), jnp.zeros((SUBLANES - len(rows), d), F32)])

    x0 = x[0]
    pv00, pv01, pv10, pv11 = pv_of(0, 0), pv_of(0, 1), pv_of(1, 0), pv_of(1, 1)
    tm_pool, tm_ffn, tm_bwd, tm_conv = min(TM_POOL, t_len), min(TM_FFN, t_len), min(TM_FFN_BWD, t_len), min(TM_CONV, t_len)
    tm_w = min(TM_FFN_W, t_len)
    wup_t = jnp.swapaxes(ffn_w_up, 1, 2)
    ex = _TwoLevelGather([wup_t[0].astype(BF16), ffn_w_down[0].astype(BF16)])
    (x1, y0), (wup0, wdn0) = _pool_fwd(x0, pv00, wp, pool_scale, tm=tm_pool, ex=ex)
    ex = _TwoLevelGather([cv_w_pw1[0].astype(BF16), cv_w_pw2[0].astype(BF16), wup_t[1].astype(BF16),
                          ffn_w_down[1].astype(BF16)])
    (x2, y1, h1, a1, c1), (w1, w2, wup1, wdn1) = _ffn_fwd(
        x1, pv01, wup0.reshape(2, n_j, fc, d), wdw[0], wdn0.reshape(n_j * fc, d), tm=tm_ffn, ex=ex)
    w2 = w2.reshape(d, d)
    wup = [wup0.reshape(2, n_j, fc, d), wup1.reshape(2, n_j, fc, d)]
    wdn = [wdn0.reshape(n_j * fc, d), wdn1.reshape(n_j * fc, d)]
    x3, y2, a2, c2 = _conv_fwd(x2, pv10, w1, w2, cvw_f, cvec, tm=tm_conv)
    (x4, y3, h3, a3, c3), _ = _ffn_fwd(x3, pv11, wup[1], wdw[1], wdn[1], tm=tm_ffn)
    dx4, sq, dy3, pgq11 = _loss_head(x4, loss_target[0], y3, pv11, tm=tm_ffn)
    loss = lax.psum(jnp.sum(sq) * (0.5 / d), MESH_AXES)

    (dhp3, gup1, gdn1, gfdw1), _ = _ffn_bwd(dy3, h3, a3, c3, wup[1], wdw[1], wdn[1], tm=tm_bwd)
    (dx3, pgp11), _ = _sub_pre_bwd(dhp3, x3, dx4, pv11, tm=tm_ffn)
    ex = _Exchange([gup1.reshape(N_DEV, fc, d), gdn1.reshape(N_DEV, -1, d)], [False, False])
    (dx2, pg10, gw1, gw2, gcvec, gcvw), (rup1, rdn1) = _conv_bwd(
        dx3, x2, y2, a2, c2, pv10, w1, w2, cvw_f, cvec, tm=tm_conv, ex=ex)
    (dy1, pgq01), _ = _sub_post_bwd(dx2, y1, pv01, tm=tm_w)
    ex = _Exchange([gw1, gw2.reshape(N_DEV, -1, d)], [False, False])
    (dhp1, gup0, gdn0, gfdw0), (rw1, rw2) = _ffn_bwd(dy1, h1, a1, c1, wup[0], wdw[0], wdn[0], tm=tm_bwd, ex=ex)
    ex = _Exchange([gdn0.reshape(N_DEV, -1, d)], [False])
    (dx1, pgp01), (rdn0,) = _sub_pre_bwd(dhp1, x1, dx2, pv01, tm=tm_ffn, ex=ex)
    ex = _Exchange([gup0.reshape(N_DEV, fc, d)], [False])
    (dx0, pg00, gwp), (rup0,) = _pool_bwd(dx1, x0, y0, pv00, wp, pool_scale, tm=tm_pool, ex=ex)
    pg01, pg11 = pgq01 + pgp01, pgq11 + pgp11

    pgs = [[pg00, pg01], [pg10, pg11]]
    g_pre = jnp.stack([jnp.stack([pgs[l][s][PG_GPRE] for s in range(2)]) for l in range(depth)])
    g_post = jnp.stack([jnp.stack([pgs[l][s][PG_GPOST] for s in range(2)]) for l in range(depth)])
    dmod = jnp.stack([jnp.concatenate([pgs[l][s][r] for s in range(2) for r in (PG_SH, PG_SC, PG_GT)])
                      for l in range(depth)])
    gfdw = jnp.stack([g.reshape(N_DEV, SUBLANES, fc)[:, :FFN_TAPS] for g in (gfdw0, gfdw1)], axis=1)
    small_g = [_to_shards(g_pre, 2), _to_shards(g_post, 2), _to_shards(gcvec[CV_B1][None], 1),
               _to_shards(gcvw[None, :CONV_TAPS], 2), _to_shards(gcvec[CV_BDW, :d][None], 1),
               _to_shards(gcvec[CV_LNG, :d][None], 1), _to_shards(gcvec[CV_LNB, :d][None], 1),
               _to_shards(gcvec[CV_B2, :d][None], 1), gfdw]
    small_send = jnp.concatenate([g.reshape(N_DEV, -1) for g in small_g], axis=1)
    small_send = jnp.pad(small_send, ((0, 0), (0, pack - small_send.shape[1])))
    gwp_send = jnp.swapaxes(gwp.reshape(gwp.shape[0], N_DEV, -1, gwp.shape[2]), 0, 1)
    rsmall, rwp, rmod, rscale = _exchange("scatter_small", [small_send, gwp_send, dmod, pg00[PG_EXTRA][None]],
                                          [False, False, True, True])

    outs = {}

    def put(name, res, shape=None):
        outs[name] = [r if shape is None else r.reshape(shape) for r in res]

    small_res = _adam_nd("adam_small", rsmall.reshape(N_DEV, -1, SUBLANES * LANES),
                         _small_pack(small_w, pack).reshape(-1, SUBLANES * LANES),
                         _small_pack(small_m, pack).reshape(-1, SUBLANES * LANES),
                         _small_pack(small_v, pack).reshape(-1, SUBLANES * LANES))
    small_names = ["pre_g", "post_g", "cv_b_pw1", "cv_w_dw", "cv_b_dw", "cv_ln_g", "cv_ln_b", "cv_b_pw2", "ffn_w_dw"]
    for k, nm in enumerate(small_names):
        outs[nm] = [r.reshape(-1)[offs[k]:offs[k + 1]].reshape(small_w[k].shape) for r in small_res]
    put("pool_w", _adam_nd("adam_pool_w", rwp[:, None], pool_w, m_pool_w, v_pool_w))
    put("cv_w_pw1", _adam_nd("adam_cv_w_pw1", rw1[:, None], cv_w_pw1, m_cv_w_pw1, v_cv_w_pw1))
    put("cv_w_pw2", _adam_nd("adam_cv_w_pw2", rw2[:, None], cv_w_pw2, m_cv_w_pw2, v_cv_w_pw2))
    outs["ffn_w_up"] = [jnp.swapaxes(r, 1, 2) for r in _adam_nd(
        "adam_ffn_w_up", [rup0, rup1], wup_t, jnp.swapaxes(m_ffn_w_up, 1, 2), jnp.swapaxes(v_ffn_w_up, 1, 2))]
    put("ffn_w_down", _adam_nd("adam_ffn_w_down", [rdn0, rdn1], ffn_w_down, m_ffn_w_down, v_ffn_w_down))
    put("ada_b", _adam_nd("adam_ada_b", rmod, ada_b, m_ada_b, v_ada_b))
    put("pool_scale", _adam_nd("adam_pool_scale", rscale, pool_scale, m_pool_scale, v_pool_scale))
    cols = ada_w.shape[2]
    dmod_cols = jnp.swapaxes(lax.dynamic_slice_in_dim(rmod, me * cols, cols, axis=2), 0, 1)
    put("ada_w", _adam_nd("adam_ada_w", _ada_bwd(c_all, dmod_cols)[None], ada_w, m_ada_w, v_ada_w))

    order = ["ada_w", "ada_b", "pre_g", "post_g", "pool_w", "pool_scale", "cv_w_pw1", "cv_b_pw1", "cv_w_dw", "cv_b_dw",
             "cv_ln_g", "cv_ln_b", "cv_w_pw2", "cv_b_pw2", "ffn_w_up", "ffn_w_dw", "ffn_w_down"]
    return (loss, dx0[None], *[outs[nm][0] for nm in order], *[outs[nm][1] for nm in order],
            *[outs[nm][2] for nm in order], *[outs[nm][3] for nm in order])
```

```python
import functools

import jax
import jax.numpy as jnp
from jax import lax
from jax.experimental import pallas as pl
from jax.experimental.pallas import tpu as pltpu

F32, BF16 = jnp.float32, jnp.bfloat16
MESH_AXES = ("x", "y", "c")
N_DEV = 8
NORM_EPS = 1e-6
ADAM_LR, ADAM_B1, ADAM_B2, ADAM_EPS, ADAM_WD, ADAM_STEP = 0.001, 0.9, 0.999, 1e-08, 0.01, 10
POOL_WINDOWS = (2, 4, 8, 16)
CONV_TAPS = 31
FFN_TAPS = 3
N_MOD = 6

SUBLANES = 8
LANES = 128
VMEM_LIMIT_BYTES = 56 * 1024 * 1024
POOL_HALO = 16
CONV_HALO = 32
FFN_HALO = 8
TM_POOL, TM_FFN, TM_FFN_BWD, TM_FFN_W, TM_CONV = 512, 512, 512, 1024, 256
CONV_ROWS = 32
DW_ROWS = 64
ADAM_BLOCK_BYTES = 1024 * 1024

PV_GPRE, PV_SC1, PV_SH, PV_GPOST, PV_GT = 0, 1, 2, 3, 4
PG_GPRE, PG_SC, PG_SH, PG_GPOST, PG_GT, PG_EXTRA = 0, 1, 2, 3, 4, 5


def _params(sem):
    return pltpu.CompilerParams(dimension_semantics=sem, vmem_limit_bytes=VMEM_LIMIT_BYTES)


def _dot(a, b):
    return jnp.dot(a, b, preferred_element_type=F32)


def _dot_nt(a, b):
    return lax.dot_general(a, b, (((1,), (1,)), ((), ())), preferred_element_type=F32)


def _dot_tn(a, b):
    return lax.dot_general(a, b, (((0,), (0,)), ((), ())), preferred_element_type=F32)


def _sigmoid(x):
    return 1.0 / (1.0 + jnp.exp(-x))


def _rms(x):
    return lax.rsqrt(jnp.mean(x * x, axis=-1, keepdims=True) + NORM_EPS)


def _colsum(v):
    return jnp.sum(v, axis=0, keepdims=True)


def _prenorm(x, pv):
    r = _rms(x)
    xn = x * r
    return xn * (pv[PV_GPRE:PV_GPRE + 1] * pv[PV_SC1:PV_SC1 + 1]) + pv[PV_SH:PV_SH + 1], xn, r


def _post(x, y, pv):
    return x + (pv[PV_GT:PV_GT + 1] * pv[PV_GPOST:PV_GPOST + 1]) * (y * _rms(y))


def _post_bwd(dxo, y, pv, pg_ref):
    ry = _rms(y)
    yn = y * ry
    gt, gpost = pv[PV_GT:PV_GT + 1], pv[PV_GPOST:PV_GPOST + 1]
    dyn = dxo * (gt * gpost)
    dy = ry * (dyn - yn * jnp.mean(dyn * yn, axis=-1, keepdims=True))
    s = _colsum(dxo * yn)
    pg_ref[PG_GPOST:PG_GPOST + 1, :] += s * gt
    pg_ref[PG_GT:PG_GT + 1, :] += s * gpost
    return dy


def _pre_bwd(dh, x, pv, pg_ref):
    r = _rms(x)
    xn = x * r
    gpre, sc1 = pv[PV_GPRE:PV_GPRE + 1], pv[PV_SC1:PV_SC1 + 1]
    dxn = dh * (gpre * sc1)
    dx = r * (dxn - xn * jnp.mean(dxn * xn, axis=-1, keepdims=True))
    p = _colsum(dh * xn)
    pg_ref[PG_GPRE:PG_GPRE + 1, :] += p * sc1
    pg_ref[PG_SC:PG_SC + 1, :] += p * gpre
    pg_ref[PG_SH:PG_SH + 1, :] += _colsum(dh)
    return dx


def _shift_down(a, k, prev):
    out = pltpu.roll(a, k, 0)
    row = lax.broadcasted_iota(jnp.int32, a.shape, 0)
    for q in range(k):
        out = jnp.where(row == q, prev[SUBLANES - k + q:SUBLANES - k + q + 1, :], out)
    return out


def _shift_up(a, k, nxt):
    rows = a.shape[0]
    out = pltpu.roll(a, rows - k, 0)
    row = lax.broadcasted_iota(jnp.int32, a.shape, 0)
    for q in range(k):
        out = jnp.where(row == rows - k + q, nxt[q:q + 1, :], out)
    return out


class _Exchange:
    def __init__(self, srcs, gathers):
        self.srcs, self.gathers, self.n = list(srcs), list(gathers), len(srcs)
        self.out_shape = [jax.ShapeDtypeStruct(((N_DEV,) + s.shape) if g else s.shape, s.dtype)
                          for s, g in zip(srcs, gathers)]
        self.specs = [pl.BlockSpec(memory_space=pl.ANY)] * self.n
        self.scratch = [pltpu.SemaphoreType.DMA((self.n, N_DEV - 1)), pltpu.SemaphoreType.DMA((self.n, N_DEV - 1)),
                        pltpu.SemaphoreType.DMA((self.n,))]

    def _copies(self, src_refs, out_refs, sems):
        send_sems, recv_sems, local_sems = sems
        x, y, c = lax.axis_index("x"), lax.axis_index("y"), lax.axis_index("c")
        me = 4 * x + 2 * y + c
        copies = []
        for a in range(self.n):
            mine = src_refs[a] if self.gathers[a] else src_refs[a].at[me]
            copies.append(pltpu.make_async_copy(mine, out_refs[a].at[me], local_sems.at[a]))
        for d in range(1, N_DEV):
            px, py, pc = (x + (d >> 2)) % 2, (y + ((d >> 1) & 1)) % 2, (c + (d & 1)) % 2
            peer = 4 * px + 2 * py + pc
            for a in range(self.n):
                src = src_refs[a] if self.gathers[a] else src_refs[a].at[peer]
                copies.append(pltpu.make_async_remote_copy(
                    src_ref=src, dst_ref=out_refs[a].at[me], send_sem=send_sems.at[a, d - 1],
                    recv_sem=recv_sems.at[a, d - 1], device_id=(px, py, pc), device_id_type=pl.DeviceIdType.MESH))
        return copies

    def start(self, src_refs, out_refs, sems):
        for cp in self._copies(src_refs, out_refs, sems):
            cp.start()

    def wait(self, src_refs, out_refs, sems):
        for cp in self._copies(src_refs, out_refs, sems):
            cp.wait()


class _TwoLevelGather:
    def __init__(self, srcs, early_forward):
        self.srcs, self.n, self.early_forward = list(srcs), len(srcs), early_forward
        self.out_shape = [jax.ShapeDtypeStruct((N_DEV,) + s.shape, s.dtype) for s in srcs]
        self.specs = [pl.BlockSpec(memory_space=pl.ANY)] * self.n
        self.scratch = [pltpu.SemaphoreType.DMA((self.n, N_DEV - 1)), pltpu.SemaphoreType.DMA((self.n, N_DEV - 1)),
                        pltpu.SemaphoreType.DMA((self.n,))]

    def _places(self):
        x, y, c = lax.axis_index("x"), lax.axis_index("y"), lax.axis_index("c")
        return (x, y, c), (x, y, 1 - c), [(1 - x, y), (x, 1 - y), (1 - x, 1 - y)]

    def _copy(self, a, k, block, to, src_refs, out_refs, sems, from_src=False):
        slot = 4 * block[0] + 2 * block[1] + block[2]
        return pltpu.make_async_remote_copy(
            src_ref=src_refs[a] if from_src else out_refs[a].at[slot], dst_ref=out_refs[a].at[slot],
            send_sem=sems[0].at[a, k], recv_sem=sems[1].at[a, k], device_id=to, device_id_type=pl.DeviceIdType.MESH)

    def _local(self, a, src_refs, out_refs, sems):
        me, _, _ = self._places()
        return pltpu.make_async_copy(src_refs[a], out_refs[a].at[4 * me[0] + 2 * me[1] + me[2]], sems[2].at[a])

    def start(self, src_refs, out_refs, sems):
        me, sibling, chips = self._places()
        for a in range(self.n):
            self._local(a, src_refs, out_refs, sems).start()
            self._copy(a, 0, me, sibling, src_refs, out_refs, sems, from_src=True).start()
        for j, chip in enumerate(chips):
            for a in range(self.n):
                self._copy(a, 1 + j, me, (*chip, me[2]), src_refs, out_refs, sems, from_src=True).start()

    def forward(self, src_refs, out_refs, sems):
        me, sibling, chips = self._places()
        for j, chip in enumerate(chips):
            for a in range(self.n):
                self._copy(a, 1 + j, (*chip, me[2]), me, src_refs, out_refs, sems).wait_recv()
                self._copy(a, 4 + j, (*chip, me[2]), sibling, src_refs, out_refs, sems).start()

    def wait(self, src_refs, out_refs, sems):
        if not self.early_forward:
            self.forward(src_refs, out_refs, sems)
        me, sibling, chips = self._places()
        for a in range(self.n):
            self._copy(a, 0, sibling, me, src_refs, out_refs, sems).wait_recv()
            for j, chip in enumerate(chips):
                self._copy(a, 4 + j, (*chip, sibling[2]), me, src_refs, out_refs, sems).wait_recv()
        for a in range(self.n):
            self._copy(a, 0, me, sibling, src_refs, out_refs, sems, from_src=True).wait_send()
            for j, chip in enumerate(chips):
                self._copy(a, 1 + j, me, (*chip, me[2]), src_refs, out_refs, sems, from_src=True).wait_send()
                self._copy(a, 4 + j, (*chip, me[2]), sibling, src_refs, out_refs, sems).wait_send()
            self._local(a, src_refs, out_refs, sems).wait()


def _exchange(name, srcs, gathers):
    ex = _Exchange(srcs, gathers)

    def body(*refs):
        src_refs, out_refs, sems = refs[:ex.n], refs[ex.n:2 * ex.n], refs[2 * ex.n:]
        ex.start(src_refs, out_refs, sems)
        ex.wait(src_refs, out_refs, sems)

    return pl.pallas_call(body, name=name, out_shape=ex.out_shape, in_specs=ex.specs, out_specs=ex.specs,
                          scratch_shapes=ex.scratch)(*srcs)


def _launch(body, name, grid, in_specs, out_specs, out_shape, scratch_shapes, args, ex=None):
    sem = ("arbitrary",) * len(grid)
    if ex is None:
        outs = pl.pallas_call(body, name=name, grid=grid, in_specs=in_specs, out_specs=out_specs, out_shape=out_shape,
                              scratch_shapes=scratch_shapes, compiler_params=_params(sem))(*args)
        return outs, []
    n_in, n_out, n_scr = len(in_specs), len(out_specs), len(scratch_shapes)

    def hosted(*refs):
        ins, ex_in = refs[:n_in], refs[n_in:n_in + ex.n]
        outs = refs[n_in + ex.n:n_in + ex.n + n_out]
        ex_out = refs[n_in + ex.n + n_out:n_in + 2 * ex.n + n_out]
        rest = refs[n_in + 2 * ex.n + n_out:]
        scratch, sems = rest[:n_scr], rest[n_scr:]
        ids = [pl.program_id(a) for a in range(len(grid))]
        first = functools.reduce(lambda p, q: p & q, [i == 0 for i in ids])
        last = functools.reduce(lambda p, q: p & q, [i == g - 1 for i, g in zip(ids, grid)])

        @pl.when(first)
        def _():
            ex.start(ex_in, ex_out, sems)

        if getattr(ex, "early_forward", False):
            @pl.when(functools.reduce(lambda p, q: p & q, [i == (3 * g // 4 if a == 0 else 0)
                                                           for a, (i, g) in enumerate(zip(ids, grid))]))
            def _():
                ex.forward(ex_in, ex_out, sems)

        body(*ins, *outs, *scratch)

        @pl.when(last)
        def _():
            ex.wait(ex_in, ex_out, sems)

    outs = pl.pallas_call(
        hosted, name=name, grid=grid, in_specs=list(in_specs) + ex.specs, out_specs=list(out_specs) + ex.specs,
        out_shape=list(out_shape) + ex.out_shape, scratch_shapes=list(scratch_shapes) + ex.scratch,
        compiler_params=_params(sem))(*args, *ex.srcs)
    return outs[:n_out], outs[n_out:]


def _ada_fwd(c_all, ada_w):
    layers, d, cols = ada_w.shape

    def body(c_ref, w_ref, o_ref):
        c = c_ref[...]
        ca = (c * _sigmoid(c)).astype(BF16)
        for l in range(layers):
            o_ref[l] = _dot(ca, w_ref[l].astype(BF16))

    return pl.pallas_call(
        body, name="ada_fwd", out_shape=jax.ShapeDtypeStruct((layers, N_DEV, cols), F32),
        compiler_params=pltpu.CompilerParams(vmem_limit_bytes=VMEM_LIMIT_BYTES),
    )(c_all, ada_w)


def _ada_bwd(c_all, dmod_cols):
    layers, _, cols = dmod_cols.shape
    d = c_all.shape[1]

    def body(c_ref, g_ref, o_ref):
        c = c_ref[...]
        ca = (c * _sigmoid(c)).astype(BF16)
        for l in range(layers):
            o_ref[l] = _dot_tn(ca, g_ref[l].astype(BF16))

    return pl.pallas_call(
        body, name="ada_bwd", out_shape=jax.ShapeDtypeStruct((layers, d, cols), F32),
        compiler_params=pltpu.CompilerParams(vmem_limit_bytes=VMEM_LIMIT_BYTES),
    )(c_all, dmod_cols)


def _pooled(hbuf, h, t0, g, tm):
    gd = h.shape[1] // len(POOL_WINDOWS)
    cols = slice(g * gd, (g + 1) * gd)
    w = POOL_WINDOWS[g]
    hg = h[:, cols]
    s = hbuf[0:POOL_HALO + tm, cols]
    span = 1
    while span < w:
        s = s + pltpu.roll(s, span, 0)
        span *= 2
    s = s[POOL_HALO:POOL_HALO + tm, :]
    t = t0 + lax.broadcasted_iota(jnp.int32, (tm, 1), 0)
    cnt = jnp.minimum(t + 1, w).astype(F32)
    return s / cnt - hg, cnt


def _pool_fwd(x, pv, wp, scale, *, tm, ex=None):
    t_len, d = x.shape
    n_i = t_len // tm
    gd = d // len(POOL_WINDOWS)

    def body(x_ref, pv_ref, wp_ref, sc_ref, xo_ref, y_ref, hbuf):
        i = pl.program_id(0)

        @pl.when(i == 0)
        def _():
            hbuf[0:POOL_HALO, :] = jnp.zeros((POOL_HALO, d), F32)

        xv, pv_ = x_ref[...], pv_ref[...]
        h, _, _ = _prenorm(xv, pv_)
        hbuf[POOL_HALO:POOL_HALO + tm, :] = h
        for g in range(len(POOL_WINDOWS)):
            pooled, _ = _pooled(hbuf, h, i * tm, g, tm)
            y_ref[:, g * gd:(g + 1) * gd] = _dot(pooled.astype(BF16), wp_ref[g])
        xo_ref[...] = _post(xv, y_ref[...] * sc_ref[...], pv_)
        hbuf[0:POOL_HALO, :] = hbuf[tm:tm + POOL_HALO, :]

    row = pl.BlockSpec((tm, d), lambda i: (i, 0))
    return _launch(
        body, "pool_fwd", (n_i,),
        [row, pl.BlockSpec((SUBLANES, d), lambda i: (0, 0)), pl.BlockSpec(wp.shape, lambda i: (0, 0, 0)),
         pl.BlockSpec((1, d), lambda i: (0, 0))],
        [row, row],
        [jax.ShapeDtypeStruct((t_len, d), F32), jax.ShapeDtypeStruct((t_len, d), F32)],
        [pltpu.VMEM((tm + POOL_HALO, d), F32)],
        (x, pv, wp, scale), ex)


def _pool_bwd(dxo, x, ypre, pv, wp, scale, *, tm, ex=None):
    t_len, d = x.shape
    n_i = t_len // tm
    gd = d // len(POOL_WINDOWS)
    hb = tm // POOL_HALO

    def body(dxo_ref, x_ref, xh_ref, y_ref, pv_ref, wp_ref, sc_ref, dx_ref, pg_ref, dwp_ref, hbuf, qbuf):
        i = pl.program_id(0)
        ti = n_i - 1 - i

        @pl.when(i == 0)
        def _():
            pg_ref[...] = jnp.zeros_like(pg_ref)
            dwp_ref[...] = jnp.zeros_like(dwp_ref)
            qbuf[tm:tm + POOL_HALO, :] = jnp.zeros((POOL_HALO, d), F32)

        xv, pv_, dxo_v, yp, sc = x_ref[...], pv_ref[...], dxo_ref[...], y_ref[...], sc_ref[...]
        dy = _post_bwd(dxo_v, yp * sc, pv_, pg_ref)
        pg_ref[PG_EXTRA:PG_EXTRA + 1, :] += _colsum(dy * yp)
        dys = dy * sc
        h, _, _ = _prenorm(xv, pv_)
        hh, _, _ = _prenorm(xh_ref[...], pv_)
        hbuf[0:POOL_HALO, :] = jnp.where(ti > 0, hh, 0.0)
        hbuf[POOL_HALO:POOL_HALO + tm, :] = h
        for g in range(len(POOL_WINDOWS)):
            cols = slice(g * gd, (g + 1) * gd)
            pooled, cnt = _pooled(hbuf, h, ti * tm, g, tm)
            dyg = dys[:, cols].astype(BF16)
            dwp_ref[g] += _dot_tn(pooled.astype(BF16), dyg)
            dp = _dot_nt(dyg, wp_ref[g])
            qbuf[0:tm, cols] = dp / cnt
            fs = qbuf[0:tm + POOL_HALO, cols]
            span = 1
            while span < POOL_WINDOWS[g]:
                fs = fs + pltpu.roll(fs, tm + POOL_HALO - span, 0)
                span *= 2
            hbuf[POOL_HALO:POOL_HALO + tm, cols] = fs[0:tm, :] - dp
        dx_ref[...] = dxo_v + _pre_bwd(hbuf[POOL_HALO:POOL_HALO + tm, :], xv, pv_, pg_ref)
        qbuf[tm:tm + POOL_HALO, :] = qbuf[0:POOL_HALO, :]

    row = pl.BlockSpec((tm, d), lambda i: (n_i - 1 - i, 0))
    halo = pl.BlockSpec((POOL_HALO, d), lambda i: (jnp.maximum((n_i - 1 - i) * hb - 1, 0), 0))
    small = pl.BlockSpec((SUBLANES, d), lambda i: (0, 0))
    return _launch(
        body, "pool_bwd", (n_i,),
        [row, row, halo, row, small, pl.BlockSpec(wp.shape, lambda i: (0, 0, 0)), pl.BlockSpec((1, d), lambda i: (0, 0))],
        [row, small, pl.BlockSpec(wp.shape, lambda i: (0, 0, 0))],
        [jax.ShapeDtypeStruct((t_len, d), F32), jax.ShapeDtypeStruct((SUBLANES, d), F32),
         jax.ShapeDtypeStruct(wp.shape, F32)],
        [pltpu.VMEM((tm + POOL_HALO, d), F32), pltpu.VMEM((tm + POOL_HALO, d), F32)],
        (dxo, x, x, ypre, pv, wp, scale), ex)


def _ffn_conv(a, prev, w):
    return w[2:3] * a + w[1:2] * _shift_down(a, 1, prev) + w[0:1] * _shift_down(a, 2, prev)


def _ffn_fwd(x, pv, wup, wdw, wdn, *, tm, ex=None):
    t_len, d = x.shape
    _, n_j, fc, _ = wup.shape
    n_i = t_len // tm

    def body(x_ref, pv_ref, wup_ref, wdw_ref, wdn_ref, xo_ref, y_ref, h_ref, a_ref, c_ref, h_s, yacc, carry):
        i, j = pl.program_id(0), pl.program_id(1)

        @pl.when(j == 0)
        def _():
            h, _, _ = _prenorm(x_ref[...], pv_ref[...])
            hb = h.astype(BF16)
            h_s[...] = hb
            h_ref[...] = hb
            yacc[...] = jnp.zeros_like(yacc)

        @pl.when((i == 0) & (j == 0))
        def _():
            carry[...] = jnp.zeros_like(carry)

        hb = h_s[...]
        conv = []
        for s in range(2):
            a = _dot_nt(hb, wup_ref[s, 0])
            a_ref[s, 0] = a.astype(BF16)
            cv = _ffn_conv(a, carry[s, j], wdw_ref[s, 0])
            c_ref[s, 0] = cv.astype(BF16)
            conv.append(cv)
            carry[s, j] = a[tm - FFN_HALO:tm, :]
        g, v = conv
        u = g * _sigmoid(g) * v
        yacc[...] += _dot(u.astype(BF16), wdn_ref[...])

        @pl.when(j == n_j - 1)
        def _():
            y = yacc[...]
            y_ref[...] = y
            xo_ref[...] = _post(x_ref[...], y, pv_ref[...])

    row = pl.BlockSpec((tm, d), lambda i, j: (i, 0))
    return _launch(
        body, "ffn_fwd", (n_i, n_j),
        [row, pl.BlockSpec((SUBLANES, d), lambda i, j: (0, 0)),
         pl.BlockSpec((2, 1, fc, d), lambda i, j: (0, j, 0, 0)),
         pl.BlockSpec((2, 1, SUBLANES, fc), lambda i, j: (0, j, 0, 0)),
         pl.BlockSpec((fc, d), lambda i, j: (j, 0))],
        [row, row, row, pl.BlockSpec((2, 1, tm, fc), lambda i, j: (0, j, i, 0)),
         pl.BlockSpec((2, 1, tm, fc), lambda i, j: (0, j, i, 0))],
        [jax.ShapeDtypeStruct((t_len, d), F32), jax.ShapeDtypeStruct((t_len, d), F32),
         jax.ShapeDtypeStruct((t_len, d), BF16), jax.ShapeDtypeStruct((2, n_j, t_len, fc), BF16),
         jax.ShapeDtypeStruct((2, n_j, t_len, fc), BF16)],
        [pltpu.VMEM((tm, d), BF16), pltpu.VMEM((tm, d), F32), pltpu.VMEM((2, n_j, FFN_HALO, fc), F32)],
        (x, pv, wup, wdw, wdn), ex)


def _ffn_bwd(dy, h, a_sav, c_sav, wup, wdw, wdn, *, tm, ex=None):
    t_len, d = dy.shape
    _, n_j, fc, _ = wup.shape
    n_i = t_len // tm
    assert n_i % 2 == 0

    def body(dyp_ref, hp_ref, a_ref, c_ref, wup_ref, wdw_ref, wdn_ref,
             dhp_ref, gup_ref, gdn_ref, dwdw_ref, da_p, u_p, acc_up, acc_dn, carry):
        j, i = pl.program_id(0), pl.program_id(1)

        @pl.when((j == 0) & (i == 0))
        def _():
            dwdw_ref[...] = jnp.zeros_like(dwdw_ref)

        @pl.when(i == 0)
        def _():
            acc_up[...] = jnp.zeros_like(acc_up)
            acc_dn[...] = jnp.zeros_like(acc_dn)
            carry[...] = jnp.zeros_like(carry)

        def step(half):
            rows = slice(half * tm, (half + 1) * tm)
            du = _dot_nt(dyp_ref[rows, :], wdn_ref[...])
            g, v = c_ref[0, 0].astype(F32), c_ref[1, 0].astype(F32)
            sg = _sigmoid(g)
            sl = g * sg
            u_p[rows, :] = (sl * v).astype(BF16)
            d2 = (du * v * (sg * (1.0 + g * (1.0 - sg))), du * sl)
            dab = []
            for s in range(2):
                w = wdw_ref[s, 0]
                nxt = carry[s]
                p1, p2 = _shift_up(d2[s], 1, nxt), _shift_up(d2[s], 2, nxt)
                carry[s] = d2[s][0:FFN_HALO, :]
                dab.append((w[2:3] * d2[s] + w[1:2] * p1 + w[0:1] * p2).astype(BF16))
                da_p[s, rows, :] = dab[s]
                a_s = a_ref[s, 0].astype(F32)
                for k, sh in ((2, d2[s]), (1, p1), (0, p2)):
                    dwdw_ref[s, j, k:k + 1, :] += _colsum(a_s * sh)
            dhp_ref[0] = (_dot(dab[0], wup_ref[0, 0]) + _dot(dab[1], wup_ref[1, 0])).astype(BF16)
            if half == 0:
                hp = hp_ref[...]
                for s in range(2):
                    acc_up[s] += _dot_tn(da_p[s], hp)
                acc_dn[...] += _dot_tn(u_p[...], dyp_ref[...])

        @pl.when(i % 2 == 0)
        def _():
            step(1)

        @pl.when(i % 2 == 1)
        def _():
            step(0)

        @pl.when(i == n_i - 1)
        def _():
            gup_ref[:, 0] = acc_up[...].astype(BF16)
            gdn_ref[...] = acc_dn[...].astype(BF16)

    chunk = lambda shape: pl.BlockSpec((2, 1) + shape, lambda j, i: (0, j, 0, 0))
    tile = pl.BlockSpec((2, 1, tm, fc), lambda j, i: (0, j, n_i - 1 - i, 0))
    pair = pl.BlockSpec((2 * tm, d), lambda j, i: ((n_i - 1 - i) // 2, 0))
    return _launch(
        body, "ffn_bwd", (n_j, n_i),
        [pair, pair, tile, tile, chunk((fc, d)), chunk((SUBLANES, fc)), pl.BlockSpec((fc, d), lambda j, i: (j, 0))],
        [pl.BlockSpec((1, tm, d), lambda j, i: (j, n_i - 1 - i, 0)), chunk((fc, d)),
         pl.BlockSpec((fc, d), lambda j, i: (j, 0)), pl.BlockSpec((2, n_j, SUBLANES, fc), lambda j, i: (0, 0, 0, 0))],
        [jax.ShapeDtypeStruct((n_j, t_len, d), BF16), jax.ShapeDtypeStruct((2, n_j, fc, d), BF16),
         jax.ShapeDtypeStruct((n_j * fc, d), BF16), jax.ShapeDtypeStruct((2, n_j, SUBLANES, fc), F32)],
        [pltpu.VMEM((2, 2 * tm, fc), BF16), pltpu.VMEM((2 * tm, fc), BF16), pltpu.VMEM((2, fc, d), F32),
         pltpu.VMEM((fc, d), F32), pltpu.VMEM((2, FFN_HALO, fc), F32)],
        (dy, h, a_sav, c_sav, wup, wdw, wdn), ex)


def _sub_post_bwd(dxo, ypre, pv, *, tm, ex=None):
    t_len, d = dxo.shape
    n_i = t_len // tm

    def body(dxo_ref, y_ref, pv_ref, dy_ref, pg_ref):
        @pl.when(pl.program_id(0) == 0)
        def _():
            pg_ref[...] = jnp.zeros_like(pg_ref)

        dy_ref[...] = _post_bwd(dxo_ref[...], y_ref[...], pv_ref[...], pg_ref).astype(BF16)

    row = pl.BlockSpec((tm, d), lambda i: (i, 0))
    small = pl.BlockSpec((SUBLANES, d), lambda i: (0, 0))
    return _launch(body, "sub_post_bwd", (n_i,), [row, row, small], [row, small],
                   [jax.ShapeDtypeStruct((t_len, d), BF16), jax.ShapeDtypeStruct((SUBLANES, d), F32)], [],
                   (dxo, ypre, pv), ex)


def _sub_pre_bwd(dhp, x, dxo, pv, *, tm, ex=None):
    n_p, t_len, d = dhp.shape
    n_i = t_len // tm

    def body(dhp_ref, x_ref, dxo_ref, pv_ref, dx_ref, pg_ref):
        @pl.when(pl.program_id(0) == 0)
        def _():
            pg_ref[...] = jnp.zeros_like(pg_ref)

        dh = dhp_ref[0].astype(F32)
        for p in range(1, n_p):
            dh = dh + dhp_ref[p].astype(F32)
        dx_ref[...] = dxo_ref[...] + _pre_bwd(dh, x_ref[...], pv_ref[...], pg_ref)

    row = pl.BlockSpec((tm, d), lambda i: (i, 0))
    small = pl.BlockSpec((SUBLANES, d), lambda i: (0, 0))
    return _launch(body, "sub_pre_bwd", (n_i,), [pl.BlockSpec((n_p, tm, d), lambda i: (0, i, 0)), row, row, small],
                   [row, small], [jax.ShapeDtypeStruct((t_len, d), F32), jax.ShapeDtypeStruct((SUBLANES, d), F32)], [],
                   (dhp, x, dxo, pv), ex)


CV_B1, CV_BDW, CV_LNG, CV_LNB, CV_B2 = 0, 1, 2, 3, 4


def _taps_by_residue(offs):
    groups = {}
    for k, off in enumerate(offs):
        groups.setdefault(off % SUBLANES, []).append((k, off // SUBLANES))
    return sorted(groups.items())


def _depthwise(buf, w_ref, out_ref, offs, tm, d):
    taps_of = _taps_by_residue(offs)

    def chunk(r, carry):
        r0 = pl.multiple_of(r * DW_ROWS, DW_ROWS)
        for cb in range(d // LANES):
            cols = slice(cb * LANES, (cb + 1) * LANES)
            win = buf[pl.ds(r0, DW_ROWS + CONV_HALO), cols]
            acc = jnp.zeros((DW_ROWS, LANES), F32)
            for b, taps in taps_of:
                wb = win if b == 0 else pltpu.roll(win, DW_ROWS + CONV_HALO - b, 0)
                for k, a in taps:
                    acc = acc + wb[SUBLANES * a:SUBLANES * a + DW_ROWS, :] * w_ref[k:k + 1, cols]
            out_ref[pl.ds(r0, DW_ROWS), cols] = acc
        return carry

    lax.fori_loop(0, tm // DW_ROWS, chunk, 0)


def _depthwise_wgrad(dbuf, ubuf, dw_ref, tm, d):
    taps_of = _taps_by_residue(tuple(2 + k for k in range(CONV_TAPS)))
    for cb in range(d // LANES):
        cols = slice(cb * LANES, (cb + 1) * LANES)

        def chunk(r, acc):
            r0 = pl.multiple_of(r * CONV_ROWS, CONV_ROWS)
            dv = dbuf[pl.ds(r0, CONV_ROWS), cols]
            win = ubuf[pl.ds(r0, CONV_ROWS + CONV_HALO), cols]
            new = list(acc)
            for b, taps in taps_of:
                wb = win if b == 0 else pltpu.roll(win, CONV_ROWS + CONV_HALO - b, 0)
                for k, a in taps:
                    for q in range(CONV_ROWS // SUBLANES):
                        new[k] = new[k] + dv[SUBLANES * q:SUBLANES * (q + 1), :] * wb[SUBLANES * (a + q):SUBLANES * (a + q + 1), :]
            return tuple(new)

        acc = lax.fori_loop(0, tm // CONV_ROWS, chunk, tuple(jnp.zeros((SUBLANES, LANES), F32) for _ in range(CONV_TAPS)))
        for k in range(CONV_TAPS):
            dw_ref[k:k + 1, cols] += _colsum(acc[k])


def _layer_norm_parts(c1):
    mu = jnp.mean(c1, axis=-1, keepdims=True)
    cen = c1 - mu
    rstd = lax.rsqrt(jnp.mean(cen * cen, axis=-1, keepdims=True) + NORM_EPS)
    return cen * rstd, rstd


def _conv_fwd(x, pv, w1, w2, wdw, vec, *, tm):
    t_len, d = x.shape
    n_i = t_len // tm
    n_q = w1.shape[0] // 2
    qc = w1.shape[2]

    def body(x_ref, pv_ref, w1_ref, w2_ref, wdw_ref, vec_ref, xo_ref, y_ref, a_ref, c1_ref, ubuf):
        i = pl.program_id(0)

        @pl.when(i == 0)
        def _():
            ubuf[0:CONV_HALO, :] = jnp.zeros((CONV_HALO, d), F32)

        xv, pv_ = x_ref[...], pv_ref[...]
        h, _, _ = _prenorm(xv, pv_)
        hb = h.astype(BF16)
        for q in range(n_q):
            cols = slice(q * qc, (q + 1) * qc)
            gcols = slice(d + q * qc, d + (q + 1) * qc)
            val = (_dot(hb, w1_ref[q]) + vec_ref[CV_B1:CV_B1 + 1, cols]).astype(BF16)
            gate = (_dot(hb, w1_ref[n_q + q]) + vec_ref[CV_B1:CV_B1 + 1, gcols]).astype(BF16)
            a_ref[:, cols] = val
            a_ref[:, gcols] = gate
            ubuf[CONV_HALO:CONV_HALO + tm, cols] = val.astype(F32) * _sigmoid(gate.astype(F32))
        _depthwise(ubuf, wdw_ref, c1_ref, tuple(2 + k for k in range(CONV_TAPS)), tm, d)
        c1 = c1_ref[...] + vec_ref[CV_BDW:CV_BDW + 1, 0:d]
        c1_ref[...] = c1
        xhat, _ = _layer_norm_parts(c1)
        ln = xhat * vec_ref[CV_LNG:CV_LNG + 1, 0:d] + vec_ref[CV_LNB:CV_LNB + 1, 0:d]
        s = ln * _sigmoid(ln)
        y = _dot(s.astype(BF16), w2_ref[...]) + vec_ref[CV_B2:CV_B2 + 1, 0:d]
        y_ref[...] = y
        xo_ref[...] = _post(xv, y, pv_)
        ubuf[0:CONV_HALO, :] = ubuf[tm:tm + CONV_HALO, :]

    row = pl.BlockSpec((tm, d), lambda i: (i, 0))
    return pl.pallas_call(
        body, name="conv_fwd", grid=(n_i,),
        in_specs=[row, pl.BlockSpec((SUBLANES, d), lambda i: (0, 0)), pl.BlockSpec(w1.shape, lambda i: (0, 0, 0)),
                  pl.BlockSpec(w2.shape, lambda i: (0, 0)), pl.BlockSpec(wdw.shape, lambda i: (0, 0)),
                  pl.BlockSpec(vec.shape, lambda i: (0, 0))],
        out_specs=[row, row, pl.BlockSpec((tm, 2 * d), lambda i: (i, 0)), row],
        out_shape=[jax.ShapeDtypeStruct((t_len, d), F32), jax.ShapeDtypeStruct((t_len, d), F32),
                   jax.ShapeDtypeStruct((t_len, 2 * d), BF16), jax.ShapeDtypeStruct((t_len, d), F32)],
        scratch_shapes=[pltpu.VMEM((tm + CONV_HALO, d), F32)],
        compiler_params=_params(("arbitrary",)),
    )(x, pv, w1, w2, wdw, vec)


def _conv_bwd(dxo, x, ypre, a_sav, c1_sav, pv, w1, w2, wdw, vec, *, tm, ex=None):
    t_len, d = x.shape
    n_i = t_len // tm
    n_q = w1.shape[0] // 2
    qc = w1.shape[2]
    hb_ = tm // CONV_HALO

    def body(dxo_ref, x_ref, y_ref, a_ref, ah_ref, c1_ref, pv_ref, w1_ref, w2_ref, wdw_ref, vec_ref,
             dx_ref, pg_ref, gw1_ref, gw2_ref, gvec_ref, gwdw_ref, ubuf, dcbuf, dubuf, acc1, acc2):
        i = pl.program_id(0)
        ti = n_i - 1 - i

        @pl.when(i == 0)
        def _():
            pg_ref[...] = jnp.zeros_like(pg_ref)
            gvec_ref[...] = jnp.zeros_like(gvec_ref)
            gwdw_ref[...] = jnp.zeros_like(gwdw_ref)
            acc1[...] = jnp.zeros_like(acc1)
            acc2[...] = jnp.zeros_like(acc2)
            dcbuf[tm:tm + CONV_HALO, :] = jnp.zeros((CONV_HALO, d), F32)

        xv, pv_, dxo_v = x_ref[...], pv_ref[...], dxo_ref[...]
        dy = _post_bwd(dxo_v, y_ref[...], pv_, pg_ref)
        gvec_ref[CV_B2:CV_B2 + 1, 0:d] += _colsum(dy)
        dyb = dy.astype(BF16)
        xhat, rstd = _layer_norm_parts(c1_ref[...])
        lng = vec_ref[CV_LNG:CV_LNG + 1, 0:d]
        ln = xhat * lng + vec_ref[CV_LNB:CV_LNB + 1, 0:d]
        sg = _sigmoid(ln)
        acc2[...] += _dot_tn((ln * sg).astype(BF16), dyb)
        dln = _dot_nt(dyb, w2_ref[...]) * (sg * (1.0 + ln * (1.0 - sg)))
        gvec_ref[CV_LNG:CV_LNG + 1, 0:d] += _colsum(dln * xhat)
        gvec_ref[CV_LNB:CV_LNB + 1, 0:d] += _colsum(dln)
        dxh = dln * lng
        dc1 = rstd * (dxh - jnp.mean(dxh, axis=-1, keepdims=True)
                      - xhat * jnp.mean(dxh * xhat, axis=-1, keepdims=True))
        gvec_ref[CV_BDW:CV_BDW + 1, 0:d] += _colsum(dc1)
        dcbuf[0:tm, :] = dc1
        for q in range(n_q):
            cols = slice(q * qc, (q + 1) * qc)
            gcols = slice(d + q * qc, d + (q + 1) * qc)
            ubuf[CONV_HALO:CONV_HALO + tm, cols] = a_ref[:, cols].astype(F32) * _sigmoid(a_ref[:, gcols].astype(F32))
            uh = ah_ref[:, cols].astype(F32) * _sigmoid(ah_ref[:, gcols].astype(F32))
            ubuf[0:CONV_HALO, cols] = jnp.where(ti > 0, uh, 0.0)
        _depthwise_wgrad(dcbuf, ubuf, gwdw_ref, tm, d)
        _depthwise(dcbuf, wdw_ref, dubuf, tuple(CONV_TAPS - 1 - k for k in range(CONV_TAPS)), tm, d)
        dcbuf[tm:tm + CONV_HALO, :] = dcbuf[0:CONV_HALO, :]
        h, _, _ = _prenorm(xv, pv_)
        hb = h.astype(BF16)
        dh = jnp.zeros((tm, d), F32)
        for q in range(n_q):
            cols = slice(q * qc, (q + 1) * qc)
            gcols = slice(d + q * qc, d + (q + 1) * qc)
            du = dubuf[:, cols]
            val, gate = a_ref[:, cols].astype(F32), a_ref[:, gcols].astype(F32)
            sgg = _sigmoid(gate)
            dval = du * sgg
            dgate = du * val * (sgg * (1.0 - sgg))
            gvec_ref[CV_B1:CV_B1 + 1, cols] += _colsum(dval)
            gvec_ref[CV_B1:CV_B1 + 1, gcols] += _colsum(dgate)
            dvb, dgb = dval.astype(BF16), dgate.astype(BF16)
            acc1[q] += _dot_tn(hb, dvb)
            acc1[n_q + q] += _dot_tn(hb, dgb)
            dh = dh + _dot_nt(dvb, w1_ref[q]) + _dot_nt(dgb, w1_ref[n_q + q])
        dx_ref[...] = dxo_v + _pre_bwd(dh, xv, pv_, pg_ref)

        @pl.when(i == n_i - 1)
        def _():
            gw1_ref[...] = acc1[...].astype(BF16)
            gw2_ref[...] = acc2[...].astype(BF16)

    row = pl.BlockSpec((tm, d), lambda i: (n_i - 1 - i, 0))
    small = pl.BlockSpec((SUBLANES, d), lambda i: (0, 0))
    whole2 = lambda shape: pl.BlockSpec(shape, lambda i: (0, 0))
    return _launch(
        body, "conv_bwd", (n_i,),
        [row, row, row,
         pl.BlockSpec((tm, 2 * d), lambda i: (n_i - 1 - i, 0)),
         pl.BlockSpec((CONV_HALO, 2 * d), lambda i: (jnp.maximum((n_i - 1 - i) * hb_ - 1, 0), 0)),
         row, small, pl.BlockSpec(w1.shape, lambda i: (0, 0, 0)), whole2(w2.shape), whole2(wdw.shape),
         whole2(vec.shape)],
        [row, small, pl.BlockSpec(w1.shape, lambda i: (0, 0, 0)), whole2(w2.shape), whole2(vec.shape),
         whole2(wdw.shape)],
        [jax.ShapeDtypeStruct((t_len, d), F32), jax.ShapeDtypeStruct((SUBLANES, d), F32),
         jax.ShapeDtypeStruct(w1.shape, BF16), jax.ShapeDtypeStruct(w2.shape, BF16),
         jax.ShapeDtypeStruct(vec.shape, F32), jax.ShapeDtypeStruct(wdw.shape, F32)],
        [pltpu.VMEM((tm + CONV_HALO, d), F32), pltpu.VMEM((tm + CONV_HALO, d), F32),
         pltpu.VMEM((tm, d), F32), pltpu.VMEM(w1.shape, F32), pltpu.VMEM(w2.shape, F32)],
        (dxo, x, ypre, a_sav, a_sav, c1_sav, pv, w1, w2, wdw, vec), ex)


def _loss_head(xo, target, ypre, pv, *, tm):
    t_len, d = xo.shape
    n_i = t_len // tm

    def body(xo_ref, t_ref, y_ref, pv_ref, dxo_ref, sq_ref, dy_ref, pg_ref):
        @pl.when(pl.program_id(0) == 0)
        def _():
            sq_ref[...] = jnp.zeros_like(sq_ref)
            pg_ref[...] = jnp.zeros_like(pg_ref)

        err = xo_ref[...] - t_ref[...]
        dxo = err * (1.0 / d)
        dxo_ref[...] = dxo
        sq_ref[...] += jnp.sum((err * err).reshape(tm // SUBLANES, SUBLANES, d), axis=0)
        dy_ref[...] = _post_bwd(dxo, y_ref[...], pv_ref[...], pg_ref).astype(BF16)

    row = pl.BlockSpec((tm, d), lambda i: (i, 0))
    small = pl.BlockSpec((SUBLANES, d), lambda i: (0, 0))
    return pl.pallas_call(
        body, name="loss_head", grid=(n_i,), in_specs=[row, row, row, small],
        out_specs=[row, small, row, small],
        out_shape=[jax.ShapeDtypeStruct((t_len, d), F32), jax.ShapeDtypeStruct((SUBLANES, d), F32),
                   jax.ShapeDtypeStruct((t_len, d), BF16), jax.ShapeDtypeStruct((SUBLANES, d), F32)],
        compiler_params=_params(("arbitrary",)),
    )(xo, target, ypre, pv)


def _adam(name, parts, w, m, v):
    layers = len(parts)
    n, rows, cols = parts[0].shape
    tr = rows
    if rows % SUBLANES == 0:
        cap = max(SUBLANES, ADAM_BLOCK_BYTES // (4 * cols))
        tr = max(t for t in range(SUBLANES, rows + 1, SUBLANES) if rows % t == 0 and (t <= cap or t == SUBLANES))
    c1 = 1.0 / (1.0 - ADAM_B1 ** ADAM_STEP)
    c2 = 1.0 / (1.0 - ADAM_B2 ** ADAM_STEP)

    def body(*refs):
        p_refs = refs[:layers]
        w_ref, m_ref, v_ref, g_ref, d_ref, mo_ref, vo_ref, g_s = refs[layers:]
        for l in range(layers):
            @pl.when(pl.program_id(0) == l)
            def _():
                g = p_refs[l][0].astype(F32)
                for k in range(1, n):
                    g = g + p_refs[l][k].astype(F32)
                g_s[...] = g

        g = g_s[...]
        m2 = ADAM_B1 * m_ref[0] + (1.0 - ADAM_B1) * g
        v2 = ADAM_B2 * v_ref[0] + (1.0 - ADAM_B2) * (g * g)
        g_ref[0] = g
        mo_ref[0] = m2
        vo_ref[0] = v2
        d_ref[0] = -ADAM_LR * ((m2 * c1) / (jnp.sqrt(v2 * c2) + ADAM_EPS) + ADAM_WD * w_ref[0])

    def part_spec(l):
        return pl.BlockSpec((n, tr, cols), lambda ll, i: (0, jnp.where(ll == l, i, 0), 0))

    blk = pl.BlockSpec((1, tr, cols), lambda ll, i: (ll, i, 0))
    out = jax.ShapeDtypeStruct((layers, rows, cols), F32)
    return pl.pallas_call(
        body, name=name, grid=(layers, rows // tr),
        in_specs=[part_spec(l) for l in range(layers)] + [blk, blk, blk],
        out_specs=[blk, blk, blk, blk], out_shape=[out, out, out, out],
        scratch_shapes=[pltpu.VMEM((tr, cols), F32)],
        compiler_params=_params(("arbitrary", "arbitrary")),
    )(*parts, w, m, v)


def _adam_nd(name, parts, w, m, v):
    shape = w.shape
    cols = shape[-1]
    if isinstance(parts, (list, tuple)):
        layers = len(parts)
    else:
        layers, parts = 1, [parts]
    rows = w.size // (cols * layers)
    flat = lambda t: t.reshape(layers, rows, cols)
    outs = _adam(name, [p.reshape(p.shape[0], rows, cols) for p in parts], flat(w), flat(m), flat(v))
    return [o.reshape(shape) for o in outs]


def _small_pack(parts, size):
    flat = jnp.concatenate([p.reshape(-1) for p in parts])
    return jnp.pad(flat, (0, size - flat.shape[0]))


def _to_shards(full, axis):
    shp = full.shape
    split = full.reshape(shp[:axis] + (N_DEV, shp[axis] // N_DEV) + shp[axis + 1:])
    return jnp.moveaxis(split, axis, 0)


def _from_shards(sh, axis):
    moved = jnp.moveaxis(sh, 0, axis)
    shp = moved.shape
    return moved.reshape(shp[:axis] + (shp[axis] * shp[axis + 1],) + shp[axis + 2:])


def kernel(x, c, ada_w, ada_b, pre_g, post_g, pool_w, pool_scale, cv_w_pw1, cv_b_pw1, cv_w_dw, cv_b_dw, cv_ln_g, cv_ln_b, cv_w_pw2, cv_b_pw2, ffn_w_up, ffn_w_dw, ffn_w_down, loss_target, m_ada_w, m_ada_b, m_pre_g, m_post_g, m_pool_w, m_pool_scale, m_cv_w_pw1, m_cv_b_pw1, m_cv_w_dw, m_cv_b_dw, m_cv_ln_g, m_cv_ln_b, m_cv_w_pw2, m_cv_b_pw2, m_ffn_w_up, m_ffn_w_dw, m_ffn_w_down, v_ada_w, v_ada_b, v_pre_g, v_post_g, v_pool_w, v_pool_scale, v_cv_w_pw1, v_cv_b_pw1, v_cv_w_dw, v_cv_b_dw, v_cv_ln_g, v_cv_ln_b, v_cv_w_pw2, v_cv_b_pw2, v_ffn_w_up, v_ffn_w_dw, v_ffn_w_down):
    t_len, d = x.shape[1], x.shape[2]
    depth = ada_w.shape[0]
    fc = ffn_w_up.shape[2]
    n_j = N_DEV // 2
    me = 4 * lax.axis_index("x") + 2 * lax.axis_index("y") + lax.axis_index("c")

    small_w = [pre_g, post_g, cv_b_pw1, cv_w_dw, cv_b_dw, cv_ln_g, cv_ln_b, cv_b_pw2, ffn_w_dw]
    small_m = [m_pre_g, m_post_g, m_cv_b_pw1, m_cv_w_dw, m_cv_b_dw, m_cv_ln_g, m_cv_ln_b, m_cv_b_pw2, m_ffn_w_dw]
    small_v = [v_pre_g, v_post_g, v_cv_b_pw1, v_cv_w_dw, v_cv_b_dw, v_cv_ln_g, v_cv_ln_b, v_cv_b_pw2, v_ffn_w_dw]
    sizes = [p.size for p in small_w]
    offs = [sum(sizes[:k]) for k in range(len(sizes) + 1)]
    pack = -(-offs[-1] // (SUBLANES * LANES)) * SUBLANES * LANES

    got = _exchange("gather_small", [c, _small_pack(small_w, pack), pool_w[0].astype(BF16)], [True] * 3)
    c_all = got[0].reshape(N_DEV, d)
    smalls = [got[1][:, offs[k]:offs[k + 1]].reshape((N_DEV,) + small_w[k].shape) for k in range(len(small_w))]
    pre_g_f, post_g_f = _from_shards(smalls[0], 2), _from_shards(smalls[1], 2)
    b1_f = _from_shards(smalls[2], 1)[0]
    cvw_f = jnp.pad(_from_shards(smalls[3], 2)[0], ((0, CONV_HALO - CONV_TAPS), (0, 0)))
    bdw_f, lng_f, lnb_f, b2_f = [_from_shards(smalls[k], 1)[0] for k in (4, 5, 6, 7)]
    fdw = jnp.pad(smalls[8], ((0, 0), (0, 0), (0, SUBLANES - FFN_TAPS), (0, 0)))
    wp = jnp.swapaxes(got[2], 0, 1).reshape(pool_w.shape[1], -1, pool_w.shape[3])
    wdw = [fdw[:, l].reshape(2, n_j, SUBLANES, fc) for l in range(depth)]
    cvec = jnp.zeros((SUBLANES, 2 * d), F32)
    cvec = cvec.at[CV_B1].set(b1_f)
    for r, vrow in ((CV_BDW, bdw_f), (CV_LNG, lng_f), (CV_LNB, lnb_f), (CV_B2, b2_f)):
        cvec = cvec.at[r, :d].set(vrow)

    mod_cols = _ada_fwd(c_all, ada_w)
    (mod_all,) = _exchange("gather_mod", [mod_cols], [True])
    mod = lax.dynamic_index_in_dim(mod_all, me, axis=2, keepdims=False)
    mod = jnp.swapaxes(mod, 0, 1).reshape(depth, N_MOD, d) + ada_b.reshape(depth, N_MOD, d)

    def pv_of(l, s):
        rows = [pre_g_f[l, s], 1.0 + mod[l, 3 * s + 1], mod[l, 3 * s], post_g_f[l, s], mod[l, 3 * s + 2]]
        return jnp.concatenate([jnp.stack(rows), jnp.zeros((SUBLANES - len(rows), d), F32)])

    x0 = x[0]
    pv00, pv01, pv10, pv11 = pv_of(0, 0), pv_of(0, 1), pv_of(1, 0), pv_of(1, 1)
    tm_pool, tm_ffn, tm_bwd, tm_conv = min(TM_POOL, t_len), min(TM_FFN, t_len), min(TM_FFN_BWD, t_len), min(TM_CONV, t_len)
    tm_w = min(TM_FFN_W, t_len)
    wup_t = jnp.swapaxes(ffn_w_up, 1, 2)
    ex = _TwoLevelGather([wup_t[0].astype(BF16), ffn_w_down[0].astype(BF16)], early_forward=False)
    (x1, y0), (wup0, wdn0) = _pool_fwd(x0, pv00, wp, pool_scale, tm=tm_pool, ex=ex)
    ex = _TwoLevelGather([cv_w_pw1[0].astype(BF16), cv_w_pw2[0].astype(BF16), wup_t[1].astype(BF16),
                          ffn_w_down[1].astype(BF16)], early_forward=True)
    (x2, y1, h1, a1, c1), (w1, w2, wup1, wdn1) = _ffn_fwd(
        x1, pv01, wup0.reshape(2, n_j, fc, d), wdw[0], wdn0.reshape(n_j * fc, d), tm=tm_ffn, ex=ex)
    w2 = w2.reshape(d, d)
    wup = [wup0.reshape(2, n_j, fc, d), wup1.reshape(2, n_j, fc, d)]
    wdn = [wdn0.reshape(n_j * fc, d), wdn1.reshape(n_j * fc, d)]
    x3, y2, a2, c2 = _conv_fwd(x2, pv10, w1, w2, cvw_f, cvec, tm=tm_conv)
    (x4, y3, h3, a3, c3), _ = _ffn_fwd(x3, pv11, wup[1], wdw[1], wdn[1], tm=tm_ffn)
    dx4, sq, dy3, pgq11 = _loss_head(x4, loss_target[0], y3, pv11, tm=tm_ffn)
    loss = lax.psum(jnp.sum(sq) * (0.5 / d), MESH_AXES)

    (dhp3, gup1, gdn1, gfdw1), _ = _ffn_bwd(dy3, h3, a3, c3, wup[1], wdw[1], wdn[1], tm=tm_bwd)
    (dx3, pgp11), _ = _sub_pre_bwd(dhp3, x3, dx4, pv11, tm=tm_ffn)
    ex = _Exchange([gup1.reshape(N_DEV, fc, d), gdn1.reshape(N_DEV, -1, d)], [False, False])
    (dx2, pg10, gw1, gw2, gcvec, gcvw), (rup1, rdn1) = _conv_bwd(
        dx3, x2, y2, a2, c2, pv10, w1, w2, cvw_f, cvec, tm=tm_conv, ex=ex)
    (dy1, pgq01), _ = _sub_post_bwd(dx2, y1, pv01, tm=tm_w)
    ex = _Exchange([gw1, gw2.reshape(N_DEV, -1, d)], [False, False])
    (dhp1, gup0, gdn0, gfdw0), (rw1, rw2) = _ffn_bwd(dy1, h1, a1, c1, wup[0], wdw[0], wdn[0], tm=tm_bwd, ex=ex)
    ex = _Exchange([gdn0.reshape(N_DEV, -1, d)], [False])
    (dx1, pgp01), (rdn0,) = _sub_pre_bwd(dhp1, x1, dx2, pv01, tm=tm_ffn, ex=ex)
    ex = _Exchange([gup0.reshape(N_DEV, fc, d)], [False])
    (dx0, pg00, gwp), (rup0,) = _pool_bwd(dx1, x0, y0, pv00, wp, pool_scale, tm=tm_pool, ex=ex)
    pg01, pg11 = pgq01 + pgp01, pgq11 + pgp11

    pgs = [[pg00, pg01], [pg10, pg11]]
    g_pre = jnp.stack([jnp.stack([pgs[l][s][PG_GPRE] for s in range(2)]) for l in range(depth)])
    g_post = jnp.stack([jnp.stack([pgs[l][s][PG_GPOST] for s in range(2)]) for l in range(depth)])
    dmod = jnp.stack([jnp.concatenate([pgs[l][s][r] for s in range(2) for r in (PG_SH, PG_SC, PG_GT)])
                      for l in range(depth)])
    gfdw = jnp.stack([g.reshape(N_DEV, SUBLANES, fc)[:, :FFN_TAPS] for g in (gfdw0, gfdw1)], axis=1)
    small_g = [_to_shards(g_pre, 2), _to_shards(g_post, 2), _to_shards(gcvec[CV_B1][None], 1),
               _to_shards(gcvw[None, :CONV_TAPS], 2), _to_shards(gcvec[CV_BDW, :d][None], 1),
               _to_shards(gcvec[CV_LNG, :d][None], 1), _to_shards(gcvec[CV_LNB, :d][None], 1),
               _to_shards(gcvec[CV_B2, :d][None], 1), gfdw]
    small_send = jnp.concatenate([g.reshape(N_DEV, -1) for g in small_g], axis=1)
    small_send = jnp.pad(small_send, ((0, 0), (0, pack - small_send.shape[1])))
    gwp_send = jnp.swapaxes(gwp.reshape(gwp.shape[0], N_DEV, -1, gwp.shape[2]), 0, 1)
    rsmall, rwp, rmod, rscale = _exchange("scatter_small", [small_send, gwp_send, dmod, pg00[PG_EXTRA][None]],
                                          [False, False, True, True])

    outs = {}

    def put(name, res, shape=None):
        outs[name] = [r if shape is None else r.reshape(shape) for r in res]

    small_res = _adam_nd("adam_small", rsmall.reshape(N_DEV, -1, SUBLANES * LANES),
                         _small_pack(small_w, pack).reshape(-1, SUBLANES * LANES),
                         _small_pack(small_m, pack).reshape(-1, SUBLANES * LANES),
                         _small_pack(small_v, pack).reshape(-1, SUBLANES * LANES))
    small_names = ["pre_g", "post_g", "cv_b_pw1", "cv_w_dw", "cv_b_dw", "cv_ln_g", "cv_ln_b", "cv_b_pw2", "ffn_w_dw"]
    for k, nm in enumerate(small_names):
        outs[nm] = [r.reshape(-1)[offs[k]:offs[k + 1]].reshape(small_w[k].shape) for r in small_res]
    put("pool_w", _adam_nd("adam_pool_w", rwp[:, None], pool_w, m_pool_w, v_pool_w))
    put("cv_w_pw1", _adam_nd("adam_cv_w_pw1", rw1[:, None], cv_w_pw1, m_cv_w_pw1, v_cv_w_pw1))
    put("cv_w_pw2", _adam_nd("adam_cv_w_pw2", rw2[:, None], cv_w_pw2, m_cv_w_pw2, v_cv_w_pw2))
    outs["ffn_w_up"] = [jnp.swapaxes(r, 1, 2) for r in _adam_nd(
        "adam_ffn_w_up", [rup0, rup1], wup_t, jnp.swapaxes(m_ffn_w_up, 1, 2), jnp.swapaxes(v_ffn_w_up, 1, 2))]
    put("ffn_w_down", _adam_nd("adam_ffn_w_down", [rdn0, rdn1], ffn_w_down, m_ffn_w_down, v_ffn_w_down))
    put("ada_b", _adam_nd("adam_ada_b", rmod, ada_b, m_ada_b, v_ada_b))
    put("pool_scale", _adam_nd("adam_pool_scale", rscale, pool_scale, m_pool_scale, v_pool_scale))
    cols = ada_w.shape[2]
    dmod_cols = jnp.swapaxes(lax.dynamic_slice_in_dim(rmod, me * cols, cols, axis=2), 0, 1)
    put("ada_w", _adam_nd("adam_ada_w", _ada_bwd(c_all, dmod_cols)[None], ada_w, m_ada_w, v_ada_w))

    order = ["ada_w", "ada_b", "pre_g", "post_g", "pool_w", "pool_scale", "cv_w_pw1", "cv_b_pw1", "cv_w_dw", "cv_b_dw",
             "cv_ln_g", "cv_ln_b", "cv_w_pw2", "cv_b_pw2", "ffn_w_up", "ffn_w_dw", "ffn_w_down"]
    return (loss, dx0[None], *[outs[nm][0] for nm in order], *[outs[nm][1] for nm in order],
            *[outs[nm][2] for nm in order], *[outs[nm][3] for nm in order])
```

```python
import functools

import jax
import jax.numpy as jnp
from jax import lax
from jax.experimental import pallas as pl
from jax.experimental.pallas import tpu as pltpu

F32, BF16 = jnp.float32, jnp.bfloat16
MESH_AXES = ("x", "y", "c")
N_DEV = 8
NORM_EPS = 1e-6
ADAM_LR, ADAM_B1, ADAM_B2, ADAM_EPS, ADAM_WD, ADAM_STEP = 0.001, 0.9, 0.999, 1e-08, 0.01, 10
POOL_WINDOWS = (2, 4, 8, 16)
CONV_TAPS = 31
FFN_TAPS = 3
N_MOD = 6

SUBLANES = 8
LANES = 128
VMEM_LIMIT_BYTES = 56 * 1024 * 1024
POOL_HALO = 16
CONV_HALO = 32
FFN_HALO = 8
TM_POOL, TM_FFN, TM_FFN_BWD, TM_FFN_W, TM_CONV = 512, 512, 512, 1024, 256
CONV_ROWS = 32
DW_ROWS = 64
ADAM_BLOCK_BYTES = 1024 * 1024

PV_GPRE, PV_SC1, PV_SH, PV_GPOST, PV_GT = 0, 1, 2, 3, 4
PG_GPRE, PG_SC, PG_SH, PG_GPOST, PG_GT, PG_EXTRA = 0, 1, 2, 3, 4, 5


def _params(sem):
    return pltpu.CompilerParams(dimension_semantics=sem, vmem_limit_bytes=VMEM_LIMIT_BYTES)


def _dot(a, b):
    return jnp.dot(a, b, preferred_element_type=F32)


def _dot_nt(a, b):
    return lax.dot_general(a, b, (((1,), (1,)), ((), ())), preferred_element_type=F32)


def _dot_tn(a, b):
    return lax.dot_general(a, b, (((0,), (0,)), ((), ())), preferred_element_type=F32)


def _sigmoid(x):
    return 1.0 / (1.0 + jnp.exp(-x))


def _rms(x):
    return lax.rsqrt(jnp.mean(x * x, axis=-1, keepdims=True) + NORM_EPS)


def _colsum(v):
    return jnp.sum(v, axis=0, keepdims=True)


def _prenorm(x, pv):
    r = _rms(x)
    xn = x * r
    return xn * (pv[PV_GPRE:PV_GPRE + 1] * pv[PV_SC1:PV_SC1 + 1]) + pv[PV_SH:PV_SH + 1], xn, r


def _post(x, y, pv):
    return x + (pv[PV_GT:PV_GT + 1] * pv[PV_GPOST:PV_GPOST + 1]) * (y * _rms(y))


def _post_bwd(dxo, y, pv, pg_ref):
    ry = _rms(y)
    yn = y * ry
    gt, gpost = pv[PV_GT:PV_GT + 1], pv[PV_GPOST:PV_GPOST + 1]
    dyn = dxo * (gt * gpost)
    dy = ry * (dyn - yn * jnp.mean(dyn * yn, axis=-1, keepdims=True))
    s = _colsum(dxo * yn)
    pg_ref[PG_GPOST:PG_GPOST + 1, :] += s * gt
    pg_ref[PG_GT:PG_GT + 1, :] += s * gpost
    return dy


def _pre_bwd(dh, x, pv, pg_ref):
    r = _rms(x)
    xn = x * r
    gpre, sc1 = pv[PV_GPRE:PV_GPRE + 1], pv[PV_SC1:PV_SC1 + 1]
    dxn = dh * (gpre * sc1)
    dx = r * (dxn - xn * jnp.mean(dxn * xn, axis=-1, keepdims=True))
    p = _colsum(dh * xn)
    pg_ref[PG_GPRE:PG_GPRE + 1, :] += p * sc1
    pg_ref[PG_SC:PG_SC + 1, :] += p * gpre
    pg_ref[PG_SH:PG_SH + 1, :] += _colsum(dh)
    return dx


def _shift_down(a, k, prev):
    out = pltpu.roll(a, k, 0)
    row = lax.broadcasted_iota(jnp.int32, a.shape, 0)
    for q in range(k):
        out = jnp.where(row == q, prev[SUBLANES - k + q:SUBLANES - k + q + 1, :], out)
    return out


def _shift_up(a, k, nxt):
    rows = a.shape[0]
    out = pltpu.roll(a, rows - k, 0)
    row = lax.broadcasted_iota(jnp.int32, a.shape, 0)
    for q in range(k):
        out = jnp.where(row == rows - k + q, nxt[q:q + 1, :], out)
    return out


class _Exchange:
    def __init__(self, srcs, gathers):
        self.srcs, self.gathers, self.n = list(srcs), list(gathers), len(srcs)
        self.out_shape = [jax.ShapeDtypeStruct(((N_DEV,) + s.shape) if g else s.shape, s.dtype)
                          for s, g in zip(srcs, gathers)]
        self.specs = [pl.BlockSpec(memory_space=pl.ANY)] * self.n
        self.scratch = [pltpu.SemaphoreType.DMA((self.n, N_DEV - 1)), pltpu.SemaphoreType.DMA((self.n, N_DEV - 1)),
                        pltpu.SemaphoreType.DMA((self.n,))]

    def _copies(self, src_refs, out_refs, sems):
        send_sems, recv_sems, local_sems = sems
        x, y, c = lax.axis_index("x"), lax.axis_index("y"), lax.axis_index("c")
        me = 4 * x + 2 * y + c
        copies = []
        for a in range(self.n):
            mine = src_refs[a] if self.gathers[a] else src_refs[a].at[me]
            copies.append(pltpu.make_async_copy(mine, out_refs[a].at[me], local_sems.at[a]))
        for d in range(1, N_DEV):
            px, py, pc = (x + (d >> 2)) % 2, (y + ((d >> 1) & 1)) % 2, (c + (d & 1)) % 2
            peer = 4 * px + 2 * py + pc
            for a in range(self.n):
                src = src_refs[a] if self.gathers[a] else src_refs[a].at[peer]
                copies.append(pltpu.make_async_remote_copy(
                    src_ref=src, dst_ref=out_refs[a].at[me], send_sem=send_sems.at[a, d - 1],
                    recv_sem=recv_sems.at[a, d - 1], device_id=(px, py, pc), device_id_type=pl.DeviceIdType.MESH))
        return copies

    def start(self, src_refs, out_refs, sems):
        for cp in self._copies(src_refs, out_refs, sems):
            cp.start()

    def wait(self, src_refs, out_refs, sems):
        for cp in self._copies(src_refs, out_refs, sems):
            cp.wait()


class _TwoLevelGather:
    def __init__(self, srcs, early_forward):
        self.srcs, self.n, self.early_forward = list(srcs), len(srcs), early_forward
        self.out_shape = [jax.ShapeDtypeStruct((N_DEV,) + s.shape, s.dtype) for s in srcs]
        self.specs = [pl.BlockSpec(memory_space=pl.ANY)] * self.n
        self.scratch = [pltpu.SemaphoreType.DMA((self.n, N_DEV - 1)), pltpu.SemaphoreType.DMA((self.n, N_DEV - 1)),
                        pltpu.SemaphoreType.DMA((self.n,))]

    def _places(self):
        x, y, c = lax.axis_index("x"), lax.axis_index("y"), lax.axis_index("c")
        return (x, y, c), (x, y, 1 - c), [(1 - x, y), (x, 1 - y), (1 - x, 1 - y)]

    def _copy(self, a, k, block, to, src_refs, out_refs, sems, from_src=False):
        slot = 4 * block[0] + 2 * block[1] + block[2]
        return pltpu.make_async_remote_copy(
            src_ref=src_refs[a] if from_src else out_refs[a].at[slot], dst_ref=out_refs[a].at[slot],
            send_sem=sems[0].at[a, k], recv_sem=sems[1].at[a, k], device_id=to, device_id_type=pl.DeviceIdType.MESH)

    def _local(self, a, src_refs, out_refs, sems):
        me, _, _ = self._places()
        return pltpu.make_async_copy(src_refs[a], out_refs[a].at[4 * me[0] + 2 * me[1] + me[2]], sems[2].at[a])

    def start(self, src_refs, out_refs, sems):
        me, sibling, chips = self._places()
        for a in range(self.n):
            self._local(a, src_refs, out_refs, sems).start()
            self._copy(a, 0, me, sibling, src_refs, out_refs, sems, from_src=True).start()
        for j, chip in enumerate(chips):
            for a in range(self.n):
                self._copy(a, 1 + j, me, (*chip, me[2]), src_refs, out_refs, sems, from_src=True).start()

    def forward(self, src_refs, out_refs, sems):
        me, sibling, chips = self._places()
        for j, chip in enumerate(chips):
            for a in range(self.n):
                self._copy(a, 1 + j, (*chip, me[2]), me, src_refs, out_refs, sems).wait_recv()
                self._copy(a, 4 + j, (*chip, me[2]), sibling, src_refs, out_refs, sems).start()

    def wait(self, src_refs, out_refs, sems):
        if not self.early_forward:
            self.forward(src_refs, out_refs, sems)
        me, sibling, chips = self._places()
        for a in range(self.n):
            self._copy(a, 0, sibling, me, src_refs, out_refs, sems).wait_recv()
            for j, chip in enumerate(chips):
                self._copy(a, 4 + j, (*chip, sibling[2]), me, src_refs, out_refs, sems).wait_recv()
        for a in range(self.n):
            self._copy(a, 0, me, sibling, src_refs, out_refs, sems, from_src=True).wait_send()
            for j, chip in enumerate(chips):
                self._copy(a, 1 + j, me, (*chip, me[2]), src_refs, out_refs, sems, from_src=True).wait_send()
                self._copy(a, 4 + j, (*chip, me[2]), sibling, src_refs, out_refs, sems).wait_send()
            self._local(a, src_refs, out_refs, sems).wait()


class _SiblingSwap:
    def __init__(self, srcs):
        self.srcs, self.n = list(srcs), len(srcs)
        self.chips = srcs[0].shape[0]
        self.out_shape = [jax.ShapeDtypeStruct((s.shape[0],) + s.shape[2:], s.dtype) for s in srcs]
        self.specs = [pl.BlockSpec(memory_space=pl.ANY)] * self.n
        self.scratch = [pltpu.SemaphoreType.DMA((self.n, self.chips)), pltpu.SemaphoreType.DMA((self.n, self.chips))]

    def _copies(self, src_refs, out_refs, sems):
        x, y, c = lax.axis_index("x"), lax.axis_index("y"), lax.axis_index("c")
        return [pltpu.make_async_remote_copy(
            src_ref=src_refs[a].at[q, 1 - c], dst_ref=out_refs[a].at[q], send_sem=sems[0].at[a, q],
            recv_sem=sems[1].at[a, q], device_id=(x, y, 1 - c), device_id_type=pl.DeviceIdType.MESH)
            for a in range(self.n) for q in range(self.chips)]

    def start(self, src_refs, out_refs, sems):
        for cp in self._copies(src_refs, out_refs, sems):
            cp.start()

    def wait(self, src_refs, out_refs, sems):
        for cp in self._copies(src_refs, out_refs, sems):
            cp.wait()


class _ChipExchange:
    def __init__(self, srcs):
        self.srcs, self.n = list(srcs), len(srcs)
        self.peers = srcs[0].shape[0] - 1
        self.out_shape = [jax.ShapeDtypeStruct(s.shape, s.dtype) for s in srcs]
        self.specs = [pl.BlockSpec(memory_space=pl.ANY)] * self.n
        self.scratch = [pltpu.SemaphoreType.DMA((self.n, self.peers)), pltpu.SemaphoreType.DMA((self.n, self.peers)),
                        pltpu.SemaphoreType.DMA((self.n,))]

    def _copies(self, src_refs, out_refs, sems):
        x, y, c = lax.axis_index("x"), lax.axis_index("y"), lax.axis_index("c")
        mine = 2 * x + y
        copies = [pltpu.make_async_copy(src_refs[a].at[mine], out_refs[a].at[mine], sems[2].at[a]) for a in range(self.n)]
        for d in range(1, self.peers + 1):
            px, py = (x + (d >> 1)) % 2, (y + (d & 1)) % 2
            for a in range(self.n):
                copies.append(pltpu.make_async_remote_copy(
                    src_ref=src_refs[a].at[2 * px + py], dst_ref=out_refs[a].at[mine], send_sem=sems[0].at[a, d - 1],
                    recv_sem=sems[1].at[a, d - 1], device_id=(px, py, c), device_id_type=pl.DeviceIdType.MESH))
        return copies

    def start(self, src_refs, out_refs, sems):
        for cp in self._copies(src_refs, out_refs, sems):
            cp.start()

    def wait(self, src_refs, out_refs, sems):
        for cp in self._copies(src_refs, out_refs, sems):
            cp.wait()


def _pair_sum(a, b):
    shape = a.shape
    cols = shape[-1]
    rows = a.size // cols
    tr = max(t for t in range(SUBLANES * 2, rows + 1, SUBLANES * 2) if rows % t == 0 and t * cols * 2 <= ADAM_BLOCK_BYTES)

    def body(a_ref, b_ref, o_ref):
        o_ref[...] = (a_ref[...].astype(F32) + b_ref[...].astype(F32)).astype(BF16)

    blk = pl.BlockSpec((tr, cols), lambda i: (i, 0))
    return pl.pallas_call(
        body, name="pair_sum", grid=(rows // tr,), in_specs=[blk, blk], out_specs=blk,
        out_shape=jax.ShapeDtypeStruct((rows, cols), BF16), compiler_params=_params(("arbitrary",)),
    )(a.reshape(rows, cols), b.reshape(rows, cols)).reshape(shape)


def _exchange(name, srcs, gathers):
    ex = _Exchange(srcs, gathers)

    def body(*refs):
        src_refs, out_refs, sems = refs[:ex.n], refs[ex.n:2 * ex.n], refs[2 * ex.n:]
        ex.start(src_refs, out_refs, sems)
        ex.wait(src_refs, out_refs, sems)

    return pl.pallas_call(body, name=name, out_shape=ex.out_shape, in_specs=ex.specs, out_specs=ex.specs,
                          scratch_shapes=ex.scratch)(*srcs)


def _launch(body, name, grid, in_specs, out_specs, out_shape, scratch_shapes, args, ex=None):
    sem = ("arbitrary",) * len(grid)
    if ex is None:
        outs = pl.pallas_call(body, name=name, grid=grid, in_specs=in_specs, out_specs=out_specs, out_shape=out_shape,
                              scratch_shapes=scratch_shapes, compiler_params=_params(sem))(*args)
        return outs, []
    n_in, n_out, n_scr = len(in_specs), len(out_specs), len(scratch_shapes)

    def hosted(*refs):
        ins, ex_in = refs[:n_in], refs[n_in:n_in + ex.n]
        outs = refs[n_in + ex.n:n_in + ex.n + n_out]
        ex_out = refs[n_in + ex.n + n_out:n_in + 2 * ex.n + n_out]
        rest = refs[n_in + 2 * ex.n + n_out:]
        scratch, sems = rest[:n_scr], rest[n_scr:]
        ids = [pl.program_id(a) for a in range(len(grid))]
        first = functools.reduce(lambda p, q: p & q, [i == 0 for i in ids])
        last = functools.reduce(lambda p, q: p & q, [i == g - 1 for i, g in zip(ids, grid)])

        @pl.when(first)
        def _():
            ex.start(ex_in, ex_out, sems)

        if getattr(ex, "early_forward", False):
            @pl.when(functools.reduce(lambda p, q: p & q, [i == (3 * g // 4 if a == 0 else 0)
                                                           for a, (i, g) in enumerate(zip(ids, grid))]))
            def _():
                ex.forward(ex_in, ex_out, sems)

        body(*ins, *outs, *scratch)

        @pl.when(last)
        def _():
            ex.wait(ex_in, ex_out, sems)

    outs = pl.pallas_call(
        hosted, name=name, grid=grid, in_specs=list(in_specs) + ex.specs, out_specs=list(out_specs) + ex.specs,
        out_shape=list(out_shape) + ex.out_shape, scratch_shapes=list(scratch_shapes) + ex.scratch,
        compiler_params=_params(sem))(*args, *ex.srcs)
    return outs[:n_out], outs[n_out:]


def _ada_fwd(c_all, ada_w):
    layers, d, cols = ada_w.shape

    def body(c_ref, w_ref, o_ref):
        c = c_ref[...]
        ca = (c * _sigmoid(c)).astype(BF16)
        for l in range(layers):
            o_ref[l] = _dot(ca, w_ref[l].astype(BF16))

    return pl.pallas_call(
        body, name="ada_fwd", out_shape=jax.ShapeDtypeStruct((layers, N_DEV, cols), F32),
        compiler_params=pltpu.CompilerParams(vmem_limit_bytes=VMEM_LIMIT_BYTES),
    )(c_all, ada_w)


def _ada_bwd(c_all, dmod_cols):
    layers, _, cols = dmod_cols.shape
    d = c_all.shape[1]

    def body(c_ref, g_ref, o_ref):
        c = c_ref[...]
        ca = (c * _sigmoid(c)).astype(BF16)
        for l in range(layers):
            o_ref[l] = _dot_tn(ca, g_ref[l].astype(BF16))

    return pl.pallas_call(
        body, name="ada_bwd", out_shape=jax.ShapeDtypeStruct((layers, d, cols), F32),
        compiler_params=pltpu.CompilerParams(vmem_limit_bytes=VMEM_LIMIT_BYTES),
    )(c_all, dmod_cols)


def _pooled(hbuf, h, t0, g, tm):
    gd = h.shape[1] // len(POOL_WINDOWS)
    cols = slice(g * gd, (g + 1) * gd)
    w = POOL_WINDOWS[g]
    hg = h[:, cols]
    s = hbuf[0:POOL_HALO + tm, cols]
    span = 1
    while span < w:
        s = s + pltpu.roll(s, span, 0)
        span *= 2
    s = s[POOL_HALO:POOL_HALO + tm, :]
    t = t0 + lax.broadcasted_iota(jnp.int32, (tm, 1), 0)
    cnt = jnp.minimum(t + 1, w).astype(F32)
    return s / cnt - hg, cnt


def _pool_fwd(x, pv, wp, scale, *, tm, ex=None):
    t_len, d = x.shape
    n_i = t_len // tm
    gd = d // len(POOL_WINDOWS)

    def body(x_ref, pv_ref, wp_ref, sc_ref, xo_ref, y_ref, hbuf):
        i = pl.program_id(0)

        @pl.when(i == 0)
        def _():
            hbuf[0:POOL_HALO, :] = jnp.zeros((POOL_HALO, d), F32)

        xv, pv_ = x_ref[...], pv_ref[...]
        h, _, _ = _prenorm(xv, pv_)
        hbuf[POOL_HALO:POOL_HALO + tm, :] = h
        for g in range(len(POOL_WINDOWS)):
            pooled, _ = _pooled(hbuf, h, i * tm, g, tm)
            y_ref[:, g * gd:(g + 1) * gd] = _dot(pooled.astype(BF16), wp_ref[g])
        xo_ref[...] = _post(xv, y_ref[...] * sc_ref[...], pv_)
        hbuf[0:POOL_HALO, :] = hbuf[tm:tm + POOL_HALO, :]

    row = pl.BlockSpec((tm, d), lambda i: (i, 0))
    return _launch(
        body, "pool_fwd", (n_i,),
        [row, pl.BlockSpec((SUBLANES, d), lambda i: (0, 0)), pl.BlockSpec(wp.shape, lambda i: (0, 0, 0)),
         pl.BlockSpec((1, d), lambda i: (0, 0))],
        [row, row],
        [jax.ShapeDtypeStruct((t_len, d), F32), jax.ShapeDtypeStruct((t_len, d), F32)],
        [pltpu.VMEM((tm + POOL_HALO, d), F32)],
        (x, pv, wp, scale), ex)


def _pool_bwd(dxo, x, ypre, pv, wp, scale, *, tm, ex=None):
    t_len, d = x.shape
    n_i = t_len // tm
    gd = d // len(POOL_WINDOWS)
    hb = tm // POOL_HALO

    def body(dxo_ref, x_ref, xh_ref, y_ref, pv_ref, wp_ref, sc_ref, dx_ref, pg_ref, dwp_ref, hbuf, qbuf):
        i = pl.program_id(0)
        ti = n_i - 1 - i

        @pl.when(i == 0)
        def _():
            pg_ref[...] = jnp.zeros_like(pg_ref)
            dwp_ref[...] = jnp.zeros_like(dwp_ref)
            qbuf[tm:tm + POOL_HALO, :] = jnp.zeros((POOL_HALO, d), F32)

        xv, pv_, dxo_v, yp, sc = x_ref[...], pv_ref[...], dxo_ref[...], y_ref[...], sc_ref[...]
        dy = _post_bwd(dxo_v, yp * sc, pv_, pg_ref)
        pg_ref[PG_EXTRA:PG_EXTRA + 1, :] += _colsum(dy * yp)
        dys = dy * sc
        h, _, _ = _prenorm(xv, pv_)
        hh, _, _ = _prenorm(xh_ref[...], pv_)
        hbuf[0:POOL_HALO, :] = jnp.where(ti > 0, hh, 0.0)
        hbuf[POOL_HALO:POOL_HALO + tm, :] = h
        for g in range(len(POOL_WINDOWS)):
            cols = slice(g * gd, (g + 1) * gd)
            pooled, cnt = _pooled(hbuf, h, ti * tm, g, tm)
            dyg = dys[:, cols].astype(BF16)
            dwp_ref[g] += _dot_tn(pooled.astype(BF16), dyg)
            dp = _dot_nt(dyg, wp_ref[g])
            qbuf[0:tm, cols] = dp / cnt
            fs = qbuf[0:tm + POOL_HALO, cols]
            span = 1
            while span < POOL_WINDOWS[g]:
                fs = fs + pltpu.roll(fs, tm + POOL_HALO - span, 0)
                span *= 2
            hbuf[POOL_HALO:POOL_HALO + tm, cols] = fs[0:tm, :] - dp
        dx_ref[...] = dxo_v + _pre_bwd(hbuf[POOL_HALO:POOL_HALO + tm, :], xv, pv_, pg_ref)
        qbuf[tm:tm + POOL_HALO, :] = qbuf[0:POOL_HALO, :]

    row = pl.BlockSpec((tm, d), lambda i: (n_i - 1 - i, 0))
    halo = pl.BlockSpec((POOL_HALO, d), lambda i: (jnp.maximum((n_i - 1 - i) * hb - 1, 0), 0))
    small = pl.BlockSpec((SUBLANES, d), lambda i: (0, 0))
    return _launch(
        body, "pool_bwd", (n_i,),
        [row, row, halo, row, small, pl.BlockSpec(wp.shape, lambda i: (0, 0, 0)), pl.BlockSpec((1, d), lambda i: (0, 0))],
        [row, small, pl.BlockSpec(wp.shape, lambda i: (0, 0, 0))],
        [jax.ShapeDtypeStruct((t_len, d), F32), jax.ShapeDtypeStruct((SUBLANES, d), F32),
         jax.ShapeDtypeStruct(wp.shape, F32)],
        [pltpu.VMEM((tm + POOL_HALO, d), F32), pltpu.VMEM((tm + POOL_HALO, d), F32)],
        (dxo, x, x, ypre, pv, wp, scale), ex)


def _ffn_conv(a, prev, w):
    return w[2:3] * a + w[1:2] * _shift_down(a, 1, prev) + w[0:1] * _shift_down(a, 2, prev)


def _ffn_fwd(x, pv, wup, wdw, wdn, *, tm, ex=None):
    t_len, d = x.shape
    _, n_j, fc, _ = wup.shape
    n_i = t_len // tm

    def body(x_ref, pv_ref, wup_ref, wdw_ref, wdn_ref, xo_ref, y_ref, h_ref, a_ref, c_ref, h_s, yacc, carry):
        i, j = pl.program_id(0), pl.program_id(1)

        @pl.when(j == 0)
        def _():
            h, _, _ = _prenorm(x_ref[...], pv_ref[...])
            hb = h.astype(BF16)
            h_s[...] = hb
            h_ref[...] = hb
            yacc[...] = jnp.zeros_like(yacc)

        @pl.when((i == 0) & (j == 0))
        def _():
            carry[...] = jnp.zeros_like(carry)

        hb = h_s[...]
        conv = []
        for s in range(2):
            a = _dot_nt(hb, wup_ref[s, 0])
            a_ref[s, 0] = a.astype(BF16)
            cv = _ffn_conv(a, carry[s, j], wdw_ref[s, 0])
            c_ref[s, 0] = cv.astype(BF16)
            conv.append(cv)
            carry[s, j] = a[tm - FFN_HALO:tm, :]
        g, v = conv
        u = g * _sigmoid(g) * v
        yacc[...] += _dot(u.astype(BF16), wdn_ref[...])

        @pl.when(j == n_j - 1)
        def _():
            y = yacc[...]
            y_ref[...] = y
            xo_ref[...] = _post(x_ref[...], y, pv_ref[...])

    row = pl.BlockSpec((tm, d), lambda i, j: (i, 0))
    return _launch(
        body, "ffn_fwd", (n_i, n_j),
        [row, pl.BlockSpec((SUBLANES, d), lambda i, j: (0, 0)),
         pl.BlockSpec((2, 1, fc, d), lambda i, j: (0, j, 0, 0)),
         pl.BlockSpec((2, 1, SUBLANES, fc), lambda i, j: (0, j, 0, 0)),
         pl.BlockSpec((fc, d), lambda i, j: (j, 0))],
        [row, row, row, pl.BlockSpec((2, 1, tm, fc), lambda i, j: (0, j, i, 0)),
         pl.BlockSpec((2, 1, tm, fc), lambda i, j: (0, j, i, 0))],
        [jax.ShapeDtypeStruct((t_len, d), F32), jax.ShapeDtypeStruct((t_len, d), F32),
         jax.ShapeDtypeStruct((t_len, d), BF16), jax.ShapeDtypeStruct((2, n_j, t_len, fc), BF16),
         jax.ShapeDtypeStruct((2, n_j, t_len, fc), BF16)],
        [pltpu.VMEM((tm, d), BF16), pltpu.VMEM((tm, d), F32), pltpu.VMEM((2, n_j, FFN_HALO, fc), F32)],
        (x, pv, wup, wdw, wdn), ex)


def _ffn_bwd(dy, h, a_sav, c_sav, wup, wdw, wdn, *, tm, ex=None):
    t_len, d = dy.shape
    _, n_j, fc, _ = wup.shape
    n_i = t_len // tm
    assert n_i % 2 == 0

    def body(dyp_ref, hp_ref, a_ref, c_ref, wup_ref, wdw_ref, wdn_ref,
             dhp_ref, gup_ref, gdn_ref, dwdw_ref, da_p, u_p, acc_up, acc_dn, carry):
        j, i = pl.program_id(0), pl.program_id(1)

        @pl.when((j == 0) & (i == 0))
        def _():
            dwdw_ref[...] = jnp.zeros_like(dwdw_ref)

        @pl.when(i == 0)
        def _():
            acc_up[...] = jnp.zeros_like(acc_up)
            acc_dn[...] = jnp.zeros_like(acc_dn)
            carry[...] = jnp.zeros_like(carry)

        def step(half):
            rows = slice(half * tm, (half + 1) * tm)
            du = _dot_nt(dyp_ref[rows, :], wdn_ref[...])
            g, v = c_ref[0, 0].astype(F32), c_ref[1, 0].astype(F32)
            sg = _sigmoid(g)
            sl = g * sg
            u_p[rows, :] = (sl * v).astype(BF16)
            d2 = (du * v * (sg * (1.0 + g * (1.0 - sg))), du * sl)
            dab = []
            for s in range(2):
                w = wdw_ref[s, 0]
                nxt = carry[s]
                p1, p2 = _shift_up(d2[s], 1, nxt), _shift_up(d2[s], 2, nxt)
                carry[s] = d2[s][0:FFN_HALO, :]
                dab.append((w[2:3] * d2[s] + w[1:2] * p1 + w[0:1] * p2).astype(BF16))
                da_p[s, rows, :] = dab[s]
                a_s = a_ref[s, 0].astype(F32)
                for k, sh in ((2, d2[s]), (1, p1), (0, p2)):
                    dwdw_ref[s, j, k:k + 1, :] += _colsum(a_s * sh)
            dhp_ref[0] = (_dot(dab[0], wup_ref[0, 0]) + _dot(dab[1], wup_ref[1, 0])).astype(BF16)
            if half == 0:
                hp = hp_ref[...]
                for s in range(2):
                    acc_up[s] += _dot_tn(da_p[s], hp)
                acc_dn[...] += _dot_tn(u_p[...], dyp_ref[...])

        @pl.when(i % 2 == 0)
        def _():
            step(1)

        @pl.when(i % 2 == 1)
        def _():
            step(0)

        @pl.when(i == n_i - 1)
        def _():
            gup_ref[:, 0] = acc_up[...].astype(BF16)
            gdn_ref[...] = acc_dn[...].astype(BF16)

    chunk = lambda shape: pl.BlockSpec((2, 1) + shape, lambda j, i: (0, j, 0, 0))
    tile = pl.BlockSpec((2, 1, tm, fc), lambda j, i: (0, j, n_i - 1 - i, 0))
    pair = pl.BlockSpec((2 * tm, d), lambda j, i: ((n_i - 1 - i) // 2, 0))
    return _launch(
        body, "ffn_bwd", (n_j, n_i),
        [pair, pair, tile, tile, chunk((fc, d)), chunk((SUBLANES, fc)), pl.BlockSpec((fc, d), lambda j, i: (j, 0))],
        [pl.BlockSpec((1, tm, d), lambda j, i: (j, n_i - 1 - i, 0)), chunk((fc, d)),
         pl.BlockSpec((fc, d), lambda j, i: (j, 0)), pl.BlockSpec((2, n_j, SUBLANES, fc), lambda j, i: (0, 0, 0, 0))],
        [jax.ShapeDtypeStruct((n_j, t_len, d), BF16), jax.ShapeDtypeStruct((2, n_j, fc, d), BF16),
         jax.ShapeDtypeStruct((n_j * fc, d), BF16), jax.ShapeDtypeStruct((2, n_j, SUBLANES, fc), F32)],
        [pltpu.VMEM((2, 2 * tm, fc), BF16), pltpu.VMEM((2 * tm, fc), BF16), pltpu.VMEM((2, fc, d), F32),
         pltpu.VMEM((fc, d), F32), pltpu.VMEM((2, FFN_HALO, fc), F32)],
        (dy, h, a_sav, c_sav, wup, wdw, wdn), ex)


def _sub_post_bwd(dxo, ypre, pv, *, tm, ex=None):
    t_len, d = dxo.shape
    n_i = t_len // tm

    def body(dxo_ref, y_ref, pv_ref, dy_ref, pg_ref):
        @pl.when(pl.program_id(0) == 0)
        def _():
            pg_ref[...] = jnp.zeros_like(pg_ref)

        dy_ref[...] = _post_bwd(dxo_ref[...], y_ref[...], pv_ref[...], pg_ref).astype(BF16)

    row = pl.BlockSpec((tm, d), lambda i: (i, 0))
    small = pl.BlockSpec((SUBLANES, d), lambda i: (0, 0))
    return _launch(body, "sub_post_bwd", (n_i,), [row, row, small], [row, small],
                   [jax.ShapeDtypeStruct((t_len, d), BF16), jax.ShapeDtypeStruct((SUBLANES, d), F32)], [],
                   (dxo, ypre, pv), ex)


def _sub_pre_bwd(dhp, x, dxo, pv, *, tm, ex=None):
    n_p, t_len, d = dhp.shape
    n_i = t_len // tm

    def body(dhp_ref, x_ref, dxo_ref, pv_ref, dx_ref, pg_ref):
        @pl.when(pl.program_id(0) == 0)
        def _():
            pg_ref[...] = jnp.zeros_like(pg_ref)

        dh = dhp_ref[0].astype(F32)
        for p in range(1, n_p):
            dh = dh + dhp_ref[p].astype(F32)
        dx_ref[...] = dxo_ref[...] + _pre_bwd(dh, x_ref[...], pv_ref[...], pg_ref)

    row = pl.BlockSpec((tm, d), lambda i: (i, 0))
    small = pl.BlockSpec((SUBLANES, d), lambda i: (0, 0))
    return _launch(body, "sub_pre_bwd", (n_i,), [pl.BlockSpec((n_p, tm, d), lambda i: (0, i, 0)), row, row, small],
                   [row, small], [jax.ShapeDtypeStruct((t_len, d), F32), jax.ShapeDtypeStruct((SUBLANES, d), F32)], [],
                   (dhp, x, dxo, pv), ex)


CV_B1, CV_BDW, CV_LNG, CV_LNB, CV_B2 = 0, 1, 2, 3, 4


def _taps_by_residue(offs):
    groups = {}
    for k, off in enumerate(offs):
        groups.setdefault(off % SUBLANES, []).append((k, off // SUBLANES))
    return sorted(groups.items())


def _depthwise(buf, w_ref, out_ref, offs, tm, d):
    taps_of = _taps_by_residue(offs)

    def chunk(r, carry):
        r0 = pl.multiple_of(r * DW_ROWS, DW_ROWS)
        for cb in range(d // LANES):
            cols = slice(cb * LANES, (cb + 1) * LANES)
            win = buf[pl.ds(r0, DW_ROWS + CONV_HALO), cols]
            acc = jnp.zeros((DW_ROWS, LANES), F32)
            for b, taps in taps_of:
                wb = win if b == 0 else pltpu.roll(win, DW_ROWS + CONV_HALO - b, 0)
                for k, a in taps:
                    acc = acc + wb[SUBLANES * a:SUBLANES * a + DW_ROWS, :] * w_ref[k:k + 1, cols]
            out_ref[pl.ds(r0, DW_ROWS), cols] = acc
        return carry

    lax.fori_loop(0, tm // DW_ROWS, chunk, 0)


def _depthwise_wgrad(dbuf, ubuf, dw_ref, tm, d):
    taps_of = _taps_by_residue(tuple(2 + k for k in range(CONV_TAPS)))
    for cb in range(d // LANES):
        cols = slice(cb * LANES, (cb + 1) * LANES)

        def chunk(r, acc):
            r0 = pl.multiple_of(r * CONV_ROWS, CONV_ROWS)
            dv = dbuf[pl.ds(r0, CONV_ROWS), cols]
            win = ubuf[pl.ds(r0, CONV_ROWS + CONV_HALO), cols]
            new = list(acc)
            for b, taps in taps_of:
                wb = win if b == 0 else pltpu.roll(win, CONV_ROWS + CONV_HALO - b, 0)
                for k, a in taps:
                    for q in range(CONV_ROWS // SUBLANES):
                        new[k] = new[k] + dv[SUBLANES * q:SUBLANES * (q + 1), :] * wb[SUBLANES * (a + q):SUBLANES * (a + q + 1), :]
            return tuple(new)

        acc = lax.fori_loop(0, tm // CONV_ROWS, chunk, tuple(jnp.zeros((SUBLANES, LANES), F32) for _ in range(CONV_TAPS)))
        for k in range(CONV_TAPS):
            dw_ref[k:k + 1, cols] += _colsum(acc[k])


def _layer_norm_parts(c1):
    mu = jnp.mean(c1, axis=-1, keepdims=True)
    cen = c1 - mu
    rstd = lax.rsqrt(jnp.mean(cen * cen, axis=-1, keepdims=True) + NORM_EPS)
    return cen * rstd, rstd


def _conv_fwd(x, pv, w1, w2, wdw, vec, *, tm):
    t_len, d = x.shape
    n_i = t_len // tm
    n_q = w1.shape[0] // 2
    qc = w1.shape[2]

    def body(x_ref, pv_ref, w1_ref, w2_ref, wdw_ref, vec_ref, xo_ref, y_ref, a_ref, c1_ref, ubuf):
        i = pl.program_id(0)

        @pl.when(i == 0)
        def _():
            ubuf[0:CONV_HALO, :] = jnp.zeros((CONV_HALO, d), F32)

        xv, pv_ = x_ref[...], pv_ref[...]
        h, _, _ = _prenorm(xv, pv_)
        hb = h.astype(BF16)
        for q in range(n_q):
            cols = slice(q * qc, (q + 1) * qc)
            gcols = slice(d + q * qc, d + (q + 1) * qc)
            val = (_dot(hb, w1_ref[q]) + vec_ref[CV_B1:CV_B1 + 1, cols]).astype(BF16)
            gate = (_dot(hb, w1_ref[n_q + q]) + vec_ref[CV_B1:CV_B1 + 1, gcols]).astype(BF16)
            a_ref[:, cols] = val
            a_ref[:, gcols] = gate
            ubuf[CONV_HALO:CONV_HALO + tm, cols] = val.astype(F32) * _sigmoid(gate.astype(F32))
        _depthwise(ubuf, wdw_ref, c1_ref, tuple(2 + k for k in range(CONV_TAPS)), tm, d)
        c1 = c1_ref[...] + vec_ref[CV_BDW:CV_BDW + 1, 0:d]
        c1_ref[...] = c1
        xhat, _ = _layer_norm_parts(c1)
        ln = xhat * vec_ref[CV_LNG:CV_LNG + 1, 0:d] + vec_ref[CV_LNB:CV_LNB + 1, 0:d]
        s = ln * _sigmoid(ln)
        y = _dot(s.astype(BF16), w2_ref[...]) + vec_ref[CV_B2:CV_B2 + 1, 0:d]
        y_ref[...] = y
        xo_ref[...] = _post(xv, y, pv_)
        ubuf[0:CONV_HALO, :] = ubuf[tm:tm + CONV_HALO, :]

    row = pl.BlockSpec((tm, d), lambda i: (i, 0))
    return pl.pallas_call(
        body, name="conv_fwd", grid=(n_i,),
        in_specs=[row, pl.BlockSpec((SUBLANES, d), lambda i: (0, 0)), pl.BlockSpec(w1.shape, lambda i: (0, 0, 0)),
                  pl.BlockSpec(w2.shape, lambda i: (0, 0)), pl.BlockSpec(wdw.shape, lambda i: (0, 0)),
                  pl.BlockSpec(vec.shape, lambda i: (0, 0))],
        out_specs=[row, row, pl.BlockSpec((tm, 2 * d), lambda i: (i, 0)), row],
        out_shape=[jax.ShapeDtypeStruct((t_len, d), F32), jax.ShapeDtypeStruct((t_len, d), F32),
                   jax.ShapeDtypeStruct((t_len, 2 * d), BF16), jax.ShapeDtypeStruct((t_len, d), F32)],
        scratch_shapes=[pltpu.VMEM((tm + CONV_HALO, d), F32)],
        compiler_params=_params(("arbitrary",)),
    )(x, pv, w1, w2, wdw, vec)


def _conv_bwd(dxo, x, ypre, a_sav, c1_sav, pv, w1, w2, wdw, vec, *, tm, ex=None):
    t_len, d = x.shape
    n_i = t_len // tm
    n_q = w1.shape[0] // 2
    qc = w1.shape[2]
    hb_ = tm // CONV_HALO

    def body(dxo_ref, x_ref, y_ref, a_ref, ah_ref, c1_ref, pv_ref, w1_ref, w2_ref, wdw_ref, vec_ref,
             dx_ref, pg_ref, gw1_ref, gw2_ref, gvec_ref, gwdw_ref, ubuf, dcbuf, dubuf, acc1, acc2):
        i = pl.program_id(0)
        ti = n_i - 1 - i

        @pl.when(i == 0)
        def _():
            pg_ref[...] = jnp.zeros_like(pg_ref)
            gvec_ref[...] = jnp.zeros_like(gvec_ref)
            gwdw_ref[...] = jnp.zeros_like(gwdw_ref)
            acc1[...] = jnp.zeros_like(acc1)
            acc2[...] = jnp.zeros_like(acc2)
            dcbuf[tm:tm + CONV_HALO, :] = jnp.zeros((CONV_HALO, d), F32)

        xv, pv_, dxo_v = x_ref[...], pv_ref[...], dxo_ref[...]
        dy = _post_bwd(dxo_v, y_ref[...], pv_, pg_ref)
        gvec_ref[CV_B2:CV_B2 + 1, 0:d] += _colsum(dy)
        dyb = dy.astype(BF16)
        xhat, rstd = _layer_norm_parts(c1_ref[...])
        lng = vec_ref[CV_LNG:CV_LNG + 1, 0:d]
        ln = xhat * lng + vec_ref[CV_LNB:CV_LNB + 1, 0:d]
        sg = _sigmoid(ln)
        acc2[...] += _dot_tn((ln * sg).astype(BF16), dyb)
        dln = _dot_nt(dyb, w2_ref[...]) * (sg * (1.0 + ln * (1.0 - sg)))
        gvec_ref[CV_LNG:CV_LNG + 1, 0:d] += _colsum(dln * xhat)
        gvec_ref[CV_LNB:CV_LNB + 1, 0:d] += _colsum(dln)
        dxh = dln * lng
        dc1 = rstd * (dxh - jnp.mean(dxh, axis=-1, keepdims=True)
                      - xhat * jnp.mean(dxh * xhat, axis=-1, keepdims=True))
        gvec_ref[CV_BDW:CV_BDW + 1, 0:d] += _colsum(dc1)
        dcbuf[0:tm, :] = dc1
        for q in range(n_q):
            cols = slice(q * qc, (q + 1) * qc)
            gcols = slice(d + q * qc, d + (q + 1) * qc)
            ubuf[CONV_HALO:CONV_HALO + tm, cols] = a_ref[:, cols].astype(F32) * _sigmoid(a_ref[:, gcols].astype(F32))
            uh = ah_ref[:, cols].astype(F32) * _sigmoid(ah_ref[:, gcols].astype(F32))
            ubuf[0:CONV_HALO, cols] = jnp.where(ti > 0, uh, 0.0)
        _depthwise_wgrad(dcbuf, ubuf, gwdw_ref, tm, d)
        _depthwise(dcbuf, wdw_ref, dubuf, tuple(CONV_TAPS - 1 - k for k in range(CONV_TAPS)), tm, d)
        dcbuf[tm:tm + CONV_HALO, :] = dcbuf[0:CONV_HALO, :]
        h, _, _ = _prenorm(xv, pv_)
        hb = h.astype(BF16)
        dh = jnp.zeros((tm, d), F32)
        for q in range(n_q):
            cols = slice(q * qc, (q + 1) * qc)
            gcols = slice(d + q * qc, d + (q + 1) * qc)
            du = dubuf[:, cols]
            val, gate = a_ref[:, cols].astype(F32), a_ref[:, gcols].astype(F32)
            sgg = _sigmoid(gate)
            dval = du * sgg
            dgate = du * val * (sgg * (1.0 - sgg))
            gvec_ref[CV_B1:CV_B1 + 1, cols] += _colsum(dval)
            gvec_ref[CV_B1:CV_B1 + 1, gcols] += _colsum(dgate)
            dvb, dgb = dval.astype(BF16), dgate.astype(BF16)
            acc1[q] += _dot_tn(hb, dvb)
            acc1[n_q + q] += _dot_tn(hb, dgb)
            dh = dh + _dot_nt(dvb, w1_ref[q]) + _dot_nt(dgb, w1_ref[n_q + q])
        dx_ref[...] = dxo_v + _pre_bwd(dh, xv, pv_, pg_ref)

        @pl.when(i == n_i - 1)
        def _():
            gw1_ref[...] = acc1[...].astype(BF16)
            gw2_ref[...] = acc2[...].astype(BF16)

    row = pl.BlockSpec((tm, d), lambda i: (n_i - 1 - i, 0))
    small = pl.BlockSpec((SUBLANES, d), lambda i: (0, 0))
    whole2 = lambda shape: pl.BlockSpec(shape, lambda i: (0, 0))
    return _launch(
        body, "conv_bwd", (n_i,),
        [row, row, row,
         pl.BlockSpec((tm, 2 * d), lambda i: (n_i - 1 - i, 0)),
         pl.BlockSpec((CONV_HALO, 2 * d), lambda i: (jnp.maximum((n_i - 1 - i) * hb_ - 1, 0), 0)),
         row, small, pl.BlockSpec(w1.shape, lambda i: (0, 0, 0)), whole2(w2.shape), whole2(wdw.shape),
         whole2(vec.shape)],
        [row, small, pl.BlockSpec(w1.shape, lambda i: (0, 0, 0)), whole2(w2.shape), whole2(vec.shape),
         whole2(wdw.shape)],
        [jax.ShapeDtypeStruct((t_len, d), F32), jax.ShapeDtypeStruct((SUBLANES, d), F32),
         jax.ShapeDtypeStruct(w1.shape, BF16), jax.ShapeDtypeStruct(w2.shape, BF16),
         jax.ShapeDtypeStruct(vec.shape, F32), jax.ShapeDtypeStruct(wdw.shape, F32)],
        [pltpu.VMEM((tm + CONV_HALO, d), F32), pltpu.VMEM((tm + CONV_HALO, d), F32),
         pltpu.VMEM((tm, d), F32), pltpu.VMEM(w1.shape, F32), pltpu.VMEM(w2.shape, F32)],
        (dxo, x, ypre, a_sav, a_sav, c1_sav, pv, w1, w2, wdw, vec), ex)


def _loss_head(xo, target, ypre, pv, *, tm):
    t_len, d = xo.shape
    n_i = t_len // tm

    def body(xo_ref, t_ref, y_ref, pv_ref, dxo_ref, sq_ref, dy_ref, pg_ref):
        @pl.when(pl.program_id(0) == 0)
        def _():
            sq_ref[...] = jnp.zeros_like(sq_ref)
            pg_ref[...] = jnp.zeros_like(pg_ref)

        err = xo_ref[...] - t_ref[...]
        dxo = err * (1.0 / d)
        dxo_ref[...] = dxo
        sq_ref[...] += jnp.sum((err * err).reshape(tm // SUBLANES, SUBLANES, d), axis=0)
        dy_ref[...] = _post_bwd(dxo, y_ref[...], pv_ref[...], pg_ref).astype(BF16)

    row = pl.BlockSpec((tm, d), lambda i: (i, 0))
    small = pl.BlockSpec((SUBLANES, d), lambda i: (0, 0))
    return pl.pallas_call(
        body, name="loss_head", grid=(n_i,), in_specs=[row, row, row, small],
        out_specs=[row, small, row, small],
        out_shape=[jax.ShapeDtypeStruct((t_len, d), F32), jax.ShapeDtypeStruct((SUBLANES, d), F32),
                   jax.ShapeDtypeStruct((t_len, d), BF16), jax.ShapeDtypeStruct((SUBLANES, d), F32)],
        compiler_params=_params(("arbitrary",)),
    )(xo, target, ypre, pv)


def _adam(name, parts, w, m, v):
    layers = len(parts)
    _, rows, cols = parts[0].shape
    tr = rows
    if rows % SUBLANES == 0:
        cap = max(SUBLANES, ADAM_BLOCK_BYTES // (4 * cols))
        tr = max(t for t in range(SUBLANES, rows + 1, SUBLANES) if rows % t == 0 and (t <= cap or t == SUBLANES))
    c1 = 1.0 / (1.0 - ADAM_B1 ** ADAM_STEP)
    c2 = 1.0 / (1.0 - ADAM_B2 ** ADAM_STEP)

    def body(*refs):
        p_refs = refs[:layers]
        w_ref, m_ref, v_ref, g_ref, d_ref, mo_ref, vo_ref, g_s = refs[layers:]
        for l in range(layers):
            @pl.when(pl.program_id(0) == l)
            def _():
                g = p_refs[l][0].astype(F32)
                for k in range(1, parts[l].shape[0]):
                    g = g + p_refs[l][k].astype(F32)
                g_s[...] = g

        g = g_s[...]
        m2 = ADAM_B1 * m_ref[0] + (1.0 - ADAM_B1) * g
        v2 = ADAM_B2 * v_ref[0] + (1.0 - ADAM_B2) * (g * g)
        g_ref[0] = g
        mo_ref[0] = m2
        vo_ref[0] = v2
        d_ref[0] = -ADAM_LR * ((m2 * c1) / (jnp.sqrt(v2 * c2) + ADAM_EPS) + ADAM_WD * w_ref[0])

    def part_spec(l):
        return pl.BlockSpec((parts[l].shape[0], tr, cols), lambda ll, i: (0, jnp.where(ll == l, i, 0), 0))

    blk = pl.BlockSpec((1, tr, cols), lambda ll, i: (ll, i, 0))
    out = jax.ShapeDtypeStruct((layers, rows, cols), F32)
    return pl.pallas_call(
        body, name=name, grid=(layers, rows // tr),
        in_specs=[part_spec(l) for l in range(layers)] + [blk, blk, blk],
        out_specs=[blk, blk, blk, blk], out_shape=[out, out, out, out],
        scratch_shapes=[pltpu.VMEM((tr, cols), F32)],
        compiler_params=_params(("arbitrary", "arbitrary")),
    )(*parts, w, m, v)


def _adam_nd(name, parts, w, m, v):
    shape = w.shape
    cols = shape[-1]
    if isinstance(parts, (list, tuple)):
        layers = len(parts)
    else:
        layers, parts = 1, [parts]
    rows = w.size // (cols * layers)
    flat = lambda t: t.reshape(layers, rows, cols)
    outs = _adam(name, [p.reshape(p.shape[0], rows, cols) for p in parts], flat(w), flat(m), flat(v))
    return [o.reshape(shape) for o in outs]


def _small_pack(parts, size):
    flat = jnp.concatenate([p.reshape(-1) for p in parts])
    return jnp.pad(flat, (0, size - flat.shape[0]))


def _to_shards(full, axis):
    shp = full.shape
    split = full.reshape(shp[:axis] + (N_DEV, shp[axis] // N_DEV) + shp[axis + 1:])
    return jnp.moveaxis(split, axis, 0)


def _from_shards(sh, axis):
    moved = jnp.moveaxis(sh, 0, axis)
    shp = moved.shape
    return moved.reshape(shp[:axis] + (shp[axis] * shp[axis + 1],) + shp[axis + 2:])


def kernel(x, c, ada_w, ada_b, pre_g, post_g, pool_w, pool_scale, cv_w_pw1, cv_b_pw1, cv_w_dw, cv_b_dw, cv_ln_g, cv_ln_b, cv_w_pw2, cv_b_pw2, ffn_w_up, ffn_w_dw, ffn_w_down, loss_target, m_ada_w, m_ada_b, m_pre_g, m_post_g, m_pool_w, m_pool_scale, m_cv_w_pw1, m_cv_b_pw1, m_cv_w_dw, m_cv_b_dw, m_cv_ln_g, m_cv_ln_b, m_cv_w_pw2, m_cv_b_pw2, m_ffn_w_up, m_ffn_w_dw, m_ffn_w_down, v_ada_w, v_ada_b, v_pre_g, v_post_g, v_pool_w, v_pool_scale, v_cv_w_pw1, v_cv_b_pw1, v_cv_w_dw, v_cv_b_dw, v_cv_ln_g, v_cv_ln_b, v_cv_w_pw2, v_cv_b_pw2, v_ffn_w_up, v_ffn_w_dw, v_ffn_w_down):
    t_len, d = x.shape[1], x.shape[2]
    depth = ada_w.shape[0]
    fc = ffn_w_up.shape[2]
    n_j = N_DEV // 2
    me = 4 * lax.axis_index("x") + 2 * lax.axis_index("y") + lax.axis_index("c")

    small_w = [pre_g, post_g, cv_b_pw1, cv_w_dw, cv_b_dw, cv_ln_g, cv_ln_b, cv_b_pw2, ffn_w_dw]
    small_m = [m_pre_g, m_post_g, m_cv_b_pw1, m_cv_w_dw, m_cv_b_dw, m_cv_ln_g, m_cv_ln_b, m_cv_b_pw2, m_ffn_w_dw]
    small_v = [v_pre_g, v_post_g, v_cv_b_pw1, v_cv_w_dw, v_cv_b_dw, v_cv_ln_g, v_cv_ln_b, v_cv_b_pw2, v_ffn_w_dw]
    sizes = [p.size for p in small_w]
    offs = [sum(sizes[:k]) for k in range(len(sizes) + 1)]
    pack = -(-offs[-1] // (SUBLANES * LANES)) * SUBLANES * LANES

    got = _exchange("gather_small", [c, _small_pack(small_w, pack), pool_w[0].astype(BF16)], [True] * 3)
    c_all = got[0].reshape(N_DEV, d)
    smalls = [got[1][:, offs[k]:offs[k + 1]].reshape((N_DEV,) + small_w[k].shape) for k in range(len(small_w))]
    pre_g_f, post_g_f = _from_shards(smalls[0], 2), _from_shards(smalls[1], 2)
    b1_f = _from_shards(smalls[2], 1)[0]
    cvw_f = jnp.pad(_from_shards(smalls[3], 2)[0], ((0, CONV_HALO - CONV_TAPS), (0, 0)))
    bdw_f, lng_f, lnb_f, b2_f = [_from_shards(smalls[k], 1)[0] for k in (4, 5, 6, 7)]
    fdw = jnp.pad(smalls[8], ((0, 0), (0, 0), (0, SUBLANES - FFN_TAPS), (0, 0)))
    wp = jnp.swapaxes(got[2], 0, 1).reshape(pool_w.shape[1], -1, pool_w.shape[3])
    wdw = [fdw[:, l].reshape(2, n_j, SUBLANES, fc) for l in range(depth)]
    cvec = jnp.zeros((SUBLANES, 2 * d), F32)
    cvec = cvec.at[CV_B1].set(b1_f)
    for r, vrow in ((CV_BDW, bdw_f), (CV_LNG, lng_f), (CV_LNB, lnb_f), (CV_B2, b2_f)):
        cvec = cvec.at[r, :d].set(vrow)

    mod_cols = _ada_fwd(c_all, ada_w)
    (mod_all,) = _exchange("gather_mod", [mod_cols], [True])
    mod = lax.dynamic_index_in_dim(mod_all, me, axis=2, keepdims=False)
    mod = jnp.swapaxes(mod, 0, 1).reshape(depth, N_MOD, d) + ada_b.reshape(depth, N_MOD, d)

    def pv_of(l, s):
        rows = [pre_g_f[l, s], 1.0 + mod[l, 3 * s + 1], mod[l, 3 * s], post_g_f[l, s], mod[l, 3 * s + 2]]
        return jnp.concatenate([jnp.stack(rows), jnp.zeros((SUBLANES - len(rows), d), F32)])

    x0 = x[0]
    pv00, pv01, pv10, pv11 = pv_of(0, 0), pv_of(0, 1), pv_of(1, 0), pv_of(1, 1)
    tm_pool, tm_ffn, tm_bwd, tm_conv = min(TM_POOL, t_len), min(TM_FFN, t_len), min(TM_FFN_BWD, t_len), min(TM_CONV, t_len)
    tm_w = min(TM_FFN_W, t_len)
    wup_t = jnp.swapaxes(ffn_w_up, 1, 2)
    ex = _TwoLevelGather([wup_t[0].astype(BF16), ffn_w_down[0].astype(BF16)], early_forward=False)
    (x1, y0), (wup0, wdn0) = _pool_fwd(x0, pv00, wp, pool_scale, tm=tm_pool, ex=ex)
    ex = _TwoLevelGather([cv_w_pw1[0].astype(BF16), cv_w_pw2[0].astype(BF16), wup_t[1].astype(BF16),
                          ffn_w_down[1].astype(BF16)], early_forward=True)
    (x2, y1, h1, a1, c1), (w1, w2, wup1, wdn1) = _ffn_fwd(
        x1, pv01, wup0.reshape(2, n_j, fc, d), wdw[0], wdn0.reshape(n_j * fc, d), tm=tm_ffn, ex=ex)
    w2 = w2.reshape(d, d)
    wup = [wup0.reshape(2, n_j, fc, d), wup1.reshape(2, n_j, fc, d)]
    wdn = [wdn0.reshape(n_j * fc, d), wdn1.reshape(n_j * fc, d)]
    x3, y2, a2, c2 = _conv_fwd(x2, pv10, w1, w2, cvw_f, cvec, tm=tm_conv)
    (x4, y3, h3, a3, c3), _ = _ffn_fwd(x3, pv11, wup[1], wdw[1], wdn[1], tm=tm_ffn)
    dx4, sq, dy3, pgq11 = _loss_head(x4, loss_target[0], y3, pv11, tm=tm_ffn)
    loss = lax.psum(jnp.sum(sq) * (0.5 / d), MESH_AXES)

    (dhp3, gup1, gdn1, gfdw1), _ = _ffn_bwd(dy3, h3, a3, c3, wup[1], wdw[1], wdn[1], tm=tm_bwd)
    (dx3, pgp11), _ = _sub_pre_bwd(dhp3, x3, dx4, pv11, tm=tm_ffn)
    ex = _Exchange([gup1.reshape(N_DEV, fc, d), gdn1.reshape(N_DEV, -1, d)], [False, False])
    (dx2, pg10, gw1, gw2, gcvec, gcvw), (rup1, rdn1) = _conv_bwd(
        dx3, x2, y2, a2, c2, pv10, w1, w2, cvw_f, cvec, tm=tm_conv, ex=ex)
    (dy1, pgq01), _ = _sub_post_bwd(dx2, y1, pv01, tm=tm_w)
    ex = _Exchange([gw1, gw2.reshape(N_DEV, -1, d)], [False, False])
    (dhp1, gup0, gdn0, gfdw0), (rw1, rw2) = _ffn_bwd(dy1, h1, a1, c1, wup[0], wdw[0], wdn[0], tm=tm_bwd, ex=ex)
    n_chip = N_DEV // 2
    my_core = lax.axis_index("c")
    pairs = [gup0.reshape(n_chip, 2, fc, d), gdn0.reshape(n_chip, 2, -1, d)]
    (dx1, pgp01), theirs = _sub_pre_bwd(dhp1, x1, dx2, pv01, tm=tm_ffn, ex=_SiblingSwap(pairs))
    sums = [_pair_sum(lax.dynamic_index_in_dim(p, my_core, axis=1, keepdims=False), t) for p, t in zip(pairs, theirs)]
    (dx0, pg00, gwp), (rup0, rdn0) = _pool_bwd(dx1, x0, y0, pv00, wp, pool_scale, tm=tm_pool, ex=_ChipExchange(sums))
    pg01, pg11 = pgq01 + pgp01, pgq11 + pgp11

    pgs = [[pg00, pg01], [pg10, pg11]]
    g_pre = jnp.stack([jnp.stack([pgs[l][s][PG_GPRE] for s in range(2)]) for l in range(depth)])
    g_post = jnp.stack([jnp.stack([pgs[l][s][PG_GPOST] for s in range(2)]) for l in range(depth)])
    dmod = jnp.stack([jnp.concatenate([pgs[l][s][r] for s in range(2) for r in (PG_SH, PG_SC, PG_GT)])
                      for l in range(depth)])
    gfdw = jnp.stack([g.reshape(N_DEV, SUBLANES, fc)[:, :FFN_TAPS] for g in (gfdw0, gfdw1)], axis=1)
    small_g = [_to_shards(g_pre, 2), _to_shards(g_post, 2), _to_shards(gcvec[CV_B1][None], 1),
               _to_shards(gcvw[None, :CONV_TAPS], 2), _to_shards(gcvec[CV_BDW, :d][None], 1),
               _to_shards(gcvec[CV_LNG, :d][None], 1), _to_shards(gcvec[CV_LNB, :d][None], 1),
               _to_shards(gcvec[CV_B2, :d][None], 1), gfdw]
    small_send = jnp.concatenate([g.reshape(N_DEV, -1) for g in small_g], axis=1)
    small_send = jnp.pad(small_send, ((0, 0), (0, pack - small_send.shape[1])))
    gwp_send = jnp.swapaxes(gwp.reshape(gwp.shape[0], N_DEV, -1, gwp.shape[2]), 0, 1)
    rsmall, rwp, rmod, rscale = _exchange("scatter_small", [small_send, gwp_send, dmod, pg00[PG_EXTRA][None]],
                                          [False, False, True, True])

    outs = {}

    def put(name, res, shape=None):
        outs[name] = [r if shape is None else r.reshape(shape) for r in res]

    small_res = _adam_nd("adam_small", rsmall.reshape(N_DEV, -1, SUBLANES * LANES),
                         _small_pack(small_w, pack).reshape(-1, SUBLANES * LANES),
                         _small_pack(small_m, pack).reshape(-1, SUBLANES * LANES),
                         _small_pack(small_v, pack).reshape(-1, SUBLANES * LANES))
    small_names = ["pre_g", "post_g", "cv_b_pw1", "cv_w_dw", "cv_b_dw", "cv_ln_g", "cv_ln_b", "cv_b_pw2", "ffn_w_dw"]
    for k, nm in enumerate(small_names):
        outs[nm] = [r.reshape(-1)[offs[k]:offs[k + 1]].reshape(small_w[k].shape) for r in small_res]
    put("pool_w", _adam_nd("adam_pool_w", rwp[:, None], pool_w, m_pool_w, v_pool_w))
    put("cv_w_pw1", _adam_nd("adam_cv_w_pw1", rw1[:, None], cv_w_pw1, m_cv_w_pw1, v_cv_w_pw1))
    put("cv_w_pw2", _adam_nd("adam_cv_w_pw2", rw2[:, None], cv_w_pw2, m_cv_w_pw2, v_cv_w_pw2))
    outs["ffn_w_up"] = [jnp.swapaxes(r, 1, 2) for r in _adam_nd(
        "adam_ffn_w_up", [rup0, rup1], wup_t, jnp.swapaxes(m_ffn_w_up, 1, 2), jnp.swapaxes(v_ffn_w_up, 1, 2))]
    put("ffn_w_down", _adam_nd("adam_ffn_w_down", [rdn0, rdn1], ffn_w_down, m_ffn_w_down, v_ffn_w_down))
    put("ada_b", _adam_nd("adam_ada_b", rmod, ada_b, m_ada_b, v_ada_b))
    put("pool_scale", _adam_nd("adam_pool_scale", rscale, pool_scale, m_pool_scale, v_pool_scale))
    cols = ada_w.shape[2]
    dmod_cols = jnp.swapaxes(lax.dynamic_slice_in_dim(rmod, me * cols, cols, axis=2), 0, 1)
    put("ada_w", _adam_nd("adam_ada_w", _ada_bwd(c_all, dmod_cols)[None], ada_w, m_ada_w, v_ada_w))

    order = ["ada_w", "ada_b", "pre_g", "post_g", "pool_w", "pool_scale", "cv_w_pw1", "cv_b_pw1", "cv_w_dw", "cv_b_dw",
             "cv_ln_g", "cv_ln_b", "cv_w_pw2", "cv_b_pw2", "ffn_w_up", "ffn_w_dw", "ffn_w_down"]
    return (loss, dx0[None], *[outs[nm][0] for nm in order], *[outs[nm][1] for nm in order],
            *[outs[nm][2] for nm in order], *[outs[nm][3] for nm in order])
```

```python
import functools

import jax
import jax.numpy as jnp
from jax import lax
from jax.experimental import pallas as pl
from jax.experimental.pallas import tpu as pltpu

F32, BF16 = jnp.float32, jnp.bfloat16
MESH_AXES = ("x", "y", "c")
N_DEV = 8
NORM_EPS = 1e-6
ADAM_LR, ADAM_B1, ADAM_B2, ADAM_EPS, ADAM_WD, ADAM_STEP = 0.001, 0.9, 0.999, 1e-08, 0.01, 10
POOL_WINDOWS = (2, 4, 8, 16)
CONV_TAPS = 31
FFN_TAPS = 3
N_MOD = 6

SUBLANES = 8
LANES = 128
VMEM_LIMIT_BYTES = 56 * 1024 * 1024
POOL_HALO = 16
CONV_HALO = 32
FFN_HALO = 8
TM_POOL, TM_FFN, TM_FFN_BWD, TM_FFN_W, TM_CONV = 512, 512, 512, 1024, 256
CONV_ROWS = 32
DW_ROWS = 64
ADAM_BLOCK_BYTES = 1024 * 1024

PV_GPRE, PV_SC1, PV_SH, PV_GPOST, PV_GT = 0, 1, 2, 3, 4
PG_GPRE, PG_SC, PG_SH, PG_GPOST, PG_GT, PG_EXTRA = 0, 1, 2, 3, 4, 5


def _params(sem):
    return pltpu.CompilerParams(dimension_semantics=sem, vmem_limit_bytes=VMEM_LIMIT_BYTES)


def _dot(a, b):
    return jnp.dot(a, b, preferred_element_type=F32)


def _dot_nt(a, b):
    return lax.dot_general(a, b, (((1,), (1,)), ((), ())), preferred_element_type=F32)


def _dot_tn(a, b):
    return lax.dot_general(a, b, (((0,), (0,)), ((), ())), preferred_element_type=F32)


def _sigmoid(x):
    return 1.0 / (1.0 + jnp.exp(-x))


def _rms(x):
    return lax.rsqrt(jnp.mean(x * x, axis=-1, keepdims=True) + NORM_EPS)


def _colsum(v):
    return jnp.sum(v, axis=0, keepdims=True)


def _prenorm(x, pv):
    r = _rms(x)
    xn = x * r
    return xn * (pv[PV_GPRE:PV_GPRE + 1] * pv[PV_SC1:PV_SC1 + 1]) + pv[PV_SH:PV_SH + 1], xn, r


def _post(x, y, pv):
    return x + (pv[PV_GT:PV_GT + 1] * pv[PV_GPOST:PV_GPOST + 1]) * (y * _rms(y))


def _post_bwd(dxo, y, pv, pg_ref):
    ry = _rms(y)
    yn = y * ry
    gt, gpost = pv[PV_GT:PV_GT + 1], pv[PV_GPOST:PV_GPOST + 1]
    dyn = dxo * (gt * gpost)
    dy = ry * (dyn - yn * jnp.mean(dyn * yn, axis=-1, keepdims=True))
    s = _colsum(dxo * yn)
    pg_ref[PG_GPOST:PG_GPOST + 1, :] += s * gt
    pg_ref[PG_GT:PG_GT + 1, :] += s * gpost
    return dy


def _pre_bwd(dh, x, pv, pg_ref):
    r = _rms(x)
    xn = x * r
    gpre, sc1 = pv[PV_GPRE:PV_GPRE + 1], pv[PV_SC1:PV_SC1 + 1]
    dxn = dh * (gpre * sc1)
    dx = r * (dxn - xn * jnp.mean(dxn * xn, axis=-1, keepdims=True))
    p = _colsum(dh * xn)
    pg_ref[PG_GPRE:PG_GPRE + 1, :] += p * sc1
    pg_ref[PG_SC:PG_SC + 1, :] += p * gpre
    pg_ref[PG_SH:PG_SH + 1, :] += _colsum(dh)
    return dx


def _shift_down(a, k, prev):
    out = pltpu.roll(a, k, 0)
    row = lax.broadcasted_iota(jnp.int32, a.shape, 0)
    for q in range(k):
        out = jnp.where(row == q, prev[SUBLANES - k + q:SUBLANES - k + q + 1, :], out)
    return out


def _shift_up(a, k, nxt):
    rows = a.shape[0]
    out = pltpu.roll(a, rows - k, 0)
    row = lax.broadcasted_iota(jnp.int32, a.shape, 0)
    for q in range(k):
        out = jnp.where(row == rows - k + q, nxt[q:q + 1, :], out)
    return out


class _Exchange:
    def __init__(self, srcs, gathers):
        self.srcs, self.gathers, self.n = list(srcs), list(gathers), len(srcs)
        self.out_shape = [jax.ShapeDtypeStruct(((N_DEV,) + s.shape) if g else s.shape, s.dtype)
                          for s, g in zip(srcs, gathers)]
        self.specs = [pl.BlockSpec(memory_space=pl.ANY)] * self.n
        self.scratch = [pltpu.SemaphoreType.DMA((self.n, N_DEV - 1)), pltpu.SemaphoreType.DMA((self.n, N_DEV - 1)),
                        pltpu.SemaphoreType.DMA((self.n,))]

    def _copies(self, src_refs, out_refs, sems):
        send_sems, recv_sems, local_sems = sems
        x, y, c = lax.axis_index("x"), lax.axis_index("y"), lax.axis_index("c")
        me = 4 * x + 2 * y + c
        copies = []
        for a in range(self.n):
            mine = src_refs[a] if self.gathers[a] else src_refs[a].at[me]
            copies.append(pltpu.make_async_copy(mine, out_refs[a].at[me], local_sems.at[a]))
        for d in range(1, N_DEV):
            px, py, pc = (x + (d >> 2)) % 2, (y + ((d >> 1) & 1)) % 2, (c + (d & 1)) % 2
            peer = 4 * px + 2 * py + pc
            for a in range(self.n):
                src = src_refs[a] if self.gathers[a] else src_refs[a].at[peer]
                copies.append(pltpu.make_async_remote_copy(
                    src_ref=src, dst_ref=out_refs[a].at[me], send_sem=send_sems.at[a, d - 1],
                    recv_sem=recv_sems.at[a, d - 1], device_id=(px, py, pc), device_id_type=pl.DeviceIdType.MESH))
        return copies

    def start(self, src_refs, out_refs, sems):
        for cp in self._copies(src_refs, out_refs, sems):
            cp.start()

    def wait(self, src_refs, out_refs, sems):
        for cp in self._copies(src_refs, out_refs, sems):
            cp.wait()


class _TwoLevelGather:
    def __init__(self, srcs, early_forward):
        self.srcs, self.n, self.early_forward = list(srcs), len(srcs), early_forward
        self.out_shape = [jax.ShapeDtypeStruct((N_DEV,) + s.shape, s.dtype) for s in srcs]
        self.specs = [pl.BlockSpec(memory_space=pl.ANY)] * self.n
        self.scratch = [pltpu.SemaphoreType.DMA((self.n, N_DEV - 1)), pltpu.SemaphoreType.DMA((self.n, N_DEV - 1)),
                        pltpu.SemaphoreType.DMA((self.n,))]

    def _places(self):
        x, y, c = lax.axis_index("x"), lax.axis_index("y"), lax.axis_index("c")
        return (x, y, c), (x, y, 1 - c), [(1 - x, y), (x, 1 - y), (1 - x, 1 - y)]

    def _copy(self, a, k, block, to, src_refs, out_refs, sems, from_src=False):
        slot = 4 * block[0] + 2 * block[1] + block[2]
        return pltpu.make_async_remote_copy(
            src_ref=src_refs[a] if from_src else out_refs[a].at[slot], dst_ref=out_refs[a].at[slot],
            send_sem=sems[0].at[a, k], recv_sem=sems[1].at[a, k], device_id=to, device_id_type=pl.DeviceIdType.MESH)

    def _local(self, a, src_refs, out_refs, sems):
        me, _, _ = self._places()
        return pltpu.make_async_copy(src_refs[a], out_refs[a].at[4 * me[0] + 2 * me[1] + me[2]], sems[2].at[a])

    def start(self, src_refs, out_refs, sems):
        me, sibling, chips = self._places()
        for a in range(self.n):
            self._local(a, src_refs, out_refs, sems).start()
            self._copy(a, 0, me, sibling, src_refs, out_refs, sems, from_src=True).start()
        for j, chip in enumerate(chips):
            for a in range(self.n):
                self._copy(a, 1 + j, me, (*chip, me[2]), src_refs, out_refs, sems, from_src=True).start()

    def forward(self, src_refs, out_refs, sems):
        me, sibling, chips = self._places()
        for j, chip in enumerate(chips):
            for a in range(self.n):
                self._copy(a, 1 + j, (*chip, me[2]), me, src_refs, out_refs, sems).wait_recv()
                self._copy(a, 4 + j, (*chip, me[2]), sibling, src_refs, out_refs, sems).start()

    def wait(self, src_refs, out_refs, sems):
        if not self.early_forward:
            self.forward(src_refs, out_refs, sems)
        me, sibling, chips = self._places()
        for a in range(self.n):
            self._copy(a, 0, sibling, me, src_refs, out_refs, sems).wait_recv()
            for j, chip in enumerate(chips):
                self._copy(a, 4 + j, (*chip, sibling[2]), me, src_refs, out_refs, sems).wait_recv()
        for a in range(self.n):
            self._copy(a, 0, me, sibling, src_refs, out_refs, sems, from_src=True).wait_send()
            for j, chip in enumerate(chips):
                self._copy(a, 1 + j, me, (*chip, me[2]), src_refs, out_refs, sems, from_src=True).wait_send()
                self._copy(a, 4 + j, (*chip, me[2]), sibling, src_refs, out_refs, sems).wait_send()
            self._local(a, src_refs, out_refs, sems).wait()


class _SiblingSwap:
    def __init__(self, srcs):
        self.srcs, self.n = list(srcs), len(srcs)
        self.chips = srcs[0].shape[0]
        self.out_shape = [jax.ShapeDtypeStruct((s.shape[0],) + s.shape[2:], s.dtype) for s in srcs]
        self.specs = [pl.BlockSpec(memory_space=pl.ANY)] * self.n
        self.scratch = [pltpu.SemaphoreType.DMA((self.n, self.chips)), pltpu.SemaphoreType.DMA((self.n, self.chips))]

    def _copies(self, src_refs, out_refs, sems):
        x, y, c = lax.axis_index("x"), lax.axis_index("y"), lax.axis_index("c")
        return [pltpu.make_async_remote_copy(
            src_ref=src_refs[a].at[q, 1 - c], dst_ref=out_refs[a].at[q], send_sem=sems[0].at[a, q],
            recv_sem=sems[1].at[a, q], device_id=(x, y, 1 - c), device_id_type=pl.DeviceIdType.MESH)
            for a in range(self.n) for q in range(self.chips)]

    def start(self, src_refs, out_refs, sems):
        for cp in self._copies(src_refs, out_refs, sems):
            cp.start()

    def wait(self, src_refs, out_refs, sems):
        for cp in self._copies(src_refs, out_refs, sems):
            cp.wait()


class _ChipExchange:
    def __init__(self, srcs):
        self.srcs, self.n = list(srcs), len(srcs)
        self.peers = srcs[0].shape[0] - 1
        self.out_shape = [jax.ShapeDtypeStruct(s.shape, s.dtype) for s in srcs]
        self.specs = [pl.BlockSpec(memory_space=pl.ANY)] * self.n
        self.scratch = [pltpu.SemaphoreType.DMA((self.n, self.peers)), pltpu.SemaphoreType.DMA((self.n, self.peers)),
                        pltpu.SemaphoreType.DMA((self.n,))]

    def _copies(self, src_refs, out_refs, sems):
        x, y, c = lax.axis_index("x"), lax.axis_index("y"), lax.axis_index("c")
        mine = 2 * x + y
        copies = [pltpu.make_async_copy(src_refs[a].at[mine], out_refs[a].at[mine], sems[2].at[a]) for a in range(self.n)]
        for d in range(1, self.peers + 1):
            px, py = (x + (d >> 1)) % 2, (y + (d & 1)) % 2
            for a in range(self.n):
                copies.append(pltpu.make_async_remote_copy(
                    src_ref=src_refs[a].at[2 * px + py], dst_ref=out_refs[a].at[mine], send_sem=sems[0].at[a, d - 1],
                    recv_sem=sems[1].at[a, d - 1], device_id=(px, py, c), device_id_type=pl.DeviceIdType.MESH))
        return copies

    def start(self, src_refs, out_refs, sems):
        for cp in self._copies(src_refs, out_refs, sems):
            cp.start()

    def wait(self, src_refs, out_refs, sems):
        for cp in self._copies(src_refs, out_refs, sems):
            cp.wait()


class _Both:
    def __init__(self, first, second):
        self.parts = (first, second)
        self.srcs = first.srcs + second.srcs
        self.n = first.n + second.n
        self.out_shape = first.out_shape + second.out_shape
        self.specs = first.specs + second.specs
        self.scratch = first.scratch + second.scratch

    def _split(self, src_refs, out_refs, sems):
        a, b = self.parts
        k = len(a.scratch)
        return ((a, src_refs[:a.n], out_refs[:a.n], sems[:k]), (b, src_refs[a.n:], out_refs[a.n:], sems[k:]))

    def start(self, src_refs, out_refs, sems):
        for ex, s, o, m in self._split(src_refs, out_refs, sems):
            ex.start(s, o, m)

    def wait(self, src_refs, out_refs, sems):
        for ex, s, o, m in self._split(src_refs, out_refs, sems):
            ex.wait(s, o, m)


def _pair_sum(a, b):
    shape = a.shape
    cols = shape[-1]
    rows = a.size // cols
    tr = max(t for t in range(SUBLANES * 2, rows + 1, SUBLANES * 2) if rows % t == 0 and t * cols * 2 <= ADAM_BLOCK_BYTES)

    def body(a_ref, b_ref, o_ref):
        o_ref[...] = (a_ref[...].astype(F32) + b_ref[...].astype(F32)).astype(BF16)

    blk = pl.BlockSpec((tr, cols), lambda i: (i, 0))
    return pl.pallas_call(
        body, name="pair_sum", grid=(rows // tr,), in_specs=[blk, blk], out_specs=blk,
        out_shape=jax.ShapeDtypeStruct((rows, cols), BF16), compiler_params=_params(("arbitrary",)),
    )(a.reshape(rows, cols), b.reshape(rows, cols)).reshape(shape)


def _exchange(name, srcs, gathers):
    ex = _Exchange(srcs, gathers)

    def body(*refs):
        src_refs, out_refs, sems = refs[:ex.n], refs[ex.n:2 * ex.n], refs[2 * ex.n:]
        ex.start(src_refs, out_refs, sems)
        ex.wait(src_refs, out_refs, sems)

    return pl.pallas_call(body, name=name, out_shape=ex.out_shape, in_specs=ex.specs, out_specs=ex.specs,
                          scratch_shapes=ex.scratch)(*srcs)


def _launch(body, name, grid, in_specs, out_specs, out_shape, scratch_shapes, args, ex=None):
    sem = ("arbitrary",) * len(grid)
    if ex is None:
        outs = pl.pallas_call(body, name=name, grid=grid, in_specs=in_specs, out_specs=out_specs, out_shape=out_shape,
                              scratch_shapes=scratch_shapes, compiler_params=_params(sem))(*args)
        return outs, []
    n_in, n_out, n_scr = len(in_specs), len(out_specs), len(scratch_shapes)

    def hosted(*refs):
        ins, ex_in = refs[:n_in], refs[n_in:n_in + ex.n]
        outs = refs[n_in + ex.n:n_in + ex.n + n_out]
        ex_out = refs[n_in + ex.n + n_out:n_in + 2 * ex.n + n_out]
        rest = refs[n_in + 2 * ex.n + n_out:]
        scratch, sems = rest[:n_scr], rest[n_scr:]
        ids = [pl.program_id(a) for a in range(len(grid))]
        first = functools.reduce(lambda p, q: p & q, [i == 0 for i in ids])
        last = functools.reduce(lambda p, q: p & q, [i == g - 1 for i, g in zip(ids, grid)])

        @pl.when(first)
        def _():
            ex.start(ex_in, ex_out, sems)

        if getattr(ex, "early_forward", False):
            @pl.when(functools.reduce(lambda p, q: p & q, [i == (3 * g // 4 if a == 0 else 0)
                                                           for a, (i, g) in enumerate(zip(ids, grid))]))
            def _():
                ex.forward(ex_in, ex_out, sems)

        body(*ins, *outs, *scratch)

        @pl.when(last)
        def _():
            ex.wait(ex_in, ex_out, sems)

    outs = pl.pallas_call(
        hosted, name=name, grid=grid, in_specs=list(in_specs) + ex.specs, out_specs=list(out_specs) + ex.specs,
        out_shape=list(out_shape) + ex.out_shape, scratch_shapes=list(scratch_shapes) + ex.scratch,
        compiler_params=_params(sem))(*args, *ex.srcs)
    return outs[:n_out], outs[n_out:]


def _ada_fwd(c_all, ada_w):
    layers, d, cols = ada_w.shape

    def body(c_ref, w_ref, o_ref):
        c = c_ref[...]
        ca = (c * _sigmoid(c)).astype(BF16)
        for l in range(layers):
            o_ref[l] = _dot(ca, w_ref[l].astype(BF16))

    return pl.pallas_call(
        body, name="ada_fwd", out_shape=jax.ShapeDtypeStruct((layers, N_DEV, cols), F32),
        compiler_params=pltpu.CompilerParams(vmem_limit_bytes=VMEM_LIMIT_BYTES),
    )(c_all, ada_w)


def _ada_bwd(c_all, dmod_cols):
    layers, _, cols = dmod_cols.shape
    d = c_all.shape[1]

    def body(c_ref, g_ref, o_ref):
        c = c_ref[...]
        ca = (c * _sigmoid(c)).astype(BF16)
        for l in range(layers):
            o_ref[l] = _dot_tn(ca, g_ref[l].astype(BF16))

    return pl.pallas_call(
        body, name="ada_bwd", out_shape=jax.ShapeDtypeStruct((layers, d, cols), F32),
        compiler_params=pltpu.CompilerParams(vmem_limit_bytes=VMEM_LIMIT_BYTES),
    )(c_all, dmod_cols)


def _pooled(hbuf, h, t0, g, tm):
    gd = h.shape[1] // len(POOL_WINDOWS)
    cols = slice(g * gd, (g + 1) * gd)
    w = POOL_WINDOWS[g]
    hg = h[:, cols]
    s = hbuf[0:POOL_HALO + tm, cols]
    span = 1
    while span < w:
        s = s + pltpu.roll(s, span, 0)
        span *= 2
    s = s[POOL_HALO:POOL_HALO + tm, :]
    t = t0 + lax.broadcasted_iota(jnp.int32, (tm, 1), 0)
    cnt = jnp.minimum(t + 1, w).astype(F32)
    return s / cnt - hg, cnt


def _pool_fwd(x, pv, wp, scale, *, tm, ex=None):
    t_len, d = x.shape
    n_i = t_len // tm
    gd = d // len(POOL_WINDOWS)

    def body(x_ref, pv_ref, wp_ref, sc_ref, xo_ref, y_ref, hbuf):
        i = pl.program_id(0)

        @pl.when(i == 0)
        def _():
            hbuf[0:POOL_HALO, :] = jnp.zeros((POOL_HALO, d), F32)

        xv, pv_ = x_ref[...], pv_ref[...]
        h, _, _ = _prenorm(xv, pv_)
        hbuf[POOL_HALO:POOL_HALO + tm, :] = h
        for g in range(len(POOL_WINDOWS)):
            pooled, _ = _pooled(hbuf, h, i * tm, g, tm)
            y_ref[:, g * gd:(g + 1) * gd] = _dot(pooled.astype(BF16), wp_ref[g])
        xo_ref[...] = _post(xv, y_ref[...] * sc_ref[...], pv_)
        hbuf[0:POOL_HALO, :] = hbuf[tm:tm + POOL_HALO, :]

    row = pl.BlockSpec((tm, d), lambda i: (i, 0))
    return _launch(
        body, "pool_fwd", (n_i,),
        [row, pl.BlockSpec((SUBLANES, d), lambda i: (0, 0)), pl.BlockSpec(wp.shape, lambda i: (0, 0, 0)),
         pl.BlockSpec((1, d), lambda i: (0, 0))],
        [row, row],
        [jax.ShapeDtypeStruct((t_len, d), F32), jax.ShapeDtypeStruct((t_len, d), F32)],
        [pltpu.VMEM((tm + POOL_HALO, d), F32)],
        (x, pv, wp, scale), ex)


def _pool_bwd(dxo, x, ypre, pv, wp, scale, *, tm, ex=None):
    t_len, d = x.shape
    n_i = t_len // tm
    gd = d // len(POOL_WINDOWS)
    hb = tm // POOL_HALO

    def body(dxo_ref, x_ref, xh_ref, y_ref, pv_ref, wp_ref, sc_ref, dx_ref, pg_ref, dwp_ref, hbuf, qbuf):
        i = pl.program_id(0)
        ti = n_i - 1 - i

        @pl.when(i == 0)
        def _():
            pg_ref[...] = jnp.zeros_like(pg_ref)
            dwp_ref[...] = jnp.zeros_like(dwp_ref)
            qbuf[tm:tm + POOL_HALO, :] = jnp.zeros((POOL_HALO, d), F32)

        xv, pv_, dxo_v, yp, sc = x_ref[...], pv_ref[...], dxo_ref[...], y_ref[...], sc_ref[...]
        dy = _post_bwd(dxo_v, yp * sc, pv_, pg_ref)
        pg_ref[PG_EXTRA:PG_EXTRA + 1, :] += _colsum(dy * yp)
        dys = dy * sc
        h, _, _ = _prenorm(xv, pv_)
        hh, _, _ = _prenorm(xh_ref[...], pv_)
        hbuf[0:POOL_HALO, :] = jnp.where(ti > 0, hh, 0.0)
        hbuf[POOL_HALO:POOL_HALO + tm, :] = h
        for g in range(len(POOL_WINDOWS)):
            cols = slice(g * gd, (g + 1) * gd)
            pooled, cnt = _pooled(hbuf, h, ti * tm, g, tm)
            dyg = dys[:, cols].astype(BF16)
            dwp_ref[g] += _dot_tn(pooled.astype(BF16), dyg)
            dp = _dot_nt(dyg, wp_ref[g])
            qbuf[0:tm, cols] = dp / cnt
            fs = qbuf[0:tm + POOL_HALO, cols]
            span = 1
            while span < POOL_WINDOWS[g]:
                fs = fs + pltpu.roll(fs, tm + POOL_HALO - span, 0)
                span *= 2
            hbuf[POOL_HALO:POOL_HALO + tm, cols] = fs[0:tm, :] - dp
        dx_ref[...] = dxo_v + _pre_bwd(hbuf[POOL_HALO:POOL_HALO + tm, :], xv, pv_, pg_ref)
        qbuf[tm:tm + POOL_HALO, :] = qbuf[0:POOL_HALO, :]

    row = pl.BlockSpec((tm, d), lambda i: (n_i - 1 - i, 0))
    halo = pl.BlockSpec((POOL_HALO, d), lambda i: (jnp.maximum((n_i - 1 - i) * hb - 1, 0), 0))
    small = pl.BlockSpec((SUBLANES, d), lambda i: (0, 0))
    return _launch(
        body, "pool_bwd", (n_i,),
        [row, row, halo, row, small, pl.BlockSpec(wp.shape, lambda i: (0, 0, 0)), pl.BlockSpec((1, d), lambda i: (0, 0))],
        [row, small, pl.BlockSpec(wp.shape, lambda i: (0, 0, 0))],
        [jax.ShapeDtypeStruct((t_len, d), F32), jax.ShapeDtypeStruct((SUBLANES, d), F32),
         jax.ShapeDtypeStruct(wp.shape, F32)],
        [pltpu.VMEM((tm + POOL_HALO, d), F32), pltpu.VMEM((tm + POOL_HALO, d), F32)],
        (dxo, x, x, ypre, pv, wp, scale), ex)


def _ffn_conv(a, prev, w):
    return w[2:3] * a + w[1:2] * _shift_down(a, 1, prev) + w[0:1] * _shift_down(a, 2, prev)


def _ffn_fwd(x, pv, wup, wdw, wdn, *, tm, ex=None):
    t_len, d = x.shape
    _, n_j, fc, _ = wup.shape
    n_i = t_len // tm

    def body(x_ref, pv_ref, wup_ref, wdw_ref, wdn_ref, xo_ref, y_ref, h_ref, a_ref, c_ref, h_s, yacc, carry):
        i, j = pl.program_id(0), pl.program_id(1)

        @pl.when(j == 0)
        def _():
            h, _, _ = _prenorm(x_ref[...], pv_ref[...])
            hb = h.astype(BF16)
            h_s[...] = hb
            h_ref[...] = hb
            yacc[...] = jnp.zeros_like(yacc)

        @pl.when((i == 0) & (j == 0))
        def _():
            carry[...] = jnp.zeros_like(carry)

        hb = h_s[...]
        conv = []
        for s in range(2):
            a = _dot_nt(hb, wup_ref[s, 0])
            a_ref[s, 0] = a.astype(BF16)
            cv = _ffn_conv(a, carry[s, j], wdw_ref[s, 0])
            c_ref[s, 0] = cv.astype(BF16)
            conv.append(cv)
            carry[s, j] = a[tm - FFN_HALO:tm, :]
        g, v = conv
        u = g * _sigmoid(g) * v
        yacc[...] += _dot(u.astype(BF16), wdn_ref[...])

        @pl.when(j == n_j - 1)
        def _():
            y = yacc[...]
            y_ref[...] = y
            xo_ref[...] = _post(x_ref[...], y, pv_ref[...])

    row = pl.BlockSpec((tm, d), lambda i, j: (i, 0))
    return _launch(
        body, "ffn_fwd", (n_i, n_j),
        [row, pl.BlockSpec((SUBLANES, d), lambda i, j: (0, 0)),
         pl.BlockSpec((2, 1, fc, d), lambda i, j: (0, j, 0, 0)),
         pl.BlockSpec((2, 1, SUBLANES, fc), lambda i, j: (0, j, 0, 0)),
         pl.BlockSpec((fc, d), lambda i, j: (j, 0))],
        [row, row, row, pl.BlockSpec((2, 1, tm, fc), lambda i, j: (0, j, i, 0)),
         pl.BlockSpec((2, 1, tm, fc), lambda i, j: (0, j, i, 0))],
        [jax.ShapeDtypeStruct((t_len, d), F32), jax.ShapeDtypeStruct((t_len, d), F32),
         jax.ShapeDtypeStruct((t_len, d), BF16), jax.ShapeDtypeStruct((2, n_j, t_len, fc), BF16),
         jax.ShapeDtypeStruct((2, n_j, t_len, fc), BF16)],
        [pltpu.VMEM((tm, d), BF16), pltpu.VMEM((tm, d), F32), pltpu.VMEM((2, n_j, FFN_HALO, fc), F32)],
        (x, pv, wup, wdw, wdn), ex)


def _ffn_bwd(dy, h, a_sav, c_sav, wup, wdw, wdn, *, tm, ex=None):
    t_len, d = dy.shape
    _, n_j, fc, _ = wup.shape
    n_i = t_len // tm
    assert n_i % 2 == 0

    def body(dyp_ref, hp_ref, a_ref, c_ref, wup_ref, wdw_ref, wdn_ref,
             dhp_ref, gup_ref, gdn_ref, dwdw_ref, da_p, u_p, acc_up, acc_dn, carry):
        j, i = pl.program_id(0), pl.program_id(1)

        @pl.when((j == 0) & (i == 0))
        def _():
            dwdw_ref[...] = jnp.zeros_like(dwdw_ref)

        @pl.when(i == 0)
        def _():
            acc_up[...] = jnp.zeros_like(acc_up)
            acc_dn[...] = jnp.zeros_like(acc_dn)
            carry[...] = jnp.zeros_like(carry)

        def step(half):
            rows = slice(half * tm, (half + 1) * tm)
            du = _dot_nt(dyp_ref[rows, :], wdn_ref[...])
            g, v = c_ref[0, 0].astype(F32), c_ref[1, 0].astype(F32)
            sg = _sigmoid(g)
            sl = g * sg
            u_p[rows, :] = (sl * v).astype(BF16)
            d2 = (du * v * (sg * (1.0 + g * (1.0 - sg))), du * sl)
            dab = []
            for s in range(2):
                w = wdw_ref[s, 0]
                nxt = carry[s]
                p1, p2 = _shift_up(d2[s], 1, nxt), _shift_up(d2[s], 2, nxt)
                carry[s] = d2[s][0:FFN_HALO, :]
                dab.append((w[2:3] * d2[s] + w[1:2] * p1 + w[0:1] * p2).astype(BF16))
                da_p[s, rows, :] = dab[s]
                a_s = a_ref[s, 0].astype(F32)
                for k, sh in ((2, d2[s]), (1, p1), (0, p2)):
                    dwdw_ref[s, j, k:k + 1, :] += _colsum(a_s * sh)
            dhp_ref[0] = (_dot(dab[0], wup_ref[0, 0]) + _dot(dab[1], wup_ref[1, 0])).astype(BF16)
            if half == 0:
                hp = hp_ref[...]
                for s in range(2):
                    acc_up[s] += _dot_tn(da_p[s], hp)
                acc_dn[...] += _dot_tn(u_p[...], dyp_ref[...])

        @pl.when(i % 2 == 0)
        def _():
            step(1)

        @pl.when(i % 2 == 1)
        def _():
            step(0)

        @pl.when(i == n_i - 1)
        def _():
            gup_ref[:, 0] = acc_up[...].astype(BF16)
            gdn_ref[...] = acc_dn[...].astype(BF16)

    chunk = lambda shape: pl.BlockSpec((2, 1) + shape, lambda j, i: (0, j, 0, 0))
    tile = pl.BlockSpec((2, 1, tm, fc), lambda j, i: (0, j, n_i - 1 - i, 0))
    pair = pl.BlockSpec((2 * tm, d), lambda j, i: ((n_i - 1 - i) // 2, 0))
    return _launch(
        body, "ffn_bwd", (n_j, n_i),
        [pair, pair, tile, tile, chunk((fc, d)), chunk((SUBLANES, fc)), pl.BlockSpec((fc, d), lambda j, i: (j, 0))],
        [pl.BlockSpec((1, tm, d), lambda j, i: (j, n_i - 1 - i, 0)), chunk((fc, d)),
         pl.BlockSpec((fc, d), lambda j, i: (j, 0)), pl.BlockSpec((2, n_j, SUBLANES, fc), lambda j, i: (0, 0, 0, 0))],
        [jax.ShapeDtypeStruct((n_j, t_len, d), BF16), jax.ShapeDtypeStruct((2, n_j, fc, d), BF16),
         jax.ShapeDtypeStruct((n_j * fc, d), BF16), jax.ShapeDtypeStruct((2, n_j, SUBLANES, fc), F32)],
        [pltpu.VMEM((2, 2 * tm, fc), BF16), pltpu.VMEM((2 * tm, fc), BF16), pltpu.VMEM((2, fc, d), F32),
         pltpu.VMEM((fc, d), F32), pltpu.VMEM((2, FFN_HALO, fc), F32)],
        (dy, h, a_sav, c_sav, wup, wdw, wdn), ex)


def _sub_post_bwd(dxo, ypre, pv, *, tm, ex=None):
    t_len, d = dxo.shape
    n_i = t_len // tm

    def body(dxo_ref, y_ref, pv_ref, dy_ref, pg_ref):
        @pl.when(pl.program_id(0) == 0)
        def _():
            pg_ref[...] = jnp.zeros_like(pg_ref)

        dy_ref[...] = _post_bwd(dxo_ref[...], y_ref[...], pv_ref[...], pg_ref).astype(BF16)

    row = pl.BlockSpec((tm, d), lambda i: (i, 0))
    small = pl.BlockSpec((SUBLANES, d), lambda i: (0, 0))
    return _launch(body, "sub_post_bwd", (n_i,), [row, row, small], [row, small],
                   [jax.ShapeDtypeStruct((t_len, d), BF16), jax.ShapeDtypeStruct((SUBLANES, d), F32)], [],
                   (dxo, ypre, pv), ex)


def _sub_pre_bwd(dhp, x, dxo, pv, *, tm, ex=None):
    n_p, t_len, d = dhp.shape
    n_i = t_len // tm

    def body(dhp_ref, x_ref, dxo_ref, pv_ref, dx_ref, pg_ref):
        @pl.when(pl.program_id(0) == 0)
        def _():
            pg_ref[...] = jnp.zeros_like(pg_ref)

        dh = dhp_ref[0].astype(F32)
        for p in range(1, n_p):
            dh = dh + dhp_ref[p].astype(F32)
        dx_ref[...] = dxo_ref[...] + _pre_bwd(dh, x_ref[...], pv_ref[...], pg_ref)

    row = pl.BlockSpec((tm, d), lambda i: (i, 0))
    small = pl.BlockSpec((SUBLANES, d), lambda i: (0, 0))
    return _launch(body, "sub_pre_bwd", (n_i,), [pl.BlockSpec((n_p, tm, d), lambda i: (0, i, 0)), row, row, small],
                   [row, small], [jax.ShapeDtypeStruct((t_len, d), F32), jax.ShapeDtypeStruct((SUBLANES, d), F32)], [],
                   (dhp, x, dxo, pv), ex)


CV_B1, CV_BDW, CV_LNG, CV_LNB, CV_B2 = 0, 1, 2, 3, 4


def _taps_by_residue(offs):
    groups = {}
    for k, off in enumerate(offs):
        groups.setdefault(off % SUBLANES, []).append((k, off // SUBLANES))
    return sorted(groups.items())


def _depthwise(buf, w_ref, out_ref, offs, tm, d):
    taps_of = _taps_by_residue(offs)

    def chunk(r, carry):
        r0 = pl.multiple_of(r * DW_ROWS, DW_ROWS)
        for cb in range(d // LANES):
            cols = slice(cb * LANES, (cb + 1) * LANES)
            win = buf[pl.ds(r0, DW_ROWS + CONV_HALO), cols]
            acc = jnp.zeros((DW_ROWS, LANES), F32)
            for b, taps in taps_of:
                wb = win if b == 0 else pltpu.roll(win, DW_ROWS + CONV_HALO - b, 0)
                for k, a in taps:
                    acc = acc + wb[SUBLANES * a:SUBLANES * a + DW_ROWS, :] * w_ref[k:k + 1, cols]
            out_ref[pl.ds(r0, DW_ROWS), cols] = acc
        return carry

    lax.fori_loop(0, tm // DW_ROWS, chunk, 0)


def _depthwise_wgrad(dbuf, ubuf, dw_ref, tm, d):
    taps_of = _taps_by_residue(tuple(2 + k for k in range(CONV_TAPS)))
    for cb in range(d // LANES):
        cols = slice(cb * LANES, (cb + 1) * LANES)

        def chunk(r, acc):
            r0 = pl.multiple_of(r * CONV_ROWS, CONV_ROWS)
            dv = dbuf[pl.ds(r0, CONV_ROWS), cols]
            win = ubuf[pl.ds(r0, CONV_ROWS + CONV_HALO), cols]
            new = list(acc)
            for b, taps in taps_of:
                wb = win if b == 0 else pltpu.roll(win, CONV_ROWS + CONV_HALO - b, 0)
                for k, a in taps:
                    for q in range(CONV_ROWS // SUBLANES):
                        new[k] = new[k] + dv[SUBLANES * q:SUBLANES * (q + 1), :] * wb[SUBLANES * (a + q):SUBLANES * (a + q + 1), :]
            return tuple(new)

        acc = lax.fori_loop(0, tm // CONV_ROWS, chunk, tuple(jnp.zeros((SUBLANES, LANES), F32) for _ in range(CONV_TAPS)))
        for k in range(CONV_TAPS):
            dw_ref[k:k + 1, cols] += _colsum(acc[k])


def _layer_norm_parts(c1):
    mu = jnp.mean(c1, axis=-1, keepdims=True)
    cen = c1 - mu
    rstd = lax.rsqrt(jnp.mean(cen * cen, axis=-1, keepdims=True) + NORM_EPS)
    return cen * rstd, rstd


def _conv_fwd(x, pv, w1, w2, wdw, vec, *, tm):
    t_len, d = x.shape
    n_i = t_len // tm
    n_q = w1.shape[0] // 2
    qc = w1.shape[2]

    def body(x_ref, pv_ref, w1_ref, w2_ref, wdw_ref, vec_ref, xo_ref, y_ref, a_ref, c1_ref, ubuf):
        i = pl.program_id(0)

        @pl.when(i == 0)
        def _():
            ubuf[0:CONV_HALO, :] = jnp.zeros((CONV_HALO, d), F32)

        xv, pv_ = x_ref[...], pv_ref[...]
        h, _, _ = _prenorm(xv, pv_)
        hb = h.astype(BF16)
        for q in range(n_q):
            cols = slice(q * qc, (q + 1) * qc)
            gcols = slice(d + q * qc, d + (q + 1) * qc)
            val = (_dot(hb, w1_ref[q]) + vec_ref[CV_B1:CV_B1 + 1, cols]).astype(BF16)
            gate = (_dot(hb, w1_ref[n_q + q]) + vec_ref[CV_B1:CV_B1 + 1, gcols]).astype(BF16)
            a_ref[:, cols] = val
            a_ref[:, gcols] = gate
            ubuf[CONV_HALO:CONV_HALO + tm, cols] = val.astype(F32) * _sigmoid(gate.astype(F32))
        _depthwise(ubuf, wdw_ref, c1_ref, tuple(2 + k for k in range(CONV_TAPS)), tm, d)
        c1 = c1_ref[...] + vec_ref[CV_BDW:CV_BDW + 1, 0:d]
        c1_ref[...] = c1
        xhat, _ = _layer_norm_parts(c1)
        ln = xhat * vec_ref[CV_LNG:CV_LNG + 1, 0:d] + vec_ref[CV_LNB:CV_LNB + 1, 0:d]
        s = ln * _sigmoid(ln)
        y = _dot(s.astype(BF16), w2_ref[...]) + vec_ref[CV_B2:CV_B2 + 1, 0:d]
        y_ref[...] = y
        xo_ref[...] = _post(xv, y, pv_)
        ubuf[0:CONV_HALO, :] = ubuf[tm:tm + CONV_HALO, :]

    row = pl.BlockSpec((tm, d), lambda i: (i, 0))
    return pl.pallas_call(
        body, name="conv_fwd", grid=(n_i,),
        in_specs=[row, pl.BlockSpec((SUBLANES, d), lambda i: (0, 0)), pl.BlockSpec(w1.shape, lambda i: (0, 0, 0)),
                  pl.BlockSpec(w2.shape, lambda i: (0, 0)), pl.BlockSpec(wdw.shape, lambda i: (0, 0)),
                  pl.BlockSpec(vec.shape, lambda i: (0, 0))],
        out_specs=[row, row, pl.BlockSpec((tm, 2 * d), lambda i: (i, 0)), row],
        out_shape=[jax.ShapeDtypeStruct((t_len, d), F32), jax.ShapeDtypeStruct((t_len, d), F32),
                   jax.ShapeDtypeStruct((t_len, 2 * d), BF16), jax.ShapeDtypeStruct((t_len, d), F32)],
        scratch_shapes=[pltpu.VMEM((tm + CONV_HALO, d), F32)],
        compiler_params=_params(("arbitrary",)),
    )(x, pv, w1, w2, wdw, vec)


def _conv_bwd(dxo, x, ypre, a_sav, c1_sav, pv, w1, w2, wdw, vec, *, tm, ex=None):
    t_len, d = x.shape
    n_i = t_len // tm
    n_q = w1.shape[0] // 2
    qc = w1.shape[2]
    hb_ = tm // CONV_HALO

    def body(dxo_ref, x_ref, y_ref, a_ref, ah_ref, c1_ref, pv_ref, w1_ref, w2_ref, wdw_ref, vec_ref,
             dx_ref, pg_ref, gw1_ref, gw2_ref, gvec_ref, gwdw_ref, ubuf, dcbuf, dubuf, acc1, acc2):
        i = pl.program_id(0)
        ti = n_i - 1 - i

        @pl.when(i == 0)
        def _():
            pg_ref[...] = jnp.zeros_like(pg_ref)
            gvec_ref[...] = jnp.zeros_like(gvec_ref)
            gwdw_ref[...] = jnp.zeros_like(gwdw_ref)
            acc1[...] = jnp.zeros_like(acc1)
            acc2[...] = jnp.zeros_like(acc2)
            dcbuf[tm:tm + CONV_HALO, :] = jnp.zeros((CONV_HALO, d), F32)

        xv, pv_, dxo_v = x_ref[...], pv_ref[...], dxo_ref[...]
        dy = _post_bwd(dxo_v, y_ref[...], pv_, pg_ref)
        gvec_ref[CV_B2:CV_B2 + 1, 0:d] += _colsum(dy)
        dyb = dy.astype(BF16)
        xhat, rstd = _layer_norm_parts(c1_ref[...])
        lng = vec_ref[CV_LNG:CV_LNG + 1, 0:d]
        ln = xhat * lng + vec_ref[CV_LNB:CV_LNB + 1, 0:d]
        sg = _sigmoid(ln)
        acc2[...] += _dot_tn((ln * sg).astype(BF16), dyb)
        dln = _dot_nt(dyb, w2_ref[...]) * (sg * (1.0 + ln * (1.0 - sg)))
        gvec_ref[CV_LNG:CV_LNG + 1, 0:d] += _colsum(dln * xhat)
        gvec_ref[CV_LNB:CV_LNB + 1, 0:d] += _colsum(dln)
        dxh = dln * lng
        dc1 = rstd * (dxh - jnp.mean(dxh, axis=-1, keepdims=True)
                      - xhat * jnp.mean(dxh * xhat, axis=-1, keepdims=True))
        gvec_ref[CV_BDW:CV_BDW + 1, 0:d] += _colsum(dc1)
        dcbuf[0:tm, :] = dc1
        for q in range(n_q):
            cols = slice(q * qc, (q + 1) * qc)
            gcols = slice(d + q * qc, d + (q + 1) * qc)
            ubuf[CONV_HALO:CONV_HALO + tm, cols] = a_ref[:, cols].astype(F32) * _sigmoid(a_ref[:, gcols].astype(F32))
            uh = ah_ref[:, cols].astype(F32) * _sigmoid(ah_ref[:, gcols].astype(F32))
            ubuf[0:CONV_HALO, cols] = jnp.where(ti > 0, uh, 0.0)
        _depthwise_wgrad(dcbuf, ubuf, gwdw_ref, tm, d)
        _depthwise(dcbuf, wdw_ref, dubuf, tuple(CONV_TAPS - 1 - k for k in range(CONV_TAPS)), tm, d)
        dcbuf[tm:tm + CONV_HALO, :] = dcbuf[0:CONV_HALO, :]
        h, _, _ = _prenorm(xv, pv_)
        hb = h.astype(BF16)
        dh = jnp.zeros((tm, d), F32)
        for q in range(n_q):
            cols = slice(q * qc, (q + 1) * qc)
            gcols = slice(d + q * qc, d + (q + 1) * qc)
            du = dubuf[:, cols]
            val, gate = a_ref[:, cols].astype(F32), a_ref[:, gcols].astype(F32)
            sgg = _sigmoid(gate)
            dval = du * sgg
            dgate = du * val * (sgg * (1.0 - sgg))
            gvec_ref[CV_B1:CV_B1 + 1, cols] += _colsum(dval)
            gvec_ref[CV_B1:CV_B1 + 1, gcols] += _colsum(dgate)
            dvb, dgb = dval.astype(BF16), dgate.astype(BF16)
            acc1[q] += _dot_tn(hb, dvb)
            acc1[n_q + q] += _dot_tn(hb, dgb)
            dh = dh + _dot_nt(dvb, w1_ref[q]) + _dot_nt(dgb, w1_ref[n_q + q])
        dx_ref[...] = dxo_v + _pre_bwd(dh, xv, pv_, pg_ref)

        @pl.when(i == n_i - 1)
        def _():
            gw1_ref[...] = acc1[...].astype(BF16)
            gw2_ref[...] = acc2[...].astype(BF16)

    row = pl.BlockSpec((tm, d), lambda i: (n_i - 1 - i, 0))
    small = pl.BlockSpec((SUBLANES, d), lambda i: (0, 0))
    whole2 = lambda shape: pl.BlockSpec(shape, lambda i: (0, 0))
    return _launch(
        body, "conv_bwd", (n_i,),
        [row, row, row,
         pl.BlockSpec((tm, 2 * d), lambda i: (n_i - 1 - i, 0)),
         pl.BlockSpec((CONV_HALO, 2 * d), lambda i: (jnp.maximum((n_i - 1 - i) * hb_ - 1, 0), 0)),
         row, small, pl.BlockSpec(w1.shape, lambda i: (0, 0, 0)), whole2(w2.shape), whole2(wdw.shape),
         whole2(vec.shape)],
        [row, small, pl.BlockSpec(w1.shape, lambda i: (0, 0, 0)), whole2(w2.shape), whole2(vec.shape),
         whole2(wdw.shape)],
        [jax.ShapeDtypeStruct((t_len, d), F32), jax.ShapeDtypeStruct((SUBLANES, d), F32),
         jax.ShapeDtypeStruct(w1.shape, BF16), jax.ShapeDtypeStruct(w2.shape, BF16),
         jax.ShapeDtypeStruct(vec.shape, F32), jax.ShapeDtypeStruct(wdw.shape, F32)],
        [pltpu.VMEM((tm + CONV_HALO, d), F32), pltpu.VMEM((tm + CONV_HALO, d), F32),
         pltpu.VMEM((tm, d), F32), pltpu.VMEM(w1.shape, F32), pltpu.VMEM(w2.shape, F32)],
        (dxo, x, ypre, a_sav, a_sav, c1_sav, pv, w1, w2, wdw, vec), ex)


def _loss_head(xo, target, ypre, pv, *, tm):
    t_len, d = xo.shape
    n_i = t_len // tm

    def body(xo_ref, t_ref, y_ref, pv_ref, dxo_ref, sq_ref, dy_ref, pg_ref):
        @pl.when(pl.program_id(0) == 0)
        def _():
            sq_ref[...] = jnp.zeros_like(sq_ref)
            pg_ref[...] = jnp.zeros_like(pg_ref)

        err = xo_ref[...] - t_ref[...]
        dxo = err * (1.0 / d)
        dxo_ref[...] = dxo
        sq_ref[...] += jnp.sum((err * err).reshape(tm // SUBLANES, SUBLANES, d), axis=0)
        dy_ref[...] = _post_bwd(dxo, y_ref[...], pv_ref[...], pg_ref).astype(BF16)

    row = pl.BlockSpec((tm, d), lambda i: (i, 0))
    small = pl.BlockSpec((SUBLANES, d), lambda i: (0, 0))
    return pl.pallas_call(
        body, name="loss_head", grid=(n_i,), in_specs=[row, row, row, small],
        out_specs=[row, small, row, small],
        out_shape=[jax.ShapeDtypeStruct((t_len, d), F32), jax.ShapeDtypeStruct((SUBLANES, d), F32),
                   jax.ShapeDtypeStruct((t_len, d), BF16), jax.ShapeDtypeStruct((SUBLANES, d), F32)],
        compiler_params=_params(("arbitrary",)),
    )(xo, target, ypre, pv)


def _adam(name, parts, w, m, v):
    layers = len(parts)
    _, rows, cols = parts[0].shape
    tr = rows
    if rows % SUBLANES == 0:
        cap = max(SUBLANES, ADAM_BLOCK_BYTES // (4 * cols))
        tr = max(t for t in range(SUBLANES, rows + 1, SUBLANES) if rows % t == 0 and (t <= cap or t == SUBLANES))
    c1 = 1.0 / (1.0 - ADAM_B1 ** ADAM_STEP)
    c2 = 1.0 / (1.0 - ADAM_B2 ** ADAM_STEP)

    def body(*refs):
        p_refs = refs[:layers]
        w_ref, m_ref, v_ref, g_ref, d_ref, mo_ref, vo_ref, g_s = refs[layers:]
        for l in range(layers):
            @pl.when(pl.program_id(0) == l)
            def _():
                g = p_refs[l][0].astype(F32)
                for k in range(1, parts[l].shape[0]):
                    g = g + p_refs[l][k].astype(F32)
                g_s[...] = g

        g = g_s[...]
        m2 = ADAM_B1 * m_ref[0] + (1.0 - ADAM_B1) * g
        v2 = ADAM_B2 * v_ref[0] + (1.0 - ADAM_B2) * (g * g)
        g_ref[0] = g
        mo_ref[0] = m2
        vo_ref[0] = v2
        d_ref[0] = -ADAM_LR * ((m2 * c1) / (jnp.sqrt(v2 * c2) + ADAM_EPS) + ADAM_WD * w_ref[0])

    def part_spec(l):
        return pl.BlockSpec((parts[l].shape[0], tr, cols), lambda ll, i: (0, jnp.where(ll == l, i, 0), 0))

    blk = pl.BlockSpec((1, tr, cols), lambda ll, i: (ll, i, 0))
    out = jax.ShapeDtypeStruct((layers, rows, cols), F32)
    return pl.pallas_call(
        body, name=name, grid=(layers, rows // tr),
        in_specs=[part_spec(l) for l in range(layers)] + [blk, blk, blk],
        out_specs=[blk, blk, blk, blk], out_shape=[out, out, out, out],
        scratch_shapes=[pltpu.VMEM((tr, cols), F32)],
        compiler_params=_params(("arbitrary", "arbitrary")),
    )(*parts, w, m, v)


def _adam_nd(name, parts, w, m, v):
    shape = w.shape
    cols = shape[-1]
    if isinstance(parts, (list, tuple)):
        layers = len(parts)
    else:
        layers, parts = 1, [parts]
    rows = w.size // (cols * layers)
    flat = lambda t: t.reshape(layers, rows, cols)
    outs = _adam(name, [p.reshape(p.shape[0], rows, cols) for p in parts], flat(w), flat(m), flat(v))
    return [o.reshape(shape) for o in outs]


def _small_pack(parts, size):
    flat = jnp.concatenate([p.reshape(-1) for p in parts])
    return jnp.pad(flat, (0, size - flat.shape[0]))


def _to_shards(full, axis):
    shp = full.shape
    split = full.reshape(shp[:axis] + (N_DEV, shp[axis] // N_DEV) + shp[axis + 1:])
    return jnp.moveaxis(split, axis, 0)


def _from_shards(sh, axis):
    moved = jnp.moveaxis(sh, 0, axis)
    shp = moved.shape
    return moved.reshape(shp[:axis] + (shp[axis] * shp[axis + 1],) + shp[axis + 2:])


def kernel(x, c, ada_w, ada_b, pre_g, post_g, pool_w, pool_scale, cv_w_pw1, cv_b_pw1, cv_w_dw, cv_b_dw, cv_ln_g, cv_ln_b, cv_w_pw2, cv_b_pw2, ffn_w_up, ffn_w_dw, ffn_w_down, loss_target, m_ada_w, m_ada_b, m_pre_g, m_post_g, m_pool_w, m_pool_scale, m_cv_w_pw1, m_cv_b_pw1, m_cv_w_dw, m_cv_b_dw, m_cv_ln_g, m_cv_ln_b, m_cv_w_pw2, m_cv_b_pw2, m_ffn_w_up, m_ffn_w_dw, m_ffn_w_down, v_ada_w, v_ada_b, v_pre_g, v_post_g, v_pool_w, v_pool_scale, v_cv_w_pw1, v_cv_b_pw1, v_cv_w_dw, v_cv_b_dw, v_cv_ln_g, v_cv_ln_b, v_cv_w_pw2, v_cv_b_pw2, v_ffn_w_up, v_ffn_w_dw, v_ffn_w_down):
    t_len, d = x.shape[1], x.shape[2]
    depth = ada_w.shape[0]
    fc = ffn_w_up.shape[2]
    n_j = N_DEV // 2
    me = 4 * lax.axis_index("x") + 2 * lax.axis_index("y") + lax.axis_index("c")

    small_w = [pre_g, post_g, cv_b_pw1, cv_w_dw, cv_b_dw, cv_ln_g, cv_ln_b, cv_b_pw2, ffn_w_dw]
    small_m = [m_pre_g, m_post_g, m_cv_b_pw1, m_cv_w_dw, m_cv_b_dw, m_cv_ln_g, m_cv_ln_b, m_cv_b_pw2, m_ffn_w_dw]
    small_v = [v_pre_g, v_post_g, v_cv_b_pw1, v_cv_w_dw, v_cv_b_dw, v_cv_ln_g, v_cv_ln_b, v_cv_b_pw2, v_ffn_w_dw]
    sizes = [p.size for p in small_w]
    offs = [sum(sizes[:k]) for k in range(len(sizes) + 1)]
    pack = -(-offs[-1] // (SUBLANES * LANES)) * SUBLANES * LANES

    got = _exchange("gather_small", [c, _small_pack(small_w, pack), pool_w[0].astype(BF16)], [True] * 3)
    c_all = got[0].reshape(N_DEV, d)
    smalls = [got[1][:, offs[k]:offs[k + 1]].reshape((N_DEV,) + small_w[k].shape) for k in range(len(small_w))]
    pre_g_f, post_g_f = _from_shards(smalls[0], 2), _from_shards(smalls[1], 2)
    b1_f = _from_shards(smalls[2], 1)[0]
    cvw_f = jnp.pad(_from_shards(smalls[3], 2)[0], ((0, CONV_HALO - CONV_TAPS), (0, 0)))
    bdw_f, lng_f, lnb_f, b2_f = [_from_shards(smalls[k], 1)[0] for k in (4, 5, 6, 7)]
    fdw = jnp.pad(smalls[8], ((0, 0), (0, 0), (0, SUBLANES - FFN_TAPS), (0, 0)))
    wp = jnp.swapaxes(got[2], 0, 1).reshape(pool_w.shape[1], -1, pool_w.shape[3])
    wdw = [fdw[:, l].reshape(2, n_j, SUBLANES, fc) for l in range(depth)]
    cvec = jnp.zeros((SUBLANES, 2 * d), F32)
    cvec = cvec.at[CV_B1].set(b1_f)
    for r, vrow in ((CV_BDW, bdw_f), (CV_LNG, lng_f), (CV_LNB, lnb_f), (CV_B2, b2_f)):
        cvec = cvec.at[r, :d].set(vrow)

    mod_cols = _ada_fwd(c_all, ada_w)
    (mod_all,) = _exchange("gather_mod", [mod_cols], [True])
    mod = lax.dynamic_index_in_dim(mod_all, me, axis=2, keepdims=False)
    mod = jnp.swapaxes(mod, 0, 1).reshape(depth, N_MOD, d) + ada_b.reshape(depth, N_MOD, d)

    def pv_of(l, s):
        rows = [pre_g_f[l, s], 1.0 + mod[l, 3 * s + 1], mod[l, 3 * s], post_g_f[l, s], mod[l, 3 * s + 2]]
        return jnp.concatenate([jnp.stack(rows), jnp.zeros((SUBLANES - len(rows), d), F32)])

    x0 = x[0]
    pv00, pv01, pv10, pv11 = pv_of(0, 0), pv_of(0, 1), pv_of(1, 0), pv_of(1, 1)
    tm_pool, tm_ffn, tm_bwd, tm_conv = min(TM_POOL, t_len), min(TM_FFN, t_len), min(TM_FFN_BWD, t_len), min(TM_CONV, t_len)
    tm_w = min(TM_FFN_W, t_len)
    wup_t = jnp.swapaxes(ffn_w_up, 1, 2)
    ex = _TwoLevelGather([wup_t[0].astype(BF16), ffn_w_down[0].astype(BF16)], early_forward=False)
    (x1, y0), (wup0, wdn0) = _pool_fwd(x0, pv00, wp, pool_scale, tm=tm_pool, ex=ex)
    ex = _TwoLevelGather([cv_w_pw1[0].astype(BF16), cv_w_pw2[0].astype(BF16), wup_t[1].astype(BF16),
                          ffn_w_down[1].astype(BF16)], early_forward=True)
    (x2, y1, h1, a1, c1), (w1, w2, wup1, wdn1) = _ffn_fwd(
        x1, pv01, wup0.reshape(2, n_j, fc, d), wdw[0], wdn0.reshape(n_j * fc, d), tm=tm_ffn, ex=ex)
    w2 = w2.reshape(d, d)
    wup = [wup0.reshape(2, n_j, fc, d), wup1.reshape(2, n_j, fc, d)]
    wdn = [wdn0.reshape(n_j * fc, d), wdn1.reshape(n_j * fc, d)]
    x3, y2, a2, c2 = _conv_fwd(x2, pv10, w1, w2, cvw_f, cvec, tm=tm_conv)
    (x4, y3, h3, a3, c3), _ = _ffn_fwd(x3, pv11, wup[1], wdw[1], wdn[1], tm=tm_ffn)
    dx4, sq, dy3, pgq11 = _loss_head(x4, loss_target[0], y3, pv11, tm=tm_ffn)
    loss = lax.psum(jnp.sum(sq) * (0.5 / d), MESH_AXES)

    (dhp3, gup1, gdn1, gfdw1), _ = _ffn_bwd(dy3, h3, a3, c3, wup[1], wdw[1], wdn[1], tm=tm_bwd)
    (dx3, pgp11), _ = _sub_pre_bwd(dhp3, x3, dx4, pv11, tm=tm_ffn)
    ex = _Exchange([gup1.reshape(N_DEV, fc, d), gdn1.reshape(N_DEV, -1, d)], [False, False])
    (dx2, pg10, gw1, gw2, gcvec, gcvw), (rup1, rdn1) = _conv_bwd(
        dx3, x2, y2, a2, c2, pv10, w1, w2, cvw_f, cvec, tm=tm_conv, ex=ex)
    (dy1, pgq01), _ = _sub_post_bwd(dx2, y1, pv01, tm=tm_w)
    ex = _Exchange([gw1, gw2.reshape(N_DEV, -1, d)], [False, False])
    (dhp1, gup0, gdn0, gfdw0), (rw1, rw2) = _ffn_bwd(dy1, h1, a1, c1, wup[0], wdw[0], wdn[0], tm=tm_bwd, ex=ex)
    n_chip = N_DEV // 2
    my_core = lax.axis_index("c")
    pair = gup0.reshape(n_chip, 2, fc, d)
    ex = _Both(_SiblingSwap([pair]), _Exchange([gdn0.reshape(N_DEV, -1, d)], [False]))
    (dx1, pgp01), (theirs, rdn0) = _sub_pre_bwd(dhp1, x1, dx2, pv01, tm=tm_ffn, ex=ex)
    pair_sum = _pair_sum(lax.dynamic_index_in_dim(pair, my_core, axis=1, keepdims=False), theirs)
    (dx0, pg00, gwp), (rup0,) = _pool_bwd(dx1, x0, y0, pv00, wp, pool_scale, tm=tm_pool, ex=_ChipExchange([pair_sum]))
    pg01, pg11 = pgq01 + pgp01, pgq11 + pgp11

    pgs = [[pg00, pg01], [pg10, pg11]]
    g_pre = jnp.stack([jnp.stack([pgs[l][s][PG_GPRE] for s in range(2)]) for l in range(depth)])
    g_post = jnp.stack([jnp.stack([pgs[l][s][PG_GPOST] for s in range(2)]) for l in range(depth)])
    dmod = jnp.stack([jnp.concatenate([pgs[l][s][r] for s in range(2) for r in (PG_SH, PG_SC, PG_GT)])
                      for l in range(depth)])
    gfdw = jnp.stack([g.reshape(N_DEV, SUBLANES, fc)[:, :FFN_TAPS] for g in (gfdw0, gfdw1)], axis=1)
    small_g = [_to_shards(g_pre, 2), _to_shards(g_post, 2), _to_shards(gcvec[CV_B1][None], 1),
               _to_shards(gcvw[None, :CONV_TAPS], 2), _to_shards(gcvec[CV_BDW, :d][None], 1),
               _to_shards(gcvec[CV_LNG, :d][None], 1), _to_shards(gcvec[CV_LNB, :d][None], 1),
               _to_shards(gcvec[CV_B2, :d][None], 1), gfdw]
    small_send = jnp.concatenate([g.reshape(N_DEV, -1) for g in small_g], axis=1)
    small_send = jnp.pad(small_send, ((0, 0), (0, pack - small_send.shape[1])))
    gwp_send = jnp.swapaxes(gwp.reshape(gwp.shape[0], N_DEV, -1, gwp.shape[2]), 0, 1)
    rsmall, rwp, rmod, rscale = _exchange("scatter_small", [small_send, gwp_send, dmod, pg00[PG_EXTRA][None]],
                                          [False, False, True, True])

    outs = {}

    def put(name, res, shape=None):
        outs[name] = [r if shape is None else r.reshape(shape) for r in res]

    small_res = _adam_nd("adam_small", rsmall.reshape(N_DEV, -1, SUBLANES * LANES),
                         _small_pack(small_w, pack).reshape(-1, SUBLANES * LANES),
                         _small_pack(small_m, pack).reshape(-1, SUBLANES * LANES),
                         _small_pack(small_v, pack).reshape(-1, SUBLANES * LANES))
    small_names = ["pre_g", "post_g", "cv_b_pw1", "cv_w_dw", "cv_b_dw", "cv_ln_g", "cv_ln_b", "cv_b_pw2", "ffn_w_dw"]
    for k, nm in enumerate(small_names):
        outs[nm] = [r.reshape(-1)[offs[k]:offs[k + 1]].reshape(small_w[k].shape) for r in small_res]
    put("pool_w", _adam_nd("adam_pool_w", rwp[:, None], pool_w, m_pool_w, v_pool_w))
    put("cv_w_pw1", _adam_nd("adam_cv_w_pw1", rw1[:, None], cv_w_pw1, m_cv_w_pw1, v_cv_w_pw1))
    put("cv_w_pw2", _adam_nd("adam_cv_w_pw2", rw2[:, None], cv_w_pw2, m_cv_w_pw2, v_cv_w_pw2))
    outs["ffn_w_up"] = [jnp.swapaxes(r, 1, 2) for r in _adam_nd(
        "adam_ffn_w_up", [rup0, rup1], wup_t, jnp.swapaxes(m_ffn_w_up, 1, 2), jnp.swapaxes(v_ffn_w_up, 1, 2))]
    put("ffn_w_down", _adam_nd("adam_ffn_w_down", [rdn0, rdn1], ffn_w_down, m_ffn_w_down, v_ffn_w_down))
    put("ada_b", _adam_nd("adam_ada_b", rmod, ada_b, m_ada_b, v_ada_b))
    put("pool_scale", _adam_nd("adam_pool_scale", rscale, pool_scale, m_pool_scale, v_pool_scale))
    cols = ada_w.shape[2]
    dmod_cols = jnp.swapaxes(lax.dynamic_slice_in_dim(rmod, me * cols, cols, axis=2), 0, 1)
    put("ada_w", _adam_nd("adam_ada_w", _ada_bwd(c_all, dmod_cols)[None], ada_w, m_ada_w, v_ada_w))

    order = ["ada_w", "ada_b", "pre_g", "post_g", "pool_w", "pool_scale", "cv_w_pw1", "cv_b_pw1", "cv_w_dw", "cv_b_dw",
             "cv_ln_g", "cv_ln_b", "cv_w_pw2", "cv_b_pw2", "ffn_w_up", "ffn_w_dw", "ffn_w_down"]
    return (loss, dx0[None], *[outs[nm][0] for nm in order], *[outs[nm][1] for nm in order],
            *[outs[nm][2] for nm in order], *[outs[nm][3] for nm in order])
```
